```python
import jax, jax.numpy as jnp
from jax import lax
import numpy as np


D_MODEL = 4096
BATCH = 16
SEQ = 2048
DEPTH = 1

GM_WIDTH = 2048
GM_GROUPS = 8
GM_GROUP_W = GM_WIDTH // GM_GROUPS
CHUNK = 128
MLA_HEADS = 32
QK_NOPE = 128
QK_ROPE = 64
QK_HEAD = QK_NOPE + QK_ROPE
V_HEAD = 128
Q_LORA = 1024
KV_LORA = 512
ROPE_THETA = 10000.0
Q_BLOCK = 128
D_FF = 4 * D_MODEL
N_BRANCH = 2
N_MOD = 6
EPS = 1e-6
OFF_Q = 2 * GM_WIDTH
OFF_KV = OFF_Q + Q_LORA
OFF_KPE = OFF_KV + KV_LORA
OFF_GATE = OFF_KPE + QK_ROPE
IN_COLS = OFF_GATE + N_BRANCH * D_MODEL

kernel_name = 'hybrid_gmlp_mla_block'


def rms_norm(x, g):
    xf = x.astype(jnp.float32)
    y = xf * lax.rsqrt(jnp.mean(xf * xf, axis=-1, keepdims=True) + EPS)
    return (y * g.astype(jnp.float32)).astype(x.dtype)


def modulate(h, shift, scale):
    return h * (1 + scale[:, None, :]) + shift[:, None, :]


def rope_tables(positions, dtype):
    inv_freq = 1.0 / (ROPE_THETA ** (jnp.arange(0, QK_ROPE, 2, dtype=jnp.float32) / QK_ROPE))
    ang = positions.astype(jnp.float32)[..., None] * inv_freq
    return (jnp.cos(ang)[:, :, None, :].astype(dtype),
            jnp.sin(ang)[:, :, None, :].astype(dtype))


def apply_rope(x, cos, sin):
    x1, x2 = jnp.split(x, 2, axis=-1)
    return jnp.concatenate([x1 * cos - x2 * sin, x2 * cos + x1 * sin], axis=-1)


def gmlp_branch(uv, g_v, w_s, b_s):
    z = jax.nn.gelu(uv, approximate=False)
    u, v = z[..., :GM_WIDTH], z[..., GM_WIDTH:]
    v = rms_norm(v, g_v)
    b, s, _ = v.shape
    v = v.reshape(b, s // CHUNK, CHUNK, GM_GROUPS, GM_GROUP_W)
    causal = jnp.tril(jnp.ones((CHUNK, CHUNK), dtype=bool))
    w = jnp.where(causal[None], w_s, 0.0)
    mixed = jnp.einsum('gts,bnsgc->bntgc', w, v) + b_s.T[None, None, :, :, None]
    return u * mixed.reshape(b, s, GM_WIDTH)


def mla_branch(q_lat, kv_lat, k_pe, cos, sin, g_q_lat, g_kv_lat, w_uq, w_ukv, g_qn, g_kn):
    b, s, _ = q_lat.shape
    q = (rms_norm(q_lat, g_q_lat) @ w_uq).reshape(b, s, MLA_HEADS, QK_HEAD)
    kv = (rms_norm(kv_lat, g_kv_lat) @ w_ukv).reshape(b, s, MLA_HEADS, QK_NOPE + V_HEAD)
    k_nope, v = kv[..., :QK_NOPE], kv[..., QK_NOPE:]
    k = jnp.concatenate([k_nope, jnp.broadcast_to(k_pe[:, :, None, :], (b, s, MLA_HEADS, QK_ROPE))], axis=-1)
    q = rms_norm(q, g_qn)
    k = rms_norm(k, g_kn)
    q = jnp.concatenate([q[..., :QK_NOPE], apply_rope(q[..., QK_NOPE:], cos, sin)], axis=-1)
    k = jnp.concatenate([k[..., :QK_NOPE], apply_rope(k[..., QK_NOPE:], cos, sin)], axis=-1)
    q = q.transpose(0, 2, 1, 3)
    k = k.transpose(0, 2, 1, 3)
    v = v.transpose(0, 2, 1, 3)
    nb = s // Q_BLOCK
    q_blocks = q.reshape(b, MLA_HEADS, nb, Q_BLOCK, QK_HEAD).transpose(2, 0, 1, 3, 4)
    key_idx = jnp.arange(s)
    scale = QK_HEAD ** -0.5

    def attend(args):
        q_blk, i = args
        q_idx = i * Q_BLOCK + jnp.arange(Q_BLOCK)
        scores = jnp.einsum('bhqd,bhkd->bhqk', q_blk, k).astype(jnp.float32) * scale
        scores = jnp.where(key_idx[None, :] <= q_idx[:, None], scores, -jnp.inf)
        p = jax.nn.softmax(scores, axis=-1).astype(v.dtype)
        return jnp.einsum('bhqk,bhkd->bhqd', p, v)

    out = lax.map(attend, (q_blocks, jnp.arange(nb)))
    return out.transpose(1, 0, 3, 2, 4).reshape(b, s, MLA_HEADS * V_HEAD)


def _fwd_setup_inputs(seed: int = 0) -> dict:
    key = jax.random.key(seed)
    ks = jax.random.split(key, 24)
    f32 = jnp.float32

    def nrm(k, shape, fan_in):
        return jax.random.normal(k, shape, f32) * (fan_in ** -0.5)

    def gain(k, shape):
        return 1.0 + 0.02 * jax.random.normal(k, shape, f32)

    L = DEPTH
    x = jax.random.normal(ks[0], (BATCH, SEQ, D_MODEL), f32)
    c = jax.random.normal(ks[1], (BATCH, D_MODEL), f32)
    positions = (jnp.arange(SEQ, dtype=jnp.int32)[None, :]
                 + jax.random.randint(ks[2], (BATCH, 1), 0, 4096, dtype=jnp.int32))
    return {
        'x': x,
        'c': c,
        'positions': positions,
        'w_ada': nrm(ks[3], (L, D_MODEL, N_MOD * D_MODEL), D_MODEL),
        'b_ada': 0.02 * jax.random.normal(ks[4], (L, N_MOD * D_MODEL), f32),
        'g_norm1': gain(ks[5], (L, D_MODEL)),
        'w_in': nrm(ks[6], (L, D_MODEL, IN_COLS), D_MODEL),
        'g_v': gain(ks[7], (L, GM_WIDTH)),
        'w_s': nrm(ks[8], (L, GM_GROUPS, CHUNK, CHUNK), CHUNK),
        'b_s': 1.0 + 0.1 * jax.random.normal(ks[9], (L, GM_GROUPS, CHUNK), f32),
        'g_q_lat': gain(ks[10], (L, Q_LORA)),
        'g_kv_lat': gain(ks[11], (L, KV_LORA)),
        'w_uq': nrm(ks[12], (L, Q_LORA, MLA_HEADS * QK_HEAD), Q_LORA),
        'w_ukv': nrm(ks[13], (L, KV_LORA, MLA_HEADS * (QK_NOPE + V_HEAD)), KV_LORA),
        'g_qn': gain(ks[14], (L, QK_HEAD)),
        'g_kn': gain(ks[15], (L, QK_HEAD)),
        'w_branch_a': nrm(ks[16], (L, GM_WIDTH, D_MODEL), GM_WIDTH),
        'w_branch_b': nrm(ks[17], (L, MLA_HEADS * V_HEAD, D_MODEL), MLA_HEADS * V_HEAD),
        'w_out': nrm(ks[18], (L, D_MODEL, D_MODEL), D_MODEL),
        'g_norm2': gain(ks[19], (L, D_MODEL)),
        'w_ff1': nrm(ks[20], (L, D_MODEL, D_FF), D_MODEL),
        'w_ff2': nrm(ks[21], (L, D_FF, D_MODEL), D_FF),
    }


def _fwd_reference(x, c, positions, w_ada, b_ada, g_norm1, w_in, g_v, w_s, b_s, g_q_lat, g_kv_lat,
              w_uq, w_ukv, g_qn, g_kn, w_branch_a, w_branch_b, w_out, g_norm2, w_ff1, w_ff2):
    cos, sin = rope_tables(positions, x.dtype)
    cond = jax.nn.silu(c)
    for l in range(DEPTH):
        mod = cond @ w_ada[l] + b_ada[l]
        sh1, sc1, ga1, sh2, sc2, ga2 = jnp.split(mod, N_MOD, axis=-1)
        h = modulate(rms_norm(x, g_norm1[l]), sh1, sc1)
        proj = h @ w_in[l]
        y_a = gmlp_branch(proj[..., :OFF_Q], g_v[l], w_s[l], b_s[l]) @ w_branch_a[l]
        y_b = mla_branch(proj[..., OFF_Q:OFF_KV], proj[..., OFF_KV:OFF_KPE], proj[..., OFF_KPE:OFF_GATE],
                         cos, sin, g_q_lat[l], g_kv_lat[l], w_uq[l], w_ukv[l], g_qn[l], g_kn[l]) @ w_branch_b[l]
        gate_a = jax.nn.sigmoid(proj[..., OFF_GATE:OFF_GATE + D_MODEL])
        gate_b = jax.nn.sigmoid(proj[..., OFF_GATE + D_MODEL:])
        mixed = gate_a * y_a + gate_b * y_b
        x = x + ga1[:, None, :] * (mixed @ w_out[l])
        h = modulate(rms_norm(x, g_norm2[l]), sh2, sc2)
        ff = jnp.square(jax.nn.relu(h @ w_ff1[l])) @ w_ff2[l]
        x = x + ga2[:, None, :] * ff
    return x


import jax as _jax
import jax.numpy as _jnp

TWIN_FORMAT = 'train_step'
FWD_PARAMS = ['x', 'c', 'positions', 'w_ada', 'b_ada', 'g_norm1', 'w_in', 'g_v', 'w_s', 'b_s', 'g_q_lat', 'g_kv_lat', 'w_uq', 'w_ukv', 'g_qn', 'g_kn', 'w_branch_a', 'w_branch_b', 'w_out', 'g_norm2', 'w_ff1', 'w_ff2']
TWIN_WEIGHTS = ['w_ada', 'b_ada', 'g_norm1', 'w_in', 'g_v', 'w_s', 'b_s', 'g_q_lat', 'g_kv_lat', 'w_uq', 'w_ukv', 'g_qn', 'g_kn', 'w_branch_a', 'w_branch_b', 'w_out', 'g_norm2', 'w_ff1', 'w_ff2']
TWIN_DIFF_INPUT = 'x'
TWIN_INPUTS = ['x', 'c', 'positions', 'w_ada', 'b_ada', 'g_norm1', 'w_in', 'g_v', 'w_s', 'b_s', 'g_q_lat', 'g_kv_lat', 'w_uq', 'w_ukv', 'g_qn', 'g_kn', 'w_branch_a', 'w_branch_b', 'w_out', 'g_norm2', 'w_ff1', 'w_ff2', 'loss_target', 'm_w_ada', 'm_b_ada', 'm_g_norm1', 'm_w_in', 'm_g_v', 'm_w_s', 'm_b_s', 'm_g_q_lat', 'm_g_kv_lat', 'm_w_uq', 'm_w_ukv', 'm_g_qn', 'm_g_kn', 'm_w_branch_a', 'm_w_branch_b', 'm_w_out', 'm_g_norm2', 'm_w_ff1', 'm_w_ff2', 'v_w_ada', 'v_b_ada', 'v_g_norm1', 'v_w_in', 'v_g_v', 'v_w_s', 'v_b_s', 'v_g_q_lat', 'v_g_kv_lat', 'v_w_uq', 'v_w_ukv', 'v_g_qn', 'v_g_kn', 'v_w_branch_a', 'v_w_branch_b', 'v_w_out', 'v_g_norm2', 'v_w_ff1', 'v_w_ff2']
TWIN_OUTPUTS = ['loss', 'grad_x', 'grad_w_ada', 'grad_b_ada', 'grad_g_norm1', 'grad_w_in', 'grad_g_v', 'grad_w_s', 'grad_b_s', 'grad_g_q_lat', 'grad_g_kv_lat', 'grad_w_uq', 'grad_w_ukv', 'grad_g_qn', 'grad_g_kn', 'grad_w_branch_a', 'grad_w_branch_b', 'grad_w_out', 'grad_g_norm2', 'grad_w_ff1', 'grad_w_ff2', 'delta_w_ada', 'delta_b_ada', 'delta_g_norm1', 'delta_w_in', 'delta_g_v', 'delta_w_s', 'delta_b_s', 'delta_g_q_lat', 'delta_g_kv_lat', 'delta_w_uq', 'delta_w_ukv', 'delta_g_qn', 'delta_g_kn', 'delta_w_branch_a', 'delta_w_branch_b', 'delta_w_out', 'delta_g_norm2', 'delta_w_ff1', 'delta_w_ff2', 'new_m_w_ada', 'new_m_b_ada', 'new_m_g_norm1', 'new_m_w_in', 'new_m_g_v', 'new_m_w_s', 'new_m_b_s', 'new_m_g_q_lat', 'new_m_g_kv_lat', 'new_m_w_uq', 'new_m_w_ukv', 'new_m_g_qn', 'new_m_g_kn', 'new_m_w_branch_a', 'new_m_w_branch_b', 'new_m_w_out', 'new_m_g_norm2', 'new_m_w_ff1', 'new_m_w_ff2', 'new_v_w_ada', 'new_v_b_ada', 'new_v_g_norm1', 'new_v_w_in', 'new_v_g_v', 'new_v_w_s', 'new_v_b_s', 'new_v_g_q_lat', 'new_v_g_kv_lat', 'new_v_w_uq', 'new_v_w_ukv', 'new_v_g_qn', 'new_v_g_kn', 'new_v_w_branch_a', 'new_v_w_branch_b', 'new_v_w_out', 'new_v_g_norm2', 'new_v_w_ff1', 'new_v_w_ff2']
TWIN_LEAF_KINDS = {'loss': 'loss', 'grad_x': 'grad_x', 'grad_w_ada': 'grad_w', 'grad_b_ada': 'grad_w', 'grad_g_norm1': 'grad_w', 'grad_w_in': 'grad_w', 'grad_g_v': 'grad_w', 'grad_w_s': 'grad_w', 'grad_b_s': 'grad_w', 'grad_g_q_lat': 'grad_w', 'grad_g_kv_lat': 'grad_w', 'grad_w_uq': 'grad_w', 'grad_w_ukv': 'grad_w', 'grad_g_qn': 'grad_w', 'grad_g_kn': 'grad_w', 'grad_w_branch_a': 'grad_w', 'grad_w_branch_b': 'grad_w', 'grad_w_out': 'grad_w', 'grad_g_norm2': 'grad_w', 'grad_w_ff1': 'grad_w', 'grad_w_ff2': 'grad_w', 'delta_w_ada': 'delta_w', 'delta_b_ada': 'delta_w', 'delta_g_norm1': 'delta_w', 'delta_w_in': 'delta_w', 'delta_g_v': 'delta_w', 'delta_w_s': 'delta_w', 'delta_b_s': 'delta_w', 'delta_g_q_lat': 'delta_w', 'delta_g_kv_lat': 'delta_w', 'delta_w_uq': 'delta_w', 'delta_w_ukv': 'delta_w', 'delta_g_qn': 'delta_w', 'delta_g_kn': 'delta_w', 'delta_w_branch_a': 'delta_w', 'delta_w_branch_b': 'delta_w', 'delta_w_out': 'delta_w', 'delta_g_norm2': 'delta_w', 'delta_w_ff1': 'delta_w', 'delta_w_ff2': 'delta_w', 'new_m_w_ada': 'new_m', 'new_m_b_ada': 'new_m', 'new_m_g_norm1': 'new_m', 'new_m_w_in': 'new_m', 'new_m_g_v': 'new_m', 'new_m_w_s': 'new_m', 'new_m_b_s': 'new_m', 'new_m_g_q_lat': 'new_m', 'new_m_g_kv_lat': 'new_m', 'new_m_w_uq': 'new_m', 'new_m_w_ukv': 'new_m', 'new_m_g_qn': 'new_m', 'new_m_g_kn': 'new_m', 'new_m_w_branch_a': 'new_m', 'new_m_w_branch_b': 'new_m', 'new_m_w_out': 'new_m', 'new_m_g_norm2': 'new_m', 'new_m_w_ff1': 'new_m', 'new_m_w_ff2': 'new_m', 'new_v_w_ada': 'new_v', 'new_v_b_ada': 'new_v', 'new_v_g_norm1': 'new_v', 'new_v_w_in': 'new_v', 'new_v_g_v': 'new_v', 'new_v_w_s': 'new_v', 'new_v_b_s': 'new_v', 'new_v_g_q_lat': 'new_v', 'new_v_g_kv_lat': 'new_v', 'new_v_w_uq': 'new_v', 'new_v_w_ukv': 'new_v', 'new_v_g_qn': 'new_v', 'new_v_g_kn': 'new_v', 'new_v_w_branch_a': 'new_v', 'new_v_w_branch_b': 'new_v', 'new_v_w_out': 'new_v', 'new_v_g_norm2': 'new_v', 'new_v_w_ff1': 'new_v', 'new_v_w_ff2': 'new_v'}


def _forward(args):
    return _fwd_reference(*[args[k] for k in FWD_PARAMS])


def _output_shape():
    def fwd():
        inp = _fwd_setup_inputs(0)
        return _fwd_reference(*[inp[k] for k in FWD_PARAMS])
    out = _jax.eval_shape(fwd)
    return out.shape, out.dtype

N_MICROBATCH = 1
ADAM_LR = 0.001
ADAM_B1 = 0.9
ADAM_B2 = 0.999
ADAM_EPS = 1e-08
ADAM_WD = 0.01
ADAM_STEP = 10
PER_EXAMPLE_BATCH_AXIS = {'x': 0, 'c': 0, 'positions': 0, 'loss_target': 0}
SHARED_INPUTS = []
_WEIGHT_DTYPES = {'w_ada': _jnp.float32, 'b_ada': _jnp.float32, 'g_norm1': _jnp.float32, 'w_in': _jnp.float32, 'g_v': _jnp.float32, 'w_s': _jnp.float32, 'b_s': _jnp.float32, 'g_q_lat': _jnp.float32, 'g_kv_lat': _jnp.float32, 'w_uq': _jnp.float32, 'w_ukv': _jnp.float32, 'g_qn': _jnp.float32, 'g_kn': _jnp.float32, 'w_branch_a': _jnp.float32, 'w_branch_b': _jnp.float32, 'w_out': _jnp.float32, 'g_norm2': _jnp.float32, 'w_ff1': _jnp.float32, 'w_ff2': _jnp.float32}
MOMENT_SCALE = {'w_ada': 3.263055e+00, 'b_ada': 8.509122e+00, 'g_norm1': 9.819292e-01, 'w_in': 6.124096e-01, 'g_v': 8.412785e-01, 'w_s': 1.081310e+00, 'b_s': 3.033561e+00, 'g_q_lat': 4.149438e-02, 'g_kv_lat': 2.900027e+00, 'w_uq': 1.732561e-02, 'w_ukv': 6.069170e-01, 'g_qn': 1.269296e-01, 'g_kn': 1.281145e-01, 'w_branch_a': 1.038719e+00, 'w_branch_b': 8.576951e-01, 'w_out': 1.327821e+00, 'g_norm2': 2.216148e+01, 'w_ff1': 1.288882e+00, 'w_ff2': 3.100389e+00}


def _to_microbatches(a, axis):
    t = _jnp.moveaxis(a, axis, 0)
    t = t.reshape((N_MICROBATCH, t.shape[0] // N_MICROBATCH) + t.shape[1:])
    return _jnp.moveaxis(t, 1, axis + 1)


def setup_inputs(seed: int = 0) -> dict:
    inp = _fwd_setup_inputs(seed)
    key = _jax.random.fold_in(_jax.random.key(seed), 7919)
    shape, _ = _output_shape()
    out = dict(inp)
    out["loss_target"] = _jax.random.normal(_jax.random.fold_in(key, 0), shape, _jnp.float32)
    for i, name in enumerate(TWIN_WEIGHTS):
        w = inp[name].astype(_jnp.float32)
        if MOMENT_SCALE is None:
            s = _jnp.sqrt(_jnp.mean(_jnp.square(w)) + 1e-30)
        else:
            s = MOMENT_SCALE[name]
        km, kv = _jax.random.split(_jax.random.fold_in(key, i + 1))
        out[name] = w
        out["m_" + name] = s * _jax.random.normal(km, w.shape, _jnp.float32)
        out["v_" + name] = (s * s) * _jax.random.uniform(kv, w.shape, _jnp.float32, 0.5, 1.5)
    if N_MICROBATCH > 1:
        for name, axis in PER_EXAMPLE_BATCH_AXIS.items():
            out[name] = _to_microbatches(out[name], axis)
    return {'x': out['x'], 'c': out['c'], 'positions': out['positions'], 'w_ada': out['w_ada'], 'b_ada': out['b_ada'], 'g_norm1': out['g_norm1'], 'w_in': out['w_in'], 'g_v': out['g_v'], 'w_s': out['w_s'], 'b_s': out['b_s'], 'g_q_lat': out['g_q_lat'], 'g_kv_lat': out['g_kv_lat'], 'w_uq': out['w_uq'], 'w_ukv': out['w_ukv'], 'g_qn': out['g_qn'], 'g_kn': out['g_kn'], 'w_branch_a': out['w_branch_a'], 'w_branch_b': out['w_branch_b'], 'w_out': out['w_out'], 'g_norm2': out['g_norm2'], 'w_ff1': out['w_ff1'], 'w_ff2': out['w_ff2'], 'loss_target': out['loss_target'], 'm_w_ada': out['m_w_ada'], 'm_b_ada': out['m_b_ada'], 'm_g_norm1': out['m_g_norm1'], 'm_w_in': out['m_w_in'], 'm_g_v': out['m_g_v'], 'm_w_s': out['m_w_s'], 'm_b_s': out['m_b_s'], 'm_g_q_lat': out['m_g_q_lat'], 'm_g_kv_lat': out['m_g_kv_lat'], 'm_w_uq': out['m_w_uq'], 'm_w_ukv': out['m_w_ukv'], 'm_g_qn': out['m_g_qn'], 'm_g_kn': out['m_g_kn'], 'm_w_branch_a': out['m_w_branch_a'], 'm_w_branch_b': out['m_w_branch_b'], 'm_w_out': out['m_w_out'], 'm_g_norm2': out['m_g_norm2'], 'm_w_ff1': out['m_w_ff1'], 'm_w_ff2': out['m_w_ff2'], 'v_w_ada': out['v_w_ada'], 'v_b_ada': out['v_b_ada'], 'v_g_norm1': out['v_g_norm1'], 'v_w_in': out['v_w_in'], 'v_g_v': out['v_g_v'], 'v_w_s': out['v_w_s'], 'v_b_s': out['v_b_s'], 'v_g_q_lat': out['v_g_q_lat'], 'v_g_kv_lat': out['v_g_kv_lat'], 'v_w_uq': out['v_w_uq'], 'v_w_ukv': out['v_w_ukv'], 'v_g_qn': out['v_g_qn'], 'v_g_kn': out['v_g_kn'], 'v_w_branch_a': out['v_w_branch_a'], 'v_w_branch_b': out['v_w_branch_b'], 'v_w_out': out['v_w_out'], 'v_g_norm2': out['v_g_norm2'], 'v_w_ff1': out['v_w_ff1'], 'v_w_ff2': out['v_w_ff2']}


def _loss(weights, diff, rest, loss_target):
    with _jax.named_scope("forward"):
        args = {**rest, TWIN_DIFF_INPUT: diff, **{k: w.astype(_WEIGHT_DTYPES[k]) for k, w in weights.items()}}
        y = _forward(args)
    with _jax.named_scope("loss_head"):
        err = _jnp.square(y.astype(_jnp.float32) - loss_target)
        return 0.5 * _jnp.sum(_jnp.mean(err, axis=-1)) if err.ndim else 0.5 * err


def _adamw(w, g, m, v):
    m = ADAM_B1 * m + (1.0 - ADAM_B1) * g
    v = ADAM_B2 * v + (1.0 - ADAM_B2) * _jnp.square(g)
    m_hat = m / (1.0 - ADAM_B1 ** ADAM_STEP)
    v_hat = v / (1.0 - ADAM_B2 ** ADAM_STEP)
    delta = -ADAM_LR * (m_hat / (_jnp.sqrt(v_hat) + ADAM_EPS) + ADAM_WD * w)
    return delta, m, v


def reference(x, c, positions, w_ada, b_ada, g_norm1, w_in, g_v, w_s, b_s, g_q_lat, g_kv_lat, w_uq, w_ukv, g_qn, g_kn, w_branch_a, w_branch_b, w_out, g_norm2, w_ff1, w_ff2, loss_target, m_w_ada, m_b_ada, m_g_norm1, m_w_in, m_g_v, m_w_s, m_b_s, m_g_q_lat, m_g_kv_lat, m_w_uq, m_w_ukv, m_g_qn, m_g_kn, m_w_branch_a, m_w_branch_b, m_w_out, m_g_norm2, m_w_ff1, m_w_ff2, v_w_ada, v_b_ada, v_g_norm1, v_w_in, v_g_v, v_w_s, v_b_s, v_g_q_lat, v_g_kv_lat, v_w_uq, v_w_ukv, v_g_qn, v_g_kn, v_w_branch_a, v_w_branch_b, v_w_out, v_g_norm2, v_w_ff1, v_w_ff2):
    given = dict(x=x, c=c, positions=positions, w_ada=w_ada, b_ada=b_ada, g_norm1=g_norm1, w_in=w_in, g_v=g_v, w_s=w_s, b_s=b_s, g_q_lat=g_q_lat, g_kv_lat=g_kv_lat, w_uq=w_uq, w_ukv=w_ukv, g_qn=g_qn, g_kn=g_kn, w_branch_a=w_branch_a, w_branch_b=w_branch_b, w_out=w_out, g_norm2=g_norm2, w_ff1=w_ff1, w_ff2=w_ff2, loss_target=loss_target, m_w_ada=m_w_ada, m_b_ada=m_b_ada, m_g_norm1=m_g_norm1, m_w_in=m_w_in, m_g_v=m_g_v, m_w_s=m_w_s, m_b_s=m_b_s, m_g_q_lat=m_g_q_lat, m_g_kv_lat=m_g_kv_lat, m_w_uq=m_w_uq, m_w_ukv=m_w_ukv, m_g_qn=m_g_qn, m_g_kn=m_g_kn, m_w_branch_a=m_w_branch_a, m_w_branch_b=m_w_branch_b, m_w_out=m_w_out, m_g_norm2=m_g_norm2, m_w_ff1=m_w_ff1, m_w_ff2=m_w_ff2, v_w_ada=v_w_ada, v_b_ada=v_b_ada, v_g_norm1=v_g_norm1, v_w_in=v_w_in, v_g_v=v_g_v, v_w_s=v_w_s, v_b_s=v_b_s, v_g_q_lat=v_g_q_lat, v_g_kv_lat=v_g_kv_lat, v_w_uq=v_w_uq, v_w_ukv=v_w_ukv, v_g_qn=v_g_qn, v_g_kn=v_g_kn, v_w_branch_a=v_w_branch_a, v_w_branch_b=v_w_branch_b, v_w_out=v_w_out, v_g_norm2=v_g_norm2, v_w_ff1=v_w_ff1, v_w_ff2=v_w_ff2)
    weights = {n: given[n] for n in TWIN_WEIGHTS}
    shared = {n: given[n] for n in SHARED_INPUTS}
    per_example = {n: given[n] for n in ['x', 'c', 'positions']}
    grad_fn = _jax.value_and_grad(_loss, argnums=(0, 1))

    def one_microbatch(ex, loss_target):
        ex = dict(ex)
        diff = ex.pop(TWIN_DIFF_INPUT)
        return grad_fn(weights, diff, {**shared, **ex}, loss_target)

    if N_MICROBATCH == 1:
        loss, (grad_w, grad_x) = one_microbatch(per_example, given["loss_target"])
    else:
        def body(carry, xs):
            loss_sum, grad_sum = carry
            l_k, (gw_k, gx_k) = one_microbatch(xs[0], xs[1])
            with _jax.named_scope("update"):
                return (loss_sum + l_k, _jax.tree.map(_jnp.add, grad_sum, gw_k)), gx_k

        init = (_jnp.zeros((), _jnp.float32), _jax.tree.map(_jnp.zeros_like, weights))
        (loss, grad_w), grad_x = _jax.lax.scan(body, init, (per_example, given["loss_target"]))
    with _jax.named_scope("update"):
        delta_w, new_m, new_v = {}, {}, {}
        for n in TWIN_WEIGHTS:
            delta_w[n], new_m[n], new_v[n] = _adamw(weights[n], grad_w[n], given["m_" + n], given["v_" + n])
    return (loss, grad_x, *[grad_w[n] for n in TWIN_WEIGHTS], *[delta_w[n] for n in TWIN_WEIGHTS],
            *[new_m[n] for n in TWIN_WEIGHTS], *[new_v[n] for n in TWIN_WEIGHTS])
```

```python
import functools
import math

import jax
import jax.numpy as jnp
from jax import lax
from jax.experimental import pallas as pl
from jax.experimental.pallas import tpu as pltpu

F32 = jnp.float32
BF16 = jnp.bfloat16
SDS = jax.ShapeDtypeStruct
MESH = pl.DeviceIdType.MESH

N_DEV = 8
N_CHIP = 4
CHUNK = 128
GROUPS = 8
NOPE = 128
ROPE = 64
HALF_ROPE = ROPE // 2
QK_HEAD = NOPE + ROPE
V_HEAD = 128
HEAD_PAD = 256
LANES = 128
ROPE_THETA = 10000.0
EPS = 1e-6
INV_SQRT2 = 1.0 / math.sqrt(2.0)
INV_SQRT_2PI = 1.0 / math.sqrt(2.0 * math.pi)

ADAM_LR = 0.001
ADAM_B1 = 0.9
ADAM_B2 = 0.999
ADAM_EPS = 1e-08
ADAM_WD = 0.01
ADAM_STEP = 10

V7X_VMEM_LIMIT_BYTES = 56 * 1024 * 1024
MM_TILE = 1024
ATTN_BLOCK = 512
ROW_BLOCK = 128

NN = (((1,), (0,)), ((), ()))
NT = (((1,), (1,)), ((), ()))
TN = (((0,), (0,)), ((), ()))


def _tile(n, pref, mult):
    t = min(pref, n)
    t -= t % mult
    while t >= mult:
        if n % t == 0:
            return t
        t -= mult
    return n


def _round_up(n, m):
    return (n + m - 1) // m * m


def _cp(*sem):
    return pltpu.CompilerParams(dimension_semantics=sem, vmem_limit_bytes=V7X_VMEM_LIMIT_BYTES)


def _dot(a, b, dims):
    return lax.dot_general(a, b, dims, preferred_element_type=F32)


def _gelu(x):
    return 0.5 * x * (1.0 + lax.erf(x * INV_SQRT2))


def _gelu_grad(x):
    return 0.5 * (1.0 + lax.erf(x * INV_SQRT2)) + x * jnp.exp(-0.5 * x * x) * INV_SQRT_2PI


def _sigmoid(x):
    return 1.0 / (1.0 + jnp.exp(-x))


def _matmul(a, b, *, mode, out_dtypes, name, epi=None, extras=()):
    if mode == "tn":
        K, M = a.shape
    else:
        M, K = a.shape
    if mode == "nt":
        N, Kb = b.shape
    else:
        Kb, N = b.shape
    assert K == Kb, (name, a.shape, b.shape)
    tm = _tile(M, MM_TILE, 128)
    tn = _tile(N, MM_TILE, 128)
    tk = _tile(K, MM_TILE, 128)
    nk = K // tk
    n_extra = len(extras)
    n_out = len(out_dtypes)
    dims = {"nn": NN, "nt": NT, "tn": TN}[mode]

    def body(a_ref, b_ref, *rest):
        extra_refs = rest[:n_extra]
        out_refs = rest[n_extra:n_extra + n_out]
        acc_ref = rest[n_extra + n_out]
        k = pl.program_id(2)

        @pl.when(k == 0)
        def _():
            acc_ref[...] = jnp.zeros_like(acc_ref)

        acc_ref[...] += _dot(a_ref[...].astype(BF16), b_ref[...].astype(BF16), dims)

        @pl.when(k == nk - 1)
        def _():
            acc = acc_ref[...]
            res = (acc,) if epi is None else epi(acc, *[e[...] for e in extra_refs])
            for o_ref, r in zip(out_refs, res):
                o_ref[...] = r.astype(o_ref.dtype)

    if mode == "tn":
        a_spec = pl.BlockSpec((tk, tm), lambda i, j, k: (k, i))
    else:
        a_spec = pl.BlockSpec((tm, tk), lambda i, j, k: (i, k))
    if mode == "nt":
        b_spec = pl.BlockSpec((tn, tk), lambda i, j, k: (j, k))
    else:
        b_spec = pl.BlockSpec((tk, tn), lambda i, j, k: (k, j))
    mn_spec = pl.BlockSpec((tm, tn), lambda i, j, k: (i, j))
    outs = pl.pallas_call(
        body, name=name, grid=(M // tm, N // tn, nk),
        in_specs=[a_spec, b_spec] + [mn_spec] * n_extra,
        out_specs=[mn_spec] * n_out,
        out_shape=[SDS((M, N), dt) for dt in out_dtypes],
        scratch_shapes=[pltpu.VMEM((tm, tn), F32)],
        compiler_params=_cp("parallel", "parallel", "arbitrary"),
    )(a, b, *extras)
    return outs[0] if n_out == 1 else outs


def _here():
    return lax.axis_index("x"), lax.axis_index("y"), lax.axis_index("c")


def _other_chips(x, y):
    return [(1 - x, y), (x, 1 - y), (1 - x, 1 - y)]


def _all_gather_small(v, name):
    shape = v.shape
    n = v.size
    n_pad = _round_up(n, 8 * LANES)
    flat = jnp.pad(v.reshape(-1), (0, n_pad - n)).reshape(8, n_pad // 8)
    m_per, cols = flat.shape

    def body(x_ref, out_ref, send_sems, recv_sems, local_sem):
        x, y, c = _here()
        me, sibling = (x, y, c), (x, y, 1 - c)
        chips = _other_chips(x, y)

        def rows(px, py, pc):
            return out_ref.at[pl.ds((4 * px + 2 * py + pc) * m_per, m_per), :]

        def copy(k, block, to, src=None):
            return pltpu.make_async_remote_copy(
                src_ref=rows(*block) if src is None else src, dst_ref=rows(*block),
                send_sem=send_sems.at[k], recv_sem=recv_sems.at[k], device_id=to, device_id_type=MESH)

        mine = pltpu.make_async_copy(x_ref, rows(*me), local_sem)
        mine.start()
        first = [copy(0, me, sibling, src=x_ref)]
        first += [copy(1 + j, me, (*chip, c), src=x_ref) for j, chip in enumerate(chips)]
        for cp in first:
            cp.start()
        passed = [copy(4 + j, (*chip, c), sibling) for j, chip in enumerate(chips)]
        for j, chip in enumerate(chips):
            copy(1 + j, (*chip, c), me).wait_recv()
            passed[j].start()
        copy(0, sibling, me).wait_recv()
        for j, chip in enumerate(chips):
            copy(4 + j, (*chip, 1 - c), me).wait_recv()
        for cp in first + passed:
            cp.wait_send()
        mine.wait()

    out = pl.pallas_call(
        body, name=name,
        out_shape=SDS((N_DEV * m_per, cols), F32),
        in_specs=[pl.BlockSpec(memory_space=pltpu.VMEM)],
        out_specs=pl.BlockSpec(memory_space=pltpu.VMEM),
        scratch_shapes=[pltpu.SemaphoreType.DMA((7,)), pltpu.SemaphoreType.DMA((7,)), pltpu.SemaphoreType.DMA],
        compiler_params=pltpu.CompilerParams(vmem_limit_bytes=V7X_VMEM_LIMIT_BYTES),
    )(flat)
    return out.reshape(N_DEV, n_pad)[:, :n].reshape((N_DEV,) + shape)


HBM_SPEC = pl.BlockSpec(memory_space=pltpu.HBM)


def _all_gather_big(shards, name):
    nw = len(shards)

    def body(*refs):
        xs = refs[:nw]
        outs = refs[nw:2 * nw]
        send_sems, recv_sems, local_sems = refs[2 * nw:]
        x, y, c = _here()
        me, sibling = (x, y, c), (x, y, 1 - c)
        chips = _other_chips(x, y)

        def blk(w, px, py, pc):
            return outs[w].at[4 * px + 2 * py + pc]

        def copy(w, k, block, to, src=None):
            return pltpu.make_async_remote_copy(
                src_ref=blk(w, *block) if src is None else src, dst_ref=blk(w, *block),
                send_sem=send_sems.at[7 * w + k], recv_sem=recv_sems.at[7 * w + k], device_id=to, device_id_type=MESH)

        mine = [pltpu.make_async_copy(xs[w], blk(w, *me), local_sems.at[w]) for w in range(nw)]
        for cp in mine:
            cp.start()
        first = []
        for w in range(nw):
            first.append(copy(w, 0, me, sibling, src=xs[w]))
            first += [copy(w, 1 + j, me, (*chip, c), src=xs[w]) for j, chip in enumerate(chips)]
        for cp in first:
            cp.start()
        passed = []
        for w in range(nw):
            for j, chip in enumerate(chips):
                copy(w, 1 + j, (*chip, c), me).wait_recv()
                fwd = copy(w, 4 + j, (*chip, c), sibling)
                fwd.start()
                passed.append(fwd)
        for w in range(nw):
            copy(w, 0, sibling, me).wait_recv()
            for j, chip in enumerate(chips):
                copy(w, 4 + j, (*chip, 1 - c), me).wait_recv()
        for cp in first + passed:
            cp.wait_send()
        for cp in mine:
            cp.wait()

    outs = pl.pallas_call(
        body, name=name,
        out_shape=[SDS((N_DEV,) + s.shape, s.dtype) for s in shards],
        in_specs=[HBM_SPEC] * nw, out_specs=[HBM_SPEC] * nw,
        scratch_shapes=[pltpu.SemaphoreType.DMA((7 * nw,)), pltpu.SemaphoreType.DMA((7 * nw,)),
                        pltpu.SemaphoreType.DMA((nw,))],
    )(*shards)
    return list(outs)


def _exchange_with_sibling(grads, name):
    nw = len(grads)

    def body(*refs):
        gs = refs[:nw]
        outs = refs[nw:2 * nw]
        send_sems, recv_sems = refs[2 * nw:]
        x, y, c = _here()
        copies = []
        for w in range(nw):
            for k in range(N_CHIP):
                copies.append(pltpu.make_async_remote_copy(
                    src_ref=gs[w].at[2 * k + (1 - c)], dst_ref=outs[w].at[k],
                    send_sem=send_sems.at[N_CHIP * w + k], recv_sem=recv_sems.at[N_CHIP * w + k],
                    device_id=(x, y, 1 - c), device_id_type=MESH))
        for cp in copies:
            cp.start()
        for cp in copies:
            cp.wait()

    outs = pl.pallas_call(
        body, name=name,
        out_shape=[SDS((N_CHIP,) + g.shape[1:], g.dtype) for g in grads],
        in_specs=[HBM_SPEC] * nw, out_specs=[HBM_SPEC] * nw,
        scratch_shapes=[pltpu.SemaphoreType.DMA((N_CHIP * nw,)), pltpu.SemaphoreType.DMA((N_CHIP * nw,))],
    )(*grads)
    return list(outs)


def _exchange_between_chips(chip_sums, name):
    nw = len(chip_sums)

    def body(*refs):
        srcs = refs[:nw]
        outs = refs[nw:2 * nw]
        send_sems, recv_sems, local_sems = refs[2 * nw:]
        x, y, c = _here()
        my_chip = 2 * x + y
        chips = _other_chips(x, y)
        local = [pltpu.make_async_copy(srcs[w].at[my_chip], outs[w].at[my_chip], local_sems.at[w]) for w in range(nw)]
        for cp in local:
            cp.start()
        copies = []
        for w in range(nw):
            for j, (px, py) in enumerate(chips):
                copies.append(pltpu.make_async_remote_copy(
                    src_ref=srcs[w].at[2 * px + py], dst_ref=outs[w].at[my_chip],
                    send_sem=send_sems.at[3 * w + j], recv_sem=recv_sems.at[3 * w + j],
                    device_id=(px, py, c), device_id_type=MESH))
        for cp in copies:
            cp.start()
        for w in range(nw):
            for j, (px, py) in enumerate(chips):
                landed = pltpu.make_async_remote_copy(
                    src_ref=srcs[w].at[2 * px + py], dst_ref=outs[w].at[2 * px + py],
                    send_sem=send_sems.at[3 * w + j], recv_sem=recv_sems.at[3 * w + j],
                    device_id=(px, py, c), device_id_type=MESH)
                landed.wait_recv()
        for cp in copies:
            cp.wait_send()
        for cp in local:
            cp.wait()

    outs = pl.pallas_call(
        body, name=name,
        out_shape=[SDS(s.shape, s.dtype) for s in chip_sums],
        in_specs=[HBM_SPEC] * nw, out_specs=[HBM_SPEC] * nw,
        scratch_shapes=[pltpu.SemaphoreType.DMA((3 * nw,)), pltpu.SemaphoreType.DMA((3 * nw,)),
                        pltpu.SemaphoreType.DMA((nw,))],
    )(*chip_sums)
    return list(outs)


def _chip_sum(g, recv, c_idx, name):
    _, m, n = g.shape
    tr = _tile(m, max(8, (1 << 20) // max(n, 1) // 8 * 8), 8)

    def body(c_ref, g_ref, r_ref, o_ref):
        o_ref[...] = (g_ref[...].astype(F32) + r_ref[...].astype(F32)).astype(o_ref.dtype)

    grid_spec = pltpu.PrefetchScalarGridSpec(
        num_scalar_prefetch=1, grid=(N_CHIP, m // tr),
        in_specs=[pl.BlockSpec((None, tr, n), lambda k, i, c_ref: (2 * k + c_ref[0], i, 0)),
                  pl.BlockSpec((None, tr, n), lambda k, i, c_ref: (k, i, 0))],
        out_specs=pl.BlockSpec((None, tr, n), lambda k, i, c_ref: (k, i, 0)))
    return pl.pallas_call(
        body, name=name, grid_spec=grid_spec, out_shape=SDS((N_CHIP, m, n), BF16),
        compiler_params=_cp("parallel", "parallel"),
    )(c_idx, g, recv)


def _adam_math(w, g, m, v):
    m = ADAM_B1 * m + (1.0 - ADAM_B1) * g
    v = ADAM_B2 * v + (1.0 - ADAM_B2) * (g * g)
    m_hat = m / (1.0 - ADAM_B1 ** ADAM_STEP)
    v_hat = v / (1.0 - ADAM_B2 ** ADAM_STEP)
    delta = -ADAM_LR * (m_hat / (jnp.sqrt(v_hat) + ADAM_EPS) + ADAM_WD * w)
    return delta, m, v


def _adam_from_parts(w, m, v, parts, name):
    R, C = w.shape
    P = parts.shape[0]
    tr = _tile(R, max(8, (1 << 19) // max(C, 1) // 8 * 8), 8)

    def body(w_ref, m_ref, v_ref, p_ref, g_out, d_out, m_out, v_out):
        g = p_ref[0].astype(F32)
        for k in range(1, P):
            g = g + p_ref[k].astype(F32)
        delta, nm, nv = _adam_math(w_ref[...], g, m_ref[...], v_ref[...])
        g_out[...] = g
        d_out[...] = delta
        m_out[...] = nm
        v_out[...] = nv

    spec = pl.BlockSpec((tr, C), lambda i: (i, 0))
    return pl.pallas_call(
        body, name=name, grid=(R // tr,),
        in_specs=[spec, spec, spec, pl.BlockSpec((P, tr, C), lambda i: (0, i, 0))],
        out_specs=[spec] * 4, out_shape=[SDS((R, C), F32)] * 4,
        compiler_params=_cp("parallel"),
    )(w, m, v, parts)


def _ada_fwd(c_all, w_ada, b_cols, name):
    nb, D = c_all.shape
    n = w_ada.shape[1]
    tn = _tile(n, 512, 128)

    def body(c_ref, w_ref, b_ref, o_ref):
        cv = c_ref[...]
        cond = (cv * _sigmoid(cv)).astype(BF16)
        o_ref[...] = _dot(cond, w_ref[...].astype(BF16), NN) + b_ref[...]

    return pl.pallas_call(
        body, name=name, grid=(n // tn,),
        in_specs=[pl.BlockSpec((nb, D), lambda j: (0, 0)), pl.BlockSpec((D, tn), lambda j: (0, j)),
                  pl.BlockSpec((1, tn), lambda j: (0, j))],
        out_specs=pl.BlockSpec((nb, tn), lambda j: (0, j)), out_shape=SDS((nb, n), F32),
        compiler_params=_cp("parallel"),
    )(c_all, w_ada, b_cols)


def _ada_bwd_adam(c_all, dmod_cols, w, m, v, name):
    nb, D = c_all.shape
    n = w.shape[1]
    tr = _tile(D, 512, 128)
    tn = _tile(n, 1024, 128)

    def body(c_ref, d_ref, w_ref, m_ref, v_ref, g_out, d_out, m_out, v_out):
        cv = c_ref[...]
        cond = (cv * _sigmoid(cv)).astype(BF16)
        g = _dot(cond, d_ref[...].astype(BF16), TN)
        delta, nm, nv = _adam_math(w_ref[...], g, m_ref[...], v_ref[...])
        g_out[...] = g
        d_out[...] = delta
        m_out[...] = nm
        v_out[...] = nv

    spec = pl.BlockSpec((tr, tn), lambda i, j: (i, j))
    return pl.pallas_call(
        body, name=name, grid=(D // tr, n // tn),
        in_specs=[pl.BlockSpec((nb, tr), lambda i, j: (0, i)), pl.BlockSpec((nb, tn), lambda i, j: (0, j)),
                  spec, spec, spec],
        out_specs=[spec] * 4, out_shape=[SDS((D, n), F32)] * 4,
        compiler_params=_cp("parallel", "parallel"),
    )(c_all, dmod_cols, w, m, v)


def _tok(tb, width):
    return pl.BlockSpec((None, tb, width), lambda b, i: (b, i, 0))


def _per_example(width):
    return pl.BlockSpec((None, 1, width), lambda b, i: (b, 0, 0))


def _shared_row(width):
    return pl.BlockSpec((1, width), lambda b, i: (0, 0))


def _norm_mod_fwd(x, g, sh, sc, name, o=None, ga=None):
    B, S, D = x.shape
    tb = _tile(S, ROW_BLOCK, 8)
    fused = o is not None

    def body(*refs):
        if fused:
            x_ref, o_ref, ga_ref, g_ref, sh_ref, sc_ref, x1_ref, h_ref, r_ref = refs
            xv = x_ref[...] + ga_ref[...] * o_ref[...]
            x1_ref[...] = xv
        else:
            x_ref, g_ref, sh_ref, sc_ref, h_ref, r_ref = refs
            xv = x_ref[...]
        rstd = lax.rsqrt(jnp.mean(xv * xv, axis=1, keepdims=True) + EPS)
        y = xv * rstd * g_ref[...]
        h_ref[...] = (y * (1.0 + sc_ref[...]) + sh_ref[...]).astype(BF16)
        r_ref[...] = rstd

    ins = [x] + ([o, ga] if fused else []) + [g, sh, sc]
    in_specs = [_tok(tb, D)] + ([_tok(tb, D), _per_example(D)] if fused else []) + [_shared_row(D), _per_example(D), _per_example(D)]
    out_specs = ([_tok(tb, D)] if fused else []) + [_tok(tb, D), _tok(tb, 1)]
    out_shape = ([SDS((B, S, D), F32)] if fused else []) + [SDS((B, S, D), BF16), SDS((B, S, 1), F32)]
    return pl.pallas_call(
        body, name=name, grid=(B, S // tb), in_specs=in_specs, out_specs=out_specs, out_shape=out_shape,
        compiler_params=_cp("parallel", "parallel"),
    )(*ins)


def _norm_mod_bwd(dh, xin, rstd, sc, g, dres, name, o=None, ga=None):
    B, S, D = xin.shape
    tb = _tile(S, ROW_BLOCK, 8)
    gated = o is not None

    def body(*refs):
        if gated:
            (dh_ref, x_ref, r_ref, sc_ref, g_ref, dres_ref, o_ref, ga_ref,
             dx_ref, dsh_ref, dsc_ref, gg_ref, dga_ref, do_ref) = refs
        else:
            dh_ref, x_ref, r_ref, sc_ref, g_ref, dres_ref, dx_ref, dsh_ref, dsc_ref, gg_ref = refs
        b, i = pl.program_id(0), pl.program_id(1)

        @pl.when(i == 0)
        def _():
            dsh_ref[...] = jnp.zeros_like(dsh_ref)
            dsc_ref[...] = jnp.zeros_like(dsc_ref)
            if gated:
                dga_ref[...] = jnp.zeros_like(dga_ref)

        @pl.when((i == 0) & (b == 0))
        def _():
            gg_ref[...] = jnp.zeros_like(gg_ref)

        dhv = dh_ref[...]
        rs = r_ref[...]
        gv = g_ref[...]
        xhat = x_ref[...] * rs
        dsh_ref[...] += jnp.sum(dhv, axis=0, keepdims=True)
        dsc_ref[...] += jnp.sum(dhv * (xhat * gv), axis=0, keepdims=True)
        dn = dhv * (1.0 + sc_ref[...])
        gg_ref[...] += jnp.sum(dn * xhat, axis=0, keepdims=True)
        dxhat = dn * gv
        cm = jnp.mean(dxhat * xhat, axis=1, keepdims=True)
        dx = dres_ref[...] + rs * (dxhat - xhat * cm)
        dx_ref[...] = dx
        if gated:
            dga_ref[...] += jnp.sum(dx * o_ref[...], axis=0, keepdims=True)
            do_ref[...] = (dx * ga_ref[...]).astype(BF16)

    ins = [dh, xin, rstd, sc, g, dres] + ([o, ga] if gated else [])
    in_specs = [_tok(tb, D), _tok(tb, D), _tok(tb, 1), _per_example(D), _shared_row(D), _tok(tb, D)]
    in_specs += [_tok(tb, D), _per_example(D)] if gated else []
    out_specs = [_tok(tb, D), _per_example(D), _per_example(D), _shared_row(D)]
    out_shape = [SDS((B, S, D), F32), SDS((B, 1, D), F32), SDS((B, 1, D), F32), SDS((1, D), F32)]
    if gated:
        out_specs += [_per_example(D), _tok(tb, D)]
        out_shape += [SDS((B, 1, D), F32), SDS((B, S, D), BF16)]
    return pl.pallas_call(
        body, name=name, grid=(B, S // tb), in_specs=in_specs, out_specs=out_specs, out_shape=out_shape,
        compiler_params=_cp("arbitrary", "arbitrary"),
    )(*ins)


def _loss_head(x1, ff, target, ga2, name):
    B, S, D = x1.shape
    tb = _tile(S, ROW_BLOCK, 8)
    nb, ni = B, S // tb

    def body(x_ref, f_ref, t_ref, ga_ref, loss_ref, dy_ref, dff_ref, dga_ref, acc_ref):
        b, i = pl.program_id(0), pl.program_id(1)

        @pl.when(i == 0)
        def _():
            dga_ref[...] = jnp.zeros_like(dga_ref)

        @pl.when((i == 0) & (b == 0))
        def _():
            acc_ref[...] = jnp.zeros_like(acc_ref)

        fv = f_ref[...]
        gav = ga_ref[...]
        err = x_ref[...] + gav * fv - t_ref[...]
        acc_ref[...] += jnp.sum(err * err, axis=0, keepdims=True)
        dy = err * (1.0 / D)
        dy_ref[...] = dy
        dff_ref[...] = (dy * gav).astype(BF16)
        dga_ref[...] += jnp.sum(dy * fv, axis=0, keepdims=True)

        @pl.when((i == ni - 1) & (b == nb - 1))
        def _():
            loss_ref[...] = jnp.sum(acc_ref[...], axis=1, keepdims=True) * (0.5 / D)

    return pl.pallas_call(
        body, name=name, grid=(B, S // tb),
        in_specs=[_tok(tb, D), _tok(tb, D), _tok(tb, D), _per_example(D)],
        out_specs=[pl.BlockSpec((1, 1), lambda b, i: (0, 0)), _tok(tb, D), _tok(tb, D), _per_example(D)],
        out_shape=[SDS((1, 1), F32), SDS((B, S, D), F32), SDS((B, S, D), BF16), SDS((B, 1, D), F32)],
        scratch_shapes=[pltpu.VMEM((1, D), F32)],
        compiler_params=_cp("arbitrary", "arbitrary"),
    )(x1, ff, target, ga2)


def _causal_mask():
    t = lax.broadcasted_iota(jnp.int32, (CHUNK, CHUNK), 0)
    s = lax.broadcasted_iota(jnp.int32, (CHUNK, CHUNK), 1)
    return s <= t


def _gmlp_fwd(proj, g_v, w_s, b_col, D, GW, name):
    T = proj.shape[0]
    tb = _tile(T, ROW_BLOCK, CHUNK)
    gw = GW // GROUPS
    ucol = (2 * D) // GW

    def body(u_ref, v_ref, g_ref, w_ref, b_ref, ga_ref, r_ref):
        zu = _gelu(u_ref[...])
        zv = _gelu(v_ref[...])
        rstd = lax.rsqrt(jnp.mean(zv * zv, axis=1, keepdims=True) + EPS)
        vn = (zv * rstd * g_ref[...]).astype(BF16)
        mask = _causal_mask()
        for g in range(GROUPS):
            wm = jnp.where(mask, w_ref[g], 0.0).astype(BF16)
            cols = slice(g * gw, (g + 1) * gw)
            for ci in range(tb // CHUNK):
                rows = slice(ci * CHUNK, (ci + 1) * CHUNK)
                mixed = _dot(wm, vn[rows, cols], NN) + b_ref[g]
                ga_ref[rows, cols] = (zu[rows, cols] * mixed).astype(BF16)
        r_ref[...] = rstd

    return pl.pallas_call(
        body, name=name, grid=(T // tb,),
        in_specs=[pl.BlockSpec((tb, GW), lambda i: (i, ucol)), pl.BlockSpec((tb, GW), lambda i: (i, ucol + 1)),
                  pl.BlockSpec((1, GW), lambda i: (0, 0)), pl.BlockSpec((GROUPS, CHUNK, CHUNK), lambda i: (0, 0, 0)),
                  pl.BlockSpec((GROUPS, CHUNK, 1), lambda i: (0, 0, 0))],
        out_specs=[pl.BlockSpec((tb, GW), lambda i: (i, 0)), pl.BlockSpec((tb, 1), lambda i: (i, 0))],
        out_shape=[SDS((T, GW), BF16), SDS((T, 1), F32)],
        compiler_params=_cp("parallel"),
    )(proj, proj, g_v, w_s, b_col)


def _gmlp_bwd(proj, dga, rstd_v, g_v, w_s, b_col, sel, D, GW, name):
    T = proj.shape[0]
    tb = _tile(T, ROW_BLOCK, CHUNK)
    gw = GW // GROUPS
    ucol = (2 * D) // GW
    nsteps = T // tb

    def body(u_ref, v_ref, dga_ref, r_ref, g_ref, w_ref, b_ref, sel_ref,
             duv_ref, gws_ref, gbs_ref, gg_ref, accb_ref, dvn_ref):
        step = pl.program_id(0)

        @pl.when(step == 0)
        def _():
            gws_ref[...] = jnp.zeros_like(gws_ref)
            gg_ref[...] = jnp.zeros_like(gg_ref)
            accb_ref[...] = jnp.zeros_like(accb_ref)

        uv = u_ref[...]
        vv = v_ref[...]
        zu = _gelu(uv)
        zv = _gelu(vv)
        rs = r_ref[...]
        gv = g_ref[...]
        vhat = zv * rs
        vnb = (vhat * gv).astype(BF16)
        dgav = dga_ref[...]
        du_gelu = _gelu_grad(uv)
        mask = _causal_mask()
        for g in range(GROUPS):
            wm = jnp.where(mask, w_ref[g], 0.0)
            wmb = wm.astype(BF16)
            wmtb = wm.T.astype(BF16)
            cols = slice(g * gw, (g + 1) * gw)
            for ci in range(tb // CHUNK):
                rows = slice(ci * CHUNK, (ci + 1) * CHUNK)
                vn_blk = vnb[rows, cols]
                mixed = _dot(wmb, vn_blk, NN) + b_ref[g]
                duv_ref[rows, cols] = (dgav[rows, cols] * mixed * du_gelu[rows, cols]).astype(BF16)
                dmix = dgav[rows, cols] * zu[rows, cols]
                accb_ref[:, cols] += dmix
                dmb = dmix.astype(BF16)
                gws_ref[g] += _dot(dmb, vn_blk, NT)
                dvn_ref[rows, cols] = _dot(wmtb, dmb, NN)
        dvn = dvn_ref[...]
        gg_ref[...] += jnp.sum(dvn * vhat, axis=0, keepdims=True)
        dvhat = dvn * gv
        cm = jnp.mean(dvhat * vhat, axis=1, keepdims=True)
        dzv = rs * (dvhat - vhat * cm)
        duv_ref[:, GW:] = (dzv * _gelu_grad(vv)).astype(BF16)

        @pl.when(step == nsteps - 1)
        def _():
            gbs_ref[...] = jnp.dot(accb_ref[...], sel_ref[...], precision=lax.Precision.HIGHEST,
                                   preferred_element_type=F32)
            for g in range(GROUPS):
                gws_ref[g] = jnp.where(mask, gws_ref[g], 0.0)

    return pl.pallas_call(
        body, name=name, grid=(nsteps,),
        in_specs=[pl.BlockSpec((tb, GW), lambda i: (i, ucol)), pl.BlockSpec((tb, GW), lambda i: (i, ucol + 1)),
                  pl.BlockSpec((tb, GW), lambda i: (i, 0)), pl.BlockSpec((tb, 1), lambda i: (i, 0)),
                  pl.BlockSpec((1, GW), lambda i: (0, 0)), pl.BlockSpec((GROUPS, CHUNK, CHUNK), lambda i: (0, 0, 0)),
                  pl.BlockSpec((GROUPS, CHUNK, 1), lambda i: (0, 0, 0)), pl.BlockSpec((GW, LANES), lambda i: (0, 0))],
        out_specs=[pl.BlockSpec((tb, 2 * GW), lambda i: (i, 0)), pl.BlockSpec((GROUPS, CHUNK, CHUNK), lambda i: (0, 0, 0)),
                   pl.BlockSpec((CHUNK, LANES), lambda i: (0, 0)), pl.BlockSpec((1, GW), lambda i: (0, 0))],
        out_shape=[SDS((T, 2 * GW), BF16), SDS((GROUPS, CHUNK, CHUNK), F32), SDS((CHUNK, LANES), F32), SDS((1, GW), F32)],
        scratch_shapes=[pltpu.VMEM((CHUNK, GW), F32), pltpu.VMEM((tb, GW), F32)],
        compiler_params=_cp("arbitrary"),
    )(proj, proj, dga, rstd_v, g_v, w_s, b_col, sel)


def _mix_fwd(proj, y_a, y_b, D, name):
    T = proj.shape[0]
    tb = _tile(T, 512, 8)
    td = _tile(D, 1024, 128)
    nd = D // td

    def body(ga_ref, gb_ref, ya_ref, yb_ref, o_ref):
        o_ref[...] = (_sigmoid(ga_ref[...]) * ya_ref[...] + _sigmoid(gb_ref[...]) * yb_ref[...]).astype(BF16)

    blk = pl.BlockSpec((tb, td), lambda i, j: (i, j))
    return pl.pallas_call(
        body, name=name, grid=(T // tb, nd),
        in_specs=[blk, pl.BlockSpec((tb, td), lambda i, j: (i, j + nd)), blk, blk],
        out_specs=blk, out_shape=SDS((T, D), BF16),
        compiler_params=_cp("parallel", "parallel"),
    )(proj, proj, y_a, y_b)


def _mix_bwd(proj, y_a, y_b, dmixed, D, name):
    T = proj.shape[0]
    tb = _tile(T, 512, 8)
    td = _tile(D, 1024, 128)
    nd = D // td

    def body(ga_ref, gb_ref, ya_ref, yb_ref, dm_ref, dya_ref, dyb_ref, dga_ref, dgb_ref):
        dm = dm_ref[...]
        sa = _sigmoid(ga_ref[...])
        sb = _sigmoid(gb_ref[...])
        dya_ref[...] = (dm * sa).astype(BF16)
        dyb_ref[...] = (dm * sb).astype(BF16)
        dga_ref[...] = (dm * ya_ref[...] * sa * (1.0 - sa)).astype(BF16)
        dgb_ref[...] = (dm * yb_ref[...] * sb * (1.0 - sb)).astype(BF16)

    blk = pl.BlockSpec((tb, td), lambda i, j: (i, j))
    return pl.pallas_call(
        body, name=name, grid=(T // tb, nd),
        in_specs=[blk, pl.BlockSpec((tb, td), lambda i, j: (i, j + nd)), blk, blk, blk],
        out_specs=[blk] * 4, out_shape=[SDS((T, D), BF16)] * 4,
        compiler_params=_cp("parallel", "parallel"),
    )(proj, proj, y_a, y_b, dmixed)


def _lat_norm_fwd(proj, g_q, g_kv, D, GW, QL, KVL, name):
    T = proj.shape[0]
    tb = _tile(T, 512, 8)
    qcol = (2 * D + 2 * GW) // QL
    kvcol = (2 * D + 2 * GW + QL) // KVL

    def body(q_ref, kv_ref, gq_ref, gkv_ref, qn_ref, kvn_ref, rq_ref, rkv_ref):
        qv = q_ref[...]
        rq = lax.rsqrt(jnp.mean(qv * qv, axis=1, keepdims=True) + EPS)
        qn_ref[...] = (qv * rq * gq_ref[...]).astype(BF16)
        rq_ref[...] = rq
        kv = kv_ref[...]
        rkv = lax.rsqrt(jnp.mean(kv * kv, axis=1, keepdims=True) + EPS)
        kvn_ref[...] = (kv * rkv * gkv_ref[...]).astype(BF16)
        rkv_ref[...] = rkv

    return pl.pallas_call(
        body, name=name, grid=(T // tb,),
        in_specs=[pl.BlockSpec((tb, QL), lambda i: (i, qcol)), pl.BlockSpec((tb, KVL), lambda i: (i, kvcol)),
                  pl.BlockSpec((1, QL), lambda i: (0, 0)), pl.BlockSpec((1, KVL), lambda i: (0, 0))],
        out_specs=[pl.BlockSpec((tb, QL), lambda i: (i, 0)), pl.BlockSpec((tb, KVL), lambda i: (i, 0)),
                   pl.BlockSpec((tb, 1), lambda i: (i, 0)), pl.BlockSpec((tb, 1), lambda i: (i, 0))],
        out_shape=[SDS((T, QL), BF16), SDS((T, KVL), BF16), SDS((T, 1), F32), SDS((T, 1), F32)],
        compiler_params=_cp("parallel"),
    )(proj, proj, g_q, g_kv)


def _lat_norm_bwd(proj, dqn, dkvn, dkpe, rq, rkv, g_q, g_kv, D, GW, QL, KVL, name):
    T = proj.shape[0]
    tb = _tile(T, 512, 8)
    qcol = (2 * D + 2 * GW) // QL
    kvcol = (2 * D + 2 * GW + QL) // KVL

    def one(xv, rs, gv, dy, gg_ref):
        xhat = xv * rs
        gg_ref[...] += jnp.sum(dy * xhat, axis=0, keepdims=True)
        dxhat = dy * gv
        cm = jnp.mean(dxhat * xhat, axis=1, keepdims=True)
        return rs * (dxhat - xhat * cm)

    def body(q_ref, kv_ref, dqn_ref, dkvn_ref, dkpe_ref, rq_ref, rkv_ref, gq_ref, gkv_ref,
             dq_ref, dkv_ref, dkpe_out, ggq_ref, ggkv_ref):
        @pl.when(pl.program_id(0) == 0)
        def _():
            ggq_ref[...] = jnp.zeros_like(ggq_ref)
            ggkv_ref[...] = jnp.zeros_like(ggkv_ref)

        dq_ref[...] = one(q_ref[...], rq_ref[...], gq_ref[...], dqn_ref[...], ggq_ref).astype(BF16)
        dkv_ref[...] = one(kv_ref[...], rkv_ref[...], gkv_ref[...], dkvn_ref[...], ggkv_ref).astype(BF16)
        dkpe_out[...] = dkpe_ref[...].astype(BF16)

    return pl.pallas_call(
        body, name=name, grid=(T // tb,),
        in_specs=[pl.BlockSpec((tb, QL), lambda i: (i, qcol)), pl.BlockSpec((tb, KVL), lambda i: (i, kvcol)),
                  pl.BlockSpec((tb, QL), lambda i: (i, 0)), pl.BlockSpec((tb, KVL), lambda i: (i, 0)),
                  pl.BlockSpec((tb, LANES), lambda i: (i, 0)),
                  pl.BlockSpec((tb, 1), lambda i: (i, 0)), pl.BlockSpec((tb, 1), lambda i: (i, 0)),
                  pl.BlockSpec((1, QL), lambda i: (0, 0)), pl.BlockSpec((1, KVL), lambda i: (0, 0))],
        out_specs=[pl.BlockSpec((tb, QL), lambda i: (i, 0)), pl.BlockSpec((tb, KVL), lambda i: (i, 0)),
                   pl.BlockSpec((tb, LANES), lambda i: (i, 0)),
                   pl.BlockSpec((1, QL), lambda i: (0, 0)), pl.BlockSpec((1, KVL), lambda i: (0, 0))],
        out_shape=[SDS((T, QL), BF16), SDS((T, KVL), BF16), SDS((T, LANES), BF16), SDS((1, QL), F32), SDS((1, KVL), F32)],
        compiler_params=_cp("arbitrary"),
    )(proj, proj, dqn, dkvn, dkpe, rq, rkv, g_q, g_kv)


def _swap_halves(r):
    lane = lax.broadcasted_iota(jnp.int32, r.shape, 1)
    lo = pltpu.roll(r, LANES - HALF_ROPE, 1)
    hi = pltpu.roll(r, HALF_ROPE, 1)
    return jnp.where(lane < HALF_ROPE, lo, jnp.where(lane < ROPE, hi, 0.0))


def _rope_fwd(r, cos_t, sin_t):
    return r * cos_t + _swap_halves(r) * sin_t


def _rope_bwd(d, cos_t, sin_t):
    return d * cos_t + _swap_halves(d * sin_t)


def _head_norm_bwd(xn, xr, rs, g_n, g_r, dyn, dyr):
    xhn, xhr = xn * rs, xr * rs
    dn, dr = dyn * g_n, dyr * g_r
    cm = (jnp.sum(dn * xhn, axis=1, keepdims=True) + jnp.sum(dr * xhr, axis=1, keepdims=True)) * (1.0 / QK_HEAD)
    return rs * (dn - xhn * cm), rs * (dr - xhr * cm), dyn * xhn, dyr * xhr


def _q_up_fwd(qn, w_uq_al, g_al, cos_t, sin_t, name):
    T, QL = qn.shape
    HP = w_uq_al.shape[1]
    tb = _tile(T, 512, 8)
    hw = _tile(HP, 1024, HEAD_PAD)

    def body(x_ref, w_ref, g_ref, c_ref, s_ref, o_ref):
        raw = _dot(x_ref[...], w_ref[...], NN)
        gv = g_ref[...]
        cv, sv = c_ref[...], s_ref[...]
        for h in range(hw // HEAD_PAD):
            xn = raw[:, h * HEAD_PAD:h * HEAD_PAD + LANES]
            xr = raw[:, h * HEAD_PAD + LANES:(h + 1) * HEAD_PAD]
            ss = jnp.sum(xn * xn, axis=1, keepdims=True) + jnp.sum(xr * xr, axis=1, keepdims=True)
            rs = lax.rsqrt(ss * (1.0 / QK_HEAD) + EPS)
            o_ref[:, h * HEAD_PAD:h * HEAD_PAD + LANES] = (xn * rs * gv[:, :LANES]).astype(BF16)
            o_ref[:, h * HEAD_PAD + LANES:(h + 1) * HEAD_PAD] = _rope_fwd(xr * rs * gv[:, LANES:], cv, sv).astype(BF16)

    return pl.pallas_call(
        body, name=name, grid=(T // tb, HP // hw),
        in_specs=[pl.BlockSpec((tb, QL), lambda i, j: (i, 0)), pl.BlockSpec((QL, hw), lambda i, j: (0, j)),
                  pl.BlockSpec((1, HEAD_PAD), lambda i, j: (0, 0)),
                  pl.BlockSpec((tb, LANES), lambda i, j: (i, 0)), pl.BlockSpec((tb, LANES), lambda i, j: (i, 0))],
        out_specs=pl.BlockSpec((tb, hw), lambda i, j: (i, j)), out_shape=SDS((T, HP), BF16),
        compiler_params=_cp("parallel", "parallel"),
    )(qn, w_uq_al, g_al, cos_t, sin_t)


def _q_up_bwd(qn, w_uq_al, g_al, cos_t, sin_t, dq, name):
    T, QL = qn.shape
    HP = w_uq_al.shape[1]
    tb = _tile(T, 512, 8)
    hw = _tile(HP, 1024, HEAD_PAD)

    def body(x_ref, w_ref, g_ref, c_ref, s_ref, dq_ref, o_ref, gg_ref):
        @pl.when((pl.program_id(0) == 0) & (pl.program_id(1) == 0))
        def _():
            gg_ref[...] = jnp.zeros_like(gg_ref)

        raw = _dot(x_ref[...], w_ref[...], NN)
        gv = g_ref[...]
        cv, sv = c_ref[...], s_ref[...]
        for h in range(hw // HEAD_PAD):
            lo, mid, hi = h * HEAD_PAD, h * HEAD_PAD + LANES, (h + 1) * HEAD_PAD
            xn, xr = raw[:, lo:mid], raw[:, mid:hi]
            ss = jnp.sum(xn * xn, axis=1, keepdims=True) + jnp.sum(xr * xr, axis=1, keepdims=True)
            rs = lax.rsqrt(ss * (1.0 / QK_HEAD) + EPS)
            dyn = dq_ref[:, lo:mid]
            dyr = _rope_bwd(dq_ref[:, mid:hi], cv, sv)
            dxn, dxr, ggn, ggr = _head_norm_bwd(xn, xr, rs, gv[:, :LANES], gv[:, LANES:], dyn, dyr)
            o_ref[:, lo:mid] = dxn.astype(BF16)
            o_ref[:, mid:hi] = dxr.astype(BF16)
            gg_ref[:, :LANES] += jnp.sum(ggn, axis=0, keepdims=True)
            gg_ref[:, LANES:] += jnp.sum(ggr, axis=0, keepdims=True)

    return pl.pallas_call(
        body, name=name, grid=(T // tb, HP // hw),
        in_specs=[pl.BlockSpec((tb, QL), lambda i, j: (i, 0)), pl.BlockSpec((QL, hw), lambda i, j: (0, j)),
                  pl.BlockSpec((1, HEAD_PAD), lambda i, j: (0, 0)),
                  pl.BlockSpec((tb, LANES), lambda i, j: (i, 0)), pl.BlockSpec((tb, LANES), lambda i, j: (i, 0)),
                  pl.BlockSpec((tb, hw), lambda i, j: (i, j))],
        out_specs=[pl.BlockSpec((tb, hw), lambda i, j: (i, j)), pl.BlockSpec((1, HEAD_PAD), lambda i, j: (0, 0))],
        out_shape=[SDS((T, HP), BF16), SDS((1, HEAD_PAD), F32)],
        compiler_params=_cp("arbitrary", "arbitrary"),
    )(qn, w_uq_al, g_al, cos_t, sin_t, dq)


def _kv_up_fwd(kvn, w_ukv, proj, g_al, cos_t, sin_t, kpe_col, name):
    T, KVL = kvn.shape
    HP = w_ukv.shape[1]
    tb = _tile(T, 512, 8)
    hw = _tile(HP, 1024, HEAD_PAD)
    nh = hw // HEAD_PAD

    def body(x_ref, w_ref, kpe_ref, g_ref, c_ref, s_ref, k_ref, v_ref):
        raw = _dot(x_ref[...], w_ref[...], NN)
        gv = g_ref[...]
        cv, sv = c_ref[...], s_ref[...]
        kpe = kpe_ref[...]
        kpe_ss = jnp.sum(kpe * kpe, axis=1, keepdims=True)
        for h in range(nh):
            lo, mid, hi = h * HEAD_PAD, h * HEAD_PAD + LANES, (h + 1) * HEAD_PAD
            xn = raw[:, lo:mid]
            rs = lax.rsqrt((jnp.sum(xn * xn, axis=1, keepdims=True) + kpe_ss) * (1.0 / QK_HEAD) + EPS)
            k_ref[:, lo:mid] = (xn * rs * gv[:, :LANES]).astype(BF16)
            k_ref[:, mid:hi] = _rope_fwd(kpe * rs * gv[:, LANES:], cv, sv).astype(BF16)
            v_ref[:, h * V_HEAD:(h + 1) * V_HEAD] = raw[:, mid:hi].astype(BF16)

    return pl.pallas_call(
        body, name=name, grid=(T // tb, HP // hw),
        in_specs=[pl.BlockSpec((tb, KVL), lambda i, j: (i, 0)), pl.BlockSpec((KVL, hw), lambda i, j: (0, j)),
                  pl.BlockSpec((tb, LANES), lambda i, j: (i, kpe_col)),
                  pl.BlockSpec((1, HEAD_PAD), lambda i, j: (0, 0)),
                  pl.BlockSpec((tb, LANES), lambda i, j: (i, 0)), pl.BlockSpec((tb, LANES), lambda i, j: (i, 0))],
        out_specs=[pl.BlockSpec((tb, hw), lambda i, j: (i, j)), pl.BlockSpec((tb, nh * V_HEAD), lambda i, j: (i, j))],
        out_shape=[SDS((T, HP), BF16), SDS((T, HP // 2), BF16)],
        compiler_params=_cp("parallel", "parallel"),
    )(kvn, w_ukv, proj, g_al, cos_t, sin_t)


def _kv_up_bwd(kvn, w_ukv, proj, g_al, cos_t, sin_t, dk, dv, kpe_col, name):
    T, KVL = kvn.shape
    HP = w_ukv.shape[1]
    tb = _tile(T, 512, 8)
    hw = _tile(HP, 1024, HEAD_PAD)
    nh = hw // HEAD_PAD

    def body(x_ref, w_ref, kpe_ref, g_ref, c_ref, s_ref, dk_ref, dv_ref, o_ref, dkpe_ref, gg_ref):
        i, j = pl.program_id(0), pl.program_id(1)

        @pl.when((i == 0) & (j == 0))
        def _():
            gg_ref[...] = jnp.zeros_like(gg_ref)

        @pl.when(j == 0)
        def _():
            dkpe_ref[...] = jnp.zeros_like(dkpe_ref)

        raw = _dot(x_ref[...], w_ref[...], NN)
        gv = g_ref[...]
        cv, sv = c_ref[...], s_ref[...]
        kpe = kpe_ref[...]
        kpe_ss = jnp.sum(kpe * kpe, axis=1, keepdims=True)
        for h in range(nh):
            lo, mid, hi = h * HEAD_PAD, h * HEAD_PAD + LANES, (h + 1) * HEAD_PAD
            xn = raw[:, lo:mid]
            rs = lax.rsqrt((jnp.sum(xn * xn, axis=1, keepdims=True) + kpe_ss) * (1.0 / QK_HEAD) + EPS)
            dyn = dk_ref[:, lo:mid]
            dyr = _rope_bwd(dk_ref[:, mid:hi], cv, sv)
            dxn, dxr, ggn, ggr = _head_norm_bwd(xn, kpe, rs, gv[:, :LANES], gv[:, LANES:], dyn, dyr)
            o_ref[:, lo:mid] = dxn.astype(BF16)
            o_ref[:, mid:hi] = dv_ref[:, h * V_HEAD:(h + 1) * V_HEAD].astype(BF16)
            dkpe_ref[...] += dxr
            gg_ref[:, :LANES] += jnp.sum(ggn, axis=0, keepdims=True)
            gg_ref[:, LANES:] += jnp.sum(ggr, axis=0, keepdims=True)

    return pl.pallas_call(
        body, name=name, grid=(T // tb, HP // hw),
        in_specs=[pl.BlockSpec((tb, KVL), lambda i, j: (i, 0)), pl.BlockSpec((KVL, hw), lambda i, j: (0, j)),
                  pl.BlockSpec((tb, LANES), lambda i, j: (i, kpe_col)),
                  pl.BlockSpec((1, HEAD_PAD), lambda i, j: (0, 0)),
                  pl.BlockSpec((tb, LANES), lambda i, j: (i, 0)), pl.BlockSpec((tb, LANES), lambda i, j: (i, 0)),
                  pl.BlockSpec((tb, hw), lambda i, j: (i, j)), pl.BlockSpec((tb, nh * V_HEAD), lambda i, j: (i, j))],
        out_specs=[pl.BlockSpec((tb, hw), lambda i, j: (i, j)), pl.BlockSpec((tb, LANES), lambda i, j: (i, 0)),
                   pl.BlockSpec((1, HEAD_PAD), lambda i, j: (0, 0))],
        out_shape=[SDS((T, HP), BF16), SDS((T, LANES), F32), SDS((1, HEAD_PAD), F32)],
        compiler_params=_cp("arbitrary", "arbitrary"),
    )(kvn, w_ukv, proj, g_al, cos_t, sin_t, dk, dv)


def _attn_fwd(q, k, v, H, name):
    B, S, _ = q.shape
    tq = _tile(S, ATTN_BLOCK, 128)
    scale = QK_HEAD ** -0.5

    def body(q_ref, k_ref, v_ref, o_ref, l_ref):
        qi = pl.program_id(2)
        qv = q_ref[...]
        row = qi * tq + lax.broadcasted_iota(jnp.int32, (tq, tq), 0)
        col0 = lax.broadcasted_iota(jnp.int32, (tq, tq), 1)

        def step(j, carry):
            m, l, acc = carry
            off = pl.multiple_of(j * tq, tq)
            kj = k_ref[pl.ds(off, tq), :]
            vj = v_ref[pl.ds(off, tq), :]
            s = _dot(qv, kj, NT) * scale
            s = jnp.where(col0 + j * tq <= row, s, -jnp.inf)
            m_new = jnp.maximum(m, jnp.max(s, axis=1, keepdims=True))
            alpha = jnp.exp(m - m_new)
            p = jnp.exp(s - m_new)
            l = alpha * l + jnp.sum(p, axis=1, keepdims=True)
            acc = alpha * acc + _dot(p.astype(BF16), vj, NN)
            return m_new, l, acc

        init = (jnp.full((tq, 1), -1e30, F32), jnp.zeros((tq, 1), F32), jnp.zeros((tq, V_HEAD), F32))
        m, l, acc = lax.fori_loop(0, qi + 1, step, init)
        o_ref[...] = (acc / l).astype(BF16)
        l_ref[...] = m + jnp.log(l)

    return pl.pallas_call(
        body, name=name, grid=(B, H, S // tq),
        in_specs=[pl.BlockSpec((None, tq, HEAD_PAD), lambda b, h, i: (b, i, h)),
                  pl.BlockSpec((None, S, HEAD_PAD), lambda b, h, i: (b, 0, h)),
                  pl.BlockSpec((None, S, V_HEAD), lambda b, h, i: (b, 0, h))],
        out_specs=[pl.BlockSpec((None, tq, V_HEAD), lambda b, h, i: (b, i, h)),
                   pl.BlockSpec((None, None, tq, 1), lambda b, h, i: (b, h, i, 0))],
        out_shape=[SDS((B, S, H * V_HEAD), BF16), SDS((B, H, S, 1), F32)],
        compiler_params=_cp("parallel", "parallel", "parallel"),
    )(q, k, v)


def _attn_bwd(q, k, v, o, lse, do, H, name):
    B, S, _ = q.shape
    tq = _tile(S, ATTN_BLOCK, 128)
    nq = S // tq
    scale = QK_HEAD ** -0.5

    def body(q_ref, k_ref, v_ref, o_ref, l_ref, do_ref, dq_ref, dk_ref, dv_ref, delta_ref):
        dq_ref[...] = jnp.zeros_like(dq_ref)
        for i in range(nq):
            rows = slice(i * tq, (i + 1) * tq)
            delta_ref[rows, :] = jnp.sum(do_ref[rows, :].astype(F32) * o_ref[rows, :].astype(F32), axis=1, keepdims=True)
        row0 = lax.broadcasted_iota(jnp.int32, (tq, tq), 0)
        col0 = lax.broadcasted_iota(jnp.int32, (tq, tq), 1)

        def kv_step(j, carry):
            offk = pl.multiple_of(j * tq, tq)
            kj = k_ref[pl.ds(offk, tq), :]
            vj = v_ref[pl.ds(offk, tq), :]

            def q_step(i, acc):
                dk_acc, dv_acc = acc
                offq = pl.multiple_of(i * tq, tq)
                qi = q_ref[pl.ds(offq, tq), :]
                doi = do_ref[pl.ds(offq, tq), :]
                s = _dot(qi, kj, NT) * scale
                p = jnp.where(col0 + j * tq <= row0 + i * tq, jnp.exp(s - l_ref[pl.ds(offq, tq), :]), 0.0)
                dv_acc = dv_acc + _dot(p.astype(BF16), doi, TN)
                dp = _dot(doi, vj, NT)
                ds = (p * (dp - delta_ref[pl.ds(offq, tq), :]) * scale).astype(BF16)
                dk_acc = dk_acc + _dot(ds, qi, TN)
                dq_ref[pl.ds(offq, tq), :] += _dot(ds, kj, NN)
                return dk_acc, dv_acc

            dk_acc, dv_acc = lax.fori_loop(
                j, nq, q_step, (jnp.zeros((tq, HEAD_PAD), F32), jnp.zeros((tq, V_HEAD), F32)))
            dk_ref[pl.ds(offk, tq), :] = dk_acc
            dv_ref[pl.ds(offk, tq), :] = dv_acc
            return carry

        lax.fori_loop(0, nq, kv_step, 0)

    qk_spec = pl.BlockSpec((None, S, HEAD_PAD), lambda b, h: (b, 0, h))
    v_spec = pl.BlockSpec((None, S, V_HEAD), lambda b, h: (b, 0, h))
    return pl.pallas_call(
        body, name=name, grid=(B, H),
        in_specs=[qk_spec, qk_spec, v_spec, v_spec, pl.BlockSpec((None, None, S, 1), lambda b, h: (b, h, 0, 0)), v_spec],
        out_specs=[qk_spec, qk_spec, v_spec],
        out_shape=[SDS((B, S, H * HEAD_PAD), F32), SDS((B, S, H * HEAD_PAD), F32), SDS((B, S, H * V_HEAD), F32)],
        scratch_shapes=[pltpu.VMEM((S, 1), F32)],
        compiler_params=_cp("parallel", "parallel"),
    )(q, k, v, o, lse, do)


def _natural(sm):
    nd, R, n = sm.shape
    return jnp.transpose(sm, (1, 0, 2)).reshape(R, nd * n)


def _col_shards(full):
    R, N = full.shape
    return jnp.transpose(full.reshape(R, N_DEV, N // N_DEV), (1, 0, 2))


def _pad_heads(w, H):
    R = w.shape[0]
    return jnp.pad(w.reshape(R, H, QK_HEAD), ((0, 0), (0, 0), (0, HEAD_PAD - QK_HEAD))).reshape(R, H * HEAD_PAD)


def _unpad_heads(w, H):
    R = w.shape[0]
    return w.reshape(R, H, HEAD_PAD)[:, :, :QK_HEAD].reshape(R, H * QK_HEAD)


def _pad_gain(g):
    return jnp.pad(g, ((0, 0), (0, HEAD_PAD - QK_HEAD)))


def _rope_tables(positions):
    inv_freq = 1.0 / (ROPE_THETA ** (jnp.arange(0, ROPE, 2, dtype=F32) / ROPE))
    ang = positions.astype(F32).reshape(-1, 1) * inv_freq
    cos, sin = jnp.cos(ang), jnp.sin(ang)
    zeros = jnp.zeros((ang.shape[0], LANES - ROPE), F32)
    return jnp.concatenate([cos, cos, zeros], axis=1), jnp.concatenate([-sin, sin, zeros], axis=1)


def kernel(x, c, positions, w_ada, b_ada, g_norm1, w_in, g_v, w_s, b_s, g_q_lat, g_kv_lat, w_uq, w_ukv, g_qn, g_kn, w_branch_a, w_branch_b, w_out, g_norm2, w_ff1, w_ff2, loss_target, m_w_ada, m_b_ada, m_g_norm1, m_w_in, m_g_v, m_w_s, m_b_s, m_g_q_lat, m_g_kv_lat, m_w_uq, m_w_ukv, m_g_qn, m_g_kn, m_w_branch_a, m_w_branch_b, m_w_out, m_g_norm2, m_w_ff1, m_w_ff2, v_w_ada, v_b_ada, v_g_norm1, v_w_in, v_g_v, v_w_s, v_b_s, v_g_q_lat, v_g_kv_lat, v_w_uq, v_w_ukv, v_g_qn, v_g_kn, v_w_branch_a, v_w_branch_b, v_w_out, v_g_norm2, v_w_ff1, v_w_ff2):
    B, S, D = x.shape
    T = B * S
    GW = g_v.shape[-1]
    QL = g_q_lat.shape[-1]
    KVL = g_kv_lat.shape[-1]
    H = w_uq.shape[-1] * N_DEV // QK_HEAD
    IN = w_in.shape[-1] * N_DEV
    OFF_GATE = IN - 2 * D
    IN_AL = _round_up(2 * D + OFF_GATE + (LANES - ROPE), 512)
    assert OFF_GATE == 2 * GW + QL + KVL + ROPE
    assert (2 * D) % GW == 0 and (2 * D + 2 * GW) % QL == 0 and (2 * D + 2 * GW + QL) % KVL == 0
    kpe_col = (2 * D + 2 * GW + QL + KVL) // LANES

    xi, yi, ci = _here()
    dev = 4 * xi + 2 * yi + ci
    c_idx = jnp.reshape(ci, (1,)).astype(jnp.int32)

    big = [w_in, w_uq, w_ukv, w_branch_a, w_branch_b, w_out, w_ff1, w_ff2]
    g_in, g_uq, g_ukv, g_ba, g_bb, g_out, g_ff1, g_ff2 = _all_gather_big([w[0].astype(BF16) for w in big], "ag_weights")
    w_in_nat = _natural(g_in)
    w_al = jnp.concatenate([w_in_nat[:, OFF_GATE:], w_in_nat[:, :OFF_GATE], jnp.zeros((D, IN_AL - IN), BF16)], axis=1)
    w_uq_al = _pad_heads(_natural(g_uq), H)
    w_ukv_f = _natural(g_ukv)
    w_ba_f = _natural(g_ba)
    w_bb_f = g_bb.reshape(-1, D)
    w_out_f = g_out.reshape(-1, D)
    w_ff1_f = _natural(g_ff1)
    w_ff2_f = g_ff2.reshape(-1, D)
    DFF = w_ff1_f.shape[1]

    n_ada = w_ada.shape[-1]
    c_all = _all_gather_small(c, "ag_c").reshape(N_DEV * B, D)
    b_cols = lax.dynamic_slice(b_ada, (0, dev * n_ada), (1, n_ada))
    mod_cols = _ada_fwd(c_all, w_ada[0], b_cols, "ada_fwd")
    mod_all = _all_gather_small(mod_cols, "ag_mod")
    mod_mine = lax.dynamic_slice(mod_all, (0, dev * B, 0), (N_DEV, B, n_ada))
    mod_mine = jnp.transpose(mod_mine, (1, 0, 2)).reshape(B, 6, 1, D)
    sh1, sc1, ga1, sh2, sc2, ga2 = [mod_mine[:, k] for k in range(6)]

    cos_t, sin_t = _rope_tables(positions)
    g_qn_al, g_kn_al = _pad_gain(g_qn), _pad_gain(g_kn)
    b_col = b_s[0].reshape(GROUPS, CHUNK, 1)
    gw = GW // GROUPS
    sel = (jnp.arange(GW)[:, None] // gw == jnp.arange(LANES)[None, :]).astype(F32)

    h1, rstd1 = _norm_mod_fwd(x, g_norm1, sh1, sc1, "norm1_fwd")
    h1f = h1.reshape(T, D)
    proj = _matmul(h1f, w_al, mode="nn", out_dtypes=[F32], name="mm_proj")
    ga_act, rstd_v = _gmlp_fwd(proj, g_v, w_s[0], b_col, D, GW, "gmlp_fwd")
    y_a = _matmul(ga_act, w_ba_f, mode="nn", out_dtypes=[F32], name="mm_ya")
    qn, kvn, rstd_q, rstd_kv = _lat_norm_fwd(proj, g_q_lat, g_kv_lat, D, GW, QL, KVL, "latnorm_fwd")
    q_al = _q_up_fwd(qn, w_uq_al, g_qn_al, cos_t, sin_t, "q_up_fwd")
    k_al, v_al = _kv_up_fwd(kvn, w_ukv_f, proj, g_kn_al, cos_t, sin_t, kpe_col, "kv_up_fwd")
    q3, k3, v3 = q_al.reshape(B, S, -1), k_al.reshape(B, S, -1), v_al.reshape(B, S, -1)
    attn, lse = _attn_fwd(q3, k3, v3, H, "attn_fwd")
    attn_f = attn.reshape(T, H * V_HEAD)
    y_b = _matmul(attn_f, w_bb_f, mode="nn", out_dtypes=[F32], name="mm_yb")
    mixed = _mix_fwd(proj, y_a, y_b, D, "mix_fwd")
    o = _matmul(mixed, w_out_f, mode="nn", out_dtypes=[F32], name="mm_o")
    x1, h2, rstd2 = _norm_mod_fwd(x, g_norm2, sh2, sc2, "norm2_fwd", o=o.reshape(B, S, D), ga=ga1)

    def relu_sq(acc):
        r = jnp.maximum(acc, 0.0)
        return r * r, r

    a_act, r_act = _matmul(h2.reshape(T, D), w_ff1_f, mode="nn", out_dtypes=[BF16, BF16], name="mm_ff1", epi=relu_sq)
    ff = _matmul(a_act, w_ff2_f, mode="nn", out_dtypes=[F32], name="mm_ff2")
    loss_part, dy, dff, d_ga2 = _loss_head(x1, ff.reshape(B, S, D), loss_target, ga2, "loss_head")
    loss = lax.psum(loss_part[0, 0], ("x", "y", "c"))

    dff_f = dff.reshape(T, D)
    df1 = _matmul(dff_f, w_ff2_f, mode="nt", out_dtypes=[BF16], name="mm_da", epi=lambda acc, r: (acc * (2.0 * r.astype(F32)),),
                  extras=(r_act,))
    gw_ff2 = _matmul(a_act, dff_f, mode="tn", out_dtypes=[BF16], name="mm_gw_ff2")
    gw_ff1 = _matmul(h2.reshape(T, D), df1, mode="tn", out_dtypes=[BF16], name="mm_gw_ff1")
    dh2 = _matmul(df1, w_ff1_f, mode="nt", out_dtypes=[F32], name="mm_dh2")
    dx1, d_sh2, d_sc2, gg_norm2, d_ga1, do = _norm_mod_bwd(
        dh2.reshape(B, S, D), x1, rstd2, sc2, g_norm2, dy, "norm2_bwd", o=o.reshape(B, S, D), ga=ga1)

    do_f = do.reshape(T, D)
    dmixed = _matmul(do_f, w_out_f, mode="nt", out_dtypes=[F32], name="mm_dmixed")
    gw_out = _matmul(mixed, do_f, mode="tn", out_dtypes=[BF16], name="mm_gw_out")
    dy_a, dy_b, dgate_a, dgate_b = _mix_bwd(proj, y_a, y_b, dmixed, D, "mix_bwd")
    gw_ba = _matmul(ga_act, dy_a, mode="tn", out_dtypes=[BF16], name="mm_gw_ba")
    dga_act = _matmul(dy_a, w_ba_f, mode="nt", out_dtypes=[F32], name="mm_dga")
    gw_bb = _matmul(attn_f, dy_b, mode="tn", out_dtypes=[BF16], name="mm_gw_bb")
    dattn = _matmul(dy_b, w_bb_f, mode="nt", out_dtypes=[BF16], name="mm_dattn")
    d_uv, gg_ws, gg_bs_t, gg_gv = _gmlp_bwd(proj, dga_act, rstd_v, g_v, w_s[0], b_col, sel, D, GW, "gmlp_bwd")
    dq, dk, dv = _attn_bwd(q3, k3, v3, attn, lse, dattn.reshape(B, S, -1), H, "attn_bwd")
    dq_raw, gg_qn = _q_up_bwd(qn, w_uq_al, g_qn_al, cos_t, sin_t, dq.reshape(T, -1), "q_up_bwd")
    dkv_raw, dkpe, gg_kn = _kv_up_bwd(kvn, w_ukv_f, proj, g_kn_al, cos_t, sin_t, dk.reshape(T, -1), dv.reshape(T, -1),
                                      kpe_col, "kv_up_bwd")
    gw_uq_al = _matmul(qn, dq_raw, mode="tn", out_dtypes=[BF16], name="mm_gw_uq")
    dqn = _matmul(dq_raw, w_uq_al, mode="nt", out_dtypes=[F32], name="mm_dqn")
    gw_ukv = _matmul(kvn, dkv_raw, mode="tn", out_dtypes=[BF16], name="mm_gw_ukv")
    dkvn = _matmul(dkv_raw, w_ukv_f, mode="nt", out_dtypes=[F32], name="mm_dkvn")
    d_qlat, d_kvlat, d_kpe16, gg_qlat, gg_kvlat = _lat_norm_bwd(
        proj, dqn, dkvn, dkpe, rstd_q, rstd_kv, g_q_lat, g_kv_lat, D, GW, QL, KVL, "latnorm_bwd")
    dproj = jnp.concatenate(
        [dgate_a, dgate_b, d_uv, d_qlat, d_kvlat, d_kpe16, jnp.zeros((T, IN_AL - (2 * D + OFF_GATE + LANES - ROPE)), BF16)], axis=1)
    gw_al = _matmul(h1f, dproj, mode="tn", out_dtypes=[BF16], name="mm_gw_in")
    dh1 = _matmul(dproj, w_al, mode="nt", out_dtypes=[F32], name="mm_dh1")
    grad_x, d_sh1, d_sc1, gg_norm1 = _norm_mod_bwd(dh1.reshape(B, S, D), x, rstd1, sc1, g_norm1, dx1, "norm1_bwd")

    dmod_mine = jnp.concatenate([d_sh1, d_sc1, d_ga1, d_sh2, d_sc2, d_ga2], axis=2).reshape(B, 6 * D)
    dmod_all = _all_gather_small(dmod_mine, "ag_dmod").reshape(N_DEV * B, 6 * D)
    dmod_cols = lax.dynamic_slice(dmod_all, (0, dev * n_ada), (N_DEV * B, n_ada))
    ada_out = _ada_bwd_adam(c_all, dmod_cols, w_ada[0], m_w_ada[0], v_w_ada[0], "ada_bwd_adam")
    nb_rows = 8
    bada_out = _adam_from_parts(b_ada.reshape(nb_rows, -1), m_b_ada.reshape(nb_rows, -1), v_b_ada.reshape(nb_rows, -1),
                                dmod_all.reshape(N_DEV * B, nb_rows, -1), "adam_b_ada")

    gw_in_nat = jnp.concatenate([gw_al[:, 2 * D:2 * D + OFF_GATE], gw_al[:, :2 * D]], axis=1)
    grads_sm = [
        _col_shards(gw_in_nat), _col_shards(_unpad_heads(gw_uq_al, H)), _col_shards(gw_ukv), _col_shards(gw_ba),
        gw_bb.reshape(N_DEV, -1, D), gw_out.reshape(N_DEV, -1, D), _col_shards(gw_ff1), gw_ff2.reshape(N_DEV, -1, D)]
    from_sibling = _exchange_with_sibling(grads_sm, "rs_sibling")
    names = ["w_in", "w_uq", "w_ukv", "w_branch_a", "w_branch_b", "w_out", "w_ff1", "w_ff2"]
    chip_sums = [_chip_sum(g, r, c_idx, "chip_sum_" + nm) for g, r, nm in zip(grads_sm, from_sibling, names)]
    parts = _exchange_between_chips(chip_sums, "rs_chips")
    ms = [m_w_in, m_w_uq, m_w_ukv, m_w_branch_a, m_w_branch_b, m_w_out, m_w_ff1, m_w_ff2]
    vs = [v_w_in, v_w_uq, v_w_ukv, v_w_branch_a, v_w_branch_b, v_w_out, v_w_ff1, v_w_ff2]
    big_out = {nm: _adam_from_parts(w[0], m[0], v[0], p, "adam_" + nm) for nm, w, m, v, p in zip(names, big, ms, vs, parts)}

    small = [("g_norm1", g_norm1, m_g_norm1, v_g_norm1, gg_norm1),
             ("g_v", g_v, m_g_v, v_g_v, gg_gv),
             ("w_s", w_s, m_w_s, v_w_s, gg_ws),
             ("b_s", b_s, m_b_s, v_b_s, jnp.transpose(gg_bs_t[:, :GROUPS])),
             ("g_q_lat", g_q_lat, m_g_q_lat, v_g_q_lat, gg_qlat),
             ("g_kv_lat", g_kv_lat, m_g_kv_lat, v_g_kv_lat, gg_kvlat),
             ("g_qn", g_qn, m_g_qn, v_g_qn, gg_qn[:, :QK_HEAD]),
             ("g_kn", g_kn, m_g_kn, v_g_kn, gg_kn[:, :QK_HEAD]),
             ("g_norm2", g_norm2, m_g_norm2, v_g_norm2, gg_norm2)]
    sizes = [w.size for _, w, _, _, _ in small]
    n_small = sum(sizes)
    n_small_pad = _round_up(n_small, 8 * LANES)

    def flat_cat(arrs):
        return jnp.pad(jnp.concatenate([a.reshape(-1) for a in arrs]), (0, n_small_pad - n_small))

    part_small = _all_gather_small(flat_cat([t[4] for t in small]), "ag_small_grads")
    small_out = _adam_from_parts(
        flat_cat([t[1] for t in small]).reshape(8, -1), flat_cat([t[2] for t in small]).reshape(8, -1),
        flat_cat([t[3] for t in small]).reshape(8, -1), part_small.reshape(N_DEV, 8, -1), "adam_small")
    offs = [sum(sizes[:i]) for i in range(len(sizes))]

    def small_piece(kind, i):
        return small_out[kind].reshape(-1)[offs[i]:offs[i] + sizes[i]].reshape(small[i][1].shape)

    small_idx = {t[0]: i for i, t in enumerate(small)}
    order = ["w_ada", "b_ada", "g_norm1", "w_in", "g_v", "w_s", "b_s", "g_q_lat", "g_kv_lat", "w_uq", "w_ukv", "g_qn", "g_kn",
             "w_branch_a", "w_branch_b", "w_out", "g_norm2", "w_ff1", "w_ff2"]

    def result(kind, nm):
        if nm == "w_ada":
            return ada_out[kind][None]
        if nm == "b_ada":
            return bada_out[kind].reshape(b_ada.shape)
        if nm in small_idx:
            return small_piece(kind, small_idx[nm])
        return big_out[nm][kind][None]

    outs = [loss, grad_x]
    for kind in range(4):
        outs += [result(kind, nm) for nm in order]
    return tuple(outs)
```

```python
import functools
import math

import jax
import jax.numpy as jnp
from jax import lax
from jax.experimental import pallas as pl
from jax.experimental.pallas import tpu as pltpu

F32 = jnp.float32
BF16 = jnp.bfloat16
SDS = jax.ShapeDtypeStruct
MESH = pl.DeviceIdType.MESH

N_DEV = 8
N_CHIP = 4
CHUNK = 128
GROUPS = 8
NOPE = 128
ROPE = 64
HALF_ROPE = ROPE // 2
QK_HEAD = NOPE + ROPE
V_HEAD = 128
HEAD_PAD = 256
LANES = 128
ROPE_THETA = 10000.0
EPS = 1e-6
INV_SQRT2 = 1.0 / math.sqrt(2.0)
INV_SQRT_2PI = 1.0 / math.sqrt(2.0 * math.pi)

ADAM_LR = 0.001
ADAM_B1 = 0.9
ADAM_B2 = 0.999
ADAM_EPS = 1e-08
ADAM_WD = 0.01
ADAM_STEP = 10

V7X_VMEM_LIMIT_BYTES = 56 * 1024 * 1024
MM_TILE = 1024
ATTN_BLOCK = 512
ROW_BLOCK = 128

NN = (((1,), (0,)), ((), ()))
NT = (((1,), (1,)), ((), ()))
TN = (((0,), (0,)), ((), ()))


def _tile(n, pref, mult):
    t = min(pref, n)
    t -= t % mult
    while t >= mult:
        if n % t == 0:
            return t
        t -= mult
    return n


def _round_up(n, m):
    return (n + m - 1) // m * m


def _cp(*sem):
    return pltpu.CompilerParams(dimension_semantics=sem, vmem_limit_bytes=V7X_VMEM_LIMIT_BYTES)


def _dot(a, b, dims):
    return lax.dot_general(a, b, dims, preferred_element_type=F32)


def _gelu(x):
    return 0.5 * x * (1.0 + lax.erf(x * INV_SQRT2))


def _gelu_grad(x):
    return 0.5 * (1.0 + lax.erf(x * INV_SQRT2)) + x * jnp.exp(-0.5 * x * x) * INV_SQRT_2PI


def _sigmoid(x):
    return 1.0 / (1.0 + jnp.exp(-x))


def _matmul(a, b, *, mode, out_dtypes, name, epi=None, extras=(), comm=None):
    if mode == "tn":
        K, M = a.shape
    else:
        M, K = a.shape
    if mode == "nt":
        N, Kb = b.shape
    else:
        Kb, N = b.shape
    assert K == Kb, (name, a.shape, b.shape)
    tm = _tile(M, MM_TILE, 128)
    tn = _tile(N, MM_TILE, 128)
    tk = _tile(K, MM_TILE, 128)
    nk = K // tk
    n_extra = len(extras)
    n_out = len(out_dtypes)
    dims = {"nn": NN, "nt": NT, "tn": TN}[mode]

    def body(a_ref, b_ref, *rest):
        extra_refs = rest[:n_extra]
        out_refs = rest[n_extra:n_extra + n_out]
        acc_ref = rest[n_extra + n_out]
        k = pl.program_id(2)

        @pl.when(k == 0)
        def _():
            acc_ref[...] = jnp.zeros_like(acc_ref)

        acc_ref[...] += _dot(a_ref[...].astype(BF16), b_ref[...].astype(BF16), dims)

        @pl.when(k == nk - 1)
        def _():
            acc = acc_ref[...]
            res = (acc,) if epi is None else epi(acc, *[e[...] for e in extra_refs])
            for o_ref, r in zip(out_refs, res):
                o_ref[...] = r.astype(o_ref.dtype)

    if mode == "tn":
        a_spec = pl.BlockSpec((tk, tm), lambda i, j, k: (k, i))
    else:
        a_spec = pl.BlockSpec((tm, tk), lambda i, j, k: (i, k))
    if mode == "nt":
        b_spec = pl.BlockSpec((tn, tk), lambda i, j, k: (j, k))
    else:
        b_spec = pl.BlockSpec((tk, tn), lambda i, j, k: (k, j))
    mn_spec = pl.BlockSpec((tm, tn), lambda i, j, k: (i, j))
    grid = (M // tm, N // tn, nk)
    in_specs = [a_spec, b_spec] + [mn_spec] * n_extra
    out_specs = [mn_spec] * n_out
    out_shape = [SDS((M, N), dt) for dt in out_dtypes]
    scratch = [pltpu.VMEM((tm, tn), F32)]
    if comm is None:
        outs = pl.pallas_call(
            body, name=name, grid=grid, in_specs=in_specs, out_specs=out_specs, out_shape=out_shape,
            scratch_shapes=scratch, compiler_params=_cp("parallel", "parallel", "arbitrary"),
        )(a, b, *extras)
        return outs[0] if n_out == 1 else outs

    def first_last():
        i, j, k = pl.program_id(0), pl.program_id(1), pl.program_id(2)
        return ((i == 0) & (j == 0) & (k == 0),
                (i == grid[0] - 1) & (j == grid[1] - 1) & (k == nk - 1))

    outs = pl.pallas_call(
        _carry(body, comm, 2 + n_extra, n_out, 1, first_last), name=name, grid=grid,
        in_specs=in_specs + [HBM_SPEC] * len(comm.ins), out_specs=out_specs + [HBM_SPEC] * len(comm.out_shape),
        out_shape=out_shape + comm.out_shape, scratch_shapes=scratch + comm.scratch,
        compiler_params=_cp("arbitrary", "arbitrary", "arbitrary"),
    )(a, b, *extras, *comm.ins)
    res = outs[0] if n_out == 1 else list(outs[:n_out])
    return res, comm.results(outs[n_out:])


def _here():
    return lax.axis_index("x"), lax.axis_index("y"), lax.axis_index("c")


def _other_chips(x, y):
    return [(1 - x, y), (x, 1 - y), (1 - x, 1 - y)]


def _all_gather_small(v, name):
    shape = v.shape
    n = v.size
    n_pad = _round_up(n, 8 * LANES)
    flat = jnp.pad(v.reshape(-1), (0, n_pad - n)).reshape(8, n_pad // 8)
    m_per, cols = flat.shape

    def body(x_ref, out_ref, send_sems, recv_sems, local_sem):
        x, y, c = _here()
        me, sibling = (x, y, c), (x, y, 1 - c)
        chips = _other_chips(x, y)

        def rows(px, py, pc):
            return out_ref.at[pl.ds((4 * px + 2 * py + pc) * m_per, m_per), :]

        def copy(k, block, to, src=None):
            return pltpu.make_async_remote_copy(
                src_ref=rows(*block) if src is None else src, dst_ref=rows(*block),
                send_sem=send_sems.at[k], recv_sem=recv_sems.at[k], device_id=to, device_id_type=MESH)

        mine = pltpu.make_async_copy(x_ref, rows(*me), local_sem)
        mine.start()
        first = [copy(0, me, sibling, src=x_ref)]
        first += [copy(1 + j, me, (*chip, c), src=x_ref) for j, chip in enumerate(chips)]
        for cp in first:
            cp.start()
        passed = [copy(4 + j, (*chip, c), sibling) for j, chip in enumerate(chips)]
        for j, chip in enumerate(chips):
            copy(1 + j, (*chip, c), me).wait_recv()
            passed[j].start()
        copy(0, sibling, me).wait_recv()
        for j, chip in enumerate(chips):
            copy(4 + j, (*chip, 1 - c), me).wait_recv()
        for cp in first + passed:
            cp.wait_send()
        mine.wait()

    out = pl.pallas_call(
        body, name=name,
        out_shape=SDS((N_DEV * m_per, cols), F32),
        in_specs=[pl.BlockSpec(memory_space=pltpu.VMEM)],
        out_specs=pl.BlockSpec(memory_space=pltpu.VMEM),
        scratch_shapes=[pltpu.SemaphoreType.DMA((7,)), pltpu.SemaphoreType.DMA((7,)), pltpu.SemaphoreType.DMA],
        compiler_params=pltpu.CompilerParams(vmem_limit_bytes=V7X_VMEM_LIMIT_BYTES),
    )(flat)
    return out.reshape(N_DEV, n_pad)[:, :n].reshape((N_DEV,) + shape)


HBM_SPEC = pl.BlockSpec(memory_space=pltpu.HBM)


class _GatherJob:
    def __init__(self, shards):
        self.ins = list(shards)
        self.nw = len(shards)
        self.out_shape = [SDS((N_DEV,) + s.shape, s.dtype) for s in shards]
        self.scratch = [pltpu.SemaphoreType.DMA((7 * self.nw,)), pltpu.SemaphoreType.DMA((7 * self.nw,)),
                        pltpu.SemaphoreType.DMA((self.nw,))]

    def _parts(self, xs, outs, sems):
        send_sems, recv_sems, local_sems = sems
        x, y, c = _here()

        def blk(w, px, py, pc):
            return outs[w].at[4 * px + 2 * py + pc]

        def copy(w, k, block, to, src=None):
            return pltpu.make_async_remote_copy(
                src_ref=blk(w, *block) if src is None else src, dst_ref=blk(w, *block),
                send_sem=send_sems.at[7 * w + k], recv_sem=recv_sems.at[7 * w + k], device_id=to, device_id_type=MESH)

        me, sibling = (x, y, c), (x, y, 1 - c)
        chips = _other_chips(x, y)
        mine = [pltpu.make_async_copy(xs[w], blk(w, *me), local_sems.at[w]) for w in range(self.nw)]
        first = []
        for w in range(self.nw):
            first.append(copy(w, 0, me, sibling, src=xs[w]))
            first += [copy(w, 1 + j, me, (*chip, c), src=xs[w]) for j, chip in enumerate(chips)]
        return copy, me, sibling, chips, c, mine, first

    def start(self, xs, outs, sems):
        _, _, _, _, _, mine, first = self._parts(xs, outs, sems)
        for cp in mine + first:
            cp.start()

    def finish(self, xs, outs, sems):
        copy, me, sibling, chips, c, mine, first = self._parts(xs, outs, sems)
        passed = []
        for w in range(self.nw):
            for j, chip in enumerate(chips):
                copy(w, 1 + j, (*chip, c), me).wait_recv()
                fwd = copy(w, 4 + j, (*chip, c), sibling)
                fwd.start()
                passed.append(fwd)
        for w in range(self.nw):
            copy(w, 0, sibling, me).wait_recv()
            for j, chip in enumerate(chips):
                copy(w, 4 + j, (*chip, 1 - c), me).wait_recv()
        for cp in first + passed:
            cp.wait_send()
        for cp in mine:
            cp.wait()


class _SiblingJob:
    def __init__(self, grads):
        self.ins = list(grads)
        self.nw = len(grads)
        self.out_shape = [SDS((N_CHIP,) + g.shape[1:], g.dtype) for g in grads]
        self.scratch = [pltpu.SemaphoreType.DMA((N_CHIP * self.nw,)), pltpu.SemaphoreType.DMA((N_CHIP * self.nw,))]

    def _copies(self, gs, outs, sems):
        send_sems, recv_sems = sems
        x, y, c = _here()
        return [pltpu.make_async_remote_copy(
            src_ref=gs[w].at[2 * k + (1 - c)], dst_ref=outs[w].at[k],
            send_sem=send_sems.at[N_CHIP * w + k], recv_sem=recv_sems.at[N_CHIP * w + k],
            device_id=(x, y, 1 - c), device_id_type=MESH) for w in range(self.nw) for k in range(N_CHIP)]

    def start(self, gs, outs, sems):
        for cp in self._copies(gs, outs, sems):
            cp.start()

    def finish(self, gs, outs, sems):
        for cp in self._copies(gs, outs, sems):
            cp.wait()


class _ChipsJob:
    def __init__(self, chip_sums):
        self.ins = list(chip_sums)
        self.nw = len(chip_sums)
        self.out_shape = [SDS(s.shape, s.dtype) for s in chip_sums]
        self.scratch = [pltpu.SemaphoreType.DMA((3 * self.nw,)), pltpu.SemaphoreType.DMA((3 * self.nw,)),
                        pltpu.SemaphoreType.DMA((self.nw,))]

    def _parts(self, srcs, outs, sems):
        send_sems, recv_sems, local_sems = sems
        x, y, c = _here()
        my_chip = 2 * x + y
        local = [pltpu.make_async_copy(srcs[w].at[my_chip], outs[w].at[my_chip], local_sems.at[w]) for w in range(self.nw)]
        sends, landed = [], []
        for w in range(self.nw):
            for j, (px, py) in enumerate(_other_chips(x, y)):
                sems_j = dict(send_sem=send_sems.at[3 * w + j], recv_sem=recv_sems.at[3 * w + j],
                              device_id=(px, py, c), device_id_type=MESH)
                sends.append(pltpu.make_async_remote_copy(
                    src_ref=srcs[w].at[2 * px + py], dst_ref=outs[w].at[my_chip], **sems_j))
                landed.append(pltpu.make_async_remote_copy(
                    src_ref=srcs[w].at[2 * px + py], dst_ref=outs[w].at[2 * px + py], **sems_j))
        return local, sends, landed

    def start(self, srcs, outs, sems):
        local, sends, _ = self._parts(srcs, outs, sems)
        for cp in local + sends:
            cp.start()

    def finish(self, srcs, outs, sems):
        local, sends, landed = self._parts(srcs, outs, sems)
        for cp in landed:
            cp.wait_recv()
        for cp in sends:
            cp.wait_send()
        for cp in local:
            cp.wait()


class _Comm:
    def __init__(self, jobs):
        self.jobs = list(jobs)
        self.ins = [a for j in self.jobs for a in j.ins]
        self.out_shape = [s for j in self.jobs for s in j.out_shape]
        self.scratch = [s for j in self.jobs for s in j.scratch]

    def _split(self, flat, counts):
        out, pos = [], 0
        for n in counts:
            out.append(flat[pos:pos + n])
            pos += n
        return out

    def _each(self, ins, outs, sems):
        return zip(self.jobs, self._split(ins, [len(j.ins) for j in self.jobs]),
                   self._split(outs, [len(j.out_shape) for j in self.jobs]),
                   self._split(sems, [len(j.scratch) for j in self.jobs]))

    def start(self, ins, outs, sems):
        for job, i, o, s in self._each(ins, outs, sems):
            job.start(i, o, s)

    def finish(self, ins, outs, sems):
        for job, i, o, s in self._each(ins, outs, sems):
            job.finish(i, o, s)

    def results(self, flat):
        return [list(r) for r in self._split(list(flat), [len(j.out_shape) for j in self.jobs])]


def _carry(body, comm, n_in, n_out, n_scratch, first_last):
    ci, co = len(comm.ins), len(comm.out_shape)

    def wrapped(*refs):
        ins, rest = refs[:n_in + ci], refs[n_in + ci:]
        outs, scr = rest[:n_out + co], rest[n_out + co:]
        c_ins, c_outs, c_sems = ins[n_in:], outs[n_out:], scr[n_scratch:]
        first, last = first_last()

        @pl.when(first)
        def _():
            comm.start(c_ins, c_outs, c_sems)

        body(*ins[:n_in], *outs[:n_out], *scr[:n_scratch])

        @pl.when(last)
        def _():
            comm.finish(c_ins, c_outs, c_sems)

    return wrapped


def _run_comm(jobs, name):
    comm = _Comm(jobs)

    def body(*refs):
        ci, co = len(comm.ins), len(comm.out_shape)
        comm.start(refs[:ci], refs[ci:ci + co], refs[ci + co:])
        comm.finish(refs[:ci], refs[ci:ci + co], refs[ci + co:])

    outs = pl.pallas_call(
        body, name=name, out_shape=comm.out_shape,
        in_specs=[HBM_SPEC] * len(comm.ins), out_specs=[HBM_SPEC] * len(comm.out_shape),
        scratch_shapes=comm.scratch,
    )(*comm.ins)
    return comm.results(outs)


def _chip_sum(g, recv, c_idx, name):
    _, m, n = g.shape
    tr = _tile(m, max(8, (1 << 20) // max(n, 1) // 8 * 8), 8)

    def body(c_ref, g_ref, r_ref, o_ref):
        o_ref[...] = (g_ref[...].astype(F32) + r_ref[...].astype(F32)).astype(o_ref.dtype)

    grid_spec = pltpu.PrefetchScalarGridSpec(
        num_scalar_prefetch=1, grid=(N_CHIP, m // tr),
        in_specs=[pl.BlockSpec((None, tr, n), lambda k, i, c_ref: (2 * k + c_ref[0], i, 0)),
                  pl.BlockSpec((None, tr, n), lambda k, i, c_ref: (k, i, 0))],
        out_specs=pl.BlockSpec((None, tr, n), lambda k, i, c_ref: (k, i, 0)))
    return pl.pallas_call(
        body, name=name, grid_spec=grid_spec, out_shape=SDS((N_CHIP, m, n), BF16),
        compiler_params=_cp("parallel", "parallel"),
    )(c_idx, g, recv)


def _adam_math(w, g, m, v):
    m = ADAM_B1 * m + (1.0 - ADAM_B1) * g
    v = ADAM_B2 * v + (1.0 - ADAM_B2) * (g * g)
    m_hat = m / (1.0 - ADAM_B1 ** ADAM_STEP)
    v_hat = v / (1.0 - ADAM_B2 ** ADAM_STEP)
    delta = -ADAM_LR * (m_hat / (jnp.sqrt(v_hat) + ADAM_EPS) + ADAM_WD * w)
    return delta, m, v


def _adam_from_parts(w, m, v, parts, name):
    R, C = w.shape
    P = parts.shape[0]
    tr = _tile(R, max(8, (1 << 19) // max(C, 1) // 8 * 8), 8)

    def body(w_ref, m_ref, v_ref, p_ref, g_out, d_out, m_out, v_out):
        g = p_ref[0].astype(F32)
        for k in range(1, P):
            g = g + p_ref[k].astype(F32)
        delta, nm, nv = _adam_math(w_ref[...], g, m_ref[...], v_ref[...])
        g_out[...] = g
        d_out[...] = delta
        m_out[...] = nm
        v_out[...] = nv

    spec = pl.BlockSpec((tr, C), lambda i: (i, 0))
    return pl.pallas_call(
        body, name=name, grid=(R // tr,),
        in_specs=[spec, spec, spec, pl.BlockSpec((P, tr, C), lambda i: (0, i, 0))],
        out_specs=[spec] * 4, out_shape=[SDS((R, C), F32)] * 4,
        compiler_params=_cp("parallel"),
    )(w, m, v, parts)


def _ada_fwd(c_all, w_ada, b_cols, name):
    nb, D = c_all.shape
    n = w_ada.shape[1]
    tn = _tile(n, 512, 128)

    def body(c_ref, w_ref, b_ref, o_ref):
        cv = c_ref[...]
        cond = (cv * _sigmoid(cv)).astype(BF16)
        o_ref[...] = _dot(cond, w_ref[...].astype(BF16), NN) + b_ref[...]

    return pl.pallas_call(
        body, name=name, grid=(n // tn,),
        in_specs=[pl.BlockSpec((nb, D), lambda j: (0, 0)), pl.BlockSpec((D, tn), lambda j: (0, j)),
                  pl.BlockSpec((1, tn), lambda j: (0, j))],
        out_specs=pl.BlockSpec((nb, tn), lambda j: (0, j)), out_shape=SDS((nb, n), F32),
        compiler_params=_cp("parallel"),
    )(c_all, w_ada, b_cols)


def _ada_bwd_adam(c_all, dmod_cols, w, m, v, name):
    nb, D = c_all.shape
    n = w.shape[1]
    tr = _tile(D, 512, 128)
    tn = _tile(n, 1024, 128)

    def body(c_ref, d_ref, w_ref, m_ref, v_ref, g_out, d_out, m_out, v_out):
        cv = c_ref[...]
        cond = (cv * _sigmoid(cv)).astype(BF16)
        g = _dot(cond, d_ref[...].astype(BF16), TN)
        delta, nm, nv = _adam_math(w_ref[...], g, m_ref[...], v_ref[...])
        g_out[...] = g
        d_out[...] = delta
        m_out[...] = nm
        v_out[...] = nv

    spec = pl.BlockSpec((tr, tn), lambda i, j: (i, j))
    return pl.pallas_call(
        body, name=name, grid=(D // tr, n // tn),
        in_specs=[pl.BlockSpec((nb, tr), lambda i, j: (0, i)), pl.BlockSpec((nb, tn), lambda i, j: (0, j)),
                  spec, spec, spec],
        out_specs=[spec] * 4, out_shape=[SDS((D, n), F32)] * 4,
        compiler_params=_cp("parallel", "parallel"),
    )(c_all, dmod_cols, w, m, v)


def _tok(tb, width):
    return pl.BlockSpec((None, tb, width), lambda b, i: (b, i, 0))


def _per_example(width):
    return pl.BlockSpec((None, 1, width), lambda b, i: (b, 0, 0))


def _shared_row(width):
    return pl.BlockSpec((1, width), lambda b, i: (0, 0))


def _norm_mod_fwd(x, g, sh, sc, name, o=None, ga=None):
    B, S, D = x.shape
    tb = _tile(S, ROW_BLOCK, 8)
    fused = o is not None

    def body(*refs):
        if fused:
            x_ref, o_ref, ga_ref, g_ref, sh_ref, sc_ref, x1_ref, h_ref, r_ref = refs
            xv = x_ref[...] + ga_ref[...] * o_ref[...]
            x1_ref[...] = xv
        else:
            x_ref, g_ref, sh_ref, sc_ref, h_ref, r_ref = refs
            xv = x_ref[...]
        rstd = lax.rsqrt(jnp.mean(xv * xv, axis=1, keepdims=True) + EPS)
        y = xv * rstd * g_ref[...]
        h_ref[...] = (y * (1.0 + sc_ref[...]) + sh_ref[...]).astype(BF16)
        r_ref[...] = rstd

    ins = [x] + ([o, ga] if fused else []) + [g, sh, sc]
    in_specs = [_tok(tb, D)] + ([_tok(tb, D), _per_example(D)] if fused else []) + [_shared_row(D), _per_example(D), _per_example(D)]
    out_specs = ([_tok(tb, D)] if fused else []) + [_tok(tb, D), _tok(tb, 1)]
    out_shape = ([SDS((B, S, D), F32)] if fused else []) + [SDS((B, S, D), BF16), SDS((B, S, 1), F32)]
    return pl.pallas_call(
        body, name=name, grid=(B, S // tb), in_specs=in_specs, out_specs=out_specs, out_shape=out_shape,
        compiler_params=_cp("parallel", "parallel"),
    )(*ins)


def _norm_mod_bwd(dh, xin, rstd, sc, g, dres, name, o=None, ga=None):
    B, S, D = xin.shape
    tb = _tile(S, ROW_BLOCK, 8)
    gated = o is not None

    def body(*refs):
        if gated:
            (dh_ref, x_ref, r_ref, sc_ref, g_ref, dres_ref, o_ref, ga_ref,
             dx_ref, dsh_ref, dsc_ref, gg_ref, dga_ref, do_ref) = refs
        else:
            dh_ref, x_ref, r_ref, sc_ref, g_ref, dres_ref, dx_ref, dsh_ref, dsc_ref, gg_ref = refs
        b, i = pl.program_id(0), pl.program_id(1)

        @pl.when(i == 0)
        def _():
            dsh_ref[...] = jnp.zeros_like(dsh_ref)
            dsc_ref[...] = jnp.zeros_like(dsc_ref)
            if gated:
                dga_ref[...] = jnp.zeros_like(dga_ref)

        @pl.when((i == 0) & (b == 0))
        def _():
            gg_ref[...] = jnp.zeros_like(gg_ref)

        dhv = dh_ref[...]
        rs = r_ref[...]
        gv = g_ref[...]
        xhat = x_ref[...] * rs
        dsh_ref[...] += jnp.sum(dhv, axis=0, keepdims=True)
        dsc_ref[...] += jnp.sum(dhv * (xhat * gv), axis=0, keepdims=True)
        dn = dhv * (1.0 + sc_ref[...])
        gg_ref[...] += jnp.sum(dn * xhat, axis=0, keepdims=True)
        dxhat = dn * gv
        cm = jnp.mean(dxhat * xhat, axis=1, keepdims=True)
        dx = dres_ref[...] + rs * (dxhat - xhat * cm)
        dx_ref[...] = dx
        if gated:
            dga_ref[...] += jnp.sum(dx * o_ref[...], axis=0, keepdims=True)
            do_ref[...] = (dx * ga_ref[...]).astype(BF16)

    ins = [dh, xin, rstd, sc, g, dres] + ([o, ga] if gated else [])
    in_specs = [_tok(tb, D), _tok(tb, D), _tok(tb, 1), _per_example(D), _shared_row(D), _tok(tb, D)]
    in_specs += [_tok(tb, D), _per_example(D)] if gated else []
    out_specs = [_tok(tb, D), _per_example(D), _per_example(D), _shared_row(D)]
    out_shape = [SDS((B, S, D), F32), SDS((B, 1, D), F32), SDS((B, 1, D), F32), SDS((1, D), F32)]
    if gated:
        out_specs += [_per_example(D), _tok(tb, D)]
        out_shape += [SDS((B, 1, D), F32), SDS((B, S, D), BF16)]
    return pl.pallas_call(
        body, name=name, grid=(B, S // tb), in_specs=in_specs, out_specs=out_specs, out_shape=out_shape,
        compiler_params=_cp("arbitrary", "arbitrary"),
    )(*ins)


def _loss_head(x1, ff, target, ga2, name):
    B, S, D = x1.shape
    tb = _tile(S, ROW_BLOCK, 8)
    nb, ni = B, S // tb

    def body(x_ref, f_ref, t_ref, ga_ref, loss_ref, dy_ref, dff_ref, dga_ref, acc_ref):
        b, i = pl.program_id(0), pl.program_id(1)

        @pl.when(i == 0)
        def _():
            dga_ref[...] = jnp.zeros_like(dga_ref)

        @pl.when((i == 0) & (b == 0))
        def _():
            acc_ref[...] = jnp.zeros_like(acc_ref)

        fv = f_ref[...]
        gav = ga_ref[...]
        err = x_ref[...] + gav * fv - t_ref[...]
        acc_ref[...] += jnp.sum(err * err, axis=0, keepdims=True)
        dy = err * (1.0 / D)
        dy_ref[...] = dy
        dff_ref[...] = (dy * gav).astype(BF16)
        dga_ref[...] += jnp.sum(dy * fv, axis=0, keepdims=True)

        @pl.when((i == ni - 1) & (b == nb - 1))
        def _():
            loss_ref[...] = jnp.sum(acc_ref[...], axis=1, keepdims=True) * (0.5 / D)

    return pl.pallas_call(
        body, name=name, grid=(B, S // tb),
        in_specs=[_tok(tb, D), _tok(tb, D), _tok(tb, D), _per_example(D)],
        out_specs=[pl.BlockSpec((1, 1), lambda b, i: (0, 0)), _tok(tb, D), _tok(tb, D), _per_example(D)],
        out_shape=[SDS((1, 1), F32), SDS((B, S, D), F32), SDS((B, S, D), BF16), SDS((B, 1, D), F32)],
        scratch_shapes=[pltpu.VMEM((1, D), F32)],
        compiler_params=_cp("arbitrary", "arbitrary"),
    )(x1, ff, target, ga2)


def _causal_mask():
    t = lax.broadcasted_iota(jnp.int32, (CHUNK, CHUNK), 0)
    s = lax.broadcasted_iota(jnp.int32, (CHUNK, CHUNK), 1)
    return s <= t


def _gmlp_fwd(proj, g_v, w_s, b_col, D, GW, name):
    T = proj.shape[0]
    tb = _tile(T, ROW_BLOCK, CHUNK)
    gw = GW // GROUPS
    ucol = (2 * D) // GW

    def body(u_ref, v_ref, g_ref, w_ref, b_ref, ga_ref, r_ref):
        zu = _gelu(u_ref[...])
        zv = _gelu(v_ref[...])
        rstd = lax.rsqrt(jnp.mean(zv * zv, axis=1, keepdims=True) + EPS)
        vn = (zv * rstd * g_ref[...]).astype(BF16)
        mask = _causal_mask()
        for g in range(GROUPS):
            wm = jnp.where(mask, w_ref[g], 0.0).astype(BF16)
            cols = slice(g * gw, (g + 1) * gw)
            for ci in range(tb // CHUNK):
                rows = slice(ci * CHUNK, (ci + 1) * CHUNK)
                mixed = _dot(wm, vn[rows, cols], NN) + b_ref[g]
                ga_ref[rows, cols] = (zu[rows, cols] * mixed).astype(BF16)
        r_ref[...] = rstd

    return pl.pallas_call(
        body, name=name, grid=(T // tb,),
        in_specs=[pl.BlockSpec((tb, GW), lambda i: (i, ucol)), pl.BlockSpec((tb, GW), lambda i: (i, ucol + 1)),
                  pl.BlockSpec((1, GW), lambda i: (0, 0)), pl.BlockSpec((GROUPS, CHUNK, CHUNK), lambda i: (0, 0, 0)),
                  pl.BlockSpec((GROUPS, CHUNK, 1), lambda i: (0, 0, 0))],
        out_specs=[pl.BlockSpec((tb, GW), lambda i: (i, 0)), pl.BlockSpec((tb, 1), lambda i: (i, 0))],
        out_shape=[SDS((T, GW), BF16), SDS((T, 1), F32)],
        compiler_params=_cp("parallel"),
    )(proj, proj, g_v, w_s, b_col)


def _gmlp_bwd(proj, dga, rstd_v, g_v, w_s, b_col, sel, D, GW, name):
    T = proj.shape[0]
    tb = _tile(T, ROW_BLOCK, CHUNK)
    gw = GW // GROUPS
    ucol = (2 * D) // GW
    nsteps = T // tb

    def body(u_ref, v_ref, dga_ref, r_ref, g_ref, w_ref, b_ref, sel_ref,
             duv_ref, gws_ref, gbs_ref, gg_ref, accb_ref, dvn_ref):
        step = pl.program_id(0)

        @pl.when(step == 0)
        def _():
            gws_ref[...] = jnp.zeros_like(gws_ref)
            gg_ref[...] = jnp.zeros_like(gg_ref)
            accb_ref[...] = jnp.zeros_like(accb_ref)

        uv = u_ref[...]
        vv = v_ref[...]
        zu = _gelu(uv)
        zv = _gelu(vv)
        rs = r_ref[...]
        gv = g_ref[...]
        vhat = zv * rs
        vnb = (vhat * gv).astype(BF16)
        dgav = dga_ref[...]
        du_gelu = _gelu_grad(uv)
        mask = _causal_mask()
        for g in range(GROUPS):
            wm = jnp.where(mask, w_ref[g], 0.0)
            wmb = wm.astype(BF16)
            wmtb = wm.T.astype(BF16)
            cols = slice(g * gw, (g + 1) * gw)
            for ci in range(tb // CHUNK):
                rows = slice(ci * CHUNK, (ci + 1) * CHUNK)
                vn_blk = vnb[rows, cols]
                mixed = _dot(wmb, vn_blk, NN) + b_ref[g]
                duv_ref[rows, cols] = (dgav[rows, cols] * mixed * du_gelu[rows, cols]).astype(BF16)
                dmix = dgav[rows, cols] * zu[rows, cols]
                accb_ref[:, cols] += dmix
                dmb = dmix.astype(BF16)
                gws_ref[g] += _dot(dmb, vn_blk, NT)
                dvn_ref[rows, cols] = _dot(wmtb, dmb, NN)
        dvn = dvn_ref[...]
        gg_ref[...] += jnp.sum(dvn * vhat, axis=0, keepdims=True)
        dvhat = dvn * gv
        cm = jnp.mean(dvhat * vhat, axis=1, keepdims=True)
        dzv = rs * (dvhat - vhat * cm)
        duv_ref[:, GW:] = (dzv * _gelu_grad(vv)).astype(BF16)

        @pl.when(step == nsteps - 1)
        def _():
            gbs_ref[...] = jnp.dot(accb_ref[...], sel_ref[...], precision=lax.Precision.HIGHEST,
                                   preferred_element_type=F32)
            for g in range(GROUPS):
                gws_ref[g] = jnp.where(mask, gws_ref[g], 0.0)

    return pl.pallas_call(
        body, name=name, grid=(nsteps,),
        in_specs=[pl.BlockSpec((tb, GW), lambda i: (i, ucol)), pl.BlockSpec((tb, GW), lambda i: (i, ucol + 1)),
                  pl.BlockSpec((tb, GW), lambda i: (i, 0)), pl.BlockSpec((tb, 1), lambda i: (i, 0)),
                  pl.BlockSpec((1, GW), lambda i: (0, 0)), pl.BlockSpec((GROUPS, CHUNK, CHUNK), lambda i: (0, 0, 0)),
                  pl.BlockSpec((GROUPS, CHUNK, 1), lambda i: (0, 0, 0)), pl.BlockSpec((GW, LANES), lambda i: (0, 0))],
        out_specs=[pl.BlockSpec((tb, 2 * GW), lambda i: (i, 0)), pl.BlockSpec((GROUPS, CHUNK, CHUNK), lambda i: (0, 0, 0)),
                   pl.BlockSpec((CHUNK, LANES), lambda i: (0, 0)), pl.BlockSpec((1, GW), lambda i: (0, 0))],
        out_shape=[SDS((T, 2 * GW), BF16), SDS((GROUPS, CHUNK, CHUNK), F32), SDS((CHUNK, LANES), F32), SDS((1, GW), F32)],
        scratch_shapes=[pltpu.VMEM((CHUNK, GW), F32), pltpu.VMEM((tb, GW), F32)],
        compiler_params=_cp("arbitrary"),
    )(proj, proj, dga, rstd_v, g_v, w_s, b_col, sel)


def _mix_fwd(proj, y_a, y_b, D, name):
    T = proj.shape[0]
    tb = _tile(T, 512, 8)
    td = _tile(D, 1024, 128)
    nd = D // td

    def body(ga_ref, gb_ref, ya_ref, yb_ref, o_ref):
        o_ref[...] = (_sigmoid(ga_ref[...]) * ya_ref[...] + _sigmoid(gb_ref[...]) * yb_ref[...]).astype(BF16)

    blk = pl.BlockSpec((tb, td), lambda i, j: (i, j))
    return pl.pallas_call(
        body, name=name, grid=(T // tb, nd),
        in_specs=[blk, pl.BlockSpec((tb, td), lambda i, j: (i, j + nd)), blk, blk],
        out_specs=blk, out_shape=SDS((T, D), BF16),
        compiler_params=_cp("parallel", "parallel"),
    )(proj, proj, y_a, y_b)


def _mix_bwd(proj, y_a, y_b, dmixed, D, name):
    T = proj.shape[0]
    tb = _tile(T, 512, 8)
    td = _tile(D, 1024, 128)
    nd = D // td

    def body(ga_ref, gb_ref, ya_ref, yb_ref, dm_ref, dya_ref, dyb_ref, dga_ref, dgb_ref):
        dm = dm_ref[...]
        sa = _sigmoid(ga_ref[...])
        sb = _sigmoid(gb_ref[...])
        dya_ref[...] = (dm * sa).astype(BF16)
        dyb_ref[...] = (dm * sb).astype(BF16)
        dga_ref[...] = (dm * ya_ref[...] * sa * (1.0 - sa)).astype(BF16)
        dgb_ref[...] = (dm * yb_ref[...] * sb * (1.0 - sb)).astype(BF16)

    blk = pl.BlockSpec((tb, td), lambda i, j: (i, j))
    return pl.pallas_call(
        body, name=name, grid=(T // tb, nd),
        in_specs=[blk, pl.BlockSpec((tb, td), lambda i, j: (i, j + nd)), blk, blk, blk],
        out_specs=[blk] * 4, out_shape=[SDS((T, D), BF16)] * 4,
        compiler_params=_cp("parallel", "parallel"),
    )(proj, proj, y_a, y_b, dmixed)


def _lat_norm_fwd(proj, g_q, g_kv, D, GW, QL, KVL, name):
    T = proj.shape[0]
    tb = _tile(T, 512, 8)
    qcol = (2 * D + 2 * GW) // QL
    kvcol = (2 * D + 2 * GW + QL) // KVL

    def body(q_ref, kv_ref, gq_ref, gkv_ref, qn_ref, kvn_ref, rq_ref, rkv_ref):
        qv = q_ref[...]
        rq = lax.rsqrt(jnp.mean(qv * qv, axis=1, keepdims=True) + EPS)
        qn_ref[...] = (qv * rq * gq_ref[...]).astype(BF16)
        rq_ref[...] = rq
        kv = kv_ref[...]
        rkv = lax.rsqrt(jnp.mean(kv * kv, axis=1, keepdims=True) + EPS)
        kvn_ref[...] = (kv * rkv * gkv_ref[...]).astype(BF16)
        rkv_ref[...] = rkv

    return pl.pallas_call(
        body, name=name, grid=(T // tb,),
        in_specs=[pl.BlockSpec((tb, QL), lambda i: (i, qcol)), pl.BlockSpec((tb, KVL), lambda i: (i, kvcol)),
                  pl.BlockSpec((1, QL), lambda i: (0, 0)), pl.BlockSpec((1, KVL), lambda i: (0, 0))],
        out_specs=[pl.BlockSpec((tb, QL), lambda i: (i, 0)), pl.BlockSpec((tb, KVL), lambda i: (i, 0)),
                   pl.BlockSpec((tb, 1), lambda i: (i, 0)), pl.BlockSpec((tb, 1), lambda i: (i, 0))],
        out_shape=[SDS((T, QL), BF16), SDS((T, KVL), BF16), SDS((T, 1), F32), SDS((T, 1), F32)],
        compiler_params=_cp("parallel"),
    )(proj, proj, g_q, g_kv)


def _lat_norm_bwd(proj, dqn, dkvn, dkpe, rq, rkv, g_q, g_kv, D, GW, QL, KVL, name):
    T = proj.shape[0]
    tb = _tile(T, 512, 8)
    qcol = (2 * D + 2 * GW) // QL
    kvcol = (2 * D + 2 * GW + QL) // KVL

    def one(xv, rs, gv, dy, gg_ref):
        xhat = xv * rs
        gg_ref[...] += jnp.sum(dy * xhat, axis=0, keepdims=True)
        dxhat = dy * gv
        cm = jnp.mean(dxhat * xhat, axis=1, keepdims=True)
        return rs * (dxhat - xhat * cm)

    def body(q_ref, kv_ref, dqn_ref, dkvn_ref, dkpe_ref, rq_ref, rkv_ref, gq_ref, gkv_ref,
             dq_ref, dkv_ref, dkpe_out, ggq_ref, ggkv_ref):
        @pl.when(pl.program_id(0) == 0)
        def _():
            ggq_ref[...] = jnp.zeros_like(ggq_ref)
            ggkv_ref[...] = jnp.zeros_like(ggkv_ref)

        dq_ref[...] = one(q_ref[...], rq_ref[...], gq_ref[...], dqn_ref[...], ggq_ref).astype(BF16)
        dkv_ref[...] = one(kv_ref[...], rkv_ref[...], gkv_ref[...], dkvn_ref[...], ggkv_ref).astype(BF16)
        dkpe_out[...] = dkpe_ref[...].astype(BF16)

    return pl.pallas_call(
        body, name=name, grid=(T // tb,),
        in_specs=[pl.BlockSpec((tb, QL), lambda i: (i, qcol)), pl.BlockSpec((tb, KVL), lambda i: (i, kvcol)),
                  pl.BlockSpec((tb, QL), lambda i: (i, 0)), pl.BlockSpec((tb, KVL), lambda i: (i, 0)),
                  pl.BlockSpec((tb, LANES), lambda i: (i, 0)),
                  pl.BlockSpec((tb, 1), lambda i: (i, 0)), pl.BlockSpec((tb, 1), lambda i: (i, 0)),
                  pl.BlockSpec((1, QL), lambda i: (0, 0)), pl.BlockSpec((1, KVL), lambda i: (0, 0))],
        out_specs=[pl.BlockSpec((tb, QL), lambda i: (i, 0)), pl.BlockSpec((tb, KVL), lambda i: (i, 0)),
                   pl.BlockSpec((tb, LANES), lambda i: (i, 0)),
                   pl.BlockSpec((1, QL), lambda i: (0, 0)), pl.BlockSpec((1, KVL), lambda i: (0, 0))],
        out_shape=[SDS((T, QL), BF16), SDS((T, KVL), BF16), SDS((T, LANES), BF16), SDS((1, QL), F32), SDS((1, KVL), F32)],
        compiler_params=_cp("arbitrary"),
    )(proj, proj, dqn, dkvn, dkpe, rq, rkv, g_q, g_kv)


def _swap_halves(r):
    lane = lax.broadcasted_iota(jnp.int32, r.shape, 1)
    lo = pltpu.roll(r, LANES - HALF_ROPE, 1)
    hi = pltpu.roll(r, HALF_ROPE, 1)
    return jnp.where(lane < HALF_ROPE, lo, jnp.where(lane < ROPE, hi, 0.0))


def _rope_fwd(r, cos_t, sin_t):
    return r * cos_t + _swap_halves(r) * sin_t


def _rope_bwd(d, cos_t, sin_t):
    return d * cos_t + _swap_halves(d * sin_t)


def _head_norm_bwd(xn, xr, rs, g_n, g_r, dyn, dyr):
    xhn, xhr = xn * rs, xr * rs
    dn, dr = dyn * g_n, dyr * g_r
    cm = (jnp.sum(dn * xhn, axis=1, keepdims=True) + jnp.sum(dr * xhr, axis=1, keepdims=True)) * (1.0 / QK_HEAD)
    return rs * (dn - xhn * cm), rs * (dr - xhr * cm), dyn * xhn, dyr * xhr


def _q_up_fwd(qn, w_uq_al, g_al, cos_t, sin_t, name):
    T, QL = qn.shape
    HP = w_uq_al.shape[1]
    tb = _tile(T, 512, 8)
    hw = _tile(HP, 1024, HEAD_PAD)

    def body(x_ref, w_ref, g_ref, c_ref, s_ref, o_ref):
        raw = _dot(x_ref[...], w_ref[...], NN)
        gv = g_ref[...]
        cv, sv = c_ref[...], s_ref[...]
        for h in range(hw // HEAD_PAD):
            xn = raw[:, h * HEAD_PAD:h * HEAD_PAD + LANES]
            xr = raw[:, h * HEAD_PAD + LANES:(h + 1) * HEAD_PAD]
            ss = jnp.sum(xn * xn, axis=1, keepdims=True) + jnp.sum(xr * xr, axis=1, keepdims=True)
            rs = lax.rsqrt(ss * (1.0 / QK_HEAD) + EPS)
            o_ref[:, h * HEAD_PAD:h * HEAD_PAD + LANES] = (xn * rs * gv[:, :LANES]).astype(BF16)
            o_ref[:, h * HEAD_PAD + LANES:(h + 1) * HEAD_PAD] = _rope_fwd(xr * rs * gv[:, LANES:], cv, sv).astype(BF16)

    return pl.pallas_call(
        body, name=name, grid=(T // tb, HP // hw),
        in_specs=[pl.BlockSpec((tb, QL), lambda i, j: (i, 0)), pl.BlockSpec((QL, hw), lambda i, j: (0, j)),
                  pl.BlockSpec((1, HEAD_PAD), lambda i, j: (0, 0)),
                  pl.BlockSpec((tb, LANES), lambda i, j: (i, 0)), pl.BlockSpec((tb, LANES), lambda i, j: (i, 0))],
        out_specs=pl.BlockSpec((tb, hw), lambda i, j: (i, j)), out_shape=SDS((T, HP), BF16),
        compiler_params=_cp("parallel", "parallel"),
    )(qn, w_uq_al, g_al, cos_t, sin_t)


def _q_up_bwd(qn, w_uq_al, g_al, cos_t, sin_t, dq, name):
    T, QL = qn.shape
    HP = w_uq_al.shape[1]
    tb = _tile(T, 512, 8)
    hw = _tile(HP, 1024, HEAD_PAD)

    def body(x_ref, w_ref, g_ref, c_ref, s_ref, dq_ref, o_ref, gg_ref):
        @pl.when((pl.program_id(0) == 0) & (pl.program_id(1) == 0))
        def _():
            gg_ref[...] = jnp.zeros_like(gg_ref)

        raw = _dot(x_ref[...], w_ref[...], NN)
        gv = g_ref[...]
        cv, sv = c_ref[...], s_ref[...]
        for h in range(hw // HEAD_PAD):
            lo, mid, hi = h * HEAD_PAD, h * HEAD_PAD + LANES, (h + 1) * HEAD_PAD
            xn, xr = raw[:, lo:mid], raw[:, mid:hi]
            ss = jnp.sum(xn * xn, axis=1, keepdims=True) + jnp.sum(xr * xr, axis=1, keepdims=True)
            rs = lax.rsqrt(ss * (1.0 / QK_HEAD) + EPS)
            dyn = dq_ref[:, lo:mid]
            dyr = _rope_bwd(dq_ref[:, mid:hi], cv, sv)
            dxn, dxr, ggn, ggr = _head_norm_bwd(xn, xr, rs, gv[:, :LANES], gv[:, LANES:], dyn, dyr)
            o_ref[:, lo:mid] = dxn.astype(BF16)
            o_ref[:, mid:hi] = dxr.astype(BF16)
            gg_ref[:, :LANES] += jnp.sum(ggn, axis=0, keepdims=True)
            gg_ref[:, LANES:] += jnp.sum(ggr, axis=0, keepdims=True)

    return pl.pallas_call(
        body, name=name, grid=(T // tb, HP // hw),
        in_specs=[pl.BlockSpec((tb, QL), lambda i, j: (i, 0)), pl.BlockSpec((QL, hw), lambda i, j: (0, j)),
                  pl.BlockSpec((1, HEAD_PAD), lambda i, j: (0, 0)),
                  pl.BlockSpec((tb, LANES), lambda i, j: (i, 0)), pl.BlockSpec((tb, LANES), lambda i, j: (i, 0)),
                  pl.BlockSpec((tb, hw), lambda i, j: (i, j))],
        out_specs=[pl.BlockSpec((tb, hw), lambda i, j: (i, j)), pl.BlockSpec((1, HEAD_PAD), lambda i, j: (0, 0))],
        out_shape=[SDS((T, HP), BF16), SDS((1, HEAD_PAD), F32)],
        compiler_params=_cp("arbitrary", "arbitrary"),
    )(qn, w_uq_al, g_al, cos_t, sin_t, dq)


def _kv_up_fwd(kvn, w_ukv, proj, g_al, cos_t, sin_t, kpe_col, name):
    T, KVL = kvn.shape
    HP = w_ukv.shape[1]
    tb = _tile(T, 512, 8)
    hw = _tile(HP, 1024, HEAD_PAD)
    nh = hw // HEAD_PAD

    def body(x_ref, w_ref, kpe_ref, g_ref, c_ref, s_ref, k_ref, v_ref):
        raw = _dot(x_ref[...], w_ref[...], NN)
        gv = g_ref[...]
        cv, sv = c_ref[...], s_ref[...]
        kpe = kpe_ref[...]
        kpe_ss = jnp.sum(kpe * kpe, axis=1, keepdims=True)
        for h in range(nh):
            lo, mid, hi = h * HEAD_PAD, h * HEAD_PAD + LANES, (h + 1) * HEAD_PAD
            xn = raw[:, lo:mid]
            rs = lax.rsqrt((jnp.sum(xn * xn, axis=1, keepdims=True) + kpe_ss) * (1.0 / QK_HEAD) + EPS)
            k_ref[:, lo:mid] = (xn * rs * gv[:, :LANES]).astype(BF16)
            k_ref[:, mid:hi] = _rope_fwd(kpe * rs * gv[:, LANES:], cv, sv).astype(BF16)
            v_ref[:, h * V_HEAD:(h + 1) * V_HEAD] = raw[:, mid:hi].astype(BF16)

    return pl.pallas_call(
        body, name=name, grid=(T // tb, HP // hw),
        in_specs=[pl.BlockSpec((tb, KVL), lambda i, j: (i, 0)), pl.BlockSpec((KVL, hw), lambda i, j: (0, j)),
                  pl.BlockSpec((tb, LANES), lambda i, j: (i, kpe_col)),
                  pl.BlockSpec((1, HEAD_PAD), lambda i, j: (0, 0)),
                  pl.BlockSpec((tb, LANES), lambda i, j: (i, 0)), pl.BlockSpec((tb, LANES), lambda i, j: (i, 0))],
        out_specs=[pl.BlockSpec((tb, hw), lambda i, j: (i, j)), pl.BlockSpec((tb, nh * V_HEAD), lambda i, j: (i, j))],
        out_shape=[SDS((T, HP), BF16), SDS((T, HP // 2), BF16)],
        compiler_params=_cp("parallel", "parallel"),
    )(kvn, w_ukv, proj, g_al, cos_t, sin_t)


def _kv_up_bwd(kvn, w_ukv, proj, g_al, cos_t, sin_t, dk, dv, kpe_col, name):
    T, KVL = kvn.shape
    HP = w_ukv.shape[1]
    tb = _tile(T, 512, 8)
    hw = _tile(HP, 1024, HEAD_PAD)
    nh = hw // HEAD_PAD

    def body(x_ref, w_ref, kpe_ref, g_ref, c_ref, s_ref, dk_ref, dv_ref, o_ref, dkpe_ref, gg_ref):
        i, j = pl.program_id(0), pl.program_id(1)

        @pl.when((i == 0) & (j == 0))
        def _():
            gg_ref[...] = jnp.zeros_like(gg_ref)

        @pl.when(j == 0)
        def _():
            dkpe_ref[...] = jnp.zeros_like(dkpe_ref)

        raw = _dot(x_ref[...], w_ref[...], NN)
        gv = g_ref[...]
        cv, sv = c_ref[...], s_ref[...]
        kpe = kpe_ref[...]
        kpe_ss = jnp.sum(kpe * kpe, axis=1, keepdims=True)
        for h in range(nh):
            lo, mid, hi = h * HEAD_PAD, h * HEAD_PAD + LANES, (h + 1) * HEAD_PAD
            xn = raw[:, lo:mid]
            rs = lax.rsqrt((jnp.sum(xn * xn, axis=1, keepdims=True) + kpe_ss) * (1.0 / QK_HEAD) + EPS)
            dyn = dk_ref[:, lo:mid]
            dyr = _rope_bwd(dk_ref[:, mid:hi], cv, sv)
            dxn, dxr, ggn, ggr = _head_norm_bwd(xn, kpe, rs, gv[:, :LANES], gv[:, LANES:], dyn, dyr)
            o_ref[:, lo:mid] = dxn.astype(BF16)
            o_ref[:, mid:hi] = dv_ref[:, h * V_HEAD:(h + 1) * V_HEAD].astype(BF16)
            dkpe_ref[...] += dxr
            gg_ref[:, :LANES] += jnp.sum(ggn, axis=0, keepdims=True)
            gg_ref[:, LANES:] += jnp.sum(ggr, axis=0, keepdims=True)

    return pl.pallas_call(
        body, name=name, grid=(T // tb, HP // hw),
        in_specs=[pl.BlockSpec((tb, KVL), lambda i, j: (i, 0)), pl.BlockSpec((KVL, hw), lambda i, j: (0, j)),
                  pl.BlockSpec((tb, LANES), lambda i, j: (i, kpe_col)),
                  pl.BlockSpec((1, HEAD_PAD), lambda i, j: (0, 0)),
                  pl.BlockSpec((tb, LANES), lambda i, j: (i, 0)), pl.BlockSpec((tb, LANES), lambda i, j: (i, 0)),
                  pl.BlockSpec((tb, hw), lambda i, j: (i, j)), pl.BlockSpec((tb, nh * V_HEAD), lambda i, j: (i, j))],
        out_specs=[pl.BlockSpec((tb, hw), lambda i, j: (i, j)), pl.BlockSpec((tb, LANES), lambda i, j: (i, 0)),
                   pl.BlockSpec((1, HEAD_PAD), lambda i, j: (0, 0))],
        out_shape=[SDS((T, HP), BF16), SDS((T, LANES), F32), SDS((1, HEAD_PAD), F32)],
        compiler_params=_cp("arbitrary", "arbitrary"),
    )(kvn, w_ukv, proj, g_al, cos_t, sin_t, dk, dv)


def _attn_fwd(q, k, v, H, name, comm=None):
    B, S, _ = q.shape
    tq = _tile(S, ATTN_BLOCK, 128)
    scale = QK_HEAD ** -0.5

    def body(q_ref, k_ref, v_ref, o_ref, l_ref):
        qi = pl.program_id(2)
        qv = q_ref[...]
        row = qi * tq + lax.broadcasted_iota(jnp.int32, (tq, tq), 0)
        col0 = lax.broadcasted_iota(jnp.int32, (tq, tq), 1)

        def step(j, carry):
            m, l, acc = carry
            off = pl.multiple_of(j * tq, tq)
            kj = k_ref[pl.ds(off, tq), :]
            vj = v_ref[pl.ds(off, tq), :]
            s = _dot(qv, kj, NT) * scale
            s = jnp.where(col0 + j * tq <= row, s, -jnp.inf)
            m_new = jnp.maximum(m, jnp.max(s, axis=1, keepdims=True))
            alpha = jnp.exp(m - m_new)
            p = jnp.exp(s - m_new)
            l = alpha * l + jnp.sum(p, axis=1, keepdims=True)
            acc = alpha * acc + _dot(p.astype(BF16), vj, NN)
            return m_new, l, acc

        init = (jnp.full((tq, 1), -1e30, F32), jnp.zeros((tq, 1), F32), jnp.zeros((tq, V_HEAD), F32))
        m, l, acc = lax.fori_loop(0, qi + 1, step, init)
        o_ref[...] = (acc / l).astype(BF16)
        l_ref[...] = m + jnp.log(l)

    grid = (B, H, S // tq)
    in_specs = [pl.BlockSpec((None, tq, HEAD_PAD), lambda b, h, i: (b, i, h)),
                pl.BlockSpec((None, S, HEAD_PAD), lambda b, h, i: (b, 0, h)),
                pl.BlockSpec((None, S, V_HEAD), lambda b, h, i: (b, 0, h))]
    out_specs = [pl.BlockSpec((None, tq, V_HEAD), lambda b, h, i: (b, i, h)),
                 pl.BlockSpec((None, None, tq, 1), lambda b, h, i: (b, h, i, 0))]
    out_shape = [SDS((B, S, H * V_HEAD), BF16), SDS((B, H, S, 1), F32)]
    if comm is None:
        return pl.pallas_call(
            body, name=name, grid=grid, in_specs=in_specs, out_specs=out_specs, out_shape=out_shape,
            compiler_params=_cp("parallel", "parallel", "parallel"),
        )(q, k, v)

    def first_last():
        b, h, i = pl.program_id(0), pl.program_id(1), pl.program_id(2)
        return (b == 0) & (h == 0) & (i == 0), (b == B - 1) & (h == H - 1) & (i == grid[2] - 1)

    outs = pl.pallas_call(
        _carry(body, comm, 3, 2, 0, first_last), name=name, grid=grid,
        in_specs=in_specs + [HBM_SPEC] * len(comm.ins), out_specs=out_specs + [HBM_SPEC] * len(comm.out_shape),
        out_shape=out_shape + comm.out_shape, scratch_shapes=comm.scratch,
        compiler_params=_cp("arbitrary", "arbitrary", "arbitrary"),
    )(q, k, v, *comm.ins)
    return outs[0], outs[1], comm.results(outs[2:])


def _attn_bwd(q, k, v, o, lse, do, H, name, comm=None):
    B, S, _ = q.shape
    tq = _tile(S, ATTN_BLOCK, 128)
    nq = S // tq
    scale = QK_HEAD ** -0.5

    def body(q_ref, k_ref, v_ref, o_ref, l_ref, do_ref, dq_ref, dk_ref, dv_ref, delta_ref):
        dq_ref[...] = jnp.zeros_like(dq_ref)
        for i in range(nq):
            rows = slice(i * tq, (i + 1) * tq)
            delta_ref[rows, :] = jnp.sum(do_ref[rows, :].astype(F32) * o_ref[rows, :].astype(F32), axis=1, keepdims=True)
        row0 = lax.broadcasted_iota(jnp.int32, (tq, tq), 0)
        col0 = lax.broadcasted_iota(jnp.int32, (tq, tq), 1)

        def kv_step(j, carry):
            offk = pl.multiple_of(j * tq, tq)
            kj = k_ref[pl.ds(offk, tq), :]
            vj = v_ref[pl.ds(offk, tq), :]

            def q_step(i, acc):
                dk_acc, dv_acc = acc
                offq = pl.multiple_of(i * tq, tq)
                qi = q_ref[pl.ds(offq, tq), :]
                doi = do_ref[pl.ds(offq, tq), :]
                s = _dot(qi, kj, NT) * scale
                p = jnp.where(col0 + j * tq <= row0 + i * tq, jnp.exp(s - l_ref[pl.ds(offq, tq), :]), 0.0)
                dv_acc = dv_acc + _dot(p.astype(BF16), doi, TN)
                dp = _dot(doi, vj, NT)
                ds = (p * (dp - delta_ref[pl.ds(offq, tq), :]) * scale).astype(BF16)
                dk_acc = dk_acc + _dot(ds, qi, TN)
                dq_ref[pl.ds(offq, tq), :] += _dot(ds, kj, NN)
                return dk_acc, dv_acc

            dk_acc, dv_acc = lax.fori_loop(
                j, nq, q_step, (jnp.zeros((tq, HEAD_PAD), F32), jnp.zeros((tq, V_HEAD), F32)))
            dk_ref[pl.ds(offk, tq), :] = dk_acc
            dv_ref[pl.ds(offk, tq), :] = dv_acc
            return carry

        lax.fori_loop(0, nq, kv_step, 0)

    qk_spec = pl.BlockSpec((None, S, HEAD_PAD), lambda b, h: (b, 0, h))
    v_spec = pl.BlockSpec((None, S, V_HEAD), lambda b, h: (b, 0, h))
    in_specs = [qk_spec, qk_spec, v_spec, v_spec, pl.BlockSpec((None, None, S, 1), lambda b, h: (b, h, 0, 0)), v_spec]
    out_specs = [qk_spec, qk_spec, v_spec]
    out_shape = [SDS((B, S, H * HEAD_PAD), F32), SDS((B, S, H * HEAD_PAD), F32), SDS((B, S, H * V_HEAD), F32)]
    scratch = [pltpu.VMEM((S, 1), F32)]
    if comm is None:
        return pl.pallas_call(
            body, name=name, grid=(B, H), in_specs=in_specs, out_specs=out_specs, out_shape=out_shape,
            scratch_shapes=scratch, compiler_params=_cp("parallel", "parallel"),
        )(q, k, v, o, lse, do)

    def first_last():
        b, h = pl.program_id(0), pl.program_id(1)
        return (b == 0) & (h == 0), (b == B - 1) & (h == H - 1)

    outs = pl.pallas_call(
        _carry(body, comm, 6, 3, 1, first_last), name=name, grid=(B, H),
        in_specs=in_specs + [HBM_SPEC] * len(comm.ins), out_specs=out_specs + [HBM_SPEC] * len(comm.out_shape),
        out_shape=out_shape + comm.out_shape, scratch_shapes=scratch + comm.scratch,
        compiler_params=_cp("arbitrary", "arbitrary"),
    )(q, k, v, o, lse, do, *comm.ins)
    return outs[0], outs[1], outs[2], comm.results(outs[3:])


def _natural(sm):
    nd, R, n = sm.shape
    return jnp.transpose(sm, (1, 0, 2)).reshape(R, nd * n)


def _col_shards(full):
    R, N = full.shape
    return jnp.transpose(full.reshape(R, N_DEV, N // N_DEV), (1, 0, 2))


def _pad_heads(w, H):
    R = w.shape[0]
    return jnp.pad(w.reshape(R, H, QK_HEAD), ((0, 0), (0, 0), (0, HEAD_PAD - QK_HEAD))).reshape(R, H * HEAD_PAD)


def _unpad_heads(w, H):
    R = w.shape[0]
    return w.reshape(R, H, HEAD_PAD)[:, :, :QK_HEAD].reshape(R, H * QK_HEAD)


def _pad_gain(g):
    return jnp.pad(g, ((0, 0), (0, HEAD_PAD - QK_HEAD)))


def _rope_tables(positions):
    inv_freq = 1.0 / (ROPE_THETA ** (jnp.arange(0, ROPE, 2, dtype=F32) / ROPE))
    ang = positions.astype(F32).reshape(-1, 1) * inv_freq
    cos, sin = jnp.cos(ang), jnp.sin(ang)
    zeros = jnp.zeros((ang.shape[0], LANES - ROPE), F32)
    return jnp.concatenate([cos, cos, zeros], axis=1), jnp.concatenate([-sin, sin, zeros], axis=1)


def kernel(x, c, positions, w_ada, b_ada, g_norm1, w_in, g_v, w_s, b_s, g_q_lat, g_kv_lat, w_uq, w_ukv, g_qn, g_kn, w_branch_a, w_branch_b, w_out, g_norm2, w_ff1, w_ff2, loss_target, m_w_ada, m_b_ada, m_g_norm1, m_w_in, m_g_v, m_w_s, m_b_s, m_g_q_lat, m_g_kv_lat, m_w_uq, m_w_ukv, m_g_qn, m_g_kn, m_w_branch_a, m_w_branch_b, m_w_out, m_g_norm2, m_w_ff1, m_w_ff2, v_w_ada, v_b_ada, v_g_norm1, v_w_in, v_g_v, v_w_s, v_b_s, v_g_q_lat, v_g_kv_lat, v_w_uq, v_w_ukv, v_g_qn, v_g_kn, v_w_branch_a, v_w_branch_b, v_w_out, v_g_norm2, v_w_ff1, v_w_ff2):
    B, S, D = x.shape
    T = B * S
    GW = g_v.shape[-1]
    QL = g_q_lat.shape[-1]
    KVL = g_kv_lat.shape[-1]
    H = w_uq.shape[-1] * N_DEV // QK_HEAD
    IN = w_in.shape[-1] * N_DEV
    OFF_GATE = IN - 2 * D
    IN_AL = _round_up(2 * D + OFF_GATE + (LANES - ROPE), 512)
    assert OFF_GATE == 2 * GW + QL + KVL + ROPE
    assert (2 * D) % GW == 0 and (2 * D + 2 * GW) % QL == 0 and (2 * D + 2 * GW + QL) % KVL == 0
    kpe_col = (2 * D + 2 * GW + QL + KVL) // LANES

    xi, yi, ci = _here()
    dev = 4 * xi + 2 * yi + ci
    c_idx = jnp.reshape(ci, (1,)).astype(jnp.int32)

    big = [w_in, w_uq, w_ukv, w_branch_a, w_branch_b, w_out, w_ff1, w_ff2]
    s_in, s_uq, s_ukv, s_ba, s_bb, s_out, s_ff1, s_ff2 = [w[0].astype(BF16) for w in big]
    ((g_in,),) = _run_comm([_GatherJob([s_in])], "ag_w_in")
    w_in_nat = _natural(g_in)
    w_al = jnp.concatenate([w_in_nat[:, OFF_GATE:], w_in_nat[:, :OFF_GATE], jnp.zeros((D, IN_AL - IN), BF16)], axis=1)

    n_ada = w_ada.shape[-1]
    c_all = _all_gather_small(c, "ag_c").reshape(N_DEV * B, D)
    b_cols = lax.dynamic_slice(b_ada, (0, dev * n_ada), (1, n_ada))
    mod_cols = _ada_fwd(c_all, w_ada[0], b_cols, "ada_fwd")
    mod_all = _all_gather_small(mod_cols, "ag_mod")
    mod_mine = lax.dynamic_slice(mod_all, (0, dev * B, 0), (N_DEV, B, n_ada))
    mod_mine = jnp.transpose(mod_mine, (1, 0, 2)).reshape(B, 6, 1, D)
    sh1, sc1, ga1, sh2, sc2, ga2 = [mod_mine[:, k] for k in range(6)]

    cos_t, sin_t = _rope_tables(positions)
    g_qn_al, g_kn_al = _pad_gain(g_qn), _pad_gain(g_kn)
    b_col = b_s[0].reshape(GROUPS, CHUNK, 1)
    gw = GW // GROUPS
    sel = (jnp.arange(GW)[:, None] // gw == jnp.arange(LANES)[None, :]).astype(F32)

    h1, rstd1 = _norm_mod_fwd(x, g_norm1, sh1, sc1, "norm1_fwd")
    h1f = h1.reshape(T, D)
    proj, ((g_uq, g_ukv, g_ba, g_bb, g_out),) = _matmul(
        h1f, w_al, mode="nn", out_dtypes=[F32], name="mm_proj", comm=_Comm([_GatherJob([s_uq, s_ukv, s_ba, s_bb, s_out])]))
    w_uq_al = _pad_heads(_natural(g_uq), H)
    w_ukv_f = _natural(g_ukv)
    w_ba_f = _natural(g_ba)
    w_bb_f = g_bb.reshape(-1, D)
    w_out_f = g_out.reshape(-1, D)
    ga_act, rstd_v = _gmlp_fwd(proj, g_v, w_s[0], b_col, D, GW, "gmlp_fwd")
    y_a = _matmul(ga_act, w_ba_f, mode="nn", out_dtypes=[F32], name="mm_ya")
    qn, kvn, rstd_q, rstd_kv = _lat_norm_fwd(proj, g_q_lat, g_kv_lat, D, GW, QL, KVL, "latnorm_fwd")
    q_al = _q_up_fwd(qn, w_uq_al, g_qn_al, cos_t, sin_t, "q_up_fwd")
    k_al, v_al = _kv_up_fwd(kvn, w_ukv_f, proj, g_kn_al, cos_t, sin_t, kpe_col, "kv_up_fwd")
    q3, k3, v3 = q_al.reshape(B, S, -1), k_al.reshape(B, S, -1), v_al.reshape(B, S, -1)
    attn, lse, ((g_ff1,),) = _attn_fwd(q3, k3, v3, H, "attn_fwd", comm=_Comm([_GatherJob([s_ff1])]))
    w_ff1_f = _natural(g_ff1)
    attn_f = attn.reshape(T, H * V_HEAD)
    y_b = _matmul(attn_f, w_bb_f, mode="nn", out_dtypes=[F32], name="mm_yb")
    mixed = _mix_fwd(proj, y_a, y_b, D, "mix_fwd")
    o = _matmul(mixed, w_out_f, mode="nn", out_dtypes=[F32], name="mm_o")
    x1, h2, rstd2 = _norm_mod_fwd(x, g_norm2, sh2, sc2, "norm2_fwd", o=o.reshape(B, S, D), ga=ga1)

    def relu_sq(acc):
        r = jnp.maximum(acc, 0.0)
        return r * r, r

    (a_act, r_act), ((g_ff2,),) = _matmul(h2.reshape(T, D), w_ff1_f, mode="nn", out_dtypes=[BF16, BF16], name="mm_ff1",
                                          epi=relu_sq, comm=_Comm([_GatherJob([s_ff2])]))
    w_ff2_f = g_ff2.reshape(-1, D)
    ff = _matmul(a_act, w_ff2_f, mode="nn", out_dtypes=[F32], name="mm_ff2")
    loss_part, dy, dff, d_ga2 = _loss_head(x1, ff.reshape(B, S, D), loss_target, ga2, "loss_head")
    loss = lax.psum(loss_part[0, 0], ("x", "y", "c"))

    dff_f = dff.reshape(T, D)
    df1 = _matmul(dff_f, w_ff2_f, mode="nt", out_dtypes=[BF16], name="mm_da", epi=lambda acc, r: (acc * (2.0 * r.astype(F32)),),
                  extras=(r_act,))
    gs_ff2 = _matmul(a_act, dff_f, mode="tn", out_dtypes=[BF16], name="mm_gw_ff2").reshape(N_DEV, -1, D)
    gw_ff1, ((sib_ff2,),) = _matmul(h2.reshape(T, D), df1, mode="tn", out_dtypes=[BF16], name="mm_gw_ff1",
                                    comm=_Comm([_SiblingJob([gs_ff2])]))
    cs_ff2 = _chip_sum(gs_ff2, sib_ff2, c_idx, "chip_sum_w_ff2")
    gs_ff1 = _col_shards(gw_ff1)
    dh2, ((parts_ff2,), (sib_ff1,)) = _matmul(df1, w_ff1_f, mode="nt", out_dtypes=[F32], name="mm_dh2",
                                              comm=_Comm([_ChipsJob([cs_ff2]), _SiblingJob([gs_ff1])]))
    cs_ff1 = _chip_sum(gs_ff1, sib_ff1, c_idx, "chip_sum_w_ff1")
    dx1, d_sh2, d_sc2, gg_norm2, d_ga1, do = _norm_mod_bwd(
        dh2.reshape(B, S, D), x1, rstd2, sc2, g_norm2, dy, "norm2_bwd", o=o.reshape(B, S, D), ga=ga1)

    do_f = do.reshape(T, D)
    dmixed = _matmul(do_f, w_out_f, mode="nt", out_dtypes=[F32], name="mm_dmixed")
    gw_out = _matmul(mixed, do_f, mode="tn", out_dtypes=[BF16], name="mm_gw_out")
    dy_a, dy_b, dgate_a, dgate_b = _mix_bwd(proj, y_a, y_b, dmixed, D, "mix_bwd")
    gw_ba = _matmul(ga_act, dy_a, mode="tn", out_dtypes=[BF16], name="mm_gw_ba")
    dga_act = _matmul(dy_a, w_ba_f, mode="nt", out_dtypes=[F32], name="mm_dga")
    gw_bb = _matmul(attn_f, dy_b, mode="tn", out_dtypes=[BF16], name="mm_gw_bb")
    dattn = _matmul(dy_b, w_bb_f, mode="nt", out_dtypes=[BF16], name="mm_dattn")
    d_uv, gg_ws, gg_bs_t, gg_gv = _gmlp_bwd(proj, dga_act, rstd_v, g_v, w_s[0], b_col, sel, D, GW, "gmlp_bwd")
    dq, dk, dv, ((parts_ff1,),) = _attn_bwd(q3, k3, v3, attn, lse, dattn.reshape(B, S, -1), H, "attn_bwd",
                                            comm=_Comm([_ChipsJob([cs_ff1])]))
    dq_raw, gg_qn = _q_up_bwd(qn, w_uq_al, g_qn_al, cos_t, sin_t, dq.reshape(T, -1), "q_up_bwd")
    dkv_raw, dkpe, gg_kn = _kv_up_bwd(kvn, w_ukv_f, proj, g_kn_al, cos_t, sin_t, dk.reshape(T, -1), dv.reshape(T, -1),
                                      kpe_col, "kv_up_bwd")
    gw_uq_al = _matmul(qn, dq_raw, mode="tn", out_dtypes=[BF16], name="mm_gw_uq")
    dqn = _matmul(dq_raw, w_uq_al, mode="nt", out_dtypes=[F32], name="mm_dqn")
    gw_ukv = _matmul(kvn, dkv_raw, mode="tn", out_dtypes=[BF16], name="mm_gw_ukv")
    dkvn = _matmul(dkv_raw, w_ukv_f, mode="nt", out_dtypes=[F32], name="mm_dkvn")
    d_qlat, d_kvlat, d_kpe16, gg_qlat, gg_kvlat = _lat_norm_bwd(
        proj, dqn, dkvn, dkpe, rstd_q, rstd_kv, g_q_lat, g_kv_lat, D, GW, QL, KVL, "latnorm_bwd")
    dproj = jnp.concatenate(
        [dgate_a, dgate_b, d_uv, d_qlat, d_kvlat, d_kpe16, jnp.zeros((T, IN_AL - (2 * D + OFF_GATE + LANES - ROPE)), BF16)], axis=1)
    mid_names = ["w_uq", "w_ukv", "w_branch_a", "w_branch_b", "w_out"]
    gs_mid = [_col_shards(_unpad_heads(gw_uq_al, H)), _col_shards(gw_ukv), _col_shards(gw_ba),
              gw_bb.reshape(N_DEV, -1, D), gw_out.reshape(N_DEV, -1, D)]
    (sib_mid,) = _run_comm([_SiblingJob(gs_mid)], "rs_sibling_mid")
    cs_mid = [_chip_sum(g, r, c_idx, "chip_sum_" + nm) for g, r, nm in zip(gs_mid, sib_mid, mid_names)]
    gw_al, (parts_mid,) = _matmul(h1f, dproj, mode="tn", out_dtypes=[BF16], name="mm_gw_in", comm=_Comm([_ChipsJob(cs_mid)]))
    gw_in_nat = jnp.concatenate([gw_al[:, 2 * D:2 * D + OFF_GATE], gw_al[:, :2 * D]], axis=1)
    gs_in = _col_shards(gw_in_nat)
    ((sib_in,),) = _run_comm([_SiblingJob([gs_in])], "rs_sibling_in")
    cs_in = _chip_sum(gs_in, sib_in, c_idx, "chip_sum_w_in")
    dh1, ((parts_in,),) = _matmul(dproj, w_al, mode="nt", out_dtypes=[F32], name="mm_dh1", comm=_Comm([_ChipsJob([cs_in])]))
    grad_x, d_sh1, d_sc1, gg_norm1 = _norm_mod_bwd(dh1.reshape(B, S, D), x, rstd1, sc1, g_norm1, dx1, "norm1_bwd")

    dmod_mine = jnp.concatenate([d_sh1, d_sc1, d_ga1, d_sh2, d_sc2, d_ga2], axis=2).reshape(B, 6 * D)
    dmod_all = _all_gather_small(dmod_mine, "ag_dmod").reshape(N_DEV * B, 6 * D)
    dmod_cols = lax.dynamic_slice(dmod_all, (0, dev * n_ada), (N_DEV * B, n_ada))
    ada_out = _ada_bwd_adam(c_all, dmod_cols, w_ada[0], m_w_ada[0], v_w_ada[0], "ada_bwd_adam")
    nb_rows = 8
    bada_out = _adam_from_parts(b_ada.reshape(nb_rows, -1), m_b_ada.reshape(nb_rows, -1), v_b_ada.reshape(nb_rows, -1),
                                dmod_all.reshape(N_DEV * B, nb_rows, -1), "adam_b_ada")

    names = ["w_in", "w_uq", "w_ukv", "w_branch_a", "w_branch_b", "w_out", "w_ff1", "w_ff2"]
    parts = [parts_in] + list(parts_mid) + [parts_ff1, parts_ff2]
    ms = [m_w_in, m_w_uq, m_w_ukv, m_w_branch_a, m_w_branch_b, m_w_out, m_w_ff1, m_w_ff2]
    vs = [v_w_in, v_w_uq, v_w_ukv, v_w_branch_a, v_w_branch_b, v_w_out, v_w_ff1, v_w_ff2]
    big_out = {nm: _adam_from_parts(w[0], m[0], v[0], p, "adam_" + nm) for nm, w, m, v, p in zip(names, big, ms, vs, parts)}

    small = [("g_norm1", g_norm1, m_g_norm1, v_g_norm1, gg_norm1),
             ("g_v", g_v, m_g_v, v_g_v, gg_gv),
             ("w_s", w_s, m_w_s, v_w_s, gg_ws),
             ("b_s", b_s, m_b_s, v_b_s, jnp.transpose(gg_bs_t[:, :GROUPS])),
             ("g_q_lat", g_q_lat, m_g_q_lat, v_g_q_lat, gg_qlat),
             ("g_kv_lat", g_kv_lat, m_g_kv_lat, v_g_kv_lat, gg_kvlat),
             ("g_qn", g_qn, m_g_qn, v_g_qn, gg_qn[:, :QK_HEAD]),
             ("g_kn", g_kn, m_g_kn, v_g_kn, gg_kn[:, :QK_HEAD]),
             ("g_norm2", g_norm2, m_g_norm2, v_g_norm2, gg_norm2)]
    sizes = [w.size for _, w, _, _, _ in small]
    n_small = sum(sizes)
    n_small_pad = _round_up(n_small, 8 * LANES)

    def flat_cat(arrs):
        return jnp.pad(jnp.concatenate([a.reshape(-1) for a in arrs]), (0, n_small_pad - n_small))

    part_small = _all_gather_small(flat_cat([t[4] for t in small]), "ag_small_grads")
    small_out = _adam_from_parts(
        flat_cat([t[1] for t in small]).reshape(8, -1), flat_cat([t[2] for t in small]).reshape(8, -1),
        flat_cat([t[3] for t in small]).reshape(8, -1), part_small.reshape(N_DEV, 8, -1), "adam_small")
    offs = [sum(sizes[:i]) for i in range(len(sizes))]

    def small_piece(kind, i):
        return small_out[kind].reshape(-1)[offs[i]:offs[i] + sizes[i]].reshape(small[i][1].shape)

    small_idx = {t[0]: i for i, t in enumerate(small)}
    order = ["w_ada", "b_ada", "g_norm1", "w_in", "g_v", "w_s", "b_s", "g_q_lat", "g_kv_lat", "w_uq", "w_ukv", "g_qn", "g_kn",
             "w_branch_a", "w_branch_b", "w_out", "g_norm2", "w_ff1", "w_ff2"]

    def result(kind, nm):
        if nm == "w_ada":
            return ada_out[kind][None]
        if nm == "b_ada":
            return bada_out[kind].reshape(b_ada.shape)
        if nm in small_idx:
            return small_piece(kind, small_idx[nm])
        return big_out[nm][kind][None]

    outs = [loss, grad_x]
    for kind in range(4):
        outs += [result(kind, nm) for nm in order]
    return tuple(outs)
```

```python
import functools
import math

import jax
import jax.numpy as jnp
from jax import lax
from jax.experimental import pallas as pl
from jax.experimental.pallas import tpu as pltpu

F32 = jnp.float32
BF16 = jnp.bfloat16
SDS = jax.ShapeDtypeStruct
MESH = pl.DeviceIdType.MESH

N_DEV = 8
N_CHIP = 4
CHUNK = 128
GROUPS = 8
NOPE = 128
ROPE = 64
HALF_ROPE = ROPE // 2
QK_HEAD = NOPE + ROPE
V_HEAD = 128
HEAD_PAD = 256
LANES = 128
ROPE_THETA = 10000.0
EPS = 1e-6
INV_SQRT2 = 1.0 / math.sqrt(2.0)
INV_SQRT_2PI = 1.0 / math.sqrt(2.0 * math.pi)

ADAM_LR = 0.001
ADAM_B1 = 0.9
ADAM_B2 = 0.999
ADAM_EPS = 1e-08
ADAM_WD = 0.01
ADAM_STEP = 10

V7X_VMEM_LIMIT_BYTES = 56 * 1024 * 1024
MM_TILE = 1024
ATTN_BLOCK = 512
ROW_BLOCK = 128

NN = (((1,), (0,)), ((), ()))
NT = (((1,), (1,)), ((), ()))
TN = (((0,), (0,)), ((), ()))


def _tile(n, pref, mult):
    t = min(pref, n)
    t -= t % mult
    while t >= mult:
        if n % t == 0:
            return t
        t -= mult
    return n


def _round_up(n, m):
    return (n + m - 1) // m * m


def _cp(*sem):
    return pltpu.CompilerParams(dimension_semantics=sem, vmem_limit_bytes=V7X_VMEM_LIMIT_BYTES)


def _dot(a, b, dims):
    return lax.dot_general(a, b, dims, preferred_element_type=F32)


def _gelu(x):
    return 0.5 * x * (1.0 + lax.erf(x * INV_SQRT2))


def _gelu_grad(x):
    return 0.5 * (1.0 + lax.erf(x * INV_SQRT2)) + x * jnp.exp(-0.5 * x * x) * INV_SQRT_2PI


def _sigmoid(x):
    return 1.0 / (1.0 + jnp.exp(-x))


def _matmul(a, b, *, mode, out_dtypes, name, epi=None, extras=(), comm=None, b_shards=False, out_shards=False):
    if mode == "tn":
        K, M = a.shape
    else:
        M, K = a.shape
    n_shard = None
    if b_shards:
        _, R, n_shard = b.shape
        N, Kb = (R, N_DEV * n_shard) if mode == "nt" else (N_DEV * n_shard, R)
    elif mode == "nt":
        N, Kb = b.shape
    else:
        Kb, N = b.shape
    assert K == Kb, (name, a.shape, b.shape)
    tm = _tile(M, MM_TILE, 128)
    tn = _tile(n_shard if (b_shards and mode != "nt") else N, MM_TILE, 128)
    tk = _tile(n_shard if (b_shards and mode == "nt") else K, MM_TILE, 128)
    if out_shards:
        assert mode == "tn" and not extras and len(out_dtypes) == 1
        n_shard = N // N_DEV
        tn = _tile(n_shard, MM_TILE, 128)
    nk = K // tk
    n_extra = len(extras)
    n_out = len(out_dtypes)
    dims = {"nn": NN, "nt": NT, "tn": TN}[mode]

    def body(a_ref, b_ref, *rest):
        extra_refs = rest[:n_extra]
        out_refs = rest[n_extra:n_extra + n_out]
        acc_ref = rest[n_extra + n_out]
        k = pl.program_id(2)

        def product():
            return _dot(a_ref[...].astype(BF16), b_ref[...].astype(BF16), dims)

        def finish(acc):
            res = (acc,) if epi is None else epi(acc, *[e[...] for e in extra_refs])
            for o_ref, r in zip(out_refs, res):
                o_ref[...] = r.astype(o_ref.dtype)

        if nk == 1:
            finish(product())
            return

        @pl.when(k == 0)
        def _():
            acc_ref[...] = product()

        if nk > 2:
            @pl.when((k > 0) & (k < nk - 1))
            def _():
                acc_ref[...] += product()

        @pl.when(k == nk - 1)
        def _():
            finish(acc_ref[...] + product())

    if mode == "tn":
        a_spec = pl.BlockSpec((tk, tm), lambda i, j, k: (k, i))
    else:
        a_spec = pl.BlockSpec((tm, tk), lambda i, j, k: (i, k))
    if b_shards and mode == "nt":
        per = n_shard // tk
        b_spec = pl.BlockSpec((None, tn, tk), lambda i, j, k: (k // per, j, k % per))
    elif b_shards:
        per = n_shard // tn
        b_spec = pl.BlockSpec((None, tk, tn), lambda i, j, k: (j // per, k, j % per))
    elif mode == "nt":
        b_spec = pl.BlockSpec((tn, tk), lambda i, j, k: (j, k))
    else:
        b_spec = pl.BlockSpec((tk, tn), lambda i, j, k: (k, j))
    mn_spec = pl.BlockSpec((tm, tn), lambda i, j, k: (i, j))
    grid = (M // tm, N // tn, nk)
    in_specs = [a_spec, b_spec] + [mn_spec] * n_extra
    if out_shards:
        per_out = n_shard // tn
        out_specs = [pl.BlockSpec((None, tm, tn), lambda i, j, k: (j // per_out, i, j % per_out))]
        out_shape = [SDS((N_DEV, M, n_shard), out_dtypes[0])]
    else:
        out_specs = [mn_spec] * n_out
        out_shape = [SDS((M, N), dt) for dt in out_dtypes]
    scratch = [pltpu.VMEM((tm, tn), F32)]
    if comm is None:
        outs = pl.pallas_call(
            body, name=name, grid=grid, in_specs=in_specs, out_specs=out_specs, out_shape=out_shape,
            scratch_shapes=scratch, compiler_params=_cp("parallel", "parallel", "arbitrary"),
        )(a, b, *extras)
        return outs[0] if n_out == 1 else outs

    def first_last():
        i, j, k = pl.program_id(0), pl.program_id(1), pl.program_id(2)
        return ((i == 0) & (j == 0) & (k == 0),
                (i == grid[0] - 1) & (j == grid[1] - 1) & (k == nk - 1))

    outs = pl.pallas_call(
        _carry(body, comm, 2 + n_extra, n_out, 1, first_last), name=name, grid=grid,
        in_specs=in_specs + [HBM_SPEC] * len(comm.ins), out_specs=out_specs + [HBM_SPEC] * len(comm.out_shape),
        out_shape=out_shape + comm.out_shape, scratch_shapes=scratch + comm.scratch,
        compiler_params=_cp("arbitrary", "arbitrary", "arbitrary"),
    )(a, b, *extras, *comm.ins)
    res = outs[0] if n_out == 1 else list(outs[:n_out])
    return res, comm.results(outs[n_out:])


def _here():
    return lax.axis_index("x"), lax.axis_index("y"), lax.axis_index("c")


def _other_chips(x, y):
    return [(1 - x, y), (x, 1 - y), (1 - x, 1 - y)]


def _all_gather_small(v, name):
    shape = v.shape
    n = v.size
    n_pad = _round_up(n, 8 * LANES)
    flat = jnp.pad(v.reshape(-1), (0, n_pad - n)).reshape(8, n_pad // 8)
    m_per, cols = flat.shape

    def body(x_ref, out_ref, send_sems, recv_sems, local_sem):
        x, y, c = _here()
        me, sibling = (x, y, c), (x, y, 1 - c)
        chips = _other_chips(x, y)

        def rows(px, py, pc):
            return out_ref.at[pl.ds((4 * px + 2 * py + pc) * m_per, m_per), :]

        def copy(k, block, to, src=None):
            return pltpu.make_async_remote_copy(
                src_ref=rows(*block) if src is None else src, dst_ref=rows(*block),
                send_sem=send_sems.at[k], recv_sem=recv_sems.at[k], device_id=to, device_id_type=MESH)

        mine = pltpu.make_async_copy(x_ref, rows(*me), local_sem)
        mine.start()
        first = [copy(0, me, sibling, src=x_ref)]
        first += [copy(1 + j, me, (*chip, c), src=x_ref) for j, chip in enumerate(chips)]
        for cp in first:
            cp.start()
        passed = [copy(4 + j, (*chip, c), sibling) for j, chip in enumerate(chips)]
        for j, chip in enumerate(chips):
            copy(1 + j, (*chip, c), me).wait_recv()
            passed[j].start()
        copy(0, sibling, me).wait_recv()
        for j, chip in enumerate(chips):
            copy(4 + j, (*chip, 1 - c), me).wait_recv()
        for cp in first + passed:
            cp.wait_send()
        mine.wait()

    out = pl.pallas_call(
        body, name=name,
        out_shape=SDS((N_DEV * m_per, cols), F32),
        in_specs=[pl.BlockSpec(memory_space=pltpu.VMEM)],
        out_specs=pl.BlockSpec(memory_space=pltpu.VMEM),
        scratch_shapes=[pltpu.SemaphoreType.DMA((7,)), pltpu.SemaphoreType.DMA((7,)), pltpu.SemaphoreType.DMA],
        compiler_params=pltpu.CompilerParams(vmem_limit_bytes=V7X_VMEM_LIMIT_BYTES),
    )(flat)
    return out.reshape(N_DEV, n_pad)[:, :n].reshape((N_DEV,) + shape)


HBM_SPEC = pl.BlockSpec(memory_space=pltpu.HBM)


class _GatherJob:
    def __init__(self, shards):
        self.ins = list(shards)
        self.nw = len(shards)
        self.out_shape = [SDS((N_DEV,) + s.shape, s.dtype) for s in shards]
        self.scratch = [pltpu.SemaphoreType.DMA((7 * self.nw,)), pltpu.SemaphoreType.DMA((7 * self.nw,)),
                        pltpu.SemaphoreType.DMA((self.nw,))]

    def _parts(self, xs, outs, sems):
        send_sems, recv_sems, local_sems = sems
        x, y, c = _here()

        def blk(w, px, py, pc):
            return outs[w].at[4 * px + 2 * py + pc]

        def copy(w, k, block, to, src=None):
            return pltpu.make_async_remote_copy(
                src_ref=blk(w, *block) if src is None else src, dst_ref=blk(w, *block),
                send_sem=send_sems.at[7 * w + k], recv_sem=recv_sems.at[7 * w + k], device_id=to, device_id_type=MESH)

        me, sibling = (x, y, c), (x, y, 1 - c)
        chips = _other_chips(x, y)
        mine = [pltpu.make_async_copy(xs[w], blk(w, *me), local_sems.at[w]) for w in range(self.nw)]
        first = []
        for w in range(self.nw):
            first.append(copy(w, 0, me, sibling, src=xs[w]))
            first += [copy(w, 1 + j, me, (*chip, c), src=xs[w]) for j, chip in enumerate(chips)]
        return copy, me, sibling, chips, c, mine, first

    def start(self, xs, outs, sems):
        _, _, _, _, _, mine, first = self._parts(xs, outs, sems)
        for cp in mine + first:
            cp.start()

    def finish(self, xs, outs, sems):
        copy, me, sibling, chips, c, mine, first = self._parts(xs, outs, sems)
        passed = []
        for w in range(self.nw):
            for j, chip in enumerate(chips):
                copy(w, 1 + j, (*chip, c), me).wait_recv()
                fwd = copy(w, 4 + j, (*chip, c), sibling)
                fwd.start()
                passed.append(fwd)
        for w in range(self.nw):
            copy(w, 0, sibling, me).wait_recv()
            for j, chip in enumerate(chips):
                copy(w, 4 + j, (*chip, 1 - c), me).wait_recv()
        for cp in first + passed:
            cp.wait_send()
        for cp in mine:
            cp.wait()


class _SiblingJob:
    def __init__(self, grads):
        self.ins = list(grads)
        self.nw = len(grads)
        self.out_shape = [SDS((N_CHIP,) + g.shape[1:], g.dtype) for g in grads]
        self.scratch = [pltpu.SemaphoreType.DMA((N_CHIP * self.nw,)), pltpu.SemaphoreType.DMA((N_CHIP * self.nw,))]

    def _copies(self, gs, outs, sems):
        send_sems, recv_sems = sems
        x, y, c = _here()
        return [pltpu.make_async_remote_copy(
            src_ref=gs[w].at[2 * k + (1 - c)], dst_ref=outs[w].at[k],
            send_sem=send_sems.at[N_CHIP * w + k], recv_sem=recv_sems.at[N_CHIP * w + k],
            device_id=(x, y, 1 - c), device_id_type=MESH) for w in range(self.nw) for k in range(N_CHIP)]

    def start(self, gs, outs, sems):
        for cp in self._copies(gs, outs, sems):
            cp.start()

    def finish(self, gs, outs, sems):
        for cp in self._copies(gs, outs, sems):
            cp.wait()


class _ChipsJob:
    def __init__(self, chip_sums):
        self.ins = list(chip_sums)
        self.nw = len(chip_sums)
        self.out_shape = [SDS(s.shape, s.dtype) for s in chip_sums]
        self.scratch = [pltpu.SemaphoreType.DMA((3 * self.nw,)), pltpu.SemaphoreType.DMA((3 * self.nw,)),
                        pltpu.SemaphoreType.DMA((self.nw,))]

    def _parts(self, srcs, outs, sems):
        send_sems, recv_sems, local_sems = sems
        x, y, c = _here()
        my_chip = 2 * x + y
        local = [pltpu.make_async_copy(srcs[w].at[my_chip], outs[w].at[my_chip], local_sems.at[w]) for w in range(self.nw)]
        sends, landed = [], []
        for w in range(self.nw):
            for j, (px, py) in enumerate(_other_chips(x, y)):
                sems_j = dict(send_sem=send_sems.at[3 * w + j], recv_sem=recv_sems.at[3 * w + j],
                              device_id=(px, py, c), device_id_type=MESH)
                sends.append(pltpu.make_async_remote_copy(
                    src_ref=srcs[w].at[2 * px + py], dst_ref=outs[w].at[my_chip], **sems_j))
                landed.append(pltpu.make_async_remote_copy(
                    src_ref=srcs[w].at[2 * px + py], dst_ref=outs[w].at[2 * px + py], **sems_j))
        return local, sends, landed

    def start(self, srcs, outs, sems):
        local, sends, _ = self._parts(srcs, outs, sems)
        for cp in local + sends:
            cp.start()

    def finish(self, srcs, outs, sems):
        local, sends, landed = self._parts(srcs, outs, sems)
        for cp in landed:
            cp.wait_recv()
        for cp in sends:
            cp.wait_send()
        for cp in local:
            cp.wait()


class _Comm:
    def __init__(self, jobs):
        self.jobs = list(jobs)
        self.ins = [a for j in self.jobs for a in j.ins]
        self.out_shape = [s for j in self.jobs for s in j.out_shape]
        self.scratch = [s for j in self.jobs for s in j.scratch]

    def _split(self, flat, counts):
        out, pos = [], 0
        for n in counts:
            out.append(flat[pos:pos + n])
            pos += n
        return out

    def _each(self, ins, outs, sems):
        return zip(self.jobs, self._split(ins, [len(j.ins) for j in self.jobs]),
                   self._split(outs, [len(j.out_shape) for j in self.jobs]),
                   self._split(sems, [len(j.scratch) for j in self.jobs]))

    def start(self, ins, outs, sems):
        for job, i, o, s in self._each(ins, outs, sems):
            job.start(i, o, s)

    def finish(self, ins, outs, sems):
        for job, i, o, s in self._each(ins, outs, sems):
            job.finish(i, o, s)

    def results(self, flat):
        return [list(r) for r in self._split(list(flat), [len(j.out_shape) for j in self.jobs])]


def _carry(body, comm, n_in, n_out, n_scratch, first_last):
    ci, co = len(comm.ins), len(comm.out_shape)

    def wrapped(*refs):
        ins, rest = refs[:n_in + ci], refs[n_in + ci:]
        outs, scr = rest[:n_out + co], rest[n_out + co:]
        c_ins, c_outs, c_sems = ins[n_in:], outs[n_out:], scr[n_scratch:]
        first, last = first_last()

        @pl.when(first)
        def _():
            comm.start(c_ins, c_outs, c_sems)

        body(*ins[:n_in], *outs[:n_out], *scr[:n_scratch])

        @pl.when(last)
        def _():
            comm.finish(c_ins, c_outs, c_sems)

    return wrapped


def _run_comm(jobs, name):
    comm = _Comm(jobs)

    def body(*refs):
        ci, co = len(comm.ins), len(comm.out_shape)
        comm.start(refs[:ci], refs[ci:ci + co], refs[ci + co:])
        comm.finish(refs[:ci], refs[ci:ci + co], refs[ci + co:])

    outs = pl.pallas_call(
        body, name=name, out_shape=comm.out_shape,
        in_specs=[HBM_SPEC] * len(comm.ins), out_specs=[HBM_SPEC] * len(comm.out_shape),
        scratch_shapes=comm.scratch,
    )(*comm.ins)
    return comm.results(outs)


def _chip_sum(g, recv, c_idx, name):
    _, m, n = g.shape
    tr = _tile(m, max(8, (1 << 20) // max(n, 1) // 8 * 8), 8)

    def body(c_ref, g_ref, r_ref, o_ref):
        o_ref[...] = (g_ref[...].astype(F32) + r_ref[...].astype(F32)).astype(o_ref.dtype)

    grid_spec = pltpu.PrefetchScalarGridSpec(
        num_scalar_prefetch=1, grid=(N_CHIP, m // tr),
        in_specs=[pl.BlockSpec((None, tr, n), lambda k, i, c_ref: (2 * k + c_ref[0], i, 0)),
                  pl.BlockSpec((None, tr, n), lambda k, i, c_ref: (k, i, 0))],
        out_specs=pl.BlockSpec((None, tr, n), lambda k, i, c_ref: (k, i, 0)))
    return pl.pallas_call(
        body, name=name, grid_spec=grid_spec, out_shape=SDS((N_CHIP, m, n), BF16),
        compiler_params=_cp("parallel", "parallel"),
    )(c_idx, g, recv)


def _adam_math(w, g, m, v):
    m = ADAM_B1 * m + (1.0 - ADAM_B1) * g
    v = ADAM_B2 * v + (1.0 - ADAM_B2) * (g * g)
    m_hat = m / (1.0 - ADAM_B1 ** ADAM_STEP)
    v_hat = v / (1.0 - ADAM_B2 ** ADAM_STEP)
    delta = -ADAM_LR * (m_hat / (jnp.sqrt(v_hat) + ADAM_EPS) + ADAM_WD * w)
    return delta, m, v


def _adam_from_parts(w, m, v, parts, name):
    R, C = w.shape
    P = parts.shape[0]
    tr = _tile(R, max(8, (1 << 19) // max(C, 1) // 8 * 8), 8)

    def body(w_ref, m_ref, v_ref, p_ref, g_out, d_out, m_out, v_out):
        g = p_ref[0].astype(F32)
        for k in range(1, P):
            g = g + p_ref[k].astype(F32)
        delta, nm, nv = _adam_math(w_ref[...], g, m_ref[...], v_ref[...])
        g_out[...] = g
        d_out[...] = delta
        m_out[...] = nm
        v_out[...] = nv

    spec = pl.BlockSpec((tr, C), lambda i: (i, 0))
    return pl.pallas_call(
        body, name=name, grid=(R // tr,),
        in_specs=[spec, spec, spec, pl.BlockSpec((P, tr, C), lambda i: (0, i, 0))],
        out_specs=[spec] * 4, out_shape=[SDS((R, C), F32)] * 4,
        compiler_params=_cp("parallel"),
    )(w, m, v, parts)


def _ada_fwd(c_all, w_ada, b_cols, name):
    nb, D = c_all.shape
    n = w_ada.shape[1]
    tn = _tile(n, 512, 128)

    def body(c_ref, w_ref, b_ref, o_ref):
        cv = c_ref[...]
        cond = (cv * _sigmoid(cv)).astype(BF16)
        o_ref[...] = _dot(cond, w_ref[...].astype(BF16), NN) + b_ref[...]

    return pl.pallas_call(
        body, name=name, grid=(n // tn,),
        in_specs=[pl.BlockSpec((nb, D), lambda j: (0, 0)), pl.BlockSpec((D, tn), lambda j: (0, j)),
                  pl.BlockSpec((1, tn), lambda j: (0, j))],
        out_specs=pl.BlockSpec((nb, tn), lambda j: (0, j)), out_shape=SDS((nb, n), F32),
        compiler_params=_cp("parallel"),
    )(c_all, w_ada, b_cols)


def _ada_bwd_adam(c_all, dmod_cols, w, m, v, name):
    nb, D = c_all.shape
    n = w.shape[1]
    tr = _tile(D, 512, 128)
    tn = _tile(n, 1024, 128)

    def body(c_ref, d_ref, w_ref, m_ref, v_ref, g_out, d_out, m_out, v_out):
        cv = c_ref[...]
        cond = (cv * _sigmoid(cv)).astype(BF16)
        g = _dot(cond, d_ref[...].astype(BF16), TN)
        delta, nm, nv = _adam_math(w_ref[...], g, m_ref[...], v_ref[...])
        g_out[...] = g
        d_out[...] = delta
        m_out[...] = nm
        v_out[...] = nv

    spec = pl.BlockSpec((tr, tn), lambda i, j: (i, j))
    return pl.pallas_call(
        body, name=name, grid=(D // tr, n // tn),
        in_specs=[pl.BlockSpec((nb, tr), lambda i, j: (0, i)), pl.BlockSpec((nb, tn), lambda i, j: (0, j)),
                  spec, spec, spec],
        out_specs=[spec] * 4, out_shape=[SDS((D, n), F32)] * 4,
        compiler_params=_cp("parallel", "parallel"),
    )(c_all, dmod_cols, w, m, v)


def _tok(tb, width):
    return pl.BlockSpec((None, tb, width), lambda b, i: (b, i, 0))


def _per_example(width):
    return pl.BlockSpec((None, 1, width), lambda b, i: (b, 0, 0))


def _shared_row(width):
    return pl.BlockSpec((1, width), lambda b, i: (0, 0))


def _norm_mod_fwd(x, g, sh, sc, name, o=None, ga=None):
    B, S, D = x.shape
    tb = _tile(S, ROW_BLOCK, 8)
    fused = o is not None

    def body(*refs):
        if fused:
            x_ref, o_ref, ga_ref, g_ref, sh_ref, sc_ref, x1_ref, h_ref, r_ref = refs
            xv = x_ref[...] + ga_ref[...] * o_ref[...]
            x1_ref[...] = xv
        else:
            x_ref, g_ref, sh_ref, sc_ref, h_ref, r_ref = refs
            xv = x_ref[...]
        rstd = lax.rsqrt(jnp.mean(xv * xv, axis=1, keepdims=True) + EPS)
        y = xv * rstd * g_ref[...]
        h_ref[...] = (y * (1.0 + sc_ref[...]) + sh_ref[...]).astype(BF16)
        r_ref[...] = rstd

    ins = [x] + ([o, ga] if fused else []) + [g, sh, sc]
    in_specs = [_tok(tb, D)] + ([_tok(tb, D), _per_example(D)] if fused else []) + [_shared_row(D), _per_example(D), _per_example(D)]
    out_specs = ([_tok(tb, D)] if fused else []) + [_tok(tb, D), _tok(tb, 1)]
    out_shape = ([SDS((B, S, D), F32)] if fused else []) + [SDS((B, S, D), BF16), SDS((B, S, 1), F32)]
    return pl.pallas_call(
        body, name=name, grid=(B, S // tb), in_specs=in_specs, out_specs=out_specs, out_shape=out_shape,
        compiler_params=_cp("parallel", "parallel"),
    )(*ins)


def _norm_mod_bwd(dh, xin, rstd, sc, g, dres, name, o=None, ga=None):
    B, S, D = xin.shape
    tb = _tile(S, ROW_BLOCK, 8)
    gated = o is not None

    def body(*refs):
        if gated:
            (dh_ref, x_ref, r_ref, sc_ref, g_ref, dres_ref, o_ref, ga_ref,
             dx_ref, dsh_ref, dsc_ref, gg_ref, dga_ref, do_ref) = refs
        else:
            dh_ref, x_ref, r_ref, sc_ref, g_ref, dres_ref, dx_ref, dsh_ref, dsc_ref, gg_ref = refs
        b, i = pl.program_id(0), pl.program_id(1)

        @pl.when(i == 0)
        def _():
            dsh_ref[...] = jnp.zeros_like(dsh_ref)
            dsc_ref[...] = jnp.zeros_like(dsc_ref)
            if gated:
                dga_ref[...] = jnp.zeros_like(dga_ref)

        @pl.when((i == 0) & (b == 0))
        def _():
            gg_ref[...] = jnp.zeros_like(gg_ref)

        dhv = dh_ref[...]
        rs = r_ref[...]
        gv = g_ref[...]
        xhat = x_ref[...] * rs
        dsh_ref[...] += jnp.sum(dhv, axis=0, keepdims=True)
        dsc_ref[...] += jnp.sum(dhv * (xhat * gv), axis=0, keepdims=True)
        dn = dhv * (1.0 + sc_ref[...])
        gg_ref[...] += jnp.sum(dn * xhat, axis=0, keepdims=True)
        dxhat = dn * gv
        cm = jnp.mean(dxhat * xhat, axis=1, keepdims=True)
        dx = dres_ref[...] + rs * (dxhat - xhat * cm)
        dx_ref[...] = dx
        if gated:
            dga_ref[...] += jnp.sum(dx * o_ref[...], axis=0, keepdims=True)
            do_ref[...] = (dx * ga_ref[...]).astype(BF16)

    ins = [dh, xin, rstd, sc, g, dres] + ([o, ga] if gated else [])
    in_specs = [_tok(tb, D), _tok(tb, D), _tok(tb, 1), _per_example(D), _shared_row(D), _tok(tb, D)]
    in_specs += [_tok(tb, D), _per_example(D)] if gated else []
    out_specs = [_tok(tb, D), _per_example(D), _per_example(D), _shared_row(D)]
    out_shape = [SDS((B, S, D), F32), SDS((B, 1, D), F32), SDS((B, 1, D), F32), SDS((1, D), F32)]
    if gated:
        out_specs += [_per_example(D), _tok(tb, D)]
        out_shape += [SDS((B, 1, D), F32), SDS((B, S, D), BF16)]
    return pl.pallas_call(
        body, name=name, grid=(B, S // tb), in_specs=in_specs, out_specs=out_specs, out_shape=out_shape,
        compiler_params=_cp("arbitrary", "arbitrary"),
    )(*ins)


def _loss_head(x1, ff, target, ga2, name):
    B, S, D = x1.shape
    tb = _tile(S, ROW_BLOCK, 8)
    nb, ni = B, S // tb

    def body(x_ref, f_ref, t_ref, ga_ref, loss_ref, dy_ref, dff_ref, dga_ref, acc_ref):
        b, i = pl.program_id(0), pl.program_id(1)

        @pl.when(i == 0)
        def _():
            dga_ref[...] = jnp.zeros_like(dga_ref)

        @pl.when((i == 0) & (b == 0))
        def _():
            acc_ref[...] = jnp.zeros_like(acc_ref)

        fv = f_ref[...]
        gav = ga_ref[...]
        err = x_ref[...] + gav * fv - t_ref[...]
        acc_ref[...] += jnp.sum(err * err, axis=0, keepdims=True)
        dy = err * (1.0 / D)
        dy_ref[...] = dy
        dff_ref[...] = (dy * gav).astype(BF16)
        dga_ref[...] += jnp.sum(dy * fv, axis=0, keepdims=True)

        @pl.when((i == ni - 1) & (b == nb - 1))
        def _():
            loss_ref[...] = jnp.sum(acc_ref[...], axis=1, keepdims=True) * (0.5 / D)

    return pl.pallas_call(
        body, name=name, grid=(B, S // tb),
        in_specs=[_tok(tb, D), _tok(tb, D), _tok(tb, D), _per_example(D)],
        out_specs=[pl.BlockSpec((1, 1), lambda b, i: (0, 0)), _tok(tb, D), _tok(tb, D), _per_example(D)],
        out_shape=[SDS((1, 1), F32), SDS((B, S, D), F32), SDS((B, S, D), BF16), SDS((B, 1, D), F32)],
        scratch_shapes=[pltpu.VMEM((1, D), F32)],
        compiler_params=_cp("arbitrary", "arbitrary"),
    )(x1, ff, target, ga2)


def _causal_mask():
    t = lax.broadcasted_iota(jnp.int32, (CHUNK, CHUNK), 0)
    s = lax.broadcasted_iota(jnp.int32, (CHUNK, CHUNK), 1)
    return s <= t


def _gmlp_fwd(proj, g_v, w_s, b_col, D, GW, name):
    T = proj.shape[0]
    tb = _tile(T, ROW_BLOCK, CHUNK)
    gw = GW // GROUPS
    ucol = (2 * D) // GW

    def body(u_ref, v_ref, g_ref, w_ref, b_ref, ga_ref, r_ref):
        zu = _gelu(u_ref[...])
        zv = _gelu(v_ref[...])
        rstd = lax.rsqrt(jnp.mean(zv * zv, axis=1, keepdims=True) + EPS)
        vn = (zv * rstd * g_ref[...]).astype(BF16)
        mask = _causal_mask()
        for g in range(GROUPS):
            wm = jnp.where(mask, w_ref[g], 0.0).astype(BF16)
            cols = slice(g * gw, (g + 1) * gw)
            for ci in range(tb // CHUNK):
                rows = slice(ci * CHUNK, (ci + 1) * CHUNK)
                mixed = _dot(wm, vn[rows, cols], NN) + b_ref[g]
                ga_ref[rows, cols] = (zu[rows, cols] * mixed).astype(BF16)
        r_ref[...] = rstd

    return pl.pallas_call(
        body, name=name, grid=(T // tb,),
        in_specs=[pl.BlockSpec((tb, GW), lambda i: (i, ucol)), pl.BlockSpec((tb, GW), lambda i: (i, ucol + 1)),
                  pl.BlockSpec((1, GW), lambda i: (0, 0)), pl.BlockSpec((GROUPS, CHUNK, CHUNK), lambda i: (0, 0, 0)),
                  pl.BlockSpec((GROUPS, CHUNK, 1), lambda i: (0, 0, 0))],
        out_specs=[pl.BlockSpec((tb, GW), lambda i: (i, 0)), pl.BlockSpec((tb, 1), lambda i: (i, 0))],
        out_shape=[SDS((T, GW), BF16), SDS((T, 1), F32)],
        compiler_params=_cp("parallel"),
    )(proj, proj, g_v, w_s, b_col)


def _gmlp_bwd(proj, dga, rstd_v, g_v, w_s, b_col, sel, dproj, D, GW, name):
    T = proj.shape[0]
    tb = _tile(T, ROW_BLOCK, CHUNK)
    gw = GW // GROUPS
    ucol = (2 * D) // GW
    assert (2 * D) % (2 * GW) == 0
    uvcol = (2 * D) // (2 * GW)
    nsteps = T // tb

    def body(u_ref, v_ref, dga_ref, r_ref, g_ref, w_ref, b_ref, sel_ref, _dproj_in,
             duv_ref, gws_ref, gbs_ref, gg_ref, accb_ref, dvn_ref):
        step = pl.program_id(0)

        @pl.when(step == 0)
        def _():
            gws_ref[...] = jnp.zeros_like(gws_ref)
            gg_ref[...] = jnp.zeros_like(gg_ref)
            accb_ref[...] = jnp.zeros_like(accb_ref)

        uv = u_ref[...]
        vv = v_ref[...]
        zu = _gelu(uv)
        zv = _gelu(vv)
        rs = r_ref[...]
        gv = g_ref[...]
        vhat = zv * rs
        vnb = (vhat * gv).astype(BF16)
        dgav = dga_ref[...]
        du_gelu = _gelu_grad(uv)
        mask = _causal_mask()
        for g in range(GROUPS):
            wm = jnp.where(mask, w_ref[g], 0.0)
            wmb = wm.astype(BF16)
            wmtb = wm.T.astype(BF16)
            cols = slice(g * gw, (g + 1) * gw)
            for ci in range(tb // CHUNK):
                rows = slice(ci * CHUNK, (ci + 1) * CHUNK)
                vn_blk = vnb[rows, cols]
                mixed = _dot(wmb, vn_blk, NN) + b_ref[g]
                duv_ref[rows, cols] = (dgav[rows, cols] * mixed * du_gelu[rows, cols]).astype(BF16)
                dmix = dgav[rows, cols] * zu[rows, cols]
                accb_ref[:, cols] += dmix
                dmb = dmix.astype(BF16)
                gws_ref[g] += _dot(dmb, vn_blk, NT)
                dvn_ref[rows, cols] = _dot(wmtb, dmb, NN)
        dvn = dvn_ref[...]
        gg_ref[...] += jnp.sum(dvn * vhat, axis=0, keepdims=True)
        dvhat = dvn * gv
        cm = jnp.mean(dvhat * vhat, axis=1, keepdims=True)
        dzv = rs * (dvhat - vhat * cm)
        duv_ref[:, GW:] = (dzv * _gelu_grad(vv)).astype(BF16)

        @pl.when(step == nsteps - 1)
        def _():
            gbs_ref[...] = jnp.dot(accb_ref[...], sel_ref[...], precision=lax.Precision.HIGHEST,
                                   preferred_element_type=F32)
            for g in range(GROUPS):
                gws_ref[g] = jnp.where(mask, gws_ref[g], 0.0)

    return pl.pallas_call(
        body, name=name, grid=(nsteps,),
        in_specs=[pl.BlockSpec((tb, GW), lambda i: (i, ucol)), pl.BlockSpec((tb, GW), lambda i: (i, ucol + 1)),
                  pl.BlockSpec((tb, GW), lambda i: (i, 0)), pl.BlockSpec((tb, 1), lambda i: (i, 0)),
                  pl.BlockSpec((1, GW), lambda i: (0, 0)), pl.BlockSpec((GROUPS, CHUNK, CHUNK), lambda i: (0, 0, 0)),
                  pl.BlockSpec((GROUPS, CHUNK, 1), lambda i: (0, 0, 0)), pl.BlockSpec((GW, LANES), lambda i: (0, 0)),
                  pl.BlockSpec(memory_space=pl.ANY)],
        out_specs=[pl.BlockSpec((tb, 2 * GW), lambda i: (i, uvcol)), pl.BlockSpec((GROUPS, CHUNK, CHUNK), lambda i: (0, 0, 0)),
                   pl.BlockSpec((CHUNK, LANES), lambda i: (0, 0)), pl.BlockSpec((1, GW), lambda i: (0, 0))],
        out_shape=[SDS(dproj.shape, BF16), SDS((GROUPS, CHUNK, CHUNK), F32), SDS((CHUNK, LANES), F32), SDS((1, GW), F32)],
        scratch_shapes=[pltpu.VMEM((CHUNK, GW), F32), pltpu.VMEM((tb, GW), F32)],
        input_output_aliases={8: 0},
        compiler_params=_cp("arbitrary"),
    )(proj, proj, dga, rstd_v, g_v, w_s, b_col, sel, dproj)


def _mix_fwd(proj, y_a, y_b, D, name):
    T = proj.shape[0]
    tb = _tile(T, 512, 8)
    td = _tile(D, 1024, 128)
    nd = D // td

    def body(ga_ref, gb_ref, ya_ref, yb_ref, o_ref):
        o_ref[...] = (_sigmoid(ga_ref[...]) * ya_ref[...] + _sigmoid(gb_ref[...]) * yb_ref[...]).astype(BF16)

    blk = pl.BlockSpec((tb, td), lambda i, j: (i, j))
    return pl.pallas_call(
        body, name=name, grid=(T // tb, nd),
        in_specs=[blk, pl.BlockSpec((tb, td), lambda i, j: (i, j + nd)), blk, blk],
        out_specs=blk, out_shape=SDS((T, D), BF16),
        compiler_params=_cp("parallel", "parallel"),
    )(proj, proj, y_a, y_b)


def _mix_bwd(proj, y_a, y_b, dmixed, D, name):
    T, width = proj.shape
    tb = _tile(T, ROW_BLOCK, 8)

    def body(g_ref, ya_ref, yb_ref, dm_ref, dya_ref, dyb_ref, dp_ref):
        dm = dm_ref[...]
        sa = _sigmoid(g_ref[:, :D])
        sb = _sigmoid(g_ref[:, D:])
        dya_ref[...] = (dm * sa).astype(BF16)
        dyb_ref[...] = (dm * sb).astype(BF16)
        dp_ref[:, :D] = (dm * ya_ref[...] * sa * (1.0 - sa)).astype(BF16)
        dp_ref[:, D:] = (dm * yb_ref[...] * sb * (1.0 - sb)).astype(BF16)

    blk = pl.BlockSpec((tb, D), lambda i: (i, 0))
    gates = pl.BlockSpec((tb, 2 * D), lambda i: (i, 0))
    return pl.pallas_call(
        body, name=name, grid=(T // tb,),
        in_specs=[gates, blk, blk, blk],
        out_specs=[blk, blk, gates], out_shape=[SDS((T, D), BF16), SDS((T, D), BF16), SDS((T, width), BF16)],
        compiler_params=_cp("parallel"),
    )(proj, y_a, y_b, dmixed)


def _lat_norm_fwd(proj, g_q, g_kv, D, GW, QL, KVL, name):
    T = proj.shape[0]
    tb = _tile(T, 512, 8)
    qcol = (2 * D + 2 * GW) // QL
    kvcol = (2 * D + 2 * GW + QL) // KVL

    def body(q_ref, kv_ref, gq_ref, gkv_ref, qn_ref, kvn_ref, rq_ref, rkv_ref):
        qv = q_ref[...]
        rq = lax.rsqrt(jnp.mean(qv * qv, axis=1, keepdims=True) + EPS)
        qn_ref[...] = (qv * rq * gq_ref[...]).astype(BF16)
        rq_ref[...] = rq
        kv = kv_ref[...]
        rkv = lax.rsqrt(jnp.mean(kv * kv, axis=1, keepdims=True) + EPS)
        kvn_ref[...] = (kv * rkv * gkv_ref[...]).astype(BF16)
        rkv_ref[...] = rkv

    return pl.pallas_call(
        body, name=name, grid=(T // tb,),
        in_specs=[pl.BlockSpec((tb, QL), lambda i: (i, qcol)), pl.BlockSpec((tb, KVL), lambda i: (i, kvcol)),
                  pl.BlockSpec((1, QL), lambda i: (0, 0)), pl.BlockSpec((1, KVL), lambda i: (0, 0))],
        out_specs=[pl.BlockSpec((tb, QL), lambda i: (i, 0)), pl.BlockSpec((tb, KVL), lambda i: (i, 0)),
                   pl.BlockSpec((tb, 1), lambda i: (i, 0)), pl.BlockSpec((tb, 1), lambda i: (i, 0))],
        out_shape=[SDS((T, QL), BF16), SDS((T, KVL), BF16), SDS((T, 1), F32), SDS((T, 1), F32)],
        compiler_params=_cp("parallel"),
    )(proj, proj, g_q, g_kv)


def _lat_norm_bwd(proj, dqn, dkvn, dkpe, rq, rkv, g_q, g_kv, dproj, D, GW, QL, KVL, name):
    T, width = proj.shape
    tb = _tile(T, 512, 8)
    qcol = (2 * D + 2 * GW) // QL
    kvcol = (2 * D + 2 * GW + QL) // KVL
    tail = width - (2 * D + 2 * GW)
    assert (2 * D + 2 * GW) % tail == 0 and tail >= QL + KVL + LANES
    tailcol = (2 * D + 2 * GW) // tail

    def one(xv, rs, gv, dy, gg_ref):
        xhat = xv * rs
        gg_ref[...] += jnp.sum(dy * xhat, axis=0, keepdims=True)
        dxhat = dy * gv
        cm = jnp.mean(dxhat * xhat, axis=1, keepdims=True)
        return rs * (dxhat - xhat * cm)

    def body(q_ref, kv_ref, dqn_ref, dkvn_ref, dkpe_ref, rq_ref, rkv_ref, gq_ref, gkv_ref, _dproj_in,
             tail_ref, ggq_ref, ggkv_ref):
        @pl.when(pl.program_id(0) == 0)
        def _():
            ggq_ref[...] = jnp.zeros_like(ggq_ref)
            ggkv_ref[...] = jnp.zeros_like(ggkv_ref)

        tail_ref[:, :QL] = one(q_ref[...], rq_ref[...], gq_ref[...], dqn_ref[...], ggq_ref).astype(BF16)
        tail_ref[:, QL:QL + KVL] = one(kv_ref[...], rkv_ref[...], gkv_ref[...], dkvn_ref[...], ggkv_ref).astype(BF16)
        tail_ref[:, QL + KVL:QL + KVL + LANES] = dkpe_ref[...].astype(BF16)
        if tail > QL + KVL + LANES:
            tail_ref[:, QL + KVL + LANES:] = jnp.zeros((tb, tail - (QL + KVL + LANES)), BF16)

    return pl.pallas_call(
        body, name=name, grid=(T // tb,),
        in_specs=[pl.BlockSpec((tb, QL), lambda i: (i, qcol)), pl.BlockSpec((tb, KVL), lambda i: (i, kvcol)),
                  pl.BlockSpec((tb, QL), lambda i: (i, 0)), pl.BlockSpec((tb, KVL), lambda i: (i, 0)),
                  pl.BlockSpec((tb, LANES), lambda i: (i, 0)),
                  pl.BlockSpec((tb, 1), lambda i: (i, 0)), pl.BlockSpec((tb, 1), lambda i: (i, 0)),
                  pl.BlockSpec((1, QL), lambda i: (0, 0)), pl.BlockSpec((1, KVL), lambda i: (0, 0)),
                  pl.BlockSpec(memory_space=pl.ANY)],
        out_specs=[pl.BlockSpec((tb, tail), lambda i: (i, tailcol)),
                   pl.BlockSpec((1, QL), lambda i: (0, 0)), pl.BlockSpec((1, KVL), lambda i: (0, 0))],
        out_shape=[SDS(dproj.shape, BF16), SDS((1, QL), F32), SDS((1, KVL), F32)],
        input_output_aliases={9: 0},
        compiler_params=_cp("arbitrary"),
    )(proj, proj, dqn, dkvn, dkpe, rq, rkv, g_q, g_kv, dproj)


def _swap_halves(r):
    lane = lax.broadcasted_iota(jnp.int32, r.shape, 1)
    lo = pltpu.roll(r, LANES - HALF_ROPE, 1)
    hi = pltpu.roll(r, HALF_ROPE, 1)
    return jnp.where(lane < HALF_ROPE, lo, jnp.where(lane < ROPE, hi, 0.0))


def _rope_fwd(r, cos_t, sin_t):
    return r * cos_t + _swap_halves(r) * sin_t


def _rope_bwd(d, cos_t, sin_t):
    return d * cos_t + _swap_halves(d * sin_t)


def _head_norm_bwd(xn, xr, rs, g_n, g_r, dyn, dyr):
    xhn, xhr = xn * rs, xr * rs
    dn, dr = dyn * g_n, dyr * g_r
    cm = (jnp.sum(dn * xhn, axis=1, keepdims=True) + jnp.sum(dr * xhr, axis=1, keepdims=True)) * (1.0 / QK_HEAD)
    return rs * (dn - xhn * cm), rs * (dr - xhr * cm), dyn * xhn, dyr * xhr


def _q_up_fwd(qn, w_uq_al, g_al, cos_t, sin_t, name):
    T, QL = qn.shape
    HP = w_uq_al.shape[1]
    tb = _tile(T, 512, 8)
    hw = _tile(HP, 1024, HEAD_PAD)

    def body(x_ref, w_ref, g_ref, c_ref, s_ref, o_ref):
        raw = _dot(x_ref[...], w_ref[...], NN)
        gv = g_ref[...]
        cv, sv = c_ref[...], s_ref[...]
        for h in range(hw // HEAD_PAD):
            xn = raw[:, h * HEAD_PAD:h * HEAD_PAD + LANES]
            xr = raw[:, h * HEAD_PAD + LANES:(h + 1) * HEAD_PAD]
            ss = jnp.sum(xn * xn, axis=1, keepdims=True) + jnp.sum(xr * xr, axis=1, keepdims=True)
            rs = lax.rsqrt(ss * (1.0 / QK_HEAD) + EPS)
            o_ref[:, h * HEAD_PAD:h * HEAD_PAD + LANES] = (xn * rs * gv[:, :LANES]).astype(BF16)
            o_ref[:, h * HEAD_PAD + LANES:(h + 1) * HEAD_PAD] = _rope_fwd(xr * rs * gv[:, LANES:], cv, sv).astype(BF16)

    return pl.pallas_call(
        body, name=name, grid=(T // tb, HP // hw),
        in_specs=[pl.BlockSpec((tb, QL), lambda i, j: (i, 0)), pl.BlockSpec((QL, hw), lambda i, j: (0, j)),
                  pl.BlockSpec((1, HEAD_PAD), lambda i, j: (0, 0)),
                  pl.BlockSpec((tb, LANES), lambda i, j: (i, 0)), pl.BlockSpec((tb, LANES), lambda i, j: (i, 0))],
        out_specs=pl.BlockSpec((tb, hw), lambda i, j: (i, j)), out_shape=SDS((T, HP), BF16),
        compiler_params=_cp("parallel", "parallel"),
    )(qn, w_uq_al, g_al, cos_t, sin_t)


def _q_up_bwd(qn, w_uq_al, g_al, cos_t, sin_t, dq, name):
    T, QL = qn.shape
    HP = w_uq_al.shape[1]
    tb = _tile(T, 512, 8)
    hw = _tile(HP, 1024, HEAD_PAD)

    def body(x_ref, w_ref, g_ref, c_ref, s_ref, dq_ref, o_ref, gg_ref):
        @pl.when((pl.program_id(0) == 0) & (pl.program_id(1) == 0))
        def _():
            gg_ref[...] = jnp.zeros_like(gg_ref)

        raw = _dot(x_ref[...], w_ref[...], NN)
        gv = g_ref[...]
        cv, sv = c_ref[...], s_ref[...]
        for h in range(hw // HEAD_PAD):
            lo, mid, hi = h * HEAD_PAD, h * HEAD_PAD + LANES, (h + 1) * HEAD_PAD
            xn, xr = raw[:, lo:mid], raw[:, mid:hi]
            ss = jnp.sum(xn * xn, axis=1, keepdims=True) + jnp.sum(xr * xr, axis=1, keepdims=True)
            rs = lax.rsqrt(ss * (1.0 / QK_HEAD) + EPS)
            dyn = dq_ref[:, lo:mid]
            dyr = _rope_bwd(dq_ref[:, mid:hi], cv, sv)
            dxn, dxr, ggn, ggr = _head_norm_bwd(xn, xr, rs, gv[:, :LANES], gv[:, LANES:], dyn, dyr)
            o_ref[:, lo:mid] = dxn.astype(BF16)
            o_ref[:, mid:hi] = dxr.astype(BF16)
            gg_ref[:, :LANES] += jnp.sum(ggn, axis=0, keepdims=True)
            gg_ref[:, LANES:] += jnp.sum(ggr, axis=0, keepdims=True)

    return pl.pallas_call(
        body, name=name, grid=(T // tb, HP // hw),
        in_specs=[pl.BlockSpec((tb, QL), lambda i, j: (i, 0)), pl.BlockSpec((QL, hw), lambda i, j: (0, j)),
                  pl.BlockSpec((1, HEAD_PAD), lambda i, j: (0, 0)),
                  pl.BlockSpec((tb, LANES), lambda i, j: (i, 0)), pl.BlockSpec((tb, LANES), lambda i, j: (i, 0)),
                  pl.BlockSpec((tb, hw), lambda i, j: (i, j))],
        out_specs=[pl.BlockSpec((tb, hw), lambda i, j: (i, j)), pl.BlockSpec((1, HEAD_PAD), lambda i, j: (0, 0))],
        out_shape=[SDS((T, HP), BF16), SDS((1, HEAD_PAD), F32)],
        compiler_params=_cp("arbitrary", "arbitrary"),
    )(qn, w_uq_al, g_al, cos_t, sin_t, dq)


def _kv_up_fwd(kvn, w_ukv, proj, g_al, cos_t, sin_t, kpe_col, name):
    T, KVL = kvn.shape
    n_shard = w_ukv.shape[2]
    HP = N_DEV * n_shard
    tb = _tile(T, 512, 8)
    hw = _tile(n_shard, 1024, HEAD_PAD)
    per = n_shard // hw
    nh = hw // HEAD_PAD

    def body(x_ref, w_ref, kpe_ref, g_ref, c_ref, s_ref, k_ref, v_ref):
        raw = _dot(x_ref[...], w_ref[...], NN)
        gv = g_ref[...]
        cv, sv = c_ref[...], s_ref[...]
        kpe = kpe_ref[...]
        kpe_ss = jnp.sum(kpe * kpe, axis=1, keepdims=True)
        for h in range(nh):
            lo, mid, hi = h * HEAD_PAD, h * HEAD_PAD + LANES, (h + 1) * HEAD_PAD
            xn = raw[:, lo:mid]
            rs = lax.rsqrt((jnp.sum(xn * xn, axis=1, keepdims=True) + kpe_ss) * (1.0 / QK_HEAD) + EPS)
            k_ref[:, lo:mid] = (xn * rs * gv[:, :LANES]).astype(BF16)
            k_ref[:, mid:hi] = _rope_fwd(kpe * rs * gv[:, LANES:], cv, sv).astype(BF16)
            v_ref[:, h * V_HEAD:(h + 1) * V_HEAD] = raw[:, mid:hi].astype(BF16)

    return pl.pallas_call(
        body, name=name, grid=(T // tb, HP // hw),
        in_specs=[pl.BlockSpec((tb, KVL), lambda i, j: (i, 0)),
                  pl.BlockSpec((None, KVL, hw), lambda i, j: (j // per, 0, j % per)),
                  pl.BlockSpec((tb, LANES), lambda i, j: (i, kpe_col)),
                  pl.BlockSpec((1, HEAD_PAD), lambda i, j: (0, 0)),
                  pl.BlockSpec((tb, LANES), lambda i, j: (i, 0)), pl.BlockSpec((tb, LANES), lambda i, j: (i, 0))],
        out_specs=[pl.BlockSpec((tb, hw), lambda i, j: (i, j)), pl.BlockSpec((tb, nh * V_HEAD), lambda i, j: (i, j))],
        out_shape=[SDS((T, HP), BF16), SDS((T, HP // 2), BF16)],
        compiler_params=_cp("parallel", "parallel"),
    )(kvn, w_ukv, proj, g_al, cos_t, sin_t)


def _kv_up_bwd(kvn, w_ukv, proj, g_al, cos_t, sin_t, dk, dv, kpe_col, name):
    T, KVL = kvn.shape
    n_shard = w_ukv.shape[2]
    HP = N_DEV * n_shard
    tb = _tile(T, 512, 8)
    hw = _tile(n_shard, 1024, HEAD_PAD)
    per = n_shard // hw
    nh = hw // HEAD_PAD

    def body(x_ref, w_ref, kpe_ref, g_ref, c_ref, s_ref, dk_ref, dv_ref, o_ref, dkpe_ref, gg_ref):
        i, j = pl.program_id(0), pl.program_id(1)

        @pl.when((i == 0) & (j == 0))
        def _():
            gg_ref[...] = jnp.zeros_like(gg_ref)

        @pl.when(j == 0)
        def _():
            dkpe_ref[...] = jnp.zeros_like(dkpe_ref)

        raw = _dot(x_ref[...], w_ref[...], NN)
        gv = g_ref[...]
        cv, sv = c_ref[...], s_ref[...]
        kpe = kpe_ref[...]
        kpe_ss = jnp.sum(kpe * kpe, axis=1, keepdims=True)
        for h in range(nh):
            lo, mid, hi = h * HEAD_PAD, h * HEAD_PAD + LANES, (h + 1) * HEAD_PAD
            xn = raw[:, lo:mid]
            rs = lax.rsqrt((jnp.sum(xn * xn, axis=1, keepdims=True) + kpe_ss) * (1.0 / QK_HEAD) + EPS)
            dyn = dk_ref[:, lo:mid]
            dyr = _rope_bwd(dk_ref[:, mid:hi], cv, sv)
            dxn, dxr, ggn, ggr = _head_norm_bwd(xn, kpe, rs, gv[:, :LANES], gv[:, LANES:], dyn, dyr)
            o_ref[:, lo:mid] = dxn.astype(BF16)
            o_ref[:, mid:hi] = dv_ref[:, h * V_HEAD:(h + 1) * V_HEAD].astype(BF16)
            dkpe_ref[...] += dxr
            gg_ref[:, :LANES] += jnp.sum(ggn, axis=0, keepdims=True)
            gg_ref[:, LANES:] += jnp.sum(ggr, axis=0, keepdims=True)

    return pl.pallas_call(
        body, name=name, grid=(T // tb, HP // hw),
        in_specs=[pl.BlockSpec((tb, KVL), lambda i, j: (i, 0)),
                  pl.BlockSpec((None, KVL, hw), lambda i, j: (j // per, 0, j % per)),
                  pl.BlockSpec((tb, LANES), lambda i, j: (i, kpe_col)),
                  pl.BlockSpec((1, HEAD_PAD), lambda i, j: (0, 0)),
                  pl.BlockSpec((tb, LANES), lambda i, j: (i, 0)), pl.BlockSpec((tb, LANES), lambda i, j: (i, 0)),
                  pl.BlockSpec((tb, hw), lambda i, j: (i, j)), pl.BlockSpec((tb, nh * V_HEAD), lambda i, j: (i, j))],
        out_specs=[pl.BlockSpec((tb, hw), lambda i, j: (i, j)), pl.BlockSpec((tb, LANES), lambda i, j: (i, 0)),
                   pl.BlockSpec((1, HEAD_PAD), lambda i, j: (0, 0))],
        out_shape=[SDS((T, HP), BF16), SDS((T, LANES), F32), SDS((1, HEAD_PAD), F32)],
        compiler_params=_cp("arbitrary", "arbitrary"),
    )(kvn, w_ukv, proj, g_al, cos_t, sin_t, dk, dv)


def _attn_fwd(q, k, v, H, name, comm=None):
    B, S, _ = q.shape
    tq = _tile(S, ATTN_BLOCK, 128)
    scale = QK_HEAD ** -0.5

    def body(q_ref, k_ref, v_ref, o_ref, l_ref):
        qi = pl.program_id(2)
        qv = q_ref[...]

        def step(j, carry, diagonal):
            m, l, acc = carry
            off = pl.multiple_of(j * tq, tq)
            kj = k_ref[pl.ds(off, tq), :]
            vj = v_ref[pl.ds(off, tq), :]
            s = _dot(qv, kj, NT) * scale
            if diagonal:
                row = lax.broadcasted_iota(jnp.int32, (tq, tq), 0)
                col = lax.broadcasted_iota(jnp.int32, (tq, tq), 1)
                s = jnp.where(col <= row, s, -jnp.inf)
            m_new = jnp.maximum(m, jnp.max(s, axis=1, keepdims=True))
            alpha = jnp.exp(m - m_new)
            p = jnp.exp(s - m_new)
            l = alpha * l + jnp.sum(p, axis=1, keepdims=True)
            acc = alpha * acc + _dot(p.astype(BF16), vj, NN)
            return m_new, l, acc

        init = (jnp.full((tq, 1), -1e30, F32), jnp.zeros((tq, 1), F32), jnp.zeros((tq, V_HEAD), F32))
        carry = lax.fori_loop(0, qi, functools.partial(step, diagonal=False), init)
        m, l, acc = step(qi, carry, diagonal=True)
        o_ref[...] = (acc / l).astype(BF16)
        l_ref[...] = m + jnp.log(l)

    grid = (B, H, S // tq)
    in_specs = [pl.BlockSpec((None, tq, HEAD_PAD), lambda b, h, i: (b, i, h)),
                pl.BlockSpec((None, S, HEAD_PAD), lambda b, h, i: (b, 0, h)),
                pl.BlockSpec((None, S, V_HEAD), lambda b, h, i: (b, 0, h))]
    out_specs = [pl.BlockSpec((None, tq, V_HEAD), lambda b, h, i: (b, i, h)),
                 pl.BlockSpec((None, None, tq, 1), lambda b, h, i: (b, h, i, 0))]
    out_shape = [SDS((B, S, H * V_HEAD), BF16), SDS((B, H, S, 1), F32)]
    if comm is None:
        return pl.pallas_call(
            body, name=name, grid=grid, in_specs=in_specs, out_specs=out_specs, out_shape=out_shape,
            compiler_params=_cp("parallel", "parallel", "parallel"),
        )(q, k, v)

    def first_last():
        b, h, i = pl.program_id(0), pl.program_id(1), pl.program_id(2)
        return (b == 0) & (h == 0) & (i == 0), (b == B - 1) & (h == H - 1) & (i == grid[2] - 1)

    outs = pl.pallas_call(
        _carry(body, comm, 3, 2, 0, first_last), name=name, grid=grid,
        in_specs=in_specs + [HBM_SPEC] * len(comm.ins), out_specs=out_specs + [HBM_SPEC] * len(comm.out_shape),
        out_shape=out_shape + comm.out_shape, scratch_shapes=comm.scratch,
        compiler_params=_cp("arbitrary", "arbitrary", "arbitrary"),
    )(q, k, v, *comm.ins)
    return outs[0], outs[1], comm.results(outs[2:])


def _attn_bwd(q, k, v, o, lse, do, H, name, comm=None):
    B, S, _ = q.shape
    tq = _tile(S, ATTN_BLOCK, 128)
    nq = S // tq
    scale = QK_HEAD ** -0.5

    def body(q_ref, k_ref, v_ref, o_ref, l_ref, do_ref, dq_ref, dk_ref, dv_ref, delta_ref):
        dq_ref[...] = jnp.zeros_like(dq_ref)
        for i in range(nq):
            rows = slice(i * tq, (i + 1) * tq)
            delta_ref[rows, :] = jnp.sum(do_ref[rows, :].astype(F32) * o_ref[rows, :].astype(F32), axis=1, keepdims=True)
        def kv_step(j, carry):
            offk = pl.multiple_of(j * tq, tq)
            kj = k_ref[pl.ds(offk, tq), :]
            vj = v_ref[pl.ds(offk, tq), :]

            def q_step(i, acc, diagonal):
                dk_acc, dv_acc = acc
                offq = pl.multiple_of(i * tq, tq)
                qi = q_ref[pl.ds(offq, tq), :]
                doi = do_ref[pl.ds(offq, tq), :]
                s = _dot(qi, kj, NT) * scale
                p = jnp.exp(s - l_ref[pl.ds(offq, tq), :])
                if diagonal:
                    row = lax.broadcasted_iota(jnp.int32, (tq, tq), 0)
                    col = lax.broadcasted_iota(jnp.int32, (tq, tq), 1)
                    p = jnp.where(col <= row, p, 0.0)
                dv_acc = dv_acc + _dot(p.astype(BF16), doi, TN)
                dp = _dot(doi, vj, NT)
                ds = (p * (dp - delta_ref[pl.ds(offq, tq), :]) * scale).astype(BF16)
                dk_acc = dk_acc + _dot(ds, qi, TN)
                dq_ref[pl.ds(offq, tq), :] += _dot(ds, kj, NN)
                return dk_acc, dv_acc

            acc = q_step(j, (jnp.zeros((tq, HEAD_PAD), F32), jnp.zeros((tq, V_HEAD), F32)), diagonal=True)
            dk_acc, dv_acc = lax.fori_loop(j + 1, nq, functools.partial(q_step, diagonal=False), acc)
            dk_ref[pl.ds(offk, tq), :] = dk_acc
            dv_ref[pl.ds(offk, tq), :] = dv_acc
            return carry

        lax.fori_loop(0, nq, kv_step, 0)

    qk_spec = pl.BlockSpec((None, S, HEAD_PAD), lambda b, h: (b, 0, h))
    v_spec = pl.BlockSpec((None, S, V_HEAD), lambda b, h: (b, 0, h))
    in_specs = [qk_spec, qk_spec, v_spec, v_spec, pl.BlockSpec((None, None, S, 1), lambda b, h: (b, h, 0, 0)), v_spec]
    out_specs = [qk_spec, qk_spec, v_spec]
    out_shape = [SDS((B, S, H * HEAD_PAD), F32), SDS((B, S, H * HEAD_PAD), F32), SDS((B, S, H * V_HEAD), F32)]
    scratch = [pltpu.VMEM((S, 1), F32)]
    if comm is None:
        return pl.pallas_call(
            body, name=name, grid=(B, H), in_specs=in_specs, out_specs=out_specs, out_shape=out_shape,
            scratch_shapes=scratch, compiler_params=_cp("parallel", "parallel"),
        )(q, k, v, o, lse, do)

    def first_last():
        b, h = pl.program_id(0), pl.program_id(1)
        return (b == 0) & (h == 0), (b == B - 1) & (h == H - 1)

    outs = pl.pallas_call(
        _carry(body, comm, 6, 3, 1, first_last), name=name, grid=(B, H),
        in_specs=in_specs + [HBM_SPEC] * len(comm.ins), out_specs=out_specs + [HBM_SPEC] * len(comm.out_shape),
        out_shape=out_shape + comm.out_shape, scratch_shapes=scratch + comm.scratch,
        compiler_params=_cp("arbitrary", "arbitrary"),
    )(q, k, v, o, lse, do, *comm.ins)
    return outs[0], outs[1], outs[2], comm.results(outs[3:])


def _natural(sm):
    nd, R, n = sm.shape
    return jnp.transpose(sm, (1, 0, 2)).reshape(R, nd * n)


def _col_shards(full):
    R, N = full.shape
    return jnp.transpose(full.reshape(R, N_DEV, N // N_DEV), (1, 0, 2))


def _pad_heads(w, H):
    R = w.shape[0]
    return jnp.pad(w.reshape(R, H, QK_HEAD), ((0, 0), (0, 0), (0, HEAD_PAD - QK_HEAD))).reshape(R, H * HEAD_PAD)


def _unpad_heads(w, H):
    R = w.shape[0]
    return w.reshape(R, H, HEAD_PAD)[:, :, :QK_HEAD].reshape(R, H * QK_HEAD)


def _pad_gain(g):
    return jnp.pad(g, ((0, 0), (0, HEAD_PAD - QK_HEAD)))


def _rope_tables(positions):
    inv_freq = 1.0 / (ROPE_THETA ** (jnp.arange(0, ROPE, 2, dtype=F32) / ROPE))
    ang = positions.astype(F32).reshape(-1, 1) * inv_freq
    cos, sin = jnp.cos(ang), jnp.sin(ang)
    zeros = jnp.zeros((ang.shape[0], LANES - ROPE), F32)
    return jnp.concatenate([cos, cos, zeros], axis=1), jnp.concatenate([-sin, sin, zeros], axis=1)


def kernel(x, c, positions, w_ada, b_ada, g_norm1, w_in, g_v, w_s, b_s, g_q_lat, g_kv_lat, w_uq, w_ukv, g_qn, g_kn, w_branch_a, w_branch_b, w_out, g_norm2, w_ff1, w_ff2, loss_target, m_w_ada, m_b_ada, m_g_norm1, m_w_in, m_g_v, m_w_s, m_b_s, m_g_q_lat, m_g_kv_lat, m_w_uq, m_w_ukv, m_g_qn, m_g_kn, m_w_branch_a, m_w_branch_b, m_w_out, m_g_norm2, m_w_ff1, m_w_ff2, v_w_ada, v_b_ada, v_g_norm1, v_w_in, v_g_v, v_w_s, v_b_s, v_g_q_lat, v_g_kv_lat, v_w_uq, v_w_ukv, v_g_qn, v_g_kn, v_w_branch_a, v_w_branch_b, v_w_out, v_g_norm2, v_w_ff1, v_w_ff2):
    B, S, D = x.shape
    T = B * S
    GW = g_v.shape[-1]
    QL = g_q_lat.shape[-1]
    KVL = g_kv_lat.shape[-1]
    H = w_uq.shape[-1] * N_DEV // QK_HEAD
    IN = w_in.shape[-1] * N_DEV
    OFF_GATE = IN - 2 * D
    IN_AL = _round_up(2 * D + OFF_GATE + (LANES - ROPE), 512)
    assert OFF_GATE == 2 * GW + QL + KVL + ROPE
    assert (2 * D) % GW == 0 and (2 * D + 2 * GW) % QL == 0 and (2 * D + 2 * GW + QL) % KVL == 0
    kpe_col = (2 * D + 2 * GW + QL + KVL) // LANES

    xi, yi, ci = _here()
    dev = 4 * xi + 2 * yi + ci
    c_idx = jnp.reshape(ci, (1,)).astype(jnp.int32)

    big = [w_in, w_uq, w_ukv, w_branch_a, w_branch_b, w_out, w_ff1, w_ff2]
    s_in, s_uq, s_ukv, s_ba, s_bb, s_out, s_ff1, s_ff2 = [w[0].astype(BF16) for w in big]
    ((g_in,),) = _run_comm([_GatherJob([s_in])], "ag_w_in")
    w_in_nat = _natural(g_in)
    w_al = jnp.concatenate([w_in_nat[:, OFF_GATE:], w_in_nat[:, :OFF_GATE], jnp.zeros((D, IN_AL - IN), BF16)], axis=1)

    n_ada = w_ada.shape[-1]
    c_all = _all_gather_small(c, "ag_c").reshape(N_DEV * B, D)
    b_cols = lax.dynamic_slice(b_ada, (0, dev * n_ada), (1, n_ada))
    mod_cols = _ada_fwd(c_all, w_ada[0], b_cols, "ada_fwd")
    mod_all = _all_gather_small(mod_cols, "ag_mod")
    mod_mine = lax.dynamic_slice(mod_all, (0, dev * B, 0), (N_DEV, B, n_ada))
    mod_mine = jnp.transpose(mod_mine, (1, 0, 2)).reshape(B, 6, 1, D)
    sh1, sc1, ga1, sh2, sc2, ga2 = [mod_mine[:, k] for k in range(6)]

    cos_t, sin_t = _rope_tables(positions)
    g_qn_al, g_kn_al = _pad_gain(g_qn), _pad_gain(g_kn)
    b_col = b_s[0].reshape(GROUPS, CHUNK, 1)
    gw = GW // GROUPS
    sel = (jnp.arange(GW)[:, None] // gw == jnp.arange(LANES)[None, :]).astype(F32)

    h1, rstd1 = _norm_mod_fwd(x, g_norm1, sh1, sc1, "norm1_fwd")
    h1f = h1.reshape(T, D)
    proj, ((g_uq, g_ukv, g_ba, g_bb, g_out),) = _matmul(
        h1f, w_al, mode="nn", out_dtypes=[F32], name="mm_proj", comm=_Comm([_GatherJob([s_uq, s_ukv, s_ba, s_bb, s_out])]))
    w_uq_al = _pad_heads(_natural(g_uq), H)
    w_bb_f = g_bb.reshape(-1, D)
    w_out_f = g_out.reshape(-1, D)
    ga_act, rstd_v = _gmlp_fwd(proj, g_v, w_s[0], b_col, D, GW, "gmlp_fwd")
    y_a = _matmul(ga_act, g_ba, mode="nn", out_dtypes=[F32], name="mm_ya", b_shards=True)
    qn, kvn, rstd_q, rstd_kv = _lat_norm_fwd(proj, g_q_lat, g_kv_lat, D, GW, QL, KVL, "latnorm_fwd")
    q_al = _q_up_fwd(qn, w_uq_al, g_qn_al, cos_t, sin_t, "q_up_fwd")
    k_al, v_al = _kv_up_fwd(kvn, g_ukv, proj, g_kn_al, cos_t, sin_t, kpe_col, "kv_up_fwd")
    q3, k3, v3 = q_al.reshape(B, S, -1), k_al.reshape(B, S, -1), v_al.reshape(B, S, -1)
    attn, lse, ((g_ff1,),) = _attn_fwd(q3, k3, v3, H, "attn_fwd", comm=_Comm([_GatherJob([s_ff1])]))
    attn_f = attn.reshape(T, H * V_HEAD)
    y_b = _matmul(attn_f, w_bb_f, mode="nn", out_dtypes=[F32], name="mm_yb")
    mixed = _mix_fwd(proj, y_a, y_b, D, "mix_fwd")
    o = _matmul(mixed, w_out_f, mode="nn", out_dtypes=[F32], name="mm_o")
    x1, h2, rstd2 = _norm_mod_fwd(x, g_norm2, sh2, sc2, "norm2_fwd", o=o.reshape(B, S, D), ga=ga1)

    def relu_sq(acc):
        r = jnp.maximum(acc, 0.0)
        return r * r, r

    (a_act, r_act), ((g_ff2,),) = _matmul(h2.reshape(T, D), g_ff1, mode="nn", out_dtypes=[BF16, BF16], name="mm_ff1",
                                          epi=relu_sq, comm=_Comm([_GatherJob([s_ff2])]), b_shards=True)
    w_ff2_f = g_ff2.reshape(-1, D)
    ff = _matmul(a_act, w_ff2_f, mode="nn", out_dtypes=[F32], name="mm_ff2")
    loss_part, dy, dff, d_ga2 = _loss_head(x1, ff.reshape(B, S, D), loss_target, ga2, "loss_head")
    loss = lax.psum(loss_part[0, 0], ("x", "y", "c"))

    dff_f = dff.reshape(T, D)
    df1 = _matmul(dff_f, w_ff2_f, mode="nt", out_dtypes=[BF16], name="mm_da", epi=lambda acc, r: (acc * (2.0 * r.astype(F32)),),
                  extras=(r_act,))
    gs_ff2 = _matmul(a_act, dff_f, mode="tn", out_dtypes=[BF16], name="mm_gw_ff2").reshape(N_DEV, -1, D)
    gs_ff1, ((sib_ff2,),) = _matmul(h2.reshape(T, D), df1, mode="tn", out_dtypes=[BF16], name="mm_gw_ff1",
                                    comm=_Comm([_SiblingJob([gs_ff2])]), out_shards=True)
    cs_ff2 = _chip_sum(gs_ff2, sib_ff2, c_idx, "chip_sum_w_ff2")
    dh2, ((parts_ff2,), (sib_ff1,)) = _matmul(df1, g_ff1, mode="nt", out_dtypes=[F32], name="mm_dh2", b_shards=True,
                                              comm=_Comm([_ChipsJob([cs_ff2]), _SiblingJob([gs_ff1])]))
    cs_ff1 = _chip_sum(gs_ff1, sib_ff1, c_idx, "chip_sum_w_ff1")
    dx1, d_sh2, d_sc2, gg_norm2, d_ga1, do = _norm_mod_bwd(
        dh2.reshape(B, S, D), x1, rstd2, sc2, g_norm2, dy, "norm2_bwd", o=o.reshape(B, S, D), ga=ga1)

    do_f = do.reshape(T, D)
    dmixed = _matmul(do_f, w_out_f, mode="nt", out_dtypes=[F32], name="mm_dmixed")
    gw_out = _matmul(mixed, do_f, mode="tn", out_dtypes=[BF16], name="mm_gw_out")
    dy_a, dy_b, dproj = _mix_bwd(proj, y_a, y_b, dmixed, D, "mix_bwd")
    gs_ba = _matmul(ga_act, dy_a, mode="tn", out_dtypes=[BF16], name="mm_gw_ba", out_shards=True)
    dga_act = _matmul(dy_a, g_ba, mode="nt", out_dtypes=[F32], name="mm_dga", b_shards=True)
    gw_bb = _matmul(attn_f, dy_b, mode="tn", out_dtypes=[BF16], name="mm_gw_bb")
    dattn = _matmul(dy_b, w_bb_f, mode="nt", out_dtypes=[BF16], name="mm_dattn")
    dproj, gg_ws, gg_bs_t, gg_gv = _gmlp_bwd(proj, dga_act, rstd_v, g_v, w_s[0], b_col, sel, dproj, D, GW, "gmlp_bwd")
    dq, dk, dv, ((parts_ff1,),) = _attn_bwd(q3, k3, v3, attn, lse, dattn.reshape(B, S, -1), H, "attn_bwd",
                                            comm=_Comm([_ChipsJob([cs_ff1])]))
    dq_raw, gg_qn = _q_up_bwd(qn, w_uq_al, g_qn_al, cos_t, sin_t, dq.reshape(T, -1), "q_up_bwd")
    dkv_raw, dkpe, gg_kn = _kv_up_bwd(kvn, g_ukv, proj, g_kn_al, cos_t, sin_t, dk.reshape(T, -1), dv.reshape(T, -1),
                                      kpe_col, "kv_up_bwd")
    gw_uq_al = _matmul(qn, dq_raw, mode="tn", out_dtypes=[BF16], name="mm_gw_uq")
    dqn = _matmul(dq_raw, w_uq_al, mode="nt", out_dtypes=[F32], name="mm_dqn")
    gs_ukv = _matmul(kvn, dkv_raw, mode="tn", out_dtypes=[BF16], name="mm_gw_ukv", out_shards=True)
    dkvn = _matmul(dkv_raw, g_ukv, mode="nt", out_dtypes=[F32], name="mm_dkvn", b_shards=True)
    dproj, gg_qlat, gg_kvlat = _lat_norm_bwd(
        proj, dqn, dkvn, dkpe, rstd_q, rstd_kv, g_q_lat, g_kv_lat, dproj, D, GW, QL, KVL, "latnorm_bwd")
    mid_names = ["w_uq", "w_ukv", "w_branch_a", "w_branch_b", "w_out"]
    gs_mid = [_col_shards(_unpad_heads(gw_uq_al, H)), gs_ukv, gs_ba,
              gw_bb.reshape(N_DEV, -1, D), gw_out.reshape(N_DEV, -1, D)]
    (sib_mid,) = _run_comm([_SiblingJob(gs_mid)], "rs_sibling_mid")
    cs_mid = [_chip_sum(g, r, c_idx, "chip_sum_" + nm) for g, r, nm in zip(gs_mid, sib_mid, mid_names)]
    gw_al, (parts_mid,) = _matmul(h1f, dproj, mode="tn", out_dtypes=[BF16], name="mm_gw_in", comm=_Comm([_ChipsJob(cs_mid)]))
    gw_in_nat = jnp.concatenate([gw_al[:, 2 * D:2 * D + OFF_GATE], gw_al[:, :2 * D]], axis=1)
    gs_in = _col_shards(gw_in_nat)
    ((sib_in,),) = _run_comm([_SiblingJob([gs_in])], "rs_sibling_in")
    cs_in = _chip_sum(gs_in, sib_in, c_idx, "chip_sum_w_in")
    dh1, ((parts_in,),) = _matmul(dproj, w_al, mode="nt", out_dtypes=[F32], name="mm_dh1", comm=_Comm([_ChipsJob([cs_in])]))
    grad_x, d_sh1, d_sc1, gg_norm1 = _norm_mod_bwd(dh1.reshape(B, S, D), x, rstd1, sc1, g_norm1, dx1, "norm1_bwd")

    dmod_mine = jnp.concatenate([d_sh1, d_sc1, d_ga1, d_sh2, d_sc2, d_ga2], axis=2).reshape(B, 6 * D)
    dmod_all = _all_gather_small(dmod_mine, "ag_dmod").reshape(N_DEV * B, 6 * D)
    dmod_cols = lax.dynamic_slice(dmod_all, (0, dev * n_ada), (N_DEV * B, n_ada))
    ada_out = _ada_bwd_adam(c_all, dmod_cols, w_ada[0], m_w_ada[0], v_w_ada[0], "ada_bwd_adam")
    nb_rows = 8
    bada_out = _adam_from_parts(b_ada.reshape(nb_rows, -1), m_b_ada.reshape(nb_rows, -1), v_b_ada.reshape(nb_rows, -1),
                                dmod_all.reshape(N_DEV * B, nb_rows, -1), "adam_b_ada")

    names = ["w_in", "w_uq", "w_ukv", "w_branch_a", "w_branch_b", "w_out", "w_ff1", "w_ff2"]
    parts = [parts_in] + list(parts_mid) + [parts_ff1, parts_ff2]
    ms = [m_w_in, m_w_uq, m_w_ukv, m_w_branch_a, m_w_branch_b, m_w_out, m_w_ff1, m_w_ff2]
    vs = [v_w_in, v_w_uq, v_w_ukv, v_w_branch_a, v_w_branch_b, v_w_out, v_w_ff1, v_w_ff2]
    big_out = {nm: _adam_from_parts(w[0], m[0], v[0], p, "adam_" + nm) for nm, w, m, v, p in zip(names, big, ms, vs, parts)}

    small = [("g_norm1", g_norm1, m_g_norm1, v_g_norm1, gg_norm1),
             ("g_v", g_v, m_g_v, v_g_v, gg_gv),
             ("w_s", w_s, m_w_s, v_w_s, gg_ws),
             ("b_s", b_s, m_b_s, v_b_s, jnp.transpose(gg_bs_t[:, :GROUPS])),
             ("g_q_lat", g_q_lat, m_g_q_lat, v_g_q_lat, gg_qlat),
             ("g_kv_lat", g_kv_lat, m_g_kv_lat, v_g_kv_lat, gg_kvlat),
             ("g_qn", g_qn, m_g_qn, v_g_qn, gg_qn[:, :QK_HEAD]),
             ("g_kn", g_kn, m_g_kn, v_g_kn, gg_kn[:, :QK_HEAD]),
             ("g_norm2", g_norm2, m_g_norm2, v_g_norm2, gg_norm2)]
    sizes = [w.size for _, w, _, _, _ in small]
    n_small = sum(sizes)
    n_small_pad = _round_up(n_small, 8 * LANES)

    def flat_cat(arrs):
        return jnp.pad(jnp.concatenate([a.reshape(-1) for a in arrs]), (0, n_small_pad - n_small))

    part_small = _all_gather_small(flat_cat([t[4] for t in small]), "ag_small_grads")
    small_out = _adam_from_parts(
        flat_cat([t[1] for t in small]).reshape(8, -1), flat_cat([t[2] for t in small]).reshape(8, -1),
        flat_cat([t[3] for t in small]).reshape(8, -1), part_small.reshape(N_DEV, 8, -1), "adam_small")
    offs = [sum(sizes[:i]) for i in range(len(sizes))]

    def small_piece(kind, i):
        return small_out[kind].reshape(-1)[offs[i]:offs[i] + sizes[i]].reshape(small[i][1].shape)

    small_idx = {t[0]: i for i, t in enumerate(small)}
    order = ["w_ada", "b_ada", "g_norm1", "w_in", "g_v", "w_s", "b_s", "g_q_lat", "g_kv_lat", "w_uq", "w_ukv", "g_qn", "g_kn",
             "w_branch_a", "w_branch_b", "w_out", "g_norm2", "w_ff1", "w_ff2"]

    def result(kind, nm):
        if nm == "w_ada":
            return ada_out[kind][None]
        if nm == "b_ada":
            return bada_out[kind].reshape(b_ada.shape)
        if nm in small_idx:
            return small_piece(kind, small_idx[nm])
        return big_out[nm][kind][None]

    outs = [loss, grad_x]
    for kind in range(4):
        outs += [result(kind, nm) for nm in order]
    return tuple(outs)
```

```python
import functools
import math

import jax
import jax.numpy as jnp
from jax import lax
from jax.experimental import pallas as pl
from jax.experimental.pallas import tpu as pltpu

F32 = jnp.float32
BF16 = jnp.bfloat16
SDS = jax.ShapeDtypeStruct
MESH = pl.DeviceIdType.MESH

N_DEV = 8
N_CHIP = 4
CHUNK = 128
GROUPS = 8
NOPE = 128
ROPE = 64
HALF_ROPE = ROPE // 2
QK_HEAD = NOPE + ROPE
V_HEAD = 128
HEAD_PAD = 256
LANES = 128
ROPE_THETA = 10000.0
EPS = 1e-6
INV_SQRT2 = 1.0 / math.sqrt(2.0)
INV_SQRT_2PI = 1.0 / math.sqrt(2.0 * math.pi)

ADAM_LR = 0.001
ADAM_B1 = 0.9
ADAM_B2 = 0.999
ADAM_EPS = 1e-08
ADAM_WD = 0.01
ADAM_STEP = 10

V7X_VMEM_LIMIT_BYTES = 56 * 1024 * 1024
MM_TILE = 1024
ATTN_BLOCK = 512
ROW_BLOCK = 128

NN = (((1,), (0,)), ((), ()))
NT = (((1,), (1,)), ((), ()))
TN = (((0,), (0,)), ((), ()))


def _tile(n, pref, mult):
    t = min(pref, n)
    t -= t % mult
    while t >= mult:
        if n % t == 0:
            return t
        t -= mult
    return n


def _round_up(n, m):
    return (n + m - 1) // m * m


BF16_SUBLANES = 16


def _block_2d(R, C, elems):
    if R % BF16_SUBLANES == 0:
        return _tile(R, max(BF16_SUBLANES, elems // C // BF16_SUBLANES * BF16_SUBLANES), BF16_SUBLANES), C
    return R, _tile(C, max(LANES, elems // R // LANES * LANES), LANES)


def _cp(*sem):
    return pltpu.CompilerParams(dimension_semantics=sem, vmem_limit_bytes=V7X_VMEM_LIMIT_BYTES)


def _dot(a, b, dims):
    return lax.dot_general(a, b, dims, preferred_element_type=F32)


def _gelu(x):
    return 0.5 * x * (1.0 + lax.erf(x * INV_SQRT2))


def _gelu_grad(x):
    return 0.5 * (1.0 + lax.erf(x * INV_SQRT2)) + x * jnp.exp(-0.5 * x * x) * INV_SQRT_2PI


def _sigmoid(x):
    return 1.0 / (1.0 + jnp.exp(-x))


def _matmul(a, b, *, mode, out_dtypes, name, epi=None, extras=(), comm=None, b_shards=False, out_shards=False):
    if mode == "tn":
        K, M = a.shape
    else:
        M, K = a.shape
    n_shard = None
    if b_shards:
        _, R, n_shard = b.shape
        N, Kb = (R, N_DEV * n_shard) if mode == "nt" else (N_DEV * n_shard, R)
    elif mode == "nt":
        N, Kb = b.shape
    else:
        Kb, N = b.shape
    assert K == Kb, (name, a.shape, b.shape)
    tm = _tile(M, MM_TILE, 128)
    tn = _tile(n_shard if (b_shards and mode != "nt") else N, MM_TILE, 128)
    tk = _tile(n_shard if (b_shards and mode == "nt") else K, MM_TILE, 128)
    if out_shards:
        assert mode == "tn" and not extras and len(out_dtypes) == 1
        n_shard = N // N_DEV
        tn = _tile(n_shard, MM_TILE, 128)
    nk = K // tk
    n_extra = len(extras)
    n_out = len(out_dtypes)
    dims = {"nn": NN, "nt": NT, "tn": TN}[mode]

    def body(a_ref, b_ref, *rest):
        extra_refs = rest[:n_extra]
        out_refs = rest[n_extra:n_extra + n_out]
        acc_ref = rest[n_extra + n_out]
        k = pl.program_id(2)

        def product():
            return _dot(a_ref[...].astype(BF16), b_ref[...].astype(BF16), dims)

        def finish(acc):
            res = (acc,) if epi is None else epi(acc, *[e[...] for e in extra_refs])
            for o_ref, r in zip(out_refs, res):
                o_ref[...] = r.astype(o_ref.dtype)

        if nk == 1:
            finish(product())
            return

        @pl.when(k == 0)
        def _():
            acc_ref[...] = product()

        if nk > 2:
            @pl.when((k > 0) & (k < nk - 1))
            def _():
                acc_ref[...] += product()

        @pl.when(k == nk - 1)
        def _():
            finish(acc_ref[...] + product())

    if mode == "tn":
        a_spec = pl.BlockSpec((tk, tm), lambda i, j, k: (k, i))
    else:
        a_spec = pl.BlockSpec((tm, tk), lambda i, j, k: (i, k))
    if b_shards and mode == "nt":
        per = n_shard // tk
        b_spec = pl.BlockSpec((None, tn, tk), lambda i, j, k: (k // per, j, k % per))
    elif b_shards:
        per = n_shard // tn
        b_spec = pl.BlockSpec((None, tk, tn), lambda i, j, k: (j // per, k, j % per))
    elif mode == "nt":
        b_spec = pl.BlockSpec((tn, tk), lambda i, j, k: (j, k))
    else:
        b_spec = pl.BlockSpec((tk, tn), lambda i, j, k: (k, j))
    mn_spec = pl.BlockSpec((tm, tn), lambda i, j, k: (i, j))
    grid = (M // tm, N // tn, nk)
    in_specs = [a_spec, b_spec] + [mn_spec] * n_extra
    if out_shards:
        per_out = n_shard // tn
        out_specs = [pl.BlockSpec((None, tm, tn), lambda i, j, k: (j // per_out, i, j % per_out))]
        out_shape = [SDS((N_DEV, M, n_shard), out_dtypes[0])]
    else:
        out_specs = [mn_spec] * n_out
        out_shape = [SDS((M, N), dt) for dt in out_dtypes]
    scratch = [pltpu.VMEM((tm, tn), F32)]
    if comm is None:
        outs = pl.pallas_call(
            body, name=name, grid=grid, in_specs=in_specs, out_specs=out_specs, out_shape=out_shape,
            scratch_shapes=scratch, compiler_params=_cp("parallel", "parallel", "arbitrary"),
        )(a, b, *extras)
        return outs[0] if n_out == 1 else outs

    def first_last():
        i, j, k = pl.program_id(0), pl.program_id(1), pl.program_id(2)
        return ((i == 0) & (j == 0) & (k == 0),
                (i == grid[0] - 1) & (j == grid[1] - 1) & (k == nk - 1))

    outs = pl.pallas_call(
        _carry(body, comm, 2 + n_extra, n_out, 1, first_last), name=name, grid=grid,
        in_specs=in_specs + [HBM_SPEC] * len(comm.ins), out_specs=out_specs + [HBM_SPEC] * len(comm.out_shape),
        out_shape=out_shape + comm.out_shape, scratch_shapes=scratch + comm.scratch,
        compiler_params=_cp("arbitrary", "arbitrary", "arbitrary"),
    )(a, b, *extras, *comm.ins)
    res = outs[0] if n_out == 1 else list(outs[:n_out])
    return res, comm.results(outs[n_out:])


def _here():
    return lax.axis_index("x"), lax.axis_index("y"), lax.axis_index("c")


def _other_chips(x, y):
    return [(1 - x, y), (x, 1 - y), (1 - x, 1 - y)]


def _all_gather_small(v, name):
    shape = v.shape
    n = v.size
    n_pad = _round_up(n, 8 * LANES)
    flat = jnp.pad(v.reshape(-1), (0, n_pad - n)).reshape(8, n_pad // 8)
    m_per, cols = flat.shape

    def body(x_ref, out_ref, send_sems, recv_sems, local_sem):
        x, y, c = _here()
        me, sibling = (x, y, c), (x, y, 1 - c)
        chips = _other_chips(x, y)

        def rows(px, py, pc):
            return out_ref.at[pl.ds((4 * px + 2 * py + pc) * m_per, m_per), :]

        def copy(k, block, to, src=None):
            return pltpu.make_async_remote_copy(
                src_ref=rows(*block) if src is None else src, dst_ref=rows(*block),
                send_sem=send_sems.at[k], recv_sem=recv_sems.at[k], device_id=to, device_id_type=MESH)

        mine = pltpu.make_async_copy(x_ref, rows(*me), local_sem)
        mine.start()
        first = [copy(0, me, sibling, src=x_ref)]
        first += [copy(1 + j, me, (*chip, c), src=x_ref) for j, chip in enumerate(chips)]
        for cp in first:
            cp.start()
        passed = [copy(4 + j, (*chip, c), sibling) for j, chip in enumerate(chips)]
        for j, chip in enumerate(chips):
            copy(1 + j, (*chip, c), me).wait_recv()
            passed[j].start()
        copy(0, sibling, me).wait_recv()
        for j, chip in enumerate(chips):
            copy(4 + j, (*chip, 1 - c), me).wait_recv()
        for cp in first + passed:
            cp.wait_send()
        mine.wait()

    out = pl.pallas_call(
        body, name=name,
        out_shape=SDS((N_DEV * m_per, cols), F32),
        in_specs=[pl.BlockSpec(memory_space=pltpu.VMEM)],
        out_specs=pl.BlockSpec(memory_space=pltpu.VMEM),
        scratch_shapes=[pltpu.SemaphoreType.DMA((7,)), pltpu.SemaphoreType.DMA((7,)), pltpu.SemaphoreType.DMA],
        compiler_params=pltpu.CompilerParams(vmem_limit_bytes=V7X_VMEM_LIMIT_BYTES),
    )(flat)
    return out.reshape(N_DEV, n_pad)[:, :n].reshape((N_DEV,) + shape)


HBM_SPEC = pl.BlockSpec(memory_space=pltpu.HBM)


class _GatherJob:
    def __init__(self, shards):
        self.ins = list(shards)
        self.nw = len(shards)
        self.out_shape = [SDS((N_DEV,) + s.shape, s.dtype) for s in shards]
        self.scratch = [pltpu.SemaphoreType.DMA((7 * self.nw,)), pltpu.SemaphoreType.DMA((7 * self.nw,)),
                        pltpu.SemaphoreType.DMA((self.nw,))]

    def _parts(self, xs, outs, sems):
        send_sems, recv_sems, local_sems = sems
        x, y, c = _here()

        def blk(w, px, py, pc):
            return outs[w].at[4 * px + 2 * py + pc]

        def copy(w, k, block, to, src=None):
            return pltpu.make_async_remote_copy(
                src_ref=blk(w, *block) if src is None else src, dst_ref=blk(w, *block),
                send_sem=send_sems.at[7 * w + k], recv_sem=recv_sems.at[7 * w + k], device_id=to, device_id_type=MESH)

        me, sibling = (x, y, c), (x, y, 1 - c)
        chips = _other_chips(x, y)
        mine = [pltpu.make_async_copy(xs[w], blk(w, *me), local_sems.at[w]) for w in range(self.nw)]
        first = []
        for w in range(self.nw):
            first.append(copy(w, 0, me, sibling, src=xs[w]))
            first += [copy(w, 1 + j, me, (*chip, c), src=xs[w]) for j, chip in enumerate(chips)]
        return copy, me, sibling, chips, c, mine, first

    def start(self, xs, outs, sems):
        _, _, _, _, _, mine, first = self._parts(xs, outs, sems)
        for cp in mine + first:
            cp.start()

    def finish(self, xs, outs, sems):
        copy, me, sibling, chips, c, mine, first = self._parts(xs, outs, sems)
        passed = []
        for w in range(self.nw):
            for j, chip in enumerate(chips):
                copy(w, 1 + j, (*chip, c), me).wait_recv()
                fwd = copy(w, 4 + j, (*chip, c), sibling)
                fwd.start()
                passed.append(fwd)
        for w in range(self.nw):
            copy(w, 0, sibling, me).wait_recv()
            for j, chip in enumerate(chips):
                copy(w, 4 + j, (*chip, 1 - c), me).wait_recv()
        for cp in first + passed:
            cp.wait_send()
        for cp in mine:
            cp.wait()


class _SiblingJob:
    def __init__(self, grads):
        self.ins = list(grads)
        self.nw = len(grads)
        self.out_shape = [SDS((N_CHIP,) + g.shape[1:], g.dtype) for g in grads]
        self.scratch = [pltpu.SemaphoreType.DMA((N_CHIP * self.nw,)), pltpu.SemaphoreType.DMA((N_CHIP * self.nw,))]

    def _copies(self, gs, outs, sems):
        send_sems, recv_sems = sems
        x, y, c = _here()
        return [pltpu.make_async_remote_copy(
            src_ref=gs[w].at[2 * k + (1 - c)], dst_ref=outs[w].at[k],
            send_sem=send_sems.at[N_CHIP * w + k], recv_sem=recv_sems.at[N_CHIP * w + k],
            device_id=(x, y, 1 - c), device_id_type=MESH) for w in range(self.nw) for k in range(N_CHIP)]

    def start(self, gs, outs, sems):
        for cp in self._copies(gs, outs, sems):
            cp.start()

    def finish(self, gs, outs, sems):
        for cp in self._copies(gs, outs, sems):
            cp.wait()


class _ChipsJob:
    def __init__(self, chip_sums):
        self.ins = list(chip_sums)
        self.nw = len(chip_sums)
        self.out_shape = [SDS(s.shape, s.dtype) for s in chip_sums]
        self.scratch = [pltpu.SemaphoreType.DMA((3 * self.nw,)), pltpu.SemaphoreType.DMA((3 * self.nw,)),
                        pltpu.SemaphoreType.DMA((self.nw,))]

    def _parts(self, srcs, outs, sems):
        send_sems, recv_sems, local_sems = sems
        x, y, c = _here()
        my_chip = 2 * x + y
        local = [pltpu.make_async_copy(srcs[w].at[my_chip], outs[w].at[my_chip], local_sems.at[w]) for w in range(self.nw)]
        sends, landed = [], []
        for w in range(self.nw):
            for j, (px, py) in enumerate(_other_chips(x, y)):
                sems_j = dict(send_sem=send_sems.at[3 * w + j], recv_sem=recv_sems.at[3 * w + j],
                              device_id=(px, py, c), device_id_type=MESH)
                sends.append(pltpu.make_async_remote_copy(
                    src_ref=srcs[w].at[2 * px + py], dst_ref=outs[w].at[my_chip], **sems_j))
                landed.append(pltpu.make_async_remote_copy(
                    src_ref=srcs[w].at[2 * px + py], dst_ref=outs[w].at[2 * px + py], **sems_j))
        return local, sends, landed

    def start(self, srcs, outs, sems):
        local, sends, _ = self._parts(srcs, outs, sems)
        for cp in local + sends:
            cp.start()

    def finish(self, srcs, outs, sems):
        local, sends, landed = self._parts(srcs, outs, sems)
        for cp in landed:
            cp.wait_recv()
        for cp in sends:
            cp.wait_send()
        for cp in local:
            cp.wait()


class _Comm:
    def __init__(self, jobs):
        self.jobs = list(jobs)
        self.ins = [a for j in self.jobs for a in j.ins]
        self.out_shape = [s for j in self.jobs for s in j.out_shape]
        self.scratch = [s for j in self.jobs for s in j.scratch]

    def _split(self, flat, counts):
        out, pos = [], 0
        for n in counts:
            out.append(flat[pos:pos + n])
            pos += n
        return out

    def _each(self, ins, outs, sems):
        return zip(self.jobs, self._split(ins, [len(j.ins) for j in self.jobs]),
                   self._split(outs, [len(j.out_shape) for j in self.jobs]),
                   self._split(sems, [len(j.scratch) for j in self.jobs]))

    def start(self, ins, outs, sems):
        for job, i, o, s in self._each(ins, outs, sems):
            job.start(i, o, s)

    def finish(self, ins, outs, sems):
        for job, i, o, s in self._each(ins, outs, sems):
            job.finish(i, o, s)

    def results(self, flat):
        return [list(r) for r in self._split(list(flat), [len(j.out_shape) for j in self.jobs])]


def _carry(body, comm, n_in, n_out, n_scratch, first_last):
    ci, co = len(comm.ins), len(comm.out_shape)

    def wrapped(*refs):
        ins, rest = refs[:n_in + ci], refs[n_in + ci:]
        outs, scr = rest[:n_out + co], rest[n_out + co:]
        c_ins, c_outs, c_sems = ins[n_in:], outs[n_out:], scr[n_scratch:]
        first, last = first_last()

        @pl.when(first)
        def _():
            comm.start(c_ins, c_outs, c_sems)

        body(*ins[:n_in], *outs[:n_out], *scr[:n_scratch])

        @pl.when(last)
        def _():
            comm.finish(c_ins, c_outs, c_sems)

    return wrapped


def _run_comm(jobs, name):
    comm = _Comm(jobs)

    def body(*refs):
        ci, co = len(comm.ins), len(comm.out_shape)
        comm.start(refs[:ci], refs[ci:ci + co], refs[ci + co:])
        comm.finish(refs[:ci], refs[ci:ci + co], refs[ci + co:])

    outs = pl.pallas_call(
        body, name=name, out_shape=comm.out_shape,
        in_specs=[HBM_SPEC] * len(comm.ins), out_specs=[HBM_SPEC] * len(comm.out_shape),
        scratch_shapes=comm.scratch,
    )(*comm.ins)
    return comm.results(outs)


def _chip_sum(g, recv, c_idx, name):
    _, m, n = g.shape
    tr, tc = _block_2d(m, n, 1 << 20)

    def body(c_ref, g_ref, r_ref, o_ref):
        o_ref[...] = (g_ref[...].astype(F32) + r_ref[...].astype(F32)).astype(o_ref.dtype)

    grid_spec = pltpu.PrefetchScalarGridSpec(
        num_scalar_prefetch=1, grid=(N_CHIP, m // tr, n // tc),
        in_specs=[pl.BlockSpec((None, tr, tc), lambda k, i, j, c_ref: (2 * k + c_ref[0], i, j)),
                  pl.BlockSpec((None, tr, tc), lambda k, i, j, c_ref: (k, i, j))],
        out_specs=pl.BlockSpec((None, tr, tc), lambda k, i, j, c_ref: (k, i, j)))
    return pl.pallas_call(
        body, name=name, grid_spec=grid_spec, out_shape=SDS((N_CHIP, m, n), BF16),
        compiler_params=_cp("parallel", "parallel", "parallel"),
    )(c_idx, g, recv)


def _adam_math(w, g, m, v):
    m = ADAM_B1 * m + (1.0 - ADAM_B1) * g
    v = ADAM_B2 * v + (1.0 - ADAM_B2) * (g * g)
    m_hat = m / (1.0 - ADAM_B1 ** ADAM_STEP)
    v_hat = v / (1.0 - ADAM_B2 ** ADAM_STEP)
    delta = -ADAM_LR * (m_hat / (jnp.sqrt(v_hat) + ADAM_EPS) + ADAM_WD * w)
    return delta, m, v


def _adam_from_parts(w, m, v, parts, name):
    R, C = w.shape
    P = parts.shape[0]
    tr, tc = _block_2d(R, C, 1 << 19)

    def body(w_ref, m_ref, v_ref, p_ref, g_out, d_out, m_out, v_out):
        g = p_ref[0].astype(F32)
        for k in range(1, P):
            g = g + p_ref[k].astype(F32)
        delta, nm, nv = _adam_math(w_ref[...], g, m_ref[...], v_ref[...])
        g_out[...] = g
        d_out[...] = delta
        m_out[...] = nm
        v_out[...] = nv

    spec = pl.BlockSpec((tr, tc), lambda i, j: (i, j))
    return pl.pallas_call(
        body, name=name, grid=(R // tr, C // tc),
        in_specs=[spec, spec, spec, pl.BlockSpec((P, tr, tc), lambda i, j: (0, i, j))],
        out_specs=[spec] * 4, out_shape=[SDS((R, C), F32)] * 4,
        compiler_params=_cp("parallel", "parallel"),
    )(w, m, v, parts)


def _ada_fwd(c_all, w_ada, b_cols, name):
    nb, D = c_all.shape
    n = w_ada.shape[1]
    tn = _tile(n, 512, 128)

    def body(c_ref, w_ref, b_ref, o_ref):
        cv = c_ref[...]
        cond = (cv * _sigmoid(cv)).astype(BF16)
        o_ref[...] = _dot(cond, w_ref[...].astype(BF16), NN) + b_ref[...]

    return pl.pallas_call(
        body, name=name, grid=(n // tn,),
        in_specs=[pl.BlockSpec((nb, D), lambda j: (0, 0)), pl.BlockSpec((D, tn), lambda j: (0, j)),
                  pl.BlockSpec((1, tn), lambda j: (0, j))],
        out_specs=pl.BlockSpec((nb, tn), lambda j: (0, j)), out_shape=SDS((nb, n), F32),
        compiler_params=_cp("parallel"),
    )(c_all, w_ada, b_cols)


def _ada_bwd_adam(c_all, dmod_cols, w, m, v, name):
    nb, D = c_all.shape
    n = w.shape[1]
    tr = _tile(D, 512, 128)
    tn = _tile(n, 1024, 128)

    def body(c_ref, d_ref, w_ref, m_ref, v_ref, g_out, d_out, m_out, v_out):
        cv = c_ref[...]
        cond = (cv * _sigmoid(cv)).astype(BF16)
        g = _dot(cond, d_ref[...].astype(BF16), TN)
        delta, nm, nv = _adam_math(w_ref[...], g, m_ref[...], v_ref[...])
        g_out[...] = g
        d_out[...] = delta
        m_out[...] = nm
        v_out[...] = nv

    spec = pl.BlockSpec((tr, tn), lambda i, j: (i, j))
    return pl.pallas_call(
        body, name=name, grid=(D // tr, n // tn),
        in_specs=[pl.BlockSpec((nb, tr), lambda i, j: (0, i)), pl.BlockSpec((nb, tn), lambda i, j: (0, j)),
                  spec, spec, spec],
        out_specs=[spec] * 4, out_shape=[SDS((D, n), F32)] * 4,
        compiler_params=_cp("parallel", "parallel"),
    )(c_all, dmod_cols, w, m, v)


def _tok(tb, width):
    return pl.BlockSpec((None, tb, width), lambda b, i: (b, i, 0))


def _per_example(width):
    return pl.BlockSpec((None, 1, width), lambda b, i: (b, 0, 0))


def _shared_row(width):
    return pl.BlockSpec((1, width), lambda b, i: (0, 0))


def _norm_mod_fwd(x, g, sh, sc, name, o=None, ga=None):
    B, S, D = x.shape
    tb = _tile(S, ROW_BLOCK, 8)
    fused = o is not None

    def body(*refs):
        if fused:
            x_ref, o_ref, ga_ref, g_ref, sh_ref, sc_ref, x1_ref, h_ref, r_ref = refs
            xv = x_ref[...] + ga_ref[...] * o_ref[...]
            x1_ref[...] = xv
        else:
            x_ref, g_ref, sh_ref, sc_ref, h_ref, r_ref = refs
            xv = x_ref[...]
        rstd = lax.rsqrt(jnp.mean(xv * xv, axis=1, keepdims=True) + EPS)
        y = xv * rstd * g_ref[...]
        h_ref[...] = (y * (1.0 + sc_ref[...]) + sh_ref[...]).astype(BF16)
        r_ref[...] = rstd

    ins = [x] + ([o, ga] if fused else []) + [g, sh, sc]
    in_specs = [_tok(tb, D)] + ([_tok(tb, D), _per_example(D)] if fused else []) + [_shared_row(D), _per_example(D), _per_example(D)]
    out_specs = ([_tok(tb, D)] if fused else []) + [_tok(tb, D), _tok(tb, 1)]
    out_shape = ([SDS((B, S, D), F32)] if fused else []) + [SDS((B, S, D), BF16), SDS((B, S, 1), F32)]
    return pl.pallas_call(
        body, name=name, grid=(B, S // tb), in_specs=in_specs, out_specs=out_specs, out_shape=out_shape,
        compiler_params=_cp("parallel", "parallel"),
    )(*ins)


def _norm_mod_bwd(dh, xin, rstd, sc, g, dres, name, o=None, ga=None):
    B, S, D = xin.shape
    tb = _tile(S, ROW_BLOCK, 8)
    gated = o is not None

    def body(*refs):
        if gated:
            (dh_ref, x_ref, r_ref, sc_ref, g_ref, dres_ref, o_ref, ga_ref,
             dx_ref, dsh_ref, dsc_ref, gg_ref, dga_ref, do_ref) = refs
        else:
            dh_ref, x_ref, r_ref, sc_ref, g_ref, dres_ref, dx_ref, dsh_ref, dsc_ref, gg_ref = refs
        b, i = pl.program_id(0), pl.program_id(1)

        @pl.when(i == 0)
        def _():
            dsh_ref[...] = jnp.zeros_like(dsh_ref)
            dsc_ref[...] = jnp.zeros_like(dsc_ref)
            if gated:
                dga_ref[...] = jnp.zeros_like(dga_ref)

        @pl.when((i == 0) & (b == 0))
        def _():
            gg_ref[...] = jnp.zeros_like(gg_ref)

        dhv = dh_ref[...]
        rs = r_ref[...]
        gv = g_ref[...]
        xhat = x_ref[...] * rs
        dsh_ref[...] += jnp.sum(dhv, axis=0, keepdims=True)
        dsc_ref[...] += jnp.sum(dhv * (xhat * gv), axis=0, keepdims=True)
        dn = dhv * (1.0 + sc_ref[...])
        gg_ref[...] += jnp.sum(dn * xhat, axis=0, keepdims=True)
        dxhat = dn * gv
        cm = jnp.mean(dxhat * xhat, axis=1, keepdims=True)
        dx = dres_ref[...] + rs * (dxhat - xhat * cm)
        dx_ref[...] = dx
        if gated:
            dga_ref[...] += jnp.sum(dx * o_ref[...], axis=0, keepdims=True)
            do_ref[...] = (dx * ga_ref[...]).astype(BF16)

    ins = [dh, xin, rstd, sc, g, dres] + ([o, ga] if gated else [])
    in_specs = [_tok(tb, D), _tok(tb, D), _tok(tb, 1), _per_example(D), _shared_row(D), _tok(tb, D)]
    in_specs += [_tok(tb, D), _per_example(D)] if gated else []
    out_specs = [_tok(tb, D), _per_example(D), _per_example(D), _shared_row(D)]
    out_shape = [SDS((B, S, D), F32), SDS((B, 1, D), F32), SDS((B, 1, D), F32), SDS((1, D), F32)]
    if gated:
        out_specs += [_per_example(D), _tok(tb, D)]
        out_shape += [SDS((B, 1, D), F32), SDS((B, S, D), BF16)]
    return pl.pallas_call(
        body, name=name, grid=(B, S // tb), in_specs=in_specs, out_specs=out_specs, out_shape=out_shape,
        compiler_params=_cp("arbitrary", "arbitrary"),
    )(*ins)


def _loss_head(x1, ff, target, ga2, name):
    B, S, D = x1.shape
    tb = _tile(S, ROW_BLOCK, 8)
    nb, ni = B, S // tb

    def body(x_ref, f_ref, t_ref, ga_ref, loss_ref, dy_ref, dff_ref, dga_ref, acc_ref):
        b, i = pl.program_id(0), pl.program_id(1)

        @pl.when(i == 0)
        def _():
            dga_ref[...] = jnp.zeros_like(dga_ref)

        @pl.when((i == 0) & (b == 0))
        def _():
            acc_ref[...] = jnp.zeros_like(acc_ref)

        fv = f_ref[...]
        gav = ga_ref[...]
        err = x_ref[...] + gav * fv - t_ref[...]
        acc_ref[...] += jnp.sum(err * err, axis=0, keepdims=True)
        dy = err * (1.0 / D)
        dy_ref[...] = dy
        dff_ref[...] = (dy * gav).astype(BF16)
        dga_ref[...] += jnp.sum(dy * fv, axis=0, keepdims=True)

        @pl.when((i == ni - 1) & (b == nb - 1))
        def _():
            loss_ref[...] = jnp.sum(acc_ref[...], axis=1, keepdims=True) * (0.5 / D)

    return pl.pallas_call(
        body, name=name, grid=(B, S // tb),
        in_specs=[_tok(tb, D), _tok(tb, D), _tok(tb, D), _per_example(D)],
        out_specs=[pl.BlockSpec((1, 1), lambda b, i: (0, 0)), _tok(tb, D), _tok(tb, D), _per_example(D)],
        out_shape=[SDS((1, 1), F32), SDS((B, S, D), F32), SDS((B, S, D), BF16), SDS((B, 1, D), F32)],
        scratch_shapes=[pltpu.VMEM((1, D), F32)],
        compiler_params=_cp("arbitrary", "arbitrary"),
    )(x1, ff, target, ga2)


def _causal_mask():
    t = lax.broadcasted_iota(jnp.int32, (CHUNK, CHUNK), 0)
    s = lax.broadcasted_iota(jnp.int32, (CHUNK, CHUNK), 1)
    return s <= t


def _gmlp_fwd(proj, g_v, w_s, b_col, D, GW, name):
    T = proj.shape[0]
    tb = _tile(T, ROW_BLOCK, CHUNK)
    gw = GW // GROUPS
    ucol = (2 * D) // GW

    def body(u_ref, v_ref, g_ref, w_ref, b_ref, ga_ref, r_ref):
        zu = _gelu(u_ref[...])
        zv = _gelu(v_ref[...])
        rstd = lax.rsqrt(jnp.mean(zv * zv, axis=1, keepdims=True) + EPS)
        vn = (zv * rstd * g_ref[...]).astype(BF16)
        mask = _causal_mask()
        for g in range(GROUPS):
            wm = jnp.where(mask, w_ref[g], 0.0).astype(BF16)
            cols = slice(g * gw, (g + 1) * gw)
            for ci in range(tb // CHUNK):
                rows = slice(ci * CHUNK, (ci + 1) * CHUNK)
                mixed = _dot(wm, vn[rows, cols], NN) + b_ref[g]
                ga_ref[rows, cols] = (zu[rows, cols] * mixed).astype(BF16)
        r_ref[...] = rstd

    return pl.pallas_call(
        body, name=name, grid=(T // tb,),
        in_specs=[pl.BlockSpec((tb, GW), lambda i: (i, ucol)), pl.BlockSpec((tb, GW), lambda i: (i, ucol + 1)),
                  pl.BlockSpec((1, GW), lambda i: (0, 0)), pl.BlockSpec((GROUPS, CHUNK, CHUNK), lambda i: (0, 0, 0)),
                  pl.BlockSpec((GROUPS, CHUNK, 1), lambda i: (0, 0, 0))],
        out_specs=[pl.BlockSpec((tb, GW), lambda i: (i, 0)), pl.BlockSpec((tb, 1), lambda i: (i, 0))],
        out_shape=[SDS((T, GW), BF16), SDS((T, 1), F32)],
        compiler_params=_cp("parallel"),
    )(proj, proj, g_v, w_s, b_col)


def _gmlp_bwd(proj, dga, rstd_v, g_v, w_s, b_col, sel, dproj, D, GW, name):
    T = proj.shape[0]
    tb = _tile(T, ROW_BLOCK, CHUNK)
    gw = GW // GROUPS
    ucol = (2 * D) // GW
    assert (2 * D) % (2 * GW) == 0
    uvcol = (2 * D) // (2 * GW)
    nsteps = T // tb

    def body(u_ref, v_ref, dga_ref, r_ref, g_ref, w_ref, b_ref, sel_ref, _dproj_in,
             duv_ref, gws_ref, gbs_ref, gg_ref, accb_ref, dvn_ref):
        step = pl.program_id(0)

        @pl.when(step == 0)
        def _():
            gws_ref[...] = jnp.zeros_like(gws_ref)
            gg_ref[...] = jnp.zeros_like(gg_ref)
            accb_ref[...] = jnp.zeros_like(accb_ref)

        uv = u_ref[...]
        vv = v_ref[...]
        zu = _gelu(uv)
        zv = _gelu(vv)
        rs = r_ref[...]
        gv = g_ref[...]
        vhat = zv * rs
        vnb = (vhat * gv).astype(BF16)
        dgav = dga_ref[...].astype(F32)
        du_gelu = _gelu_grad(uv)
        mask = _causal_mask()
        for g in range(GROUPS):
            wm = jnp.where(mask, w_ref[g], 0.0)
            wmb = wm.astype(BF16)
            wmtb = wm.T.astype(BF16)
            cols = slice(g * gw, (g + 1) * gw)
            for ci in range(tb // CHUNK):
                rows = slice(ci * CHUNK, (ci + 1) * CHUNK)
                vn_blk = vnb[rows, cols]
                mixed = _dot(wmb, vn_blk, NN) + b_ref[g]
                duv_ref[rows, cols] = (dgav[rows, cols] * mixed * du_gelu[rows, cols]).astype(BF16)
                dmix = dgav[rows, cols] * zu[rows, cols]
                accb_ref[:, cols] += dmix
                dmb = dmix.astype(BF16)
                gws_ref[g] += _dot(dmb, vn_blk, NT)
                dvn_ref[rows, cols] = _dot(wmtb, dmb, NN)
        dvn = dvn_ref[...]
        gg_ref[...] += jnp.sum(dvn * vhat, axis=0, keepdims=True)
        dvhat = dvn * gv
        cm = jnp.mean(dvhat * vhat, axis=1, keepdims=True)
        dzv = rs * (dvhat - vhat * cm)
        duv_ref[:, GW:] = (dzv * _gelu_grad(vv)).astype(BF16)

        @pl.when(step == nsteps - 1)
        def _():
            gbs_ref[...] = jnp.dot(accb_ref[...], sel_ref[...], precision=lax.Precision.HIGHEST,
                                   preferred_element_type=F32)
            for g in range(GROUPS):
                gws_ref[g] = jnp.where(mask, gws_ref[g], 0.0)

    return pl.pallas_call(
        body, name=name, grid=(nsteps,),
        in_specs=[pl.BlockSpec((tb, GW), lambda i: (i, ucol)), pl.BlockSpec((tb, GW), lambda i: (i, ucol + 1)),
                  pl.BlockSpec((tb, GW), lambda i: (i, 0)), pl.BlockSpec((tb, 1), lambda i: (i, 0)),
                  pl.BlockSpec((1, GW), lambda i: (0, 0)), pl.BlockSpec((GROUPS, CHUNK, CHUNK), lambda i: (0, 0, 0)),
                  pl.BlockSpec((GROUPS, CHUNK, 1), lambda i: (0, 0, 0)), pl.BlockSpec((GW, LANES), lambda i: (0, 0)),
                  pl.BlockSpec(memory_space=pl.ANY)],
        out_specs=[pl.BlockSpec((tb, 2 * GW), lambda i: (i, uvcol)), pl.BlockSpec((GROUPS, CHUNK, CHUNK), lambda i: (0, 0, 0)),
                   pl.BlockSpec((CHUNK, LANES), lambda i: (0, 0)), pl.BlockSpec((1, GW), lambda i: (0, 0))],
        out_shape=[SDS(dproj.shape, BF16), SDS((GROUPS, CHUNK, CHUNK), F32), SDS((CHUNK, LANES), F32), SDS((1, GW), F32)],
        scratch_shapes=[pltpu.VMEM((CHUNK, GW), F32), pltpu.VMEM((tb, GW), F32)],
        input_output_aliases={8: 0},
        compiler_params=_cp("arbitrary"),
    )(proj, proj, dga, rstd_v, g_v, w_s, b_col, sel, dproj)


def _mix_fwd(proj, y_a, y_b, D, name):
    T = proj.shape[0]
    tb = _tile(T, 512, 8)
    td = _tile(D, 1024, 128)
    nd = D // td

    def body(ga_ref, gb_ref, ya_ref, yb_ref, o_ref):
        o_ref[...] = (_sigmoid(ga_ref[...]) * ya_ref[...] + _sigmoid(gb_ref[...]) * yb_ref[...]).astype(BF16)

    blk = pl.BlockSpec((tb, td), lambda i, j: (i, j))
    return pl.pallas_call(
        body, name=name, grid=(T // tb, nd),
        in_specs=[blk, pl.BlockSpec((tb, td), lambda i, j: (i, j + nd)), blk, blk],
        out_specs=blk, out_shape=SDS((T, D), BF16),
        compiler_params=_cp("parallel", "parallel"),
    )(proj, proj, y_a, y_b)


def _mix_bwd(proj, y_a, y_b, dmixed, D, name):
    T, width = proj.shape
    tb = _tile(T, ROW_BLOCK, 8)

    def body(g_ref, ya_ref, yb_ref, dm_ref, dya_ref, dyb_ref, dp_ref):
        dm = dm_ref[...].astype(F32)
        sa = _sigmoid(g_ref[:, :D])
        sb = _sigmoid(g_ref[:, D:])
        dya_ref[...] = (dm * sa).astype(BF16)
        dyb_ref[...] = (dm * sb).astype(BF16)
        dp_ref[:, :D] = (dm * ya_ref[...] * sa * (1.0 - sa)).astype(BF16)
        dp_ref[:, D:] = (dm * yb_ref[...] * sb * (1.0 - sb)).astype(BF16)

    blk = pl.BlockSpec((tb, D), lambda i: (i, 0))
    gates = pl.BlockSpec((tb, 2 * D), lambda i: (i, 0))
    return pl.pallas_call(
        body, name=name, grid=(T // tb,),
        in_specs=[gates, blk, blk, blk],
        out_specs=[blk, blk, gates], out_shape=[SDS((T, D), BF16), SDS((T, D), BF16), SDS((T, width), BF16)],
        compiler_params=_cp("parallel"),
    )(proj, y_a, y_b, dmixed)


def _lat_norm_fwd(proj, g_q, g_kv, D, GW, QL, KVL, name):
    T = proj.shape[0]
    tb = _tile(T, 512, 8)
    qcol = (2 * D + 2 * GW) // QL
    kvcol = (2 * D + 2 * GW + QL) // KVL

    def body(q_ref, kv_ref, gq_ref, gkv_ref, qn_ref, kvn_ref, rq_ref, rkv_ref):
        qv = q_ref[...]
        rq = lax.rsqrt(jnp.mean(qv * qv, axis=1, keepdims=True) + EPS)
        qn_ref[...] = (qv * rq * gq_ref[...]).astype(BF16)
        rq_ref[...] = rq
        kv = kv_ref[...]
        rkv = lax.rsqrt(jnp.mean(kv * kv, axis=1, keepdims=True) + EPS)
        kvn_ref[...] = (kv * rkv * gkv_ref[...]).astype(BF16)
        rkv_ref[...] = rkv

    return pl.pallas_call(
        body, name=name, grid=(T // tb,),
        in_specs=[pl.BlockSpec((tb, QL), lambda i: (i, qcol)), pl.BlockSpec((tb, KVL), lambda i: (i, kvcol)),
                  pl.BlockSpec((1, QL), lambda i: (0, 0)), pl.BlockSpec((1, KVL), lambda i: (0, 0))],
        out_specs=[pl.BlockSpec((tb, QL), lambda i: (i, 0)), pl.BlockSpec((tb, KVL), lambda i: (i, 0)),
                   pl.BlockSpec((tb, 1), lambda i: (i, 0)), pl.BlockSpec((tb, 1), lambda i: (i, 0))],
        out_shape=[SDS((T, QL), BF16), SDS((T, KVL), BF16), SDS((T, 1), F32), SDS((T, 1), F32)],
        compiler_params=_cp("parallel"),
    )(proj, proj, g_q, g_kv)


def _lat_norm_bwd(proj, dqn, dkvn, dkpe, rq, rkv, g_q, g_kv, dproj, D, GW, QL, KVL, name):
    T, width = proj.shape
    tb = _tile(T, 512, 8)
    qcol = (2 * D + 2 * GW) // QL
    kvcol = (2 * D + 2 * GW + QL) // KVL
    tail = width - (2 * D + 2 * GW)
    assert (2 * D + 2 * GW) % tail == 0 and tail >= QL + KVL + LANES
    tailcol = (2 * D + 2 * GW) // tail

    def one(xv, rs, gv, dy, gg_ref):
        xhat = xv * rs
        gg_ref[...] += jnp.sum(dy * xhat, axis=0, keepdims=True)
        dxhat = dy * gv
        cm = jnp.mean(dxhat * xhat, axis=1, keepdims=True)
        return rs * (dxhat - xhat * cm)

    def body(q_ref, kv_ref, dqn_ref, dkvn_ref, dkpe_ref, rq_ref, rkv_ref, gq_ref, gkv_ref, _dproj_in,
             tail_ref, ggq_ref, ggkv_ref):
        @pl.when(pl.program_id(0) == 0)
        def _():
            ggq_ref[...] = jnp.zeros_like(ggq_ref)
            ggkv_ref[...] = jnp.zeros_like(ggkv_ref)

        tail_ref[:, :QL] = one(q_ref[...], rq_ref[...], gq_ref[...], dqn_ref[...], ggq_ref).astype(BF16)
        tail_ref[:, QL:QL + KVL] = one(kv_ref[...], rkv_ref[...], gkv_ref[...], dkvn_ref[...], ggkv_ref).astype(BF16)
        tail_ref[:, QL + KVL:QL + KVL + LANES] = dkpe_ref[...].astype(BF16)
        if tail > QL + KVL + LANES:
            tail_ref[:, QL + KVL + LANES:] = jnp.zeros((tb, tail - (QL + KVL + LANES)), BF16)

    return pl.pallas_call(
        body, name=name, grid=(T // tb,),
        in_specs=[pl.BlockSpec((tb, QL), lambda i: (i, qcol)), pl.BlockSpec((tb, KVL), lambda i: (i, kvcol)),
                  pl.BlockSpec((tb, QL), lambda i: (i, 0)), pl.BlockSpec((tb, KVL), lambda i: (i, 0)),
                  pl.BlockSpec((tb, LANES), lambda i: (i, 0)),
                  pl.BlockSpec((tb, 1), lambda i: (i, 0)), pl.BlockSpec((tb, 1), lambda i: (i, 0)),
                  pl.BlockSpec((1, QL), lambda i: (0, 0)), pl.BlockSpec((1, KVL), lambda i: (0, 0)),
                  pl.BlockSpec(memory_space=pl.ANY)],
        out_specs=[pl.BlockSpec((tb, tail), lambda i: (i, tailcol)),
                   pl.BlockSpec((1, QL), lambda i: (0, 0)), pl.BlockSpec((1, KVL), lambda i: (0, 0))],
        out_shape=[SDS(dproj.shape, BF16), SDS((1, QL), F32), SDS((1, KVL), F32)],
        input_output_aliases={9: 0},
        compiler_params=_cp("arbitrary"),
    )(proj, proj, dqn, dkvn, dkpe, rq, rkv, g_q, g_kv, dproj)


def _swap_halves(r):
    lane = lax.broadcasted_iota(jnp.int32, r.shape, 1)
    lo = pltpu.roll(r, LANES - HALF_ROPE, 1)
    hi = pltpu.roll(r, HALF_ROPE, 1)
    return jnp.where(lane < HALF_ROPE, lo, jnp.where(lane < ROPE, hi, 0.0))


def _rope_fwd(r, cos_t, sin_t):
    return r * cos_t + _swap_halves(r) * sin_t


def _rope_bwd(d, cos_t, sin_t):
    return d * cos_t + _swap_halves(d * sin_t)


def _head_norm_bwd(xn, xr, rs, g_n, g_r, dyn, dyr):
    xhn, xhr = xn * rs, xr * rs
    dn, dr = dyn * g_n, dyr * g_r
    cm = (jnp.sum(dn * xhn, axis=1, keepdims=True) + jnp.sum(dr * xhr, axis=1, keepdims=True)) * (1.0 / QK_HEAD)
    return rs * (dn - xhn * cm), rs * (dr - xhr * cm), dyn * xhn, dyr * xhr


def _q_up_fwd(qn, w_uq_al, g_al, cos_t, sin_t, name):
    T, QL = qn.shape
    HP = w_uq_al.shape[1]
    tb = _tile(T, 512, 8)
    hw = _tile(HP, 1024, HEAD_PAD)

    def body(x_ref, w_ref, g_ref, c_ref, s_ref, o_ref):
        raw = _dot(x_ref[...], w_ref[...], NN)
        gv = g_ref[...]
        cv, sv = c_ref[...], s_ref[...]
        for h in range(hw // HEAD_PAD):
            xn = raw[:, h * HEAD_PAD:h * HEAD_PAD + LANES]
            xr = raw[:, h * HEAD_PAD + LANES:(h + 1) * HEAD_PAD]
            ss = jnp.sum(xn * xn, axis=1, keepdims=True) + jnp.sum(xr * xr, axis=1, keepdims=True)
            rs = lax.rsqrt(ss * (1.0 / QK_HEAD) + EPS)
            o_ref[:, h * HEAD_PAD:h * HEAD_PAD + LANES] = (xn * rs * gv[:, :LANES]).astype(BF16)
            o_ref[:, h * HEAD_PAD + LANES:(h + 1) * HEAD_PAD] = _rope_fwd(xr * rs * gv[:, LANES:], cv, sv).astype(BF16)

    return pl.pallas_call(
        body, name=name, grid=(T // tb, HP // hw),
        in_specs=[pl.BlockSpec((tb, QL), lambda i, j: (i, 0)), pl.BlockSpec((QL, hw), lambda i, j: (0, j)),
                  pl.BlockSpec((1, HEAD_PAD), lambda i, j: (0, 0)),
                  pl.BlockSpec((tb, LANES), lambda i, j: (i, 0)), pl.BlockSpec((tb, LANES), lambda i, j: (i, 0))],
        out_specs=pl.BlockSpec((tb, hw), lambda i, j: (i, j)), out_shape=SDS((T, HP), BF16),
        compiler_params=_cp("parallel", "parallel"),
    )(qn, w_uq_al, g_al, cos_t, sin_t)


def _q_up_bwd(qn, w_uq_al, g_al, cos_t, sin_t, dq, name):
    T, QL = qn.shape
    HP = w_uq_al.shape[1]
    tb = _tile(T, 512, 8)
    hw = _tile(HP, 1024, HEAD_PAD)

    def body(x_ref, w_ref, g_ref, c_ref, s_ref, dq_ref, o_ref, gg_ref):
        @pl.when((pl.program_id(0) == 0) & (pl.program_id(1) == 0))
        def _():
            gg_ref[...] = jnp.zeros_like(gg_ref)

        raw = _dot(x_ref[...], w_ref[...], NN)
        gv = g_ref[...]
        cv, sv = c_ref[...], s_ref[...]
        for h in range(hw // HEAD_PAD):
            lo, mid, hi = h * HEAD_PAD, h * HEAD_PAD + LANES, (h + 1) * HEAD_PAD
            xn, xr = raw[:, lo:mid], raw[:, mid:hi]
            ss = jnp.sum(xn * xn, axis=1, keepdims=True) + jnp.sum(xr * xr, axis=1, keepdims=True)
            rs = lax.rsqrt(ss * (1.0 / QK_HEAD) + EPS)
            dyn = dq_ref[:, lo:mid]
            dyr = _rope_bwd(dq_ref[:, mid:hi], cv, sv)
            dxn, dxr, ggn, ggr = _head_norm_bwd(xn, xr, rs, gv[:, :LANES], gv[:, LANES:], dyn, dyr)
            o_ref[:, lo:mid] = dxn.astype(BF16)
            o_ref[:, mid:hi] = dxr.astype(BF16)
            gg_ref[:, :LANES] += jnp.sum(ggn, axis=0, keepdims=True)
            gg_ref[:, LANES:] += jnp.sum(ggr, axis=0, keepdims=True)

    return pl.pallas_call(
        body, name=name, grid=(T // tb, HP // hw),
        in_specs=[pl.BlockSpec((tb, QL), lambda i, j: (i, 0)), pl.BlockSpec((QL, hw), lambda i, j: (0, j)),
                  pl.BlockSpec((1, HEAD_PAD), lambda i, j: (0, 0)),
                  pl.BlockSpec((tb, LANES), lambda i, j: (i, 0)), pl.BlockSpec((tb, LANES), lambda i, j: (i, 0)),
                  pl.BlockSpec((tb, hw), lambda i, j: (i, j))],
        out_specs=[pl.BlockSpec((tb, hw), lambda i, j: (i, j)), pl.BlockSpec((1, HEAD_PAD), lambda i, j: (0, 0))],
        out_shape=[SDS((T, HP), BF16), SDS((1, HEAD_PAD), F32)],
        compiler_params=_cp("arbitrary", "arbitrary"),
    )(qn, w_uq_al, g_al, cos_t, sin_t, dq)


def _kv_up_fwd(kvn, w_ukv, proj, g_al, cos_t, sin_t, kpe_col, name):
    T, KVL = kvn.shape
    n_shard = w_ukv.shape[2]
    HP = N_DEV * n_shard
    tb = _tile(T, 512, 8)
    hw = _tile(n_shard, 1024, HEAD_PAD)
    per = n_shard // hw
    nh = hw // HEAD_PAD

    def body(x_ref, w_ref, kpe_ref, g_ref, c_ref, s_ref, k_ref, v_ref):
        raw = _dot(x_ref[...], w_ref[...], NN)
        gv = g_ref[...]
        cv, sv = c_ref[...], s_ref[...]
        kpe = kpe_ref[...]
        kpe_ss = jnp.sum(kpe * kpe, axis=1, keepdims=True)
        for h in range(nh):
            lo, mid, hi = h * HEAD_PAD, h * HEAD_PAD + LANES, (h + 1) * HEAD_PAD
            xn = raw[:, lo:mid]
            rs = lax.rsqrt((jnp.sum(xn * xn, axis=1, keepdims=True) + kpe_ss) * (1.0 / QK_HEAD) + EPS)
            k_ref[:, lo:mid] = (xn * rs * gv[:, :LANES]).astype(BF16)
            k_ref[:, mid:hi] = _rope_fwd(kpe * rs * gv[:, LANES:], cv, sv).astype(BF16)
            v_ref[:, h * V_HEAD:(h + 1) * V_HEAD] = raw[:, mid:hi].astype(BF16)

    return pl.pallas_call(
        body, name=name, grid=(T // tb, HP // hw),
        in_specs=[pl.BlockSpec((tb, KVL), lambda i, j: (i, 0)),
                  pl.BlockSpec((None, KVL, hw), lambda i, j: (j // per, 0, j % per)),
                  pl.BlockSpec((tb, LANES), lambda i, j: (i, kpe_col)),
                  pl.BlockSpec((1, HEAD_PAD), lambda i, j: (0, 0)),
                  pl.BlockSpec((tb, LANES), lambda i, j: (i, 0)), pl.BlockSpec((tb, LANES), lambda i, j: (i, 0))],
        out_specs=[pl.BlockSpec((tb, hw), lambda i, j: (i, j)), pl.BlockSpec((tb, nh * V_HEAD), lambda i, j: (i, j))],
        out_shape=[SDS((T, HP), BF16), SDS((T, HP // 2), BF16)],
        compiler_params=_cp("parallel", "parallel"),
    )(kvn, w_ukv, proj, g_al, cos_t, sin_t)


def _kv_up_bwd(kvn, w_ukv, proj, g_al, cos_t, sin_t, dk, dv, kpe_col, name):
    T, KVL = kvn.shape
    n_shard = w_ukv.shape[2]
    HP = N_DEV * n_shard
    tb = _tile(T, 512, 8)
    hw = _tile(n_shard, 1024, HEAD_PAD)
    per = n_shard // hw
    nh = hw // HEAD_PAD

    def body(x_ref, w_ref, kpe_ref, g_ref, c_ref, s_ref, dk_ref, dv_ref, o_ref, dkpe_ref, gg_ref):
        i, j = pl.program_id(0), pl.program_id(1)

        @pl.when((i == 0) & (j == 0))
        def _():
            gg_ref[...] = jnp.zeros_like(gg_ref)

        @pl.when(j == 0)
        def _():
            dkpe_ref[...] = jnp.zeros_like(dkpe_ref)

        raw = _dot(x_ref[...], w_ref[...], NN)
        gv = g_ref[...]
        cv, sv = c_ref[...], s_ref[...]
        kpe = kpe_ref[...]
        kpe_ss = jnp.sum(kpe * kpe, axis=1, keepdims=True)
        for h in range(nh):
            lo, mid, hi = h * HEAD_PAD, h * HEAD_PAD + LANES, (h + 1) * HEAD_PAD
            xn = raw[:, lo:mid]
            rs = lax.rsqrt((jnp.sum(xn * xn, axis=1, keepdims=True) + kpe_ss) * (1.0 / QK_HEAD) + EPS)
            dyn = dk_ref[:, lo:mid]
            dyr = _rope_bwd(dk_ref[:, mid:hi], cv, sv)
            dxn, dxr, ggn, ggr = _head_norm_bwd(xn, kpe, rs, gv[:, :LANES], gv[:, LANES:], dyn, dyr)
            o_ref[:, lo:mid] = dxn.astype(BF16)
            o_ref[:, mid:hi] = dv_ref[:, h * V_HEAD:(h + 1) * V_HEAD].astype(BF16)
            dkpe_ref[...] += dxr
            gg_ref[:, :LANES] += jnp.sum(ggn, axis=0, keepdims=True)
            gg_ref[:, LANES:] += jnp.sum(ggr, axis=0, keepdims=True)

    return pl.pallas_call(
        body, name=name, grid=(T // tb, HP // hw),
        in_specs=[pl.BlockSpec((tb, KVL), lambda i, j: (i, 0)),
                  pl.BlockSpec((None, KVL, hw), lambda i, j: (j // per, 0, j % per)),
                  pl.BlockSpec((tb, LANES), lambda i, j: (i, kpe_col)),
                  pl.BlockSpec((1, HEAD_PAD), lambda i, j: (0, 0)),
                  pl.BlockSpec((tb, LANES), lambda i, j: (i, 0)), pl.BlockSpec((tb, LANES), lambda i, j: (i, 0)),
                  pl.BlockSpec((tb, hw), lambda i, j: (i, j)), pl.BlockSpec((tb, nh * V_HEAD), lambda i, j: (i, j))],
        out_specs=[pl.BlockSpec((tb, hw), lambda i, j: (i, j)), pl.BlockSpec((tb, LANES), lambda i, j: (i, 0)),
                   pl.BlockSpec((1, HEAD_PAD), lambda i, j: (0, 0))],
        out_shape=[SDS((T, HP), BF16), SDS((T, LANES), F32), SDS((1, HEAD_PAD), F32)],
        compiler_params=_cp("arbitrary", "arbitrary"),
    )(kvn, w_ukv, proj, g_al, cos_t, sin_t, dk, dv)


def _attn_fwd(q, k, v, H, name, comm=None):
    B, S, _ = q.shape
    tq = _tile(S, ATTN_BLOCK, 128)
    scale = QK_HEAD ** -0.5

    def body(q_ref, k_ref, v_ref, o_ref, l_ref):
        qi = pl.program_id(2)
        qv = q_ref[...]

        def step(j, carry, diagonal):
            m, l, acc = carry
            off = pl.multiple_of(j * tq, tq)
            kj = k_ref[pl.ds(off, tq), :]
            vj = v_ref[pl.ds(off, tq), :]
            s = _dot(qv, kj, NT) * scale
            if diagonal:
                row = lax.broadcasted_iota(jnp.int32, (tq, tq), 0)
                col = lax.broadcasted_iota(jnp.int32, (tq, tq), 1)
                s = jnp.where(col <= row, s, -jnp.inf)
            m_new = jnp.maximum(m, jnp.max(s, axis=1, keepdims=True))
            alpha = jnp.exp(m - m_new)
            p = jnp.exp(s - m_new)
            l = alpha * l + jnp.sum(p, axis=1, keepdims=True)
            acc = alpha * acc + _dot(p.astype(BF16), vj, NN)
            return m_new, l, acc

        init = (jnp.full((tq, 1), -1e30, F32), jnp.zeros((tq, 1), F32), jnp.zeros((tq, V_HEAD), F32))
        carry = lax.fori_loop(0, qi, functools.partial(step, diagonal=False), init)
        m, l, acc = step(qi, carry, diagonal=True)
        o_ref[...] = (acc / l).astype(BF16)
        l_ref[...] = m + jnp.log(l)

    grid = (B, H, S // tq)
    in_specs = [pl.BlockSpec((None, tq, HEAD_PAD), lambda b, h, i: (b, i, h)),
                pl.BlockSpec((None, S, HEAD_PAD), lambda b, h, i: (b, 0, h)),
                pl.BlockSpec((None, S, V_HEAD), lambda b, h, i: (b, 0, h))]
    out_specs = [pl.BlockSpec((None, tq, V_HEAD), lambda b, h, i: (b, i, h)),
                 pl.BlockSpec((None, None, tq, 1), lambda b, h, i: (b, h, i, 0))]
    out_shape = [SDS((B, S, H * V_HEAD), BF16), SDS((B, H, S, 1), F32)]
    if comm is None:
        return pl.pallas_call(
            body, name=name, grid=grid, in_specs=in_specs, out_specs=out_specs, out_shape=out_shape,
            compiler_params=_cp("parallel", "parallel", "parallel"),
        )(q, k, v)

    def first_last():
        b, h, i = pl.program_id(0), pl.program_id(1), pl.program_id(2)
        return (b == 0) & (h == 0) & (i == 0), (b == B - 1) & (h == H - 1) & (i == grid[2] - 1)

    outs = pl.pallas_call(
        _carry(body, comm, 3, 2, 0, first_last), name=name, grid=grid,
        in_specs=in_specs + [HBM_SPEC] * len(comm.ins), out_specs=out_specs + [HBM_SPEC] * len(comm.out_shape),
        out_shape=out_shape + comm.out_shape, scratch_shapes=comm.scratch,
        compiler_params=_cp("arbitrary", "arbitrary", "arbitrary"),
    )(q, k, v, *comm.ins)
    return outs[0], outs[1], comm.results(outs[2:])


def _attn_bwd(q, k, v, o, lse, do, H, name, comm=None):
    B, S, _ = q.shape
    tq = _tile(S, ATTN_BLOCK, 128)
    nq = S // tq
    scale = QK_HEAD ** -0.5

    def body(q_ref, k_ref, v_ref, o_ref, l_ref, do_ref, dq_ref, dk_ref, dv_ref, delta_ref, dq_acc):
        dq_acc[...] = jnp.zeros_like(dq_acc)
        for i in range(nq):
            rows = slice(i * tq, (i + 1) * tq)
            delta_ref[rows, :] = jnp.sum(do_ref[rows, :].astype(F32) * o_ref[rows, :].astype(F32), axis=1, keepdims=True)
        def kv_step(j, carry):
            offk = pl.multiple_of(j * tq, tq)
            kj = k_ref[pl.ds(offk, tq), :]
            vj = v_ref[pl.ds(offk, tq), :]

            def q_step(i, acc, diagonal):
                dk_acc, dv_acc = acc
                offq = pl.multiple_of(i * tq, tq)
                qi = q_ref[pl.ds(offq, tq), :]
                doi = do_ref[pl.ds(offq, tq), :]
                s = _dot(qi, kj, NT) * scale
                p = jnp.exp(s - l_ref[pl.ds(offq, tq), :])
                if diagonal:
                    row = lax.broadcasted_iota(jnp.int32, (tq, tq), 0)
                    col = lax.broadcasted_iota(jnp.int32, (tq, tq), 1)
                    p = jnp.where(col <= row, p, 0.0)
                dv_acc = dv_acc + _dot(p.astype(BF16), doi, TN)
                dp = _dot(doi, vj, NT)
                ds = (p * (dp - delta_ref[pl.ds(offq, tq), :]) * scale).astype(BF16)
                dk_acc = dk_acc + _dot(ds, qi, TN)
                dq_acc[pl.ds(offq, tq), :] += _dot(ds, kj, NN)
                return dk_acc, dv_acc

            acc = q_step(j, (jnp.zeros((tq, HEAD_PAD), F32), jnp.zeros((tq, V_HEAD), F32)), diagonal=True)
            dk_acc, dv_acc = lax.fori_loop(j + 1, nq, functools.partial(q_step, diagonal=False), acc)
            dk_ref[pl.ds(offk, tq), :] = dk_acc.astype(BF16)
            dv_ref[pl.ds(offk, tq), :] = dv_acc.astype(BF16)
            return carry

        lax.fori_loop(0, nq, kv_step, 0)
        dq_ref[...] = dq_acc[...].astype(BF16)

    qk_spec = pl.BlockSpec((None, S, HEAD_PAD), lambda b, h: (b, 0, h))
    v_spec = pl.BlockSpec((None, S, V_HEAD), lambda b, h: (b, 0, h))
    in_specs = [qk_spec, qk_spec, v_spec, v_spec, pl.BlockSpec((None, None, S, 1), lambda b, h: (b, h, 0, 0)), v_spec]
    out_specs = [qk_spec, qk_spec, v_spec]
    out_shape = [SDS((B, S, H * HEAD_PAD), BF16), SDS((B, S, H * HEAD_PAD), BF16), SDS((B, S, H * V_HEAD), BF16)]
    scratch = [pltpu.VMEM((S, 1), F32), pltpu.VMEM((S, HEAD_PAD), F32)]
    if comm is None:
        return pl.pallas_call(
            body, name=name, grid=(B, H), in_specs=in_specs, out_specs=out_specs, out_shape=out_shape,
            scratch_shapes=scratch, compiler_params=_cp("parallel", "parallel"),
        )(q, k, v, o, lse, do)

    def first_last():
        b, h = pl.program_id(0), pl.program_id(1)
        return (b == 0) & (h == 0), (b == B - 1) & (h == H - 1)

    outs = pl.pallas_call(
        _carry(body, comm, 6, 3, 2, first_last), name=name, grid=(B, H),
        in_specs=in_specs + [HBM_SPEC] * len(comm.ins), out_specs=out_specs + [HBM_SPEC] * len(comm.out_shape),
        out_shape=out_shape + comm.out_shape, scratch_shapes=scratch + comm.scratch,
        compiler_params=_cp("arbitrary", "arbitrary"),
    )(q, k, v, o, lse, do, *comm.ins)
    return outs[0], outs[1], outs[2], comm.results(outs[3:])


def _natural(sm):
    nd, R, n = sm.shape
    return jnp.transpose(sm, (1, 0, 2)).reshape(R, nd * n)


def _col_shards(full):
    R, N = full.shape
    return jnp.transpose(full.reshape(R, N_DEV, N // N_DEV), (1, 0, 2))


def _shard_rows(sm, lo, hi, n):
    out = []
    for j in range(N_DEV):
        a, b = max(lo, j * n), min(hi, (j + 1) * n)
        if a < b:
            out.append(sm[j, a - j * n:b - j * n])
    return out


def _al_rows(g_t, lo, hi, off_gate, n_gate):
    parts = []
    if lo < off_gate:
        parts.append(g_t[n_gate + lo:n_gate + min(hi, off_gate)])
    if hi > off_gate:
        parts.append(g_t[max(lo, off_gate) - off_gate:hi - off_gate])
    return parts[0] if len(parts) == 1 else jnp.concatenate(parts, axis=0)


def _pad_heads(w, H):
    R = w.shape[0]
    return jnp.pad(w.reshape(R, H, QK_HEAD), ((0, 0), (0, 0), (0, HEAD_PAD - QK_HEAD))).reshape(R, H * HEAD_PAD)


def _unpad_heads(w, H):
    R = w.shape[0]
    return w.reshape(R, H, HEAD_PAD)[:, :, :QK_HEAD].reshape(R, H * QK_HEAD)


def _pad_gain(g):
    return jnp.pad(g, ((0, 0), (0, HEAD_PAD - QK_HEAD)))


def _rope_tables(positions):
    inv_freq = 1.0 / (ROPE_THETA ** (jnp.arange(0, ROPE, 2, dtype=F32) / ROPE))
    ang = positions.astype(F32).reshape(-1, 1) * inv_freq
    cos, sin = jnp.cos(ang), jnp.sin(ang)
    zeros = jnp.zeros((ang.shape[0], LANES - ROPE), F32)
    return jnp.concatenate([cos, cos, zeros], axis=1), jnp.concatenate([-sin, sin, zeros], axis=1)


def kernel(x, c, positions, w_ada, b_ada, g_norm1, w_in, g_v, w_s, b_s, g_q_lat, g_kv_lat, w_uq, w_ukv, g_qn, g_kn, w_branch_a, w_branch_b, w_out, g_norm2, w_ff1, w_ff2, loss_target, m_w_ada, m_b_ada, m_g_norm1, m_w_in, m_g_v, m_w_s, m_b_s, m_g_q_lat, m_g_kv_lat, m_w_uq, m_w_ukv, m_g_qn, m_g_kn, m_w_branch_a, m_w_branch_b, m_w_out, m_g_norm2, m_w_ff1, m_w_ff2, v_w_ada, v_b_ada, v_g_norm1, v_w_in, v_g_v, v_w_s, v_b_s, v_g_q_lat, v_g_kv_lat, v_w_uq, v_w_ukv, v_g_qn, v_g_kn, v_w_branch_a, v_w_branch_b, v_w_out, v_g_norm2, v_w_ff1, v_w_ff2):
    B, S, D = x.shape
    T = B * S
    GW = g_v.shape[-1]
    QL = g_q_lat.shape[-1]
    KVL = g_kv_lat.shape[-1]
    H = w_uq.shape[-1] * N_DEV // QK_HEAD
    IN = w_in.shape[-1] * N_DEV
    OFF_GATE = IN - 2 * D
    IN_AL = _round_up(2 * D + OFF_GATE + (LANES - ROPE), 512)
    assert OFF_GATE == 2 * GW + QL + KVL + ROPE
    assert (2 * D) % GW == 0 and (2 * D + 2 * GW) % QL == 0 and (2 * D + 2 * GW + QL) % KVL == 0
    kpe_col = (2 * D + 2 * GW + QL + KVL) // LANES

    xi, yi, ci = _here()
    dev = 4 * xi + 2 * yi + ci
    c_idx = jnp.reshape(ci, (1,)).astype(jnp.int32)

    big = [w_in, w_uq, w_ukv, w_branch_a, w_branch_b, w_out, w_ff1, w_ff2]
    s_uq, s_ukv, s_ba, s_bb, s_out, s_ff1, s_ff2 = [w[0].astype(BF16) for w in big[1:]]
    n_in = IN // N_DEV
    s_in_t = jnp.transpose(w_in[0]).astype(BF16)
    ((g_in_t,),) = _run_comm([_GatherJob([s_in_t])], "ag_w_in")
    w_al_t = jnp.concatenate(_shard_rows(g_in_t, OFF_GATE, IN, n_in) + _shard_rows(g_in_t, 0, OFF_GATE, n_in)
                             + [jnp.zeros((IN_AL - IN, D), BF16)], axis=0)

    n_ada = w_ada.shape[-1]
    c_all = _all_gather_small(c, "ag_c").reshape(N_DEV * B, D)
    b_cols = lax.dynamic_slice(b_ada, (0, dev * n_ada), (1, n_ada))
    mod_cols = _ada_fwd(c_all, w_ada[0], b_cols, "ada_fwd")
    mod_all = _all_gather_small(mod_cols, "ag_mod")
    mod_mine = lax.dynamic_slice(mod_all, (0, dev * B, 0), (N_DEV, B, n_ada))
    mod_mine = jnp.transpose(mod_mine, (1, 0, 2)).reshape(B, 6, 1, D)
    sh1, sc1, ga1, sh2, sc2, ga2 = [mod_mine[:, k] for k in range(6)]

    cos_t, sin_t = _rope_tables(positions)
    g_qn_al, g_kn_al = _pad_gain(g_qn), _pad_gain(g_kn)
    b_col = b_s[0].reshape(GROUPS, CHUNK, 1)
    gw = GW // GROUPS
    sel = (jnp.arange(GW)[:, None] // gw == jnp.arange(LANES)[None, :]).astype(F32)

    h1, rstd1 = _norm_mod_fwd(x, g_norm1, sh1, sc1, "norm1_fwd")
    h1f = h1.reshape(T, D)
    proj, ((g_uq, g_ukv, g_ba, g_bb, g_out),) = _matmul(
        h1f, w_al_t, mode="nt", out_dtypes=[F32], name="mm_proj", comm=_Comm([_GatherJob([s_uq, s_ukv, s_ba, s_bb, s_out])]))
    w_uq_al = _pad_heads(_natural(g_uq), H)
    w_bb_f = g_bb.reshape(-1, D)
    w_out_f = g_out.reshape(-1, D)
    ga_act, rstd_v = _gmlp_fwd(proj, g_v, w_s[0], b_col, D, GW, "gmlp_fwd")
    y_a = _matmul(ga_act, g_ba, mode="nn", out_dtypes=[BF16], name="mm_ya", b_shards=True)
    qn, kvn, rstd_q, rstd_kv = _lat_norm_fwd(proj, g_q_lat, g_kv_lat, D, GW, QL, KVL, "latnorm_fwd")
    q_al = _q_up_fwd(qn, w_uq_al, g_qn_al, cos_t, sin_t, "q_up_fwd")
    k_al, v_al = _kv_up_fwd(kvn, g_ukv, proj, g_kn_al, cos_t, sin_t, kpe_col, "kv_up_fwd")
    q3, k3, v3 = q_al.reshape(B, S, -1), k_al.reshape(B, S, -1), v_al.reshape(B, S, -1)
    attn, lse, ((g_ff1,),) = _attn_fwd(q3, k3, v3, H, "attn_fwd", comm=_Comm([_GatherJob([s_ff1])]))
    attn_f = attn.reshape(T, H * V_HEAD)
    y_b = _matmul(attn_f, w_bb_f, mode="nn", out_dtypes=[BF16], name="mm_yb")
    mixed = _mix_fwd(proj, y_a, y_b, D, "mix_fwd")
    o = _matmul(mixed, w_out_f, mode="nn", out_dtypes=[F32], name="mm_o")
    x1, h2, rstd2 = _norm_mod_fwd(x, g_norm2, sh2, sc2, "norm2_fwd", o=o.reshape(B, S, D), ga=ga1)

    def relu_sq(acc):
        r = jnp.maximum(acc, 0.0)
        return r * r, r

    (a_act, r_act), ((g_ff2,),) = _matmul(h2.reshape(T, D), g_ff1, mode="nn", out_dtypes=[BF16, BF16], name="mm_ff1",
                                          epi=relu_sq, comm=_Comm([_GatherJob([s_ff2])]), b_shards=True)
    w_ff2_f = g_ff2.reshape(-1, D)
    ff = _matmul(a_act, w_ff2_f, mode="nn", out_dtypes=[F32], name="mm_ff2")
    loss_part, dy, dff, d_ga2 = _loss_head(x1, ff.reshape(B, S, D), loss_target, ga2, "loss_head")
    loss = lax.psum(loss_part[0, 0], ("x", "y", "c"))

    dff_f = dff.reshape(T, D)
    df1 = _matmul(dff_f, w_ff2_f, mode="nt", out_dtypes=[BF16], name="mm_da", epi=lambda acc, r: (acc * (2.0 * r.astype(F32)),),
                  extras=(r_act,))
    gs_ff2 = _matmul(a_act, dff_f, mode="tn", out_dtypes=[BF16], name="mm_gw_ff2").reshape(N_DEV, -1, D)
    gs_ff1, ((sib_ff2,),) = _matmul(h2.reshape(T, D), df1, mode="tn", out_dtypes=[BF16], name="mm_gw_ff1",
                                    comm=_Comm([_SiblingJob([gs_ff2])]), out_shards=True)
    cs_ff2 = _chip_sum(gs_ff2, sib_ff2, c_idx, "chip_sum_w_ff2")
    dh2, ((parts_ff2,), (sib_ff1,)) = _matmul(df1, g_ff1, mode="nt", out_dtypes=[F32], name="mm_dh2", b_shards=True,
                                              comm=_Comm([_ChipsJob([cs_ff2]), _SiblingJob([gs_ff1])]))
    cs_ff1 = _chip_sum(gs_ff1, sib_ff1, c_idx, "chip_sum_w_ff1")
    dx1, d_sh2, d_sc2, gg_norm2, d_ga1, do = _norm_mod_bwd(
        dh2.reshape(B, S, D), x1, rstd2, sc2, g_norm2, dy, "norm2_bwd", o=o.reshape(B, S, D), ga=ga1)

    do_f = do.reshape(T, D)
    dmixed = _matmul(do_f, w_out_f, mode="nt", out_dtypes=[BF16], name="mm_dmixed")
    gw_out = _matmul(mixed, do_f, mode="tn", out_dtypes=[BF16], name="mm_gw_out")
    dy_a, dy_b, dproj = _mix_bwd(proj, y_a, y_b, dmixed, D, "mix_bwd")
    gs_ba = _matmul(ga_act, dy_a, mode="tn", out_dtypes=[BF16], name="mm_gw_ba", out_shards=True)
    dga_act = _matmul(dy_a, g_ba, mode="nt", out_dtypes=[BF16], name="mm_dga", b_shards=True)
    gw_bb = _matmul(attn_f, dy_b, mode="tn", out_dtypes=[BF16], name="mm_gw_bb")
    dattn = _matmul(dy_b, w_bb_f, mode="nt", out_dtypes=[BF16], name="mm_dattn")
    dproj, gg_ws, gg_bs_t, gg_gv = _gmlp_bwd(proj, dga_act, rstd_v, g_v, w_s[0], b_col, sel, dproj, D, GW, "gmlp_bwd")
    dq, dk, dv, ((parts_ff1,),) = _attn_bwd(q3, k3, v3, attn, lse, dattn.reshape(B, S, -1), H, "attn_bwd",
                                            comm=_Comm([_ChipsJob([cs_ff1])]))
    dq_raw, gg_qn = _q_up_bwd(qn, w_uq_al, g_qn_al, cos_t, sin_t, dq.reshape(T, -1), "q_up_bwd")
    dkv_raw, dkpe, gg_kn = _kv_up_bwd(kvn, g_ukv, proj, g_kn_al, cos_t, sin_t, dk.reshape(T, -1), dv.reshape(T, -1),
                                      kpe_col, "kv_up_bwd")
    gw_uq_al = _matmul(qn, dq_raw, mode="tn", out_dtypes=[BF16], name="mm_gw_uq")
    dqn = _matmul(dq_raw, w_uq_al, mode="nt", out_dtypes=[F32], name="mm_dqn")
    gs_ukv = _matmul(kvn, dkv_raw, mode="tn", out_dtypes=[BF16], name="mm_gw_ukv", out_shards=True)
    dkvn = _matmul(dkv_raw, g_ukv, mode="nt", out_dtypes=[F32], name="mm_dkvn", b_shards=True)
    dproj, gg_qlat, gg_kvlat = _lat_norm_bwd(
        proj, dqn, dkvn, dkpe, rstd_q, rstd_kv, g_q_lat, g_kv_lat, dproj, D, GW, QL, KVL, "latnorm_bwd")
    mid_names = ["w_uq", "w_ukv", "w_branch_a", "w_branch_b", "w_out"]
    gs_mid = [_col_shards(_unpad_heads(gw_uq_al, H)), gs_ukv, gs_ba,
              gw_bb.reshape(N_DEV, -1, D), gw_out.reshape(N_DEV, -1, D)]
    (sib_mid,) = _run_comm([_SiblingJob(gs_mid)], "rs_sibling_mid")
    cs_mid = [_chip_sum(g, r, c_idx, "chip_sum_" + nm) for g, r, nm in zip(gs_mid, sib_mid, mid_names)]
    gw_al_t, (parts_mid,) = _matmul(dproj, h1f, mode="tn", out_dtypes=[BF16], name="mm_gw_in", comm=_Comm([_ChipsJob(cs_mid)]))
    gs_in = jnp.stack([_al_rows(gw_al_t, j * n_in, (j + 1) * n_in, OFF_GATE, 2 * D) for j in range(N_DEV)])
    ((sib_in,),) = _run_comm([_SiblingJob([gs_in])], "rs_sibling_in")
    cs_in = _chip_sum(gs_in, sib_in, c_idx, "chip_sum_w_in")
    dh1, ((parts_in,),) = _matmul(dproj, w_al_t, mode="nn", out_dtypes=[F32], name="mm_dh1", comm=_Comm([_ChipsJob([cs_in])]))
    grad_x, d_sh1, d_sc1, gg_norm1 = _norm_mod_bwd(dh1.reshape(B, S, D), x, rstd1, sc1, g_norm1, dx1, "norm1_bwd")

    dmod_mine = jnp.concatenate([d_sh1, d_sc1, d_ga1, d_sh2, d_sc2, d_ga2], axis=2).reshape(B, 6 * D)
    dmod_all = _all_gather_small(dmod_mine, "ag_dmod").reshape(N_DEV * B, 6 * D)
    dmod_cols = lax.dynamic_slice(dmod_all, (0, dev * n_ada), (N_DEV * B, n_ada))
    ada_out = _ada_bwd_adam(c_all, dmod_cols, w_ada[0], m_w_ada[0], v_w_ada[0], "ada_bwd_adam")
    nb_rows = 8
    bada_out = _adam_from_parts(b_ada.reshape(nb_rows, -1), m_b_ada.reshape(nb_rows, -1), v_b_ada.reshape(nb_rows, -1),
                                dmod_all.reshape(N_DEV * B, nb_rows, -1), "adam_b_ada")

    names = ["w_in", "w_uq", "w_ukv", "w_branch_a", "w_branch_b", "w_out", "w_ff1", "w_ff2"]
    parts = [parts_in] + list(parts_mid) + [parts_ff1, parts_ff2]
    ms = [m_w_in, m_w_uq, m_w_ukv, m_w_branch_a, m_w_branch_b, m_w_out, m_w_ff1, m_w_ff2]
    vs = [v_w_in, v_w_uq, v_w_ukv, v_w_branch_a, v_w_branch_b, v_w_out, v_w_ff1, v_w_ff2]
    big_out = {}
    for nm, w, m, v, p in zip(names, big, ms, vs, parts):
        if nm == "w_in":
            res = _adam_from_parts(jnp.transpose(w[0]), jnp.transpose(m[0]), jnp.transpose(v[0]), p, "adam_" + nm)
            big_out[nm] = [jnp.transpose(r) for r in res]
        else:
            big_out[nm] = _adam_from_parts(w[0], m[0], v[0], p, "adam_" + nm)

    small = [("g_norm1", g_norm1, m_g_norm1, v_g_norm1, gg_norm1),
             ("g_v", g_v, m_g_v, v_g_v, gg_gv),
             ("w_s", w_s, m_w_s, v_w_s, gg_ws),
             ("b_s", b_s, m_b_s, v_b_s, jnp.transpose(gg_bs_t[:, :GROUPS])),
             ("g_q_lat", g_q_lat, m_g_q_lat, v_g_q_lat, gg_qlat),
             ("g_kv_lat", g_kv_lat, m_g_kv_lat, v_g_kv_lat, gg_kvlat),
             ("g_qn", g_qn, m_g_qn, v_g_qn, gg_qn[:, :QK_HEAD]),
             ("g_kn", g_kn, m_g_kn, v_g_kn, gg_kn[:, :QK_HEAD]),
             ("g_norm2", g_norm2, m_g_norm2, v_g_norm2, gg_norm2)]
    sizes = [w.size for _, w, _, _, _ in small]
    n_small = sum(sizes)
    n_small_pad = _round_up(n_small, 8 * LANES)

    def flat_cat(arrs):
        return jnp.pad(jnp.concatenate([a.reshape(-1) for a in arrs]), (0, n_small_pad - n_small))

    part_small = _all_gather_small(flat_cat([t[4] for t in small]), "ag_small_grads")
    small_out = _adam_from_parts(
        flat_cat([t[1] for t in small]).reshape(8, -1), flat_cat([t[2] for t in small]).reshape(8, -1),
        flat_cat([t[3] for t in small]).reshape(8, -1), part_small.reshape(N_DEV, 8, -1), "adam_small")
    offs = [sum(sizes[:i]) for i in range(len(sizes))]

    def small_piece(kind, i):
        return small_out[kind].reshape(-1)[offs[i]:offs[i] + sizes[i]].reshape(small[i][1].shape)

    small_idx = {t[0]: i for i, t in enumerate(small)}
    order = ["w_ada", "b_ada", "g_norm1", "w_in", "g_v", "w_s", "b_s", "g_q_lat", "g_kv_lat", "w_uq", "w_ukv", "g_qn", "g_kn",
             "w_branch_a", "w_branch_b", "w_out", "g_norm2", "w_ff1", "w_ff2"]

    def result(kind, nm):
        if nm == "w_ada":
            return ada_out[kind][None]
        if nm == "b_ada":
            return bada_out[kind].reshape(b_ada.shape)
        if nm in small_idx:
            return small_piece(kind, small_idx[nm])
        return big_out[nm][kind][None]

    outs = [loss, grad_x]
    for kind in range(4):
        outs += [result(kind, nm) for nm in order]
    return tuple(outs)
```

```python
import functools
import math

import jax
import jax.numpy as jnp
from jax import lax
from jax.experimental import pallas as pl
from jax.experimental.pallas import tpu as pltpu

F32 = jnp.float32
BF16 = jnp.bfloat16
SDS = jax.ShapeDtypeStruct
MESH = pl.DeviceIdType.MESH

N_DEV = 8
N_CHIP = 4
CHUNK = 128
GROUPS = 8
NOPE = 128
ROPE = 64
HALF_ROPE = ROPE // 2
QK_HEAD = NOPE + ROPE
V_HEAD = 128
HEAD_PAD = 256
LANES = 128
ROPE_THETA = 10000.0
EPS = 1e-6
INV_SQRT2 = 1.0 / math.sqrt(2.0)
INV_SQRT_2PI = 1.0 / math.sqrt(2.0 * math.pi)

ADAM_LR = 0.001
ADAM_B1 = 0.9
ADAM_B2 = 0.999
ADAM_EPS = 1e-08
ADAM_WD = 0.01
ADAM_STEP = 10

V7X_VMEM_LIMIT_BYTES = 56 * 1024 * 1024
MM_TILE = 1024
MM_TILE_K = 2048
ATTN_BLOCK = 512
ROW_BLOCK = 128

NN = (((1,), (0,)), ((), ()))
NT = (((1,), (1,)), ((), ()))
TN = (((0,), (0,)), ((), ()))


def _tile(n, pref, mult):
    t = min(pref, n)
    t -= t % mult
    while t >= mult:
        if n % t == 0:
            return t
        t -= mult
    return n


def _round_up(n, m):
    return (n + m - 1) // m * m


BF16_SUBLANES = 16


def _block_2d(R, C, elems):
    if R % BF16_SUBLANES == 0:
        return _tile(R, max(BF16_SUBLANES, elems // C // BF16_SUBLANES * BF16_SUBLANES), BF16_SUBLANES), C
    return R, _tile(C, max(LANES, elems // R // LANES * LANES), LANES)


def _cp(*sem):
    return pltpu.CompilerParams(dimension_semantics=sem, vmem_limit_bytes=V7X_VMEM_LIMIT_BYTES)


def _dot(a, b, dims):
    return lax.dot_general(a, b, dims, preferred_element_type=F32)


def _gelu(x):
    return 0.5 * x * (1.0 + lax.erf(x * INV_SQRT2))


def _gelu_grad(x):
    return 0.5 * (1.0 + lax.erf(x * INV_SQRT2)) + x * jnp.exp(-0.5 * x * x) * INV_SQRT_2PI


def _sigmoid(x):
    return 1.0 / (1.0 + jnp.exp(-x))


def _matmul(a, b, *, mode, out_dtypes, name, epi=None, extras=(), comm=None, b_shards=False, out_shards=False):
    if mode == "tn":
        K, M = a.shape
    else:
        M, K = a.shape
    n_shard = None
    if b_shards:
        _, R, n_shard = b.shape
        N, Kb = (R, N_DEV * n_shard) if mode == "nt" else (N_DEV * n_shard, R)
    elif mode == "nt":
        N, Kb = b.shape
    else:
        Kb, N = b.shape
    assert K == Kb, (name, a.shape, b.shape)
    tm = _tile(M, MM_TILE, 128)
    tn = _tile(n_shard if (b_shards and mode != "nt") else N, MM_TILE, 128)
    tk = _tile(n_shard if (b_shards and mode == "nt") else K, MM_TILE_K, 128)
    if out_shards:
        assert mode == "tn" and not extras and len(out_dtypes) == 1
        n_shard = N // N_DEV
        tn = _tile(n_shard, MM_TILE, 128)
    nk = K // tk
    n_extra = len(extras)
    n_out = len(out_dtypes)
    dims = {"nn": NN, "nt": NT, "tn": TN}[mode]

    def body(a_ref, b_ref, *rest):
        extra_refs = rest[:n_extra]
        out_refs = rest[n_extra:n_extra + n_out]
        acc_ref = rest[n_extra + n_out]
        k = pl.program_id(2)

        def product():
            return _dot(a_ref[...].astype(BF16), b_ref[...].astype(BF16), dims)

        def finish(acc):
            res = (acc,) if epi is None else epi(acc, *[e[...] for e in extra_refs])
            for o_ref, r in zip(out_refs, res):
                o_ref[...] = r.astype(o_ref.dtype)

        if nk == 1:
            finish(product())
            return

        @pl.when(k == 0)
        def _():
            acc_ref[...] = product()

        if nk > 2:
            @pl.when((k > 0) & (k < nk - 1))
            def _():
                acc_ref[...] += product()

        @pl.when(k == nk - 1)
        def _():
            finish(acc_ref[...] + product())

    if mode == "tn":
        a_spec = pl.BlockSpec((tk, tm), lambda i, j, k: (k, i))
    else:
        a_spec = pl.BlockSpec((tm, tk), lambda i, j, k: (i, k))
    if b_shards and mode == "nt":
        per = n_shard // tk
        b_spec = pl.BlockSpec((None, tn, tk), lambda i, j, k: (k // per, j, k % per))
    elif b_shards:
        per = n_shard // tn
        b_spec = pl.BlockSpec((None, tk, tn), lambda i, j, k: (j // per, k, j % per))
    elif mode == "nt":
        b_spec = pl.BlockSpec((tn, tk), lambda i, j, k: (j, k))
    else:
        b_spec = pl.BlockSpec((tk, tn), lambda i, j, k: (k, j))
    mn_spec = pl.BlockSpec((tm, tn), lambda i, j, k: (i, j))
    grid = (M // tm, N // tn, nk)
    in_specs = [a_spec, b_spec] + [mn_spec] * n_extra
    if out_shards:
        per_out = n_shard // tn
        out_specs = [pl.BlockSpec((None, tm, tn), lambda i, j, k: (j // per_out, i, j % per_out))]
        out_shape = [SDS((N_DEV, M, n_shard), out_dtypes[0])]
    else:
        out_specs = [mn_spec] * n_out
        out_shape = [SDS((M, N), dt) for dt in out_dtypes]
    scratch = [pltpu.VMEM((tm, tn), F32)]
    if comm is None:
        outs = pl.pallas_call(
            body, name=name, grid=grid, in_specs=in_specs, out_specs=out_specs, out_shape=out_shape,
            scratch_shapes=scratch, compiler_params=_cp("parallel", "parallel", "arbitrary"),
        )(a, b, *extras)
        return outs[0] if n_out == 1 else outs

    def first_last():
        i, j, k = pl.program_id(0), pl.program_id(1), pl.program_id(2)
        return ((i == 0) & (j == 0) & (k == 0),
                (i == grid[0] - 1) & (j == grid[1] - 1) & (k == nk - 1))

    outs = pl.pallas_call(
        _carry(body, comm, 2 + n_extra, n_out, 1, first_last), name=name, grid=grid,
        in_specs=in_specs + [HBM_SPEC] * len(comm.ins), out_specs=out_specs + [HBM_SPEC] * len(comm.out_shape),
        out_shape=out_shape + comm.out_shape, scratch_shapes=scratch + comm.scratch,
        compiler_params=_cp("arbitrary", "arbitrary", "arbitrary"),
    )(a, b, *extras, *comm.ins)
    res = outs[0] if n_out == 1 else list(outs[:n_out])
    return res, comm.results(outs[n_out:])


def _here():
    return lax.axis_index("x"), lax.axis_index("y"), lax.axis_index("c")


def _other_chips(x, y):
    return [(1 - x, y), (x, 1 - y), (1 - x, 1 - y)]


def _all_gather_small(v, name):
    shape = v.shape
    n = v.size
    n_pad = _round_up(n, 8 * LANES)
    flat = jnp.pad(v.reshape(-1), (0, n_pad - n)).reshape(8, n_pad // 8)
    m_per, cols = flat.shape

    def body(x_ref, out_ref, send_sems, recv_sems, local_sem):
        x, y, c = _here()
        me, sibling = (x, y, c), (x, y, 1 - c)
        chips = _other_chips(x, y)

        def rows(px, py, pc):
            return out_ref.at[pl.ds((4 * px + 2 * py + pc) * m_per, m_per), :]

        def copy(k, block, to, src=None):
            return pltpu.make_async_remote_copy(
                src_ref=rows(*block) if src is None else src, dst_ref=rows(*block),
                send_sem=send_sems.at[k], recv_sem=recv_sems.at[k], device_id=to, device_id_type=MESH)

        mine = pltpu.make_async_copy(x_ref, rows(*me), local_sem)
        mine.start()
        first = [copy(0, me, sibling, src=x_ref)]
        first += [copy(1 + j, me, (*chip, c), src=x_ref) for j, chip in enumerate(chips)]
        for cp in first:
            cp.start()
        passed = [copy(4 + j, (*chip, c), sibling) for j, chip in enumerate(chips)]
        for j, chip in enumerate(chips):
            copy(1 + j, (*chip, c), me).wait_recv()
            passed[j].start()
        copy(0, sibling, me).wait_recv()
        for j, chip in enumerate(chips):
            copy(4 + j, (*chip, 1 - c), me).wait_recv()
        for cp in first + passed:
            cp.wait_send()
        mine.wait()

    out = pl.pallas_call(
        body, name=name,
        out_shape=SDS((N_DEV * m_per, cols), F32),
        in_specs=[pl.BlockSpec(memory_space=pltpu.VMEM)],
        out_specs=pl.BlockSpec(memory_space=pltpu.VMEM),
        scratch_shapes=[pltpu.SemaphoreType.DMA((7,)), pltpu.SemaphoreType.DMA((7,)), pltpu.SemaphoreType.DMA],
        compiler_params=pltpu.CompilerParams(vmem_limit_bytes=V7X_VMEM_LIMIT_BYTES),
    )(flat)
    return out.reshape(N_DEV, n_pad)[:, :n].reshape((N_DEV,) + shape)


HBM_SPEC = pl.BlockSpec(memory_space=pltpu.HBM)


class _GatherJob:
    def __init__(self, shards):
        self.ins = list(shards)
        self.nw = len(shards)
        self.out_shape = [SDS((N_DEV,) + s.shape, s.dtype) for s in shards]
        self.scratch = [pltpu.SemaphoreType.DMA((7 * self.nw,)), pltpu.SemaphoreType.DMA((7 * self.nw,)),
                        pltpu.SemaphoreType.DMA((self.nw,))]

    def _parts(self, xs, outs, sems):
        send_sems, recv_sems, local_sems = sems
        x, y, c = _here()

        def blk(w, px, py, pc):
            return outs[w].at[4 * px + 2 * py + pc]

        def copy(w, k, block, to, src=None):
            return pltpu.make_async_remote_copy(
                src_ref=blk(w, *block) if src is None else src, dst_ref=blk(w, *block),
                send_sem=send_sems.at[7 * w + k], recv_sem=recv_sems.at[7 * w + k], device_id=to, device_id_type=MESH)

        me, sibling = (x, y, c), (x, y, 1 - c)
        chips = _other_chips(x, y)
        mine = [pltpu.make_async_copy(xs[w], blk(w, *me), local_sems.at[w]) for w in range(self.nw)]
        first = []
        for w in range(self.nw):
            first.append(copy(w, 0, me, sibling, src=xs[w]))
            first += [copy(w, 1 + j, me, (*chip, c), src=xs[w]) for j, chip in enumerate(chips)]
        return copy, me, sibling, chips, c, mine, first

    def start(self, xs, outs, sems):
        _, _, _, _, _, mine, first = self._parts(xs, outs, sems)
        for cp in mine + first:
            cp.start()

    def finish(self, xs, outs, sems):
        copy, me, sibling, chips, c, mine, first = self._parts(xs, outs, sems)
        passed = []
        for w in range(self.nw):
            for j, chip in enumerate(chips):
                copy(w, 1 + j, (*chip, c), me).wait_recv()
                fwd = copy(w, 4 + j, (*chip, c), sibling)
                fwd.start()
                passed.append(fwd)
        for w in range(self.nw):
            copy(w, 0, sibling, me).wait_recv()
            for j, chip in enumerate(chips):
                copy(w, 4 + j, (*chip, 1 - c), me).wait_recv()
        for cp in first + passed:
            cp.wait_send()
        for cp in mine:
            cp.wait()


class _SiblingJob:
    def __init__(self, grads):
        self.ins = list(grads)
        self.nw = len(grads)
        self.out_shape = [SDS((N_CHIP,) + g.shape[1:], g.dtype) for g in grads]
        self.scratch = [pltpu.SemaphoreType.DMA((N_CHIP * self.nw,)), pltpu.SemaphoreType.DMA((N_CHIP * self.nw,))]

    def _copies(self, gs, outs, sems):
        send_sems, recv_sems = sems
        x, y, c = _here()
        return [pltpu.make_async_remote_copy(
            src_ref=gs[w].at[2 * k + (1 - c)], dst_ref=outs[w].at[k],
            send_sem=send_sems.at[N_CHIP * w + k], recv_sem=recv_sems.at[N_CHIP * w + k],
            device_id=(x, y, 1 - c), device_id_type=MESH) for w in range(self.nw) for k in range(N_CHIP)]

    def start(self, gs, outs, sems):
        for cp in self._copies(gs, outs, sems):
            cp.start()

    def finish(self, gs, outs, sems):
        for cp in self._copies(gs, outs, sems):
            cp.wait()


class _ChipsJob:
    def __init__(self, chip_sums):
        self.ins = list(chip_sums)
        self.nw = len(chip_sums)
        self.out_shape = [SDS(s.shape, s.dtype) for s in chip_sums]
        self.scratch = [pltpu.SemaphoreType.DMA((3 * self.nw,)), pltpu.SemaphoreType.DMA((3 * self.nw,)),
                        pltpu.SemaphoreType.DMA((self.nw,))]

    def _parts(self, srcs, outs, sems):
        send_sems, recv_sems, local_sems = sems
        x, y, c = _here()
        my_chip = 2 * x + y
        local = [pltpu.make_async_copy(srcs[w].at[my_chip], outs[w].at[my_chip], local_sems.at[w]) for w in range(self.nw)]
        sends, landed = [], []
        for w in range(self.nw):
            for j, (px, py) in enumerate(_other_chips(x, y)):
                sems_j = dict(send_sem=send_sems.at[3 * w + j], recv_sem=recv_sems.at[3 * w + j],
                              device_id=(px, py, c), device_id_type=MESH)
                sends.append(pltpu.make_async_remote_copy(
                    src_ref=srcs[w].at[2 * px + py], dst_ref=outs[w].at[my_chip], **sems_j))
                landed.append(pltpu.make_async_remote_copy(
                    src_ref=srcs[w].at[2 * px + py], dst_ref=outs[w].at[2 * px + py], **sems_j))
        return local, sends, landed

    def start(self, srcs, outs, sems):
        local, sends, _ = self._parts(srcs, outs, sems)
        for cp in local + sends:
            cp.start()

    def finish(self, srcs, outs, sems):
        local, sends, landed = self._parts(srcs, outs, sems)
        for cp in landed:
            cp.wait_recv()
        for cp in sends:
            cp.wait_send()
        for cp in local:
            cp.wait()


class _Comm:
    def __init__(self, jobs):
        self.jobs = list(jobs)
        self.ins = [a for j in self.jobs for a in j.ins]
        self.out_shape = [s for j in self.jobs for s in j.out_shape]
        self.scratch = [s for j in self.jobs for s in j.scratch]

    def _split(self, flat, counts):
        out, pos = [], 0
        for n in counts:
            out.append(flat[pos:pos + n])
            pos += n
        return out

    def _each(self, ins, outs, sems):
        return zip(self.jobs, self._split(ins, [len(j.ins) for j in self.jobs]),
                   self._split(outs, [len(j.out_shape) for j in self.jobs]),
                   self._split(sems, [len(j.scratch) for j in self.jobs]))

    def start(self, ins, outs, sems):
        for job, i, o, s in self._each(ins, outs, sems):
            job.start(i, o, s)

    def finish(self, ins, outs, sems):
        for job, i, o, s in self._each(ins, outs, sems):
            job.finish(i, o, s)

    def results(self, flat):
        return [list(r) for r in self._split(list(flat), [len(j.out_shape) for j in self.jobs])]


def _carry(body, comm, n_in, n_out, n_scratch, first_last):
    ci, co = len(comm.ins), len(comm.out_shape)

    def wrapped(*refs):
        ins, rest = refs[:n_in + ci], refs[n_in + ci:]
        outs, scr = rest[:n_out + co], rest[n_out + co:]
        c_ins, c_outs, c_sems = ins[n_in:], outs[n_out:], scr[n_scratch:]
        first, last = first_last()

        @pl.when(first)
        def _():
            comm.start(c_ins, c_outs, c_sems)

        body(*ins[:n_in], *outs[:n_out], *scr[:n_scratch])

        @pl.when(last)
        def _():
            comm.finish(c_ins, c_outs, c_sems)

    return wrapped


def _run_comm(jobs, name):
    comm = _Comm(jobs)

    def body(*refs):
        ci, co = len(comm.ins), len(comm.out_shape)
        comm.start(refs[:ci], refs[ci:ci + co], refs[ci + co:])
        comm.finish(refs[:ci], refs[ci:ci + co], refs[ci + co:])

    outs = pl.pallas_call(
        body, name=name, out_shape=comm.out_shape,
        in_specs=[HBM_SPEC] * len(comm.ins), out_specs=[HBM_SPEC] * len(comm.out_shape),
        scratch_shapes=comm.scratch,
    )(*comm.ins)
    return comm.results(outs)


def _chip_sum(g, recv, c_idx, name):
    _, m, n = g.shape
    tr, tc = _block_2d(m, n, 1 << 20)

    def body(c_ref, g_ref, r_ref, o_ref):
        o_ref[...] = (g_ref[...].astype(F32) + r_ref[...].astype(F32)).astype(o_ref.dtype)

    grid_spec = pltpu.PrefetchScalarGridSpec(
        num_scalar_prefetch=1, grid=(N_CHIP, m // tr, n // tc),
        in_specs=[pl.BlockSpec((None, tr, tc), lambda k, i, j, c_ref: (2 * k + c_ref[0], i, j)),
                  pl.BlockSpec((None, tr, tc), lambda k, i, j, c_ref: (k, i, j))],
        out_specs=pl.BlockSpec((None, tr, tc), lambda k, i, j, c_ref: (k, i, j)))
    return pl.pallas_call(
        body, name=name, grid_spec=grid_spec, out_shape=SDS((N_CHIP, m, n), BF16),
        compiler_params=_cp("parallel", "parallel", "parallel"),
    )(c_idx, g, recv)


def _adam_math(w, g, m, v):
    m = ADAM_B1 * m + (1.0 - ADAM_B1) * g
    v = ADAM_B2 * v + (1.0 - ADAM_B2) * (g * g)
    m_hat = m / (1.0 - ADAM_B1 ** ADAM_STEP)
    v_hat = v / (1.0 - ADAM_B2 ** ADAM_STEP)
    delta = -ADAM_LR * (m_hat / (jnp.sqrt(v_hat) + ADAM_EPS) + ADAM_WD * w)
    return delta, m, v


def _adam_from_parts(w, m, v, parts, name):
    R, C = w.shape
    P = parts.shape[0]
    tr, tc = _block_2d(R, C, 1 << 19)

    def body(w_ref, m_ref, v_ref, p_ref, g_out, d_out, m_out, v_out):
        g = p_ref[0].astype(F32)
        for k in range(1, P):
            g = g + p_ref[k].astype(F32)
        delta, nm, nv = _adam_math(w_ref[...], g, m_ref[...], v_ref[...])
        g_out[...] = g
        d_out[...] = delta
        m_out[...] = nm
        v_out[...] = nv

    spec = pl.BlockSpec((tr, tc), lambda i, j: (i, j))
    return pl.pallas_call(
        body, name=name, grid=(R // tr, C // tc),
        in_specs=[spec, spec, spec, pl.BlockSpec((P, tr, tc), lambda i, j: (0, i, j))],
        out_specs=[spec] * 4, out_shape=[SDS((R, C), F32)] * 4,
        compiler_params=_cp("parallel", "parallel"),
    )(w, m, v, parts)


def _ada_fwd(c_all, w_ada, b_cols, name):
    nb, D = c_all.shape
    n = w_ada.shape[1]
    tn = _tile(n, 512, 128)

    def body(c_ref, w_ref, b_ref, o_ref):
        cv = c_ref[...]
        cond = (cv * _sigmoid(cv)).astype(BF16)
        o_ref[...] = _dot(cond, w_ref[...].astype(BF16), NN) + b_ref[...]

    return pl.pallas_call(
        body, name=name, grid=(n // tn,),
        in_specs=[pl.BlockSpec((nb, D), lambda j: (0, 0)), pl.BlockSpec((D, tn), lambda j: (0, j)),
                  pl.BlockSpec((1, tn), lambda j: (0, j))],
        out_specs=pl.BlockSpec((nb, tn), lambda j: (0, j)), out_shape=SDS((nb, n), F32),
        compiler_params=_cp("parallel"),
    )(c_all, w_ada, b_cols)


def _ada_bwd_adam(c_all, dmod_cols, w, m, v, name):
    nb, D = c_all.shape
    n = w.shape[1]
    tr = _tile(D, 512, 128)
    tn = _tile(n, 1024, 128)

    def body(c_ref, d_ref, w_ref, m_ref, v_ref, g_out, d_out, m_out, v_out):
        cv = c_ref[...]
        cond = (cv * _sigmoid(cv)).astype(BF16)
        g = _dot(cond, d_ref[...].astype(BF16), TN)
        delta, nm, nv = _adam_math(w_ref[...], g, m_ref[...], v_ref[...])
        g_out[...] = g
        d_out[...] = delta
        m_out[...] = nm
        v_out[...] = nv

    spec = pl.BlockSpec((tr, tn), lambda i, j: (i, j))
    return pl.pallas_call(
        body, name=name, grid=(D // tr, n // tn),
        in_specs=[pl.BlockSpec((nb, tr), lambda i, j: (0, i)), pl.BlockSpec((nb, tn), lambda i, j: (0, j)),
                  spec, spec, spec],
        out_specs=[spec] * 4, out_shape=[SDS((D, n), F32)] * 4,
        compiler_params=_cp("parallel", "parallel"),
    )(c_all, dmod_cols, w, m, v)


def _tok(tb, width):
    return pl.BlockSpec((None, tb, width), lambda b, i: (b, i, 0))


def _per_example(width):
    return pl.BlockSpec((None, 1, width), lambda b, i: (b, 0, 0))


def _shared_row(width):
    return pl.BlockSpec((1, width), lambda b, i: (0, 0))


def _norm_mod_fwd(x, g, sh, sc, name, o=None, ga=None):
    B, S, D = x.shape
    tb = _tile(S, ROW_BLOCK, 8)
    fused = o is not None

    def body(*refs):
        if fused:
            x_ref, o_ref, ga_ref, g_ref, sh_ref, sc_ref, x1_ref, h_ref, r_ref = refs
            xv = x_ref[...] + ga_ref[...] * o_ref[...]
            x1_ref[...] = xv
        else:
            x_ref, g_ref, sh_ref, sc_ref, h_ref, r_ref = refs
            xv = x_ref[...]
        rstd = lax.rsqrt(jnp.mean(xv * xv, axis=1, keepdims=True) + EPS)
        y = xv * rstd * g_ref[...]
        h_ref[...] = (y * (1.0 + sc_ref[...]) + sh_ref[...]).astype(BF16)
        r_ref[...] = rstd

    ins = [x] + ([o, ga] if fused else []) + [g, sh, sc]
    in_specs = [_tok(tb, D)] + ([_tok(tb, D), _per_example(D)] if fused else []) + [_shared_row(D), _per_example(D), _per_example(D)]
    out_specs = ([_tok(tb, D)] if fused else []) + [_tok(tb, D), _tok(tb, 1)]
    out_shape = ([SDS((B, S, D), F32)] if fused else []) + [SDS((B, S, D), BF16), SDS((B, S, 1), F32)]
    return pl.pallas_call(
        body, name=name, grid=(B, S // tb), in_specs=in_specs, out_specs=out_specs, out_shape=out_shape,
        compiler_params=_cp("parallel", "parallel"),
    )(*ins)


def _norm_mod_bwd(dh, xin, rstd, sc, g, dres, name, o=None, ga=None):
    B, S, D = xin.shape
    tb = _tile(S, ROW_BLOCK, 8)
    gated = o is not None

    def body(*refs):
        if gated:
            (dh_ref, x_ref, r_ref, sc_ref, g_ref, dres_ref, o_ref, ga_ref,
             dx_ref, dsh_ref, dsc_ref, gg_ref, dga_ref, do_ref) = refs
        else:
            dh_ref, x_ref, r_ref, sc_ref, g_ref, dres_ref, dx_ref, dsh_ref, dsc_ref, gg_ref = refs
        b, i = pl.program_id(0), pl.program_id(1)

        @pl.when(i == 0)
        def _():
            dsh_ref[...] = jnp.zeros_like(dsh_ref)
            dsc_ref[...] = jnp.zeros_like(dsc_ref)
            if gated:
                dga_ref[...] = jnp.zeros_like(dga_ref)

        @pl.when((i == 0) & (b == 0))
        def _():
            gg_ref[...] = jnp.zeros_like(gg_ref)

        dhv = dh_ref[...]
        rs = r_ref[...]
        gv = g_ref[...]
        xhat = x_ref[...] * rs
        dsh_ref[...] += jnp.sum(dhv, axis=0, keepdims=True)
        dsc_ref[...] += jnp.sum(dhv * (xhat * gv), axis=0, keepdims=True)
        dn = dhv * (1.0 + sc_ref[...])
        gg_ref[...] += jnp.sum(dn * xhat, axis=0, keepdims=True)
        dxhat = dn * gv
        cm = jnp.mean(dxhat * xhat, axis=1, keepdims=True)
        dx = dres_ref[...] + rs * (dxhat - xhat * cm)
        dx_ref[...] = dx
        if gated:
            dga_ref[...] += jnp.sum(dx * o_ref[...], axis=0, keepdims=True)
            do_ref[...] = (dx * ga_ref[...]).astype(BF16)

    ins = [dh, xin, rstd, sc, g, dres] + ([o, ga] if gated else [])
    in_specs = [_tok(tb, D), _tok(tb, D), _tok(tb, 1), _per_example(D), _shared_row(D), _tok(tb, D)]
    in_specs += [_tok(tb, D), _per_example(D)] if gated else []
    out_specs = [_tok(tb, D), _per_example(D), _per_example(D), _shared_row(D)]
    out_shape = [SDS((B, S, D), F32), SDS((B, 1, D), F32), SDS((B, 1, D), F32), SDS((1, D), F32)]
    if gated:
        out_specs += [_per_example(D), _tok(tb, D)]
        out_shape += [SDS((B, 1, D), F32), SDS((B, S, D), BF16)]
    return pl.pallas_call(
        body, name=name, grid=(B, S // tb), in_specs=in_specs, out_specs=out_specs, out_shape=out_shape,
        compiler_params=_cp("arbitrary", "arbitrary"),
    )(*ins)


def _loss_head(x1, ff, target, ga2, name):
    B, S, D = x1.shape
    tb = _tile(S, ROW_BLOCK, 8)
    nb, ni = B, S // tb

    def body(x_ref, f_ref, t_ref, ga_ref, loss_ref, dy_ref, dff_ref, dga_ref, acc_ref):
        b, i = pl.program_id(0), pl.program_id(1)

        @pl.when(i == 0)
        def _():
            dga_ref[...] = jnp.zeros_like(dga_ref)

        @pl.when((i == 0) & (b == 0))
        def _():
            acc_ref[...] = jnp.zeros_like(acc_ref)

        fv = f_ref[...]
        gav = ga_ref[...]
        err = x_ref[...] + gav * fv - t_ref[...]
        acc_ref[...] += jnp.sum(err * err, axis=0, keepdims=True)
        dy = err * (1.0 / D)
        dy_ref[...] = dy
        dff_ref[...] = (dy * gav).astype(BF16)
        dga_ref[...] += jnp.sum(dy * fv, axis=0, keepdims=True)

        @pl.when((i == ni - 1) & (b == nb - 1))
        def _():
            loss_ref[...] = jnp.sum(acc_ref[...], axis=1, keepdims=True) * (0.5 / D)

    return pl.pallas_call(
        body, name=name, grid=(B, S // tb),
        in_specs=[_tok(tb, D), _tok(tb, D), _tok(tb, D), _per_example(D)],
        out_specs=[pl.BlockSpec((1, 1), lambda b, i: (0, 0)), _tok(tb, D), _tok(tb, D), _per_example(D)],
        out_shape=[SDS((1, 1), F32), SDS((B, S, D), F32), SDS((B, S, D), BF16), SDS((B, 1, D), F32)],
        scratch_shapes=[pltpu.VMEM((1, D), F32)],
        compiler_params=_cp("arbitrary", "arbitrary"),
    )(x1, ff, target, ga2)


def _causal_mask():
    t = lax.broadcasted_iota(jnp.int32, (CHUNK, CHUNK), 0)
    s = lax.broadcasted_iota(jnp.int32, (CHUNK, CHUNK), 1)
    return s <= t


def _gmlp_fwd(proj, g_v, w_s, b_col, D, GW, name):
    T = proj.shape[0]
    tb = _tile(T, ROW_BLOCK, CHUNK)
    gw = GW // GROUPS
    ucol = (2 * D) // GW

    def body(u_ref, v_ref, g_ref, w_ref, b_ref, ga_ref, r_ref):
        zu = _gelu(u_ref[...])
        zv = _gelu(v_ref[...])
        rstd = lax.rsqrt(jnp.mean(zv * zv, axis=1, keepdims=True) + EPS)
        vn = (zv * rstd * g_ref[...]).astype(BF16)
        mask = _causal_mask()
        for g in range(GROUPS):
            wm = jnp.where(mask, w_ref[g], 0.0).astype(BF16)
            cols = slice(g * gw, (g + 1) * gw)
            for ci in range(tb // CHUNK):
                rows = slice(ci * CHUNK, (ci + 1) * CHUNK)
                mixed = _dot(wm, vn[rows, cols], NN) + b_ref[g]
                ga_ref[rows, cols] = (zu[rows, cols] * mixed).astype(BF16)
        r_ref[...] = rstd

    return pl.pallas_call(
        body, name=name, grid=(T // tb,),
        in_specs=[pl.BlockSpec((tb, GW), lambda i: (i, ucol)), pl.BlockSpec((tb, GW), lambda i: (i, ucol + 1)),
                  pl.BlockSpec((1, GW), lambda i: (0, 0)), pl.BlockSpec((GROUPS, CHUNK, CHUNK), lambda i: (0, 0, 0)),
                  pl.BlockSpec((GROUPS, CHUNK, 1), lambda i: (0, 0, 0))],
        out_specs=[pl.BlockSpec((tb, GW), lambda i: (i, 0)), pl.BlockSpec((tb, 1), lambda i: (i, 0))],
        out_shape=[SDS((T, GW), BF16), SDS((T, 1), F32)],
        compiler_params=_cp("parallel"),
    )(proj, proj, g_v, w_s, b_col)


def _gmlp_bwd(proj, dga, rstd_v, g_v, w_s, b_col, sel, dproj, D, GW, name):
    T = proj.shape[0]
    tb = _tile(T, ROW_BLOCK, CHUNK)
    gw = GW // GROUPS
    ucol = (2 * D) // GW
    assert (2 * D) % (2 * GW) == 0
    uvcol = (2 * D) // (2 * GW)
    nsteps = T // tb

    def body(u_ref, v_ref, dga_ref, r_ref, g_ref, w_ref, b_ref, sel_ref, _dproj_in,
             duv_ref, gws_ref, gbs_ref, gg_ref, accb_ref, dvn_ref):
        step = pl.program_id(0)

        @pl.when(step == 0)
        def _():
            gws_ref[...] = jnp.zeros_like(gws_ref)
            gg_ref[...] = jnp.zeros_like(gg_ref)
            accb_ref[...] = jnp.zeros_like(accb_ref)

        uv = u_ref[...]
        vv = v_ref[...]
        zu = _gelu(uv)
        zv = _gelu(vv)
        rs = r_ref[...]
        gv = g_ref[...]
        vhat = zv * rs
        vnb = (vhat * gv).astype(BF16)
        dgav = dga_ref[...].astype(F32)
        du_gelu = _gelu_grad(uv)
        mask = _causal_mask()
        for g in range(GROUPS):
            wm = jnp.where(mask, w_ref[g], 0.0)
            wmb = wm.astype(BF16)
            wmtb = wm.T.astype(BF16)
            cols = slice(g * gw, (g + 1) * gw)
            for ci in range(tb // CHUNK):
                rows = slice(ci * CHUNK, (ci + 1) * CHUNK)
                vn_blk = vnb[rows, cols]
                mixed = _dot(wmb, vn_blk, NN) + b_ref[g]
                duv_ref[rows, cols] = (dgav[rows, cols] * mixed * du_gelu[rows, cols]).astype(BF16)
                dmix = dgav[rows, cols] * zu[rows, cols]
                accb_ref[:, cols] += dmix
                dmb = dmix.astype(BF16)
                gws_ref[g] += _dot(dmb, vn_blk, NT)
                dvn_ref[rows, cols] = _dot(wmtb, dmb, NN)
        dvn = dvn_ref[...]
        gg_ref[...] += jnp.sum(dvn * vhat, axis=0, keepdims=True)
        dvhat = dvn * gv
        cm = jnp.mean(dvhat * vhat, axis=1, keepdims=True)
        dzv = rs * (dvhat - vhat * cm)
        duv_ref[:, GW:] = (dzv * _gelu_grad(vv)).astype(BF16)

        @pl.when(step == nsteps - 1)
        def _():
            gbs_ref[...] = jnp.dot(accb_ref[...], sel_ref[...], precision=lax.Precision.HIGHEST,
                                   preferred_element_type=F32)
            for g in range(GROUPS):
                gws_ref[g] = jnp.where(mask, gws_ref[g], 0.0)

    return pl.pallas_call(
        body, name=name, grid=(nsteps,),
        in_specs=[pl.BlockSpec((tb, GW), lambda i: (i, ucol)), pl.BlockSpec((tb, GW), lambda i: (i, ucol + 1)),
                  pl.BlockSpec((tb, GW), lambda i: (i, 0)), pl.BlockSpec((tb, 1), lambda i: (i, 0)),
                  pl.BlockSpec((1, GW), lambda i: (0, 0)), pl.BlockSpec((GROUPS, CHUNK, CHUNK), lambda i: (0, 0, 0)),
                  pl.BlockSpec((GROUPS, CHUNK, 1), lambda i: (0, 0, 0)), pl.BlockSpec((GW, LANES), lambda i: (0, 0)),
                  pl.BlockSpec(memory_space=pl.ANY)],
        out_specs=[pl.BlockSpec((tb, 2 * GW), lambda i: (i, uvcol)), pl.BlockSpec((GROUPS, CHUNK, CHUNK), lambda i: (0, 0, 0)),
                   pl.BlockSpec((CHUNK, LANES), lambda i: (0, 0)), pl.BlockSpec((1, GW), lambda i: (0, 0))],
        out_shape=[SDS(dproj.shape, BF16), SDS((GROUPS, CHUNK, CHUNK), F32), SDS((CHUNK, LANES), F32), SDS((1, GW), F32)],
        scratch_shapes=[pltpu.VMEM((CHUNK, GW), F32), pltpu.VMEM((tb, GW), F32)],
        input_output_aliases={8: 0},
        compiler_params=_cp("arbitrary"),
    )(proj, proj, dga, rstd_v, g_v, w_s, b_col, sel, dproj)


def _mix_fwd(proj, y_a, y_b, D, name):
    T = proj.shape[0]
    tb = _tile(T, 512, 8)
    td = _tile(D, 1024, 128)
    nd = D // td

    def body(ga_ref, gb_ref, ya_ref, yb_ref, o_ref):
        o_ref[...] = (_sigmoid(ga_ref[...]) * ya_ref[...] + _sigmoid(gb_ref[...]) * yb_ref[...]).astype(BF16)

    blk = pl.BlockSpec((tb, td), lambda i, j: (i, j))
    return pl.pallas_call(
        body, name=name, grid=(T // tb, nd),
        in_specs=[blk, pl.BlockSpec((tb, td), lambda i, j: (i, j + nd)), blk, blk],
        out_specs=blk, out_shape=SDS((T, D), BF16),
        compiler_params=_cp("parallel", "parallel"),
    )(proj, proj, y_a, y_b)


def _mix_bwd(proj, y_a, y_b, dmixed, D, name):
    T, width = proj.shape
    tb = _tile(T, ROW_BLOCK, 8)

    def body(g_ref, ya_ref, yb_ref, dm_ref, dya_ref, dyb_ref, dp_ref):
        dm = dm_ref[...].astype(F32)
        sa = _sigmoid(g_ref[:, :D])
        sb = _sigmoid(g_ref[:, D:])
        dya_ref[...] = (dm * sa).astype(BF16)
        dyb_ref[...] = (dm * sb).astype(BF16)
        dp_ref[:, :D] = (dm * ya_ref[...] * sa * (1.0 - sa)).astype(BF16)
        dp_ref[:, D:] = (dm * yb_ref[...] * sb * (1.0 - sb)).astype(BF16)

    blk = pl.BlockSpec((tb, D), lambda i: (i, 0))
    gates = pl.BlockSpec((tb, 2 * D), lambda i: (i, 0))
    return pl.pallas_call(
        body, name=name, grid=(T // tb,),
        in_specs=[gates, blk, blk, blk],
        out_specs=[blk, blk, gates], out_shape=[SDS((T, D), BF16), SDS((T, D), BF16), SDS((T, width), BF16)],
        compiler_params=_cp("parallel"),
    )(proj, y_a, y_b, dmixed)


def _lat_norm_fwd(proj, g_q, g_kv, D, GW, QL, KVL, name):
    T = proj.shape[0]
    tb = _tile(T, 512, 8)
    qcol = (2 * D + 2 * GW) // QL
    kvcol = (2 * D + 2 * GW + QL) // KVL

    def body(q_ref, kv_ref, gq_ref, gkv_ref, qn_ref, kvn_ref, rq_ref, rkv_ref):
        qv = q_ref[...]
        rq = lax.rsqrt(jnp.mean(qv * qv, axis=1, keepdims=True) + EPS)
        qn_ref[...] = (qv * rq * gq_ref[...]).astype(BF16)
        rq_ref[...] = rq
        kv = kv_ref[...]
        rkv = lax.rsqrt(jnp.mean(kv * kv, axis=1, keepdims=True) + EPS)
        kvn_ref[...] = (kv * rkv * gkv_ref[...]).astype(BF16)
        rkv_ref[...] = rkv

    return pl.pallas_call(
        body, name=name, grid=(T // tb,),
        in_specs=[pl.BlockSpec((tb, QL), lambda i: (i, qcol)), pl.BlockSpec((tb, KVL), lambda i: (i, kvcol)),
                  pl.BlockSpec((1, QL), lambda i: (0, 0)), pl.BlockSpec((1, KVL), lambda i: (0, 0))],
        out_specs=[pl.BlockSpec((tb, QL), lambda i: (i, 0)), pl.BlockSpec((tb, KVL), lambda i: (i, 0)),
                   pl.BlockSpec((tb, 1), lambda i: (i, 0)), pl.BlockSpec((tb, 1), lambda i: (i, 0))],
        out_shape=[SDS((T, QL), BF16), SDS((T, KVL), BF16), SDS((T, 1), F32), SDS((T, 1), F32)],
        compiler_params=_cp("parallel"),
    )(proj, proj, g_q, g_kv)


def _lat_norm_bwd(proj, dqn, dkvn, dkpe, rq, rkv, g_q, g_kv, dproj, D, GW, QL, KVL, name):
    T, width = proj.shape
    tb = _tile(T, 512, 8)
    qcol = (2 * D + 2 * GW) // QL
    kvcol = (2 * D + 2 * GW + QL) // KVL
    tail = width - (2 * D + 2 * GW)
    assert (2 * D + 2 * GW) % tail == 0 and tail >= QL + KVL + LANES
    tailcol = (2 * D + 2 * GW) // tail

    def one(xv, rs, gv, dy, gg_ref):
        xhat = xv * rs
        gg_ref[...] += jnp.sum(dy * xhat, axis=0, keepdims=True)
        dxhat = dy * gv
        cm = jnp.mean(dxhat * xhat, axis=1, keepdims=True)
        return rs * (dxhat - xhat * cm)

    def body(q_ref, kv_ref, dqn_ref, dkvn_ref, dkpe_ref, rq_ref, rkv_ref, gq_ref, gkv_ref, _dproj_in,
             tail_ref, ggq_ref, ggkv_ref):
        @pl.when(pl.program_id(0) == 0)
        def _():
            ggq_ref[...] = jnp.zeros_like(ggq_ref)
            ggkv_ref[...] = jnp.zeros_like(ggkv_ref)

        tail_ref[:, :QL] = one(q_ref[...], rq_ref[...], gq_ref[...], dqn_ref[...], ggq_ref).astype(BF16)
        tail_ref[:, QL:QL + KVL] = one(kv_ref[...], rkv_ref[...], gkv_ref[...], dkvn_ref[...], ggkv_ref).astype(BF16)
        tail_ref[:, QL + KVL:QL + KVL + LANES] = dkpe_ref[...].astype(BF16)
        if tail > QL + KVL + LANES:
            tail_ref[:, QL + KVL + LANES:] = jnp.zeros((tb, tail - (QL + KVL + LANES)), BF16)

    return pl.pallas_call(
        body, name=name, grid=(T // tb,),
        in_specs=[pl.BlockSpec((tb, QL), lambda i: (i, qcol)), pl.BlockSpec((tb, KVL), lambda i: (i, kvcol)),
                  pl.BlockSpec((tb, QL), lambda i: (i, 0)), pl.BlockSpec((tb, KVL), lambda i: (i, 0)),
                  pl.BlockSpec((tb, LANES), lambda i: (i, 0)),
                  pl.BlockSpec((tb, 1), lambda i: (i, 0)), pl.BlockSpec((tb, 1), lambda i: (i, 0)),
                  pl.BlockSpec((1, QL), lambda i: (0, 0)), pl.BlockSpec((1, KVL), lambda i: (0, 0)),
                  pl.BlockSpec(memory_space=pl.ANY)],
        out_specs=[pl.BlockSpec((tb, tail), lambda i: (i, tailcol)),
                   pl.BlockSpec((1, QL), lambda i: (0, 0)), pl.BlockSpec((1, KVL), lambda i: (0, 0))],
        out_shape=[SDS(dproj.shape, BF16), SDS((1, QL), F32), SDS((1, KVL), F32)],
        input_output_aliases={9: 0},
        compiler_params=_cp("arbitrary"),
    )(proj, proj, dqn, dkvn, dkpe, rq, rkv, g_q, g_kv, dproj)


def _swap_halves(r):
    lane = lax.broadcasted_iota(jnp.int32, r.shape, 1)
    lo = pltpu.roll(r, LANES - HALF_ROPE, 1)
    hi = pltpu.roll(r, HALF_ROPE, 1)
    return jnp.where(lane < HALF_ROPE, lo, jnp.where(lane < ROPE, hi, 0.0))


def _rope_fwd(r, cos_t, sin_t):
    return r * cos_t + _swap_halves(r) * sin_t


def _rope_bwd(d, cos_t, sin_t):
    return d * cos_t + _swap_halves(d * sin_t)


def _head_norm_bwd(xn, xr, rs, g_n, g_r, dyn, dyr):
    xhn, xhr = xn * rs, xr * rs
    dn, dr = dyn * g_n, dyr * g_r
    cm = (jnp.sum(dn * xhn, axis=1, keepdims=True) + jnp.sum(dr * xhr, axis=1, keepdims=True)) * (1.0 / QK_HEAD)
    return rs * (dn - xhn * cm), rs * (dr - xhr * cm), dyn * xhn, dyr * xhr


def _q_up_fwd(qn, w_uq_al, g_al, cos_t, sin_t, name):
    T, QL = qn.shape
    HP = w_uq_al.shape[1]
    tb = _tile(T, 512, 8)
    hw = _tile(HP, 1024, HEAD_PAD)

    def body(x_ref, w_ref, g_ref, c_ref, s_ref, o_ref):
        raw = _dot(x_ref[...], w_ref[...], NN)
        gv = g_ref[...]
        cv, sv = c_ref[...], s_ref[...]
        for h in range(hw // HEAD_PAD):
            xn = raw[:, h * HEAD_PAD:h * HEAD_PAD + LANES]
            xr = raw[:, h * HEAD_PAD + LANES:(h + 1) * HEAD_PAD]
            ss = jnp.sum(xn * xn, axis=1, keepdims=True) + jnp.sum(xr * xr, axis=1, keepdims=True)
            rs = lax.rsqrt(ss * (1.0 / QK_HEAD) + EPS)
            o_ref[:, h * HEAD_PAD:h * HEAD_PAD + LANES] = (xn * rs * gv[:, :LANES]).astype(BF16)
            o_ref[:, h * HEAD_PAD + LANES:(h + 1) * HEAD_PAD] = _rope_fwd(xr * rs * gv[:, LANES:], cv, sv).astype(BF16)

    return pl.pallas_call(
        body, name=name, grid=(T // tb, HP // hw),
        in_specs=[pl.BlockSpec((tb, QL), lambda i, j: (i, 0)), pl.BlockSpec((QL, hw), lambda i, j: (0, j)),
                  pl.BlockSpec((1, HEAD_PAD), lambda i, j: (0, 0)),
                  pl.BlockSpec((tb, LANES), lambda i, j: (i, 0)), pl.BlockSpec((tb, LANES), lambda i, j: (i, 0))],
        out_specs=pl.BlockSpec((tb, hw), lambda i, j: (i, j)), out_shape=SDS((T, HP), BF16),
        compiler_params=_cp("parallel", "parallel"),
    )(qn, w_uq_al, g_al, cos_t, sin_t)


def _q_up_bwd(qn, w_uq_al, g_al, cos_t, sin_t, dq, name):
    T, QL = qn.shape
    HP = w_uq_al.shape[1]
    tb = _tile(T, 512, 8)
    hw = _tile(HP, 1024, HEAD_PAD)

    def body(x_ref, w_ref, g_ref, c_ref, s_ref, dq_ref, o_ref, gg_ref):
        @pl.when((pl.program_id(0) == 0) & (pl.program_id(1) == 0))
        def _():
            gg_ref[...] = jnp.zeros_like(gg_ref)

        raw = _dot(x_ref[...], w_ref[...], NN)
        gv = g_ref[...]
        cv, sv = c_ref[...], s_ref[...]
        for h in range(hw // HEAD_PAD):
            lo, mid, hi = h * HEAD_PAD, h * HEAD_PAD + LANES, (h + 1) * HEAD_PAD
            xn, xr = raw[:, lo:mid], raw[:, mid:hi]
            ss = jnp.sum(xn * xn, axis=1, keepdims=True) + jnp.sum(xr * xr, axis=1, keepdims=True)
            rs = lax.rsqrt(ss * (1.0 / QK_HEAD) + EPS)
            dyn = dq_ref[:, lo:mid]
            dyr = _rope_bwd(dq_ref[:, mid:hi], cv, sv)
            dxn, dxr, ggn, ggr = _head_norm_bwd(xn, xr, rs, gv[:, :LANES], gv[:, LANES:], dyn, dyr)
            o_ref[:, lo:mid] = dxn.astype(BF16)
            o_ref[:, mid:hi] = dxr.astype(BF16)
            gg_ref[:, :LANES] += jnp.sum(ggn, axis=0, keepdims=True)
            gg_ref[:, LANES:] += jnp.sum(ggr, axis=0, keepdims=True)

    return pl.pallas_call(
        body, name=name, grid=(T // tb, HP // hw),
        in_specs=[pl.BlockSpec((tb, QL), lambda i, j: (i, 0)), pl.BlockSpec((QL, hw), lambda i, j: (0, j)),
                  pl.BlockSpec((1, HEAD_PAD), lambda i, j: (0, 0)),
                  pl.BlockSpec((tb, LANES), lambda i, j: (i, 0)), pl.BlockSpec((tb, LANES), lambda i, j: (i, 0)),
                  pl.BlockSpec((tb, hw), lambda i, j: (i, j))],
        out_specs=[pl.BlockSpec((tb, hw), lambda i, j: (i, j)), pl.BlockSpec((1, HEAD_PAD), lambda i, j: (0, 0))],
        out_shape=[SDS((T, HP), BF16), SDS((1, HEAD_PAD), F32)],
        compiler_params=_cp("arbitrary", "arbitrary"),
    )(qn, w_uq_al, g_al, cos_t, sin_t, dq)


def _kv_up_fwd(kvn, w_ukv, proj, g_al, cos_t, sin_t, kpe_col, name):
    T, KVL = kvn.shape
    n_shard = w_ukv.shape[2]
    HP = N_DEV * n_shard
    tb = _tile(T, 512, 8)
    hw = _tile(n_shard, 1024, HEAD_PAD)
    per = n_shard // hw
    nh = hw // HEAD_PAD

    def body(x_ref, w_ref, kpe_ref, g_ref, c_ref, s_ref, k_ref, v_ref):
        raw = _dot(x_ref[...], w_ref[...], NN)
        gv = g_ref[...]
        cv, sv = c_ref[...], s_ref[...]
        kpe = kpe_ref[...]
        kpe_ss = jnp.sum(kpe * kpe, axis=1, keepdims=True)
        for h in range(nh):
            lo, mid, hi = h * HEAD_PAD, h * HEAD_PAD + LANES, (h + 1) * HEAD_PAD
            xn = raw[:, lo:mid]
            rs = lax.rsqrt((jnp.sum(xn * xn, axis=1, keepdims=True) + kpe_ss) * (1.0 / QK_HEAD) + EPS)
            k_ref[:, lo:mid] = (xn * rs * gv[:, :LANES]).astype(BF16)
            k_ref[:, mid:hi] = _rope_fwd(kpe * rs * gv[:, LANES:], cv, sv).astype(BF16)
            v_ref[:, h * V_HEAD:(h + 1) * V_HEAD] = raw[:, mid:hi].astype(BF16)

    return pl.pallas_call(
        body, name=name, grid=(T // tb, HP // hw),
        in_specs=[pl.BlockSpec((tb, KVL), lambda i, j: (i, 0)),
                  pl.BlockSpec((None, KVL, hw), lambda i, j: (j // per, 0, j % per)),
                  pl.BlockSpec((tb, LANES), lambda i, j: (i, kpe_col)),
                  pl.BlockSpec((1, HEAD_PAD), lambda i, j: (0, 0)),
                  pl.BlockSpec((tb, LANES), lambda i, j: (i, 0)), pl.BlockSpec((tb, LANES), lambda i, j: (i, 0))],
        out_specs=[pl.BlockSpec((tb, hw), lambda i, j: (i, j)), pl.BlockSpec((tb, nh * V_HEAD), lambda i, j: (i, j))],
        out_shape=[SDS((T, HP), BF16), SDS((T, HP // 2), BF16)],
        compiler_params=_cp("parallel", "parallel"),
    )(kvn, w_ukv, proj, g_al, cos_t, sin_t)


def _kv_up_bwd(kvn, w_ukv, proj, g_al, cos_t, sin_t, dk, dv, kpe_col, name):
    T, KVL = kvn.shape
    n_shard = w_ukv.shape[2]
    HP = N_DEV * n_shard
    tb = _tile(T, 512, 8)
    hw = _tile(n_shard, 1024, HEAD_PAD)
    per = n_shard // hw
    nh = hw // HEAD_PAD

    def body(x_ref, w_ref, kpe_ref, g_ref, c_ref, s_ref, dk_ref, dv_ref, o_ref, dkpe_ref, gg_ref):
        i, j = pl.program_id(0), pl.program_id(1)

        @pl.when((i == 0) & (j == 0))
        def _():
            gg_ref[...] = jnp.zeros_like(gg_ref)

        @pl.when(j == 0)
        def _():
            dkpe_ref[...] = jnp.zeros_like(dkpe_ref)

        raw = _dot(x_ref[...], w_ref[...], NN)
        gv = g_ref[...]
        cv, sv = c_ref[...], s_ref[...]
        kpe = kpe_ref[...]
        kpe_ss = jnp.sum(kpe * kpe, axis=1, keepdims=True)
        for h in range(nh):
            lo, mid, hi = h * HEAD_PAD, h * HEAD_PAD + LANES, (h + 1) * HEAD_PAD
            xn = raw[:, lo:mid]
            rs = lax.rsqrt((jnp.sum(xn * xn, axis=1, keepdims=True) + kpe_ss) * (1.0 / QK_HEAD) + EPS)
            dyn = dk_ref[:, lo:mid]
            dyr = _rope_bwd(dk_ref[:, mid:hi], cv, sv)
            dxn, dxr, ggn, ggr = _head_norm_bwd(xn, kpe, rs, gv[:, :LANES], gv[:, LANES:], dyn, dyr)
            o_ref[:, lo:mid] = dxn.astype(BF16)
            o_ref[:, mid:hi] = dv_ref[:, h * V_HEAD:(h + 1) * V_HEAD].astype(BF16)
            dkpe_ref[...] += dxr
            gg_ref[:, :LANES] += jnp.sum(ggn, axis=0, keepdims=True)
            gg_ref[:, LANES:] += jnp.sum(ggr, axis=0, keepdims=True)

    return pl.pallas_call(
        body, name=name, grid=(T // tb, HP // hw),
        in_specs=[pl.BlockSpec((tb, KVL), lambda i, j: (i, 0)),
                  pl.BlockSpec((None, KVL, hw), lambda i, j: (j // per, 0, j % per)),
                  pl.BlockSpec((tb, LANES), lambda i, j: (i, kpe_col)),
                  pl.BlockSpec((1, HEAD_PAD), lambda i, j: (0, 0)),
                  pl.BlockSpec((tb, LANES), lambda i, j: (i, 0)), pl.BlockSpec((tb, LANES), lambda i, j: (i, 0)),
                  pl.BlockSpec((tb, hw), lambda i, j: (i, j)), pl.BlockSpec((tb, nh * V_HEAD), lambda i, j: (i, j))],
        out_specs=[pl.BlockSpec((tb, hw), lambda i, j: (i, j)), pl.BlockSpec((tb, LANES), lambda i, j: (i, 0)),
                   pl.BlockSpec((1, HEAD_PAD), lambda i, j: (0, 0))],
        out_shape=[SDS((T, HP), BF16), SDS((T, LANES), F32), SDS((1, HEAD_PAD), F32)],
        compiler_params=_cp("arbitrary", "arbitrary"),
    )(kvn, w_ukv, proj, g_al, cos_t, sin_t, dk, dv)


def _attn_fwd(q, k, v, H, name, comm=None):
    B, S, _ = q.shape
    tq = _tile(S, ATTN_BLOCK, 128)
    scale = QK_HEAD ** -0.5

    def body(q_ref, k_ref, v_ref, o_ref, l_ref):
        qi = pl.program_id(2)
        qv = q_ref[...]

        def step(j, carry, diagonal):
            m, l, acc = carry
            off = pl.multiple_of(j * tq, tq)
            kj = k_ref[pl.ds(off, tq), :]
            vj = v_ref[pl.ds(off, tq), :]
            s = _dot(qv, kj, NT) * scale
            if diagonal:
                row = lax.broadcasted_iota(jnp.int32, (tq, tq), 0)
                col = lax.broadcasted_iota(jnp.int32, (tq, tq), 1)
                s = jnp.where(col <= row, s, -jnp.inf)
            m_new = jnp.maximum(m, jnp.max(s, axis=1, keepdims=True))
            alpha = jnp.exp(m - m_new)
            p = jnp.exp(s - m_new)
            l = alpha * l + jnp.sum(p, axis=1, keepdims=True)
            acc = alpha * acc + _dot(p.astype(BF16), vj, NN)
            return m_new, l, acc

        init = (jnp.full((tq, 1), -1e30, F32), jnp.zeros((tq, 1), F32), jnp.zeros((tq, V_HEAD), F32))
        carry = lax.fori_loop(0, qi, functools.partial(step, diagonal=False), init)
        m, l, acc = step(qi, carry, diagonal=True)
        o_ref[...] = (acc / l).astype(BF16)
        l_ref[...] = m + jnp.log(l)

    grid = (B, H, S // tq)
    in_specs = [pl.BlockSpec((None, tq, HEAD_PAD), lambda b, h, i: (b, i, h)),
                pl.BlockSpec((None, S, HEAD_PAD), lambda b, h, i: (b, 0, h)),
                pl.BlockSpec((None, S, V_HEAD), lambda b, h, i: (b, 0, h))]
    out_specs = [pl.BlockSpec((None, tq, V_HEAD), lambda b, h, i: (b, i, h)),
                 pl.BlockSpec((None, None, tq, 1), lambda b, h, i: (b, h, i, 0))]
    out_shape = [SDS((B, S, H * V_HEAD), BF16), SDS((B, H, S, 1), F32)]
    if comm is None:
        return pl.pallas_call(
            body, name=name, grid=grid, in_specs=in_specs, out_specs=out_specs, out_shape=out_shape,
            compiler_params=_cp("parallel", "parallel", "parallel"),
        )(q, k, v)

    def first_last():
        b, h, i = pl.program_id(0), pl.program_id(1), pl.program_id(2)
        return (b == 0) & (h == 0) & (i == 0), (b == B - 1) & (h == H - 1) & (i == grid[2] - 1)

    outs = pl.pallas_call(
        _carry(body, comm, 3, 2, 0, first_last), name=name, grid=grid,
        in_specs=in_specs + [HBM_SPEC] * len(comm.ins), out_specs=out_specs + [HBM_SPEC] * len(comm.out_shape),
        out_shape=out_shape + comm.out_shape, scratch_shapes=comm.scratch,
        compiler_params=_cp("arbitrary", "arbitrary", "arbitrary"),
    )(q, k, v, *comm.ins)
    return outs[0], outs[1], comm.results(outs[2:])


def _attn_bwd(q, k, v, o, lse, do, H, name, comm=None):
    B, S, _ = q.shape
    tq = _tile(S, ATTN_BLOCK, 128)
    nq = S // tq
    scale = QK_HEAD ** -0.5

    def body(q_ref, k_ref, v_ref, o_ref, l_ref, do_ref, dq_ref, dk_ref, dv_ref, delta_ref, dq_acc):
        dq_acc[...] = jnp.zeros_like(dq_acc)
        for i in range(nq):
            rows = slice(i * tq, (i + 1) * tq)
            delta_ref[rows, :] = jnp.sum(do_ref[rows, :].astype(F32) * o_ref[rows, :].astype(F32), axis=1, keepdims=True)
        def kv_step(j, carry):
            offk = pl.multiple_of(j * tq, tq)
            kj = k_ref[pl.ds(offk, tq), :]
            vj = v_ref[pl.ds(offk, tq), :]

            def q_step(i, acc, diagonal):
                dk_acc, dv_acc = acc
                offq = pl.multiple_of(i * tq, tq)
                qi = q_ref[pl.ds(offq, tq), :]
                doi = do_ref[pl.ds(offq, tq), :]
                s = _dot(qi, kj, NT) * scale
                p = jnp.exp(s - l_ref[pl.ds(offq, tq), :])
                if diagonal:
                    row = lax.broadcasted_iota(jnp.int32, (tq, tq), 0)
                    col = lax.broadcasted_iota(jnp.int32, (tq, tq), 1)
                    p = jnp.where(col <= row, p, 0.0)
                dv_acc = dv_acc + _dot(p.astype(BF16), doi, TN)
                dp = _dot(doi, vj, NT)
                ds = (p * (dp - delta_ref[pl.ds(offq, tq), :]) * scale).astype(BF16)
                dk_acc = dk_acc + _dot(ds, qi, TN)
                dq_acc[pl.ds(offq, tq), :] += _dot(ds, kj, NN)
                return dk_acc, dv_acc

            acc = q_step(j, (jnp.zeros((tq, HEAD_PAD), F32), jnp.zeros((tq, V_HEAD), F32)), diagonal=True)
            dk_acc, dv_acc = lax.fori_loop(j + 1, nq, functools.partial(q_step, diagonal=False), acc)
            dk_ref[pl.ds(offk, tq), :] = dk_acc.astype(BF16)
            dv_ref[pl.ds(offk, tq), :] = dv_acc.astype(BF16)
            return carry

        lax.fori_loop(0, nq, kv_step, 0)
        dq_ref[...] = dq_acc[...].astype(BF16)

    qk_spec = pl.BlockSpec((None, S, HEAD_PAD), lambda b, h: (b, 0, h))
    v_spec = pl.BlockSpec((None, S, V_HEAD), lambda b, h: (b, 0, h))
    in_specs = [qk_spec, qk_spec, v_spec, v_spec, pl.BlockSpec((None, None, S, 1), lambda b, h: (b, h, 0, 0)), v_spec]
    out_specs = [qk_spec, qk_spec, v_spec]
    out_shape = [SDS((B, S, H * HEAD_PAD), BF16), SDS((B, S, H * HEAD_PAD), BF16), SDS((B, S, H * V_HEAD), BF16)]
    scratch = [pltpu.VMEM((S, 1), F32), pltpu.VMEM((S, HEAD_PAD), F32)]
    if comm is None:
        return pl.pallas_call(
            body, name=name, grid=(B, H), in_specs=in_specs, out_specs=out_specs, out_shape=out_shape,
            scratch_shapes=scratch, compiler_params=_cp("parallel", "parallel"),
        )(q, k, v, o, lse, do)

    def first_last():
        b, h = pl.program_id(0), pl.program_id(1)
        return (b == 0) & (h == 0), (b == B - 1) & (h == H - 1)

    outs = pl.pallas_call(
        _carry(body, comm, 6, 3, 2, first_last), name=name, grid=(B, H),
        in_specs=in_specs + [HBM_SPEC] * len(comm.ins), out_specs=out_specs + [HBM_SPEC] * len(comm.out_shape),
        out_shape=out_shape + comm.out_shape, scratch_shapes=scratch + comm.scratch,
        compiler_params=_cp("arbitrary", "arbitrary"),
    )(q, k, v, o, lse, do, *comm.ins)
    return outs[0], outs[1], outs[2], comm.results(outs[3:])


def _natural(sm):
    nd, R, n = sm.shape
    return jnp.transpose(sm, (1, 0, 2)).reshape(R, nd * n)


def _col_shards(full):
    R, N = full.shape
    return jnp.transpose(full.reshape(R, N_DEV, N // N_DEV), (1, 0, 2))


def _shard_rows(sm, lo, hi, n):
    out = []
    for j in range(N_DEV):
        a, b = max(lo, j * n), min(hi, (j + 1) * n)
        if a < b:
            out.append(sm[j, a - j * n:b - j * n])
    return out


def _al_rows(g_t, lo, hi, off_gate, n_gate):
    parts = []
    if lo < off_gate:
        parts.append(g_t[n_gate + lo:n_gate + min(hi, off_gate)])
    if hi > off_gate:
        parts.append(g_t[max(lo, off_gate) - off_gate:hi - off_gate])
    return parts[0] if len(parts) == 1 else jnp.concatenate(parts, axis=0)


def _pad_heads(w, H):
    R = w.shape[0]
    return jnp.pad(w.reshape(R, H, QK_HEAD), ((0, 0), (0, 0), (0, HEAD_PAD - QK_HEAD))).reshape(R, H * HEAD_PAD)


def _unpad_heads(w, H):
    R = w.shape[0]
    return w.reshape(R, H, HEAD_PAD)[:, :, :QK_HEAD].reshape(R, H * QK_HEAD)


def _pad_gain(g):
    return jnp.pad(g, ((0, 0), (0, HEAD_PAD - QK_HEAD)))


def _rope_tables(positions):
    inv_freq = 1.0 / (ROPE_THETA ** (jnp.arange(0, ROPE, 2, dtype=F32) / ROPE))
    ang = positions.astype(F32).reshape(-1, 1) * inv_freq
    cos, sin = jnp.cos(ang), jnp.sin(ang)
    zeros = jnp.zeros((ang.shape[0], LANES - ROPE), F32)
    return jnp.concatenate([cos, cos, zeros], axis=1), jnp.concatenate([-sin, sin, zeros], axis=1)


def kernel(x, c, positions, w_ada, b_ada, g_norm1, w_in, g_v, w_s, b_s, g_q_lat, g_kv_lat, w_uq, w_ukv, g_qn, g_kn, w_branch_a, w_branch_b, w_out, g_norm2, w_ff1, w_ff2, loss_target, m_w_ada, m_b_ada, m_g_norm1, m_w_in, m_g_v, m_w_s, m_b_s, m_g_q_lat, m_g_kv_lat, m_w_uq, m_w_ukv, m_g_qn, m_g_kn, m_w_branch_a, m_w_branch_b, m_w_out, m_g_norm2, m_w_ff1, m_w_ff2, v_w_ada, v_b_ada, v_g_norm1, v_w_in, v_g_v, v_w_s, v_b_s, v_g_q_lat, v_g_kv_lat, v_w_uq, v_w_ukv, v_g_qn, v_g_kn, v_w_branch_a, v_w_branch_b, v_w_out, v_g_norm2, v_w_ff1, v_w_ff2):
    B, S, D = x.shape
    T = B * S
    GW = g_v.shape[-1]
    QL = g_q_lat.shape[-1]
    KVL = g_kv_lat.shape[-1]
    H = w_uq.shape[-1] * N_DEV // QK_HEAD
    IN = w_in.shape[-1] * N_DEV
    OFF_GATE = IN - 2 * D
    IN_AL = _round_up(2 * D + OFF_GATE + (LANES - ROPE), 512)
    assert OFF_GATE == 2 * GW + QL + KVL + ROPE
    assert (2 * D) % GW == 0 and (2 * D + 2 * GW) % QL == 0 and (2 * D + 2 * GW + QL) % KVL == 0
    kpe_col = (2 * D + 2 * GW + QL + KVL) // LANES

    xi, yi, ci = _here()
    dev = 4 * xi + 2 * yi + ci
    c_idx = jnp.reshape(ci, (1,)).astype(jnp.int32)

    big = [w_in, w_uq, w_ukv, w_branch_a, w_branch_b, w_out, w_ff1, w_ff2]
    s_uq, s_ukv, s_ba, s_bb, s_out, s_ff1, s_ff2 = [w[0].astype(BF16) for w in big[1:]]
    n_in = IN // N_DEV
    s_in_t = jnp.transpose(w_in[0]).astype(BF16)
    ((g_in_t,),) = _run_comm([_GatherJob([s_in_t])], "ag_w_in")
    w_al_t = jnp.concatenate(_shard_rows(g_in_t, OFF_GATE, IN, n_in) + _shard_rows(g_in_t, 0, OFF_GATE, n_in)
                             + [jnp.zeros((IN_AL - IN, D), BF16)], axis=0)

    n_ada = w_ada.shape[-1]
    c_all = _all_gather_small(c, "ag_c").reshape(N_DEV * B, D)
    b_cols = lax.dynamic_slice(b_ada, (0, dev * n_ada), (1, n_ada))
    mod_cols = _ada_fwd(c_all, w_ada[0], b_cols, "ada_fwd")
    mod_all = _all_gather_small(mod_cols, "ag_mod")
    mod_mine = lax.dynamic_slice(mod_all, (0, dev * B, 0), (N_DEV, B, n_ada))
    mod_mine = jnp.transpose(mod_mine, (1, 0, 2)).reshape(B, 6, 1, D)
    sh1, sc1, ga1, sh2, sc2, ga2 = [mod_mine[:, k] for k in range(6)]

    cos_t, sin_t = _rope_tables(positions)
    g_qn_al, g_kn_al = _pad_gain(g_qn), _pad_gain(g_kn)
    b_col = b_s[0].reshape(GROUPS, CHUNK, 1)
    gw = GW // GROUPS
    sel = (jnp.arange(GW)[:, None] // gw == jnp.arange(LANES)[None, :]).astype(F32)

    h1, rstd1 = _norm_mod_fwd(x, g_norm1, sh1, sc1, "norm1_fwd")
    h1f = h1.reshape(T, D)
    proj, ((g_uq, g_ukv, g_ba, g_bb, g_out),) = _matmul(
        h1f, w_al_t, mode="nt", out_dtypes=[F32], name="mm_proj", comm=_Comm([_GatherJob([s_uq, s_ukv, s_ba, s_bb, s_out])]))
    w_uq_al = _pad_heads(_natural(g_uq), H)
    w_bb_f = g_bb.reshape(-1, D)
    w_out_f = g_out.reshape(-1, D)
    ga_act, rstd_v = _gmlp_fwd(proj, g_v, w_s[0], b_col, D, GW, "gmlp_fwd")
    y_a = _matmul(ga_act, g_ba, mode="nn", out_dtypes=[BF16], name="mm_ya", b_shards=True)
    qn, kvn, rstd_q, rstd_kv = _lat_norm_fwd(proj, g_q_lat, g_kv_lat, D, GW, QL, KVL, "latnorm_fwd")
    q_al = _q_up_fwd(qn, w_uq_al, g_qn_al, cos_t, sin_t, "q_up_fwd")
    k_al, v_al = _kv_up_fwd(kvn, g_ukv, proj, g_kn_al, cos_t, sin_t, kpe_col, "kv_up_fwd")
    q3, k3, v3 = q_al.reshape(B, S, -1), k_al.reshape(B, S, -1), v_al.reshape(B, S, -1)
    attn, lse, ((g_ff1,),) = _attn_fwd(q3, k3, v3, H, "attn_fwd", comm=_Comm([_GatherJob([s_ff1])]))
    attn_f = attn.reshape(T, H * V_HEAD)
    y_b = _matmul(attn_f, w_bb_f, mode="nn", out_dtypes=[BF16], name="mm_yb")
    mixed = _mix_fwd(proj, y_a, y_b, D, "mix_fwd")
    o = _matmul(mixed, w_out_f, mode="nn", out_dtypes=[F32], name="mm_o")
    x1, h2, rstd2 = _norm_mod_fwd(x, g_norm2, sh2, sc2, "norm2_fwd", o=o.reshape(B, S, D), ga=ga1)

    def relu_sq(acc):
        r = jnp.maximum(acc, 0.0)
        return r * r, r

    (a_act, r_act), ((g_ff2,),) = _matmul(h2.reshape(T, D), g_ff1, mode="nn", out_dtypes=[BF16, BF16], name="mm_ff1",
                                          epi=relu_sq, comm=_Comm([_GatherJob([s_ff2])]), b_shards=True)
    w_ff2_f = g_ff2.reshape(-1, D)
    ff = _matmul(a_act, w_ff2_f, mode="nn", out_dtypes=[F32], name="mm_ff2")
    loss_part, dy, dff, d_ga2 = _loss_head(x1, ff.reshape(B, S, D), loss_target, ga2, "loss_head")
    loss = lax.psum(loss_part[0, 0], ("x", "y", "c"))

    dff_f = dff.reshape(T, D)
    df1 = _matmul(dff_f, w_ff2_f, mode="nt", out_dtypes=[BF16], name="mm_da", epi=lambda acc, r: (acc * (2.0 * r.astype(F32)),),
                  extras=(r_act,))
    gs_ff2 = _matmul(a_act, dff_f, mode="tn", out_dtypes=[BF16], name="mm_gw_ff2").reshape(N_DEV, -1, D)
    gs_ff1, ((sib_ff2,),) = _matmul(h2.reshape(T, D), df1, mode="tn", out_dtypes=[BF16], name="mm_gw_ff1",
                                    comm=_Comm([_SiblingJob([gs_ff2])]), out_shards=True)
    cs_ff2 = _chip_sum(gs_ff2, sib_ff2, c_idx, "chip_sum_w_ff2")
    dh2, ((parts_ff2,), (sib_ff1,)) = _matmul(df1, g_ff1, mode="nt", out_dtypes=[F32], name="mm_dh2", b_shards=True,
                                              comm=_Comm([_ChipsJob([cs_ff2]), _SiblingJob([gs_ff1])]))
    cs_ff1 = _chip_sum(gs_ff1, sib_ff1, c_idx, "chip_sum_w_ff1")
    dx1, d_sh2, d_sc2, gg_norm2, d_ga1, do = _norm_mod_bwd(
        dh2.reshape(B, S, D), x1, rstd2, sc2, g_norm2, dy, "norm2_bwd", o=o.reshape(B, S, D), ga=ga1)

    do_f = do.reshape(T, D)
    dmixed = _matmul(do_f, w_out_f, mode="nt", out_dtypes=[BF16], name="mm_dmixed")
    gw_out = _matmul(mixed, do_f, mode="tn", out_dtypes=[BF16], name="mm_gw_out")
    dy_a, dy_b, dproj = _mix_bwd(proj, y_a, y_b, dmixed, D, "mix_bwd")
    gs_ba = _matmul(ga_act, dy_a, mode="tn", out_dtypes=[BF16], name="mm_gw_ba", out_shards=True)
    dga_act = _matmul(dy_a, g_ba, mode="nt", out_dtypes=[BF16], name="mm_dga", b_shards=True)
    gw_bb = _matmul(attn_f, dy_b, mode="tn", out_dtypes=[BF16], name="mm_gw_bb")
    dattn = _matmul(dy_b, w_bb_f, mode="nt", out_dtypes=[BF16], name="mm_dattn")
    dproj, gg_ws, gg_bs_t, gg_gv = _gmlp_bwd(proj, dga_act, rstd_v, g_v, w_s[0], b_col, sel, dproj, D, GW, "gmlp_bwd")
    gs_early = [gs_ba, gw_bb.reshape(N_DEV, -1, D), gw_out.reshape(N_DEV, -1, D)]
    dq, dk, dv, ((parts_ff1,), sib_early) = _attn_bwd(q3, k3, v3, attn, lse, dattn.reshape(B, S, -1), H, "attn_bwd",
                                                      comm=_Comm([_ChipsJob([cs_ff1]), _SiblingJob(gs_early)]))
    cs_early = [_chip_sum(g, r, c_idx, "chip_sum_" + nm)
                for g, r, nm in zip(gs_early, sib_early, ["w_branch_a", "w_branch_b", "w_out"])]
    dq_raw, gg_qn = _q_up_bwd(qn, w_uq_al, g_qn_al, cos_t, sin_t, dq.reshape(T, -1), "q_up_bwd")
    dkv_raw, dkpe, gg_kn = _kv_up_bwd(kvn, g_ukv, proj, g_kn_al, cos_t, sin_t, dk.reshape(T, -1), dv.reshape(T, -1),
                                      kpe_col, "kv_up_bwd")
    gw_uq_al = _matmul(qn, dq_raw, mode="tn", out_dtypes=[BF16], name="mm_gw_uq")
    dqn = _matmul(dq_raw, w_uq_al, mode="nt", out_dtypes=[F32], name="mm_dqn")
    gs_ukv = _matmul(kvn, dkv_raw, mode="tn", out_dtypes=[BF16], name="mm_gw_ukv", out_shards=True)
    dkvn = _matmul(dkv_raw, g_ukv, mode="nt", out_dtypes=[F32], name="mm_dkvn", b_shards=True)
    dproj, gg_qlat, gg_kvlat = _lat_norm_bwd(
        proj, dqn, dkvn, dkpe, rstd_q, rstd_kv, g_q_lat, g_kv_lat, dproj, D, GW, QL, KVL, "latnorm_bwd")
    gs_late = [_col_shards(_unpad_heads(gw_uq_al, H)), gs_ukv]
    (sib_late,) = _run_comm([_SiblingJob(gs_late)], "rs_sibling_late")
    cs_mid = [_chip_sum(g, r, c_idx, "chip_sum_" + nm) for g, r, nm in zip(gs_late, sib_late, ["w_uq", "w_ukv"])] + cs_early
    gw_al_t, (parts_mid,) = _matmul(dproj, h1f, mode="tn", out_dtypes=[BF16], name="mm_gw_in", comm=_Comm([_ChipsJob(cs_mid)]))
    gs_in = jnp.stack([_al_rows(gw_al_t, j * n_in, (j + 1) * n_in, OFF_GATE, 2 * D) for j in range(N_DEV)])
    ((sib_in,),) = _run_comm([_SiblingJob([gs_in])], "rs_sibling_in")
    cs_in = _chip_sum(gs_in, sib_in, c_idx, "chip_sum_w_in")
    dh1, ((parts_in,),) = _matmul(dproj, w_al_t, mode="nn", out_dtypes=[F32], name="mm_dh1", comm=_Comm([_ChipsJob([cs_in])]))
    grad_x, d_sh1, d_sc1, gg_norm1 = _norm_mod_bwd(dh1.reshape(B, S, D), x, rstd1, sc1, g_norm1, dx1, "norm1_bwd")

    dmod_mine = jnp.concatenate([d_sh1, d_sc1, d_ga1, d_sh2, d_sc2, d_ga2], axis=2).reshape(B, 6 * D)
    dmod_all = _all_gather_small(dmod_mine, "ag_dmod").reshape(N_DEV * B, 6 * D)
    dmod_cols = lax.dynamic_slice(dmod_all, (0, dev * n_ada), (N_DEV * B, n_ada))
    ada_out = _ada_bwd_adam(c_all, dmod_cols, w_ada[0], m_w_ada[0], v_w_ada[0], "ada_bwd_adam")
    nb_rows = 8
    bada_out = _adam_from_parts(b_ada.reshape(nb_rows, -1), m_b_ada.reshape(nb_rows, -1), v_b_ada.reshape(nb_rows, -1),
                                dmod_all.reshape(N_DEV * B, nb_rows, -1), "adam_b_ada")

    names = ["w_in", "w_uq", "w_ukv", "w_branch_a", "w_branch_b", "w_out", "w_ff1", "w_ff2"]
    parts = [parts_in] + list(parts_mid) + [parts_ff1, parts_ff2]
    ms = [m_w_in, m_w_uq, m_w_ukv, m_w_branch_a, m_w_branch_b, m_w_out, m_w_ff1, m_w_ff2]
    vs = [v_w_in, v_w_uq, v_w_ukv, v_w_branch_a, v_w_branch_b, v_w_out, v_w_ff1, v_w_ff2]
    big_out = {}
    for nm, w, m, v, p in zip(names, big, ms, vs, parts):
        if nm == "w_in":
            res = _adam_from_parts(jnp.transpose(w[0]), jnp.transpose(m[0]), jnp.transpose(v[0]), p, "adam_" + nm)
            big_out[nm] = [jnp.transpose(r) for r in res]
        else:
            big_out[nm] = _adam_from_parts(w[0], m[0], v[0], p, "adam_" + nm)

    small = [("g_norm1", g_norm1, m_g_norm1, v_g_norm1, gg_norm1),
             ("g_v", g_v, m_g_v, v_g_v, gg_gv),
             ("w_s", w_s, m_w_s, v_w_s, gg_ws),
             ("b_s", b_s, m_b_s, v_b_s, jnp.transpose(gg_bs_t[:, :GROUPS])),
             ("g_q_lat", g_q_lat, m_g_q_lat, v_g_q_lat, gg_qlat),
             ("g_kv_lat", g_kv_lat, m_g_kv_lat, v_g_kv_lat, gg_kvlat),
             ("g_qn", g_qn, m_g_qn, v_g_qn, gg_qn[:, :QK_HEAD]),
             ("g_kn", g_kn, m_g_kn, v_g_kn, gg_kn[:, :QK_HEAD]),
             ("g_norm2", g_norm2, m_g_norm2, v_g_norm2, gg_norm2)]
    sizes = [w.size for _, w, _, _, _ in small]
    n_small = sum(sizes)
    n_small_pad = _round_up(n_small, 8 * LANES)

    def flat_cat(arrs):
        return jnp.pad(jnp.concatenate([a.reshape(-1) for a in arrs]), (0, n_small_pad - n_small))

    part_small = _all_gather_small(flat_cat([t[4] for t in small]), "ag_small_grads")
    small_out = _adam_from_parts(
        flat_cat([t[1] for t in small]).reshape(8, -1), flat_cat([t[2] for t in small]).reshape(8, -1),
        flat_cat([t[3] for t in small]).reshape(8, -1), part_small.reshape(N_DEV, 8, -1), "adam_small")
    offs = [sum(sizes[:i]) for i in range(len(sizes))]

    def small_piece(kind, i):
        return small_out[kind].reshape(-1)[offs[i]:offs[i] + sizes[i]].reshape(small[i][1].shape)

    small_idx = {t[0]: i for i, t in enumerate(small)}
    order = ["w_ada", "b_ada", "g_norm1", "w_in", "g_v", "w_s", "b_s", "g_q_lat", "g_kv_lat", "w_uq", "w_ukv", "g_qn", "g_kn",
             "w_branch_a", "w_branch_b", "w_out", "g_norm2", "w_ff1", "w_ff2"]

    def result(kind, nm):
        if nm == "w_ada":
            return ada_out[kind][None]
        if nm == "b_ada":
            return bada_out[kind].reshape(b_ada.shape)
        if nm in small_idx:
            return small_piece(kind, small_idx[nm])
        return big_out[nm][kind][None]

    outs = [loss, grad_x]
    for kind in range(4):
        outs += [result(kind, nm) for nm in order]
    return tuple(outs)
```

```python
import functools
import math

import jax
import jax.numpy as jnp
from jax import lax
from jax.experimental import pallas as pl
from jax.experimental.pallas import tpu as pltpu

F32 = jnp.float32
BF16 = jnp.bfloat16
SDS = jax.ShapeDtypeStruct
MESH = pl.DeviceIdType.MESH

N_DEV = 8
N_CHIP = 4
CHUNK = 128
GROUPS = 8
NOPE = 128
ROPE = 64
HALF_ROPE = ROPE // 2
QK_HEAD = NOPE + ROPE
V_HEAD = 128
HEAD_PAD = 256
LANES = 128
ROPE_THETA = 10000.0
EPS = 1e-6
INV_SQRT2 = 1.0 / math.sqrt(2.0)
INV_SQRT_2PI = 1.0 / math.sqrt(2.0 * math.pi)

ADAM_LR = 0.001
ADAM_B1 = 0.9
ADAM_B2 = 0.999
ADAM_EPS = 1e-08
ADAM_WD = 0.01
ADAM_STEP = 10

V7X_VMEM_LIMIT_BYTES = 56 * 1024 * 1024
MM_TILE = 1024
MM_TILE_K = 4096
ATTN_BLOCK = 512
ROW_BLOCK = 128

NN = (((1,), (0,)), ((), ()))
NT = (((1,), (1,)), ((), ()))
TN = (((0,), (0,)), ((), ()))


def _tile(n, pref, mult):
    t = min(pref, n)
    t -= t % mult
    while t >= mult:
        if n % t == 0:
            return t
        t -= mult
    return n


def _round_up(n, m):
    return (n + m - 1) // m * m


BF16_SUBLANES = 16


def _block_2d(R, C, elems):
    if R % BF16_SUBLANES == 0:
        return _tile(R, max(BF16_SUBLANES, elems // C // BF16_SUBLANES * BF16_SUBLANES), BF16_SUBLANES), C
    return R, _tile(C, max(LANES, elems // R // LANES * LANES), LANES)


def _cp(*sem):
    return pltpu.CompilerParams(dimension_semantics=sem, vmem_limit_bytes=V7X_VMEM_LIMIT_BYTES)


def _dot(a, b, dims):
    return lax.dot_general(a, b, dims, preferred_element_type=F32)


def _gelu(x):
    return 0.5 * x * (1.0 + lax.erf(x * INV_SQRT2))


def _gelu_grad(x):
    return 0.5 * (1.0 + lax.erf(x * INV_SQRT2)) + x * jnp.exp(-0.5 * x * x) * INV_SQRT_2PI


def _sigmoid(x):
    return 1.0 / (1.0 + jnp.exp(-x))


def _matmul(a, b, *, mode, out_dtypes, name, epi=None, extras=(), comm=None, b_shards=False, out_shards=False):
    if mode == "tn":
        K, M = a.shape
    else:
        M, K = a.shape
    n_shard = None
    if b_shards:
        _, R, n_shard = b.shape
        N, Kb = (R, N_DEV * n_shard) if mode == "nt" else (N_DEV * n_shard, R)
    elif mode == "nt":
        N, Kb = b.shape
    else:
        Kb, N = b.shape
    assert K == Kb, (name, a.shape, b.shape)
    tm = _tile(M, MM_TILE, 128)
    tn = _tile(n_shard if (b_shards and mode != "nt") else N, MM_TILE, 128)
    tk = _tile(n_shard if (b_shards and mode == "nt") else K, MM_TILE_K, 128)
    if out_shards:
        assert mode == "tn" and not extras and len(out_dtypes) == 1
        n_shard = N // N_DEV
        tn = _tile(n_shard, MM_TILE, 128)
    nk = K // tk
    n_extra = len(extras)
    n_out = len(out_dtypes)
    dims = {"nn": NN, "nt": NT, "tn": TN}[mode]

    def body(a_ref, b_ref, *rest):
        extra_refs = rest[:n_extra]
        out_refs = rest[n_extra:n_extra + n_out]
        acc_ref = rest[n_extra + n_out] if nk > 1 else None
        k = pl.program_id(2)

        def product():
            return _dot(a_ref[...].astype(BF16), b_ref[...].astype(BF16), dims)

        def finish(acc):
            res = (acc,) if epi is None else epi(acc, *[e[...] for e in extra_refs])
            for o_ref, r in zip(out_refs, res):
                o_ref[...] = r.astype(o_ref.dtype)

        if nk == 1:
            finish(product())
            return

        @pl.when(k == 0)
        def _():
            acc_ref[...] = product()

        if nk > 2:
            @pl.when((k > 0) & (k < nk - 1))
            def _():
                acc_ref[...] += product()

        @pl.when(k == nk - 1)
        def _():
            finish(acc_ref[...] + product())

    if mode == "tn":
        a_spec = pl.BlockSpec((tk, tm), lambda i, j, k: (k, i))
    else:
        a_spec = pl.BlockSpec((tm, tk), lambda i, j, k: (i, k))
    if b_shards and mode == "nt":
        per = n_shard // tk
        b_spec = pl.BlockSpec((None, tn, tk), lambda i, j, k: (k // per, j, k % per))
    elif b_shards:
        per = n_shard // tn
        b_spec = pl.BlockSpec((None, tk, tn), lambda i, j, k: (j // per, k, j % per))
    elif mode == "nt":
        b_spec = pl.BlockSpec((tn, tk), lambda i, j, k: (j, k))
    else:
        b_spec = pl.BlockSpec((tk, tn), lambda i, j, k: (k, j))
    mn_spec = pl.BlockSpec((tm, tn), lambda i, j, k: (i, j))
    grid = (M // tm, N // tn, nk)
    in_specs = [a_spec, b_spec] + [mn_spec] * n_extra
    if out_shards:
        per_out = n_shard // tn
        out_specs = [pl.BlockSpec((None, tm, tn), lambda i, j, k: (j // per_out, i, j % per_out))]
        out_shape = [SDS((N_DEV, M, n_shard), out_dtypes[0])]
    else:
        out_specs = [mn_spec] * n_out
        out_shape = [SDS((M, N), dt) for dt in out_dtypes]
    scratch = [pltpu.VMEM((tm, tn), F32)] if nk > 1 else []
    if comm is None:
        outs = pl.pallas_call(
            body, name=name, grid=grid, in_specs=in_specs, out_specs=out_specs, out_shape=out_shape,
            scratch_shapes=scratch, compiler_params=_cp("parallel", "parallel", "arbitrary"),
        )(a, b, *extras)
        return outs[0] if n_out == 1 else outs

    def first_last():
        i, j, k = pl.program_id(0), pl.program_id(1), pl.program_id(2)
        return ((i == 0) & (j == 0) & (k == 0),
                (i == grid[0] - 1) & (j == grid[1] - 1) & (k == nk - 1))

    outs = pl.pallas_call(
        _carry(body, comm, 2 + n_extra, n_out, len(scratch), first_last), name=name, grid=grid,
        in_specs=in_specs + [HBM_SPEC] * len(comm.ins), out_specs=out_specs + [HBM_SPEC] * len(comm.out_shape),
        out_shape=out_shape + comm.out_shape, scratch_shapes=scratch + comm.scratch,
        compiler_params=_cp("arbitrary", "arbitrary", "arbitrary"),
    )(a, b, *extras, *comm.ins)
    res = outs[0] if n_out == 1 else list(outs[:n_out])
    return res, comm.results(outs[n_out:])


def _here():
    return lax.axis_index("x"), lax.axis_index("y"), lax.axis_index("c")


def _other_chips(x, y):
    return [(1 - x, y), (x, 1 - y), (1 - x, 1 - y)]


def _all_gather_small(v, name):
    shape = v.shape
    n = v.size
    n_pad = _round_up(n, 8 * LANES)
    flat = jnp.pad(v.reshape(-1), (0, n_pad - n)).reshape(8, n_pad // 8)
    m_per, cols = flat.shape

    def body(x_ref, out_ref, send_sems, recv_sems, local_sem):
        x, y, c = _here()
        me, sibling = (x, y, c), (x, y, 1 - c)
        chips = _other_chips(x, y)

        def rows(px, py, pc):
            return out_ref.at[pl.ds((4 * px + 2 * py + pc) * m_per, m_per), :]

        def copy(k, block, to, src=None):
            return pltpu.make_async_remote_copy(
                src_ref=rows(*block) if src is None else src, dst_ref=rows(*block),
                send_sem=send_sems.at[k], recv_sem=recv_sems.at[k], device_id=to, device_id_type=MESH)

        mine = pltpu.make_async_copy(x_ref, rows(*me), local_sem)
        mine.start()
        first = [copy(0, me, sibling, src=x_ref)]
        first += [copy(1 + j, me, (*chip, c), src=x_ref) for j, chip in enumerate(chips)]
        for cp in first:
            cp.start()
        passed = [copy(4 + j, (*chip, c), sibling) for j, chip in enumerate(chips)]
        for j, chip in enumerate(chips):
            copy(1 + j, (*chip, c), me).wait_recv()
            passed[j].start()
        copy(0, sibling, me).wait_recv()
        for j, chip in enumerate(chips):
            copy(4 + j, (*chip, 1 - c), me).wait_recv()
        for cp in first + passed:
            cp.wait_send()
        mine.wait()

    out = pl.pallas_call(
        body, name=name,
        out_shape=SDS((N_DEV * m_per, cols), F32),
        in_specs=[pl.BlockSpec(memory_space=pltpu.VMEM)],
        out_specs=pl.BlockSpec(memory_space=pltpu.VMEM),
        scratch_shapes=[pltpu.SemaphoreType.DMA((7,)), pltpu.SemaphoreType.DMA((7,)), pltpu.SemaphoreType.DMA],
        compiler_params=pltpu.CompilerParams(vmem_limit_bytes=V7X_VMEM_LIMIT_BYTES),
    )(flat)
    return out.reshape(N_DEV, n_pad)[:, :n].reshape((N_DEV,) + shape)


HBM_SPEC = pl.BlockSpec(memory_space=pltpu.HBM)


class _GatherJob:
    def __init__(self, shards):
        self.ins = list(shards)
        self.nw = len(shards)
        self.out_shape = [SDS((N_DEV,) + s.shape, s.dtype) for s in shards]
        self.scratch = [pltpu.SemaphoreType.DMA((7 * self.nw,)), pltpu.SemaphoreType.DMA((7 * self.nw,)),
                        pltpu.SemaphoreType.DMA((self.nw,))]

    def _parts(self, xs, outs, sems):
        send_sems, recv_sems, local_sems = sems
        x, y, c = _here()

        def blk(w, px, py, pc):
            return outs[w].at[4 * px + 2 * py + pc]

        def copy(w, k, block, to, src=None):
            return pltpu.make_async_remote_copy(
                src_ref=blk(w, *block) if src is None else src, dst_ref=blk(w, *block),
                send_sem=send_sems.at[7 * w + k], recv_sem=recv_sems.at[7 * w + k], device_id=to, device_id_type=MESH)

        me, sibling = (x, y, c), (x, y, 1 - c)
        chips = _other_chips(x, y)
        mine = [pltpu.make_async_copy(xs[w], blk(w, *me), local_sems.at[w]) for w in range(self.nw)]
        first = []
        for w in range(self.nw):
            first.append(copy(w, 0, me, sibling, src=xs[w]))
            first += [copy(w, 1 + j, me, (*chip, c), src=xs[w]) for j, chip in enumerate(chips)]
        return copy, me, sibling, chips, c, mine, first

    def start(self, xs, outs, sems):
        _, _, _, _, _, mine, first = self._parts(xs, outs, sems)
        for cp in mine + first:
            cp.start()

    def finish(self, xs, outs, sems):
        copy, me, sibling, chips, c, mine, first = self._parts(xs, outs, sems)
        passed = []
        for w in range(self.nw):
            for j, chip in enumerate(chips):
                copy(w, 1 + j, (*chip, c), me).wait_recv()
                fwd = copy(w, 4 + j, (*chip, c), sibling)
                fwd.start()
                passed.append(fwd)
        for w in range(self.nw):
            copy(w, 0, sibling, me).wait_recv()
            for j, chip in enumerate(chips):
                copy(w, 4 + j, (*chip, 1 - c), me).wait_recv()
        for cp in first + passed:
            cp.wait_send()
        for cp in mine:
            cp.wait()


class _SiblingJob:
    def __init__(self, grads):
        self.ins = list(grads)
        self.nw = len(grads)
        self.out_shape = [SDS((N_CHIP,) + g.shape[1:], g.dtype) for g in grads]
        self.scratch = [pltpu.SemaphoreType.DMA((N_CHIP * self.nw,)), pltpu.SemaphoreType.DMA((N_CHIP * self.nw,))]

    def _copies(self, gs, outs, sems):
        send_sems, recv_sems = sems
        x, y, c = _here()
        return [pltpu.make_async_remote_copy(
            src_ref=gs[w].at[2 * k + (1 - c)], dst_ref=outs[w].at[k],
            send_sem=send_sems.at[N_CHIP * w + k], recv_sem=recv_sems.at[N_CHIP * w + k],
            device_id=(x, y, 1 - c), device_id_type=MESH) for w in range(self.nw) for k in range(N_CHIP)]

    def start(self, gs, outs, sems):
        for cp in self._copies(gs, outs, sems):
            cp.start()

    def finish(self, gs, outs, sems):
        for cp in self._copies(gs, outs, sems):
            cp.wait()


class _ChipsJob:
    def __init__(self, chip_sums):
        self.ins = list(chip_sums)
        self.nw = len(chip_sums)
        self.out_shape = [SDS(s.shape, s.dtype) for s in chip_sums]
        self.scratch = [pltpu.SemaphoreType.DMA((3 * self.nw,)), pltpu.SemaphoreType.DMA((3 * self.nw,)),
                        pltpu.SemaphoreType.DMA((self.nw,))]

    def _parts(self, srcs, outs, sems):
        send_sems, recv_sems, local_sems = sems
        x, y, c = _here()
        my_chip = 2 * x + y
        local = [pltpu.make_async_copy(srcs[w].at[my_chip], outs[w].at[my_chip], local_sems.at[w]) for w in range(self.nw)]
        sends, landed = [], []
        for w in range(self.nw):
            for j, (px, py) in enumerate(_other_chips(x, y)):
                sems_j = dict(send_sem=send_sems.at[3 * w + j], recv_sem=recv_sems.at[3 * w + j],
                              device_id=(px, py, c), device_id_type=MESH)
                sends.append(pltpu.make_async_remote_copy(
                    src_ref=srcs[w].at[2 * px + py], dst_ref=outs[w].at[my_chip], **sems_j))
                landed.append(pltpu.make_async_remote_copy(
                    src_ref=srcs[w].at[2 * px + py], dst_ref=outs[w].at[2 * px + py], **sems_j))
        return local, sends, landed

    def start(self, srcs, outs, sems):
        local, sends, _ = self._parts(srcs, outs, sems)
        for cp in local + sends:
            cp.start()

    def finish(self, srcs, outs, sems):
        local, sends, landed = self._parts(srcs, outs, sems)
        for cp in landed:
            cp.wait_recv()
        for cp in sends:
            cp.wait_send()
        for cp in local:
            cp.wait()


class _Comm:
    def __init__(self, jobs):
        self.jobs = list(jobs)
        self.ins = [a for j in self.jobs for a in j.ins]
        self.out_shape = [s for j in self.jobs for s in j.out_shape]
        self.scratch = [s for j in self.jobs for s in j.scratch]

    def _split(self, flat, counts):
        out, pos = [], 0
        for n in counts:
            out.append(flat[pos:pos + n])
            pos += n
        return out

    def _each(self, ins, outs, sems):
        return zip(self.jobs, self._split(ins, [len(j.ins) for j in self.jobs]),
                   self._split(outs, [len(j.out_shape) for j in self.jobs]),
                   self._split(sems, [len(j.scratch) for j in self.jobs]))

    def start(self, ins, outs, sems):
        for job, i, o, s in self._each(ins, outs, sems):
            job.start(i, o, s)

    def finish(self, ins, outs, sems):
        for job, i, o, s in self._each(ins, outs, sems):
            job.finish(i, o, s)

    def results(self, flat):
        return [list(r) for r in self._split(list(flat), [len(j.out_shape) for j in self.jobs])]


def _carry(body, comm, n_in, n_out, n_scratch, first_last):
    ci, co = len(comm.ins), len(comm.out_shape)

    def wrapped(*refs):
        ins, rest = refs[:n_in + ci], refs[n_in + ci:]
        outs, scr = rest[:n_out + co], rest[n_out + co:]
        c_ins, c_outs, c_sems = ins[n_in:], outs[n_out:], scr[n_scratch:]
        first, last = first_last()

        @pl.when(first)
        def _():
            comm.start(c_ins, c_outs, c_sems)

        body(*ins[:n_in], *outs[:n_out], *scr[:n_scratch])

        @pl.when(last)
        def _():
            comm.finish(c_ins, c_outs, c_sems)

    return wrapped


def _run_comm(jobs, name):
    comm = _Comm(jobs)

    def body(*refs):
        ci, co = len(comm.ins), len(comm.out_shape)
        comm.start(refs[:ci], refs[ci:ci + co], refs[ci + co:])
        comm.finish(refs[:ci], refs[ci:ci + co], refs[ci + co:])

    outs = pl.pallas_call(
        body, name=name, out_shape=comm.out_shape,
        in_specs=[HBM_SPEC] * len(comm.ins), out_specs=[HBM_SPEC] * len(comm.out_shape),
        scratch_shapes=comm.scratch,
    )(*comm.ins)
    return comm.results(outs)


def _chip_sum(g, recv, c_idx, name):
    _, m, n = g.shape
    tr, tc = _block_2d(m, n, 1 << 20)

    def body(c_ref, g_ref, r_ref, o_ref):
        o_ref[...] = (g_ref[...].astype(F32) + r_ref[...].astype(F32)).astype(o_ref.dtype)

    grid_spec = pltpu.PrefetchScalarGridSpec(
        num_scalar_prefetch=1, grid=(N_CHIP, m // tr, n // tc),
        in_specs=[pl.BlockSpec((None, tr, tc), lambda k, i, j, c_ref: (2 * k + c_ref[0], i, j)),
                  pl.BlockSpec((None, tr, tc), lambda k, i, j, c_ref: (k, i, j))],
        out_specs=pl.BlockSpec((None, tr, tc), lambda k, i, j, c_ref: (k, i, j)))
    return pl.pallas_call(
        body, name=name, grid_spec=grid_spec, out_shape=SDS((N_CHIP, m, n), BF16),
        compiler_params=_cp("parallel", "parallel", "parallel"),
    )(c_idx, g, recv)


def _adam_math(w, g, m, v):
    m = ADAM_B1 * m + (1.0 - ADAM_B1) * g
    v = ADAM_B2 * v + (1.0 - ADAM_B2) * (g * g)
    m_hat = m / (1.0 - ADAM_B1 ** ADAM_STEP)
    v_hat = v / (1.0 - ADAM_B2 ** ADAM_STEP)
    delta = -ADAM_LR * (m_hat / (jnp.sqrt(v_hat) + ADAM_EPS) + ADAM_WD * w)
    return delta, m, v


def _adam_from_parts(w, m, v, parts, name):
    R, C = w.shape
    P = parts.shape[0]
    tr, tc = _block_2d(R, C, 1 << 19)

    def body(w_ref, m_ref, v_ref, p_ref, g_out, d_out, m_out, v_out):
        g = p_ref[0].astype(F32)
        for k in range(1, P):
            g = g + p_ref[k].astype(F32)
        delta, nm, nv = _adam_math(w_ref[...], g, m_ref[...], v_ref[...])
        g_out[...] = g
        d_out[...] = delta
        m_out[...] = nm
        v_out[...] = nv

    spec = pl.BlockSpec((tr, tc), lambda i, j: (i, j))
    return pl.pallas_call(
        body, name=name, grid=(R // tr, C // tc),
        in_specs=[spec, spec, spec, pl.BlockSpec((P, tr, tc), lambda i, j: (0, i, j))],
        out_specs=[spec] * 4, out_shape=[SDS((R, C), F32)] * 4,
        compiler_params=_cp("parallel", "parallel"),
    )(w, m, v, parts)


def _ada_fwd(c_all, w_ada, b_cols, name):
    nb, D = c_all.shape
    n = w_ada.shape[1]
    tn = _tile(n, 512, 128)

    def body(c_ref, w_ref, b_ref, o_ref):
        cv = c_ref[...]
        cond = (cv * _sigmoid(cv)).astype(BF16)
        o_ref[...] = _dot(cond, w_ref[...].astype(BF16), NN) + b_ref[...]

    return pl.pallas_call(
        body, name=name, grid=(n // tn,),
        in_specs=[pl.BlockSpec((nb, D), lambda j: (0, 0)), pl.BlockSpec((D, tn), lambda j: (0, j)),
                  pl.BlockSpec((1, tn), lambda j: (0, j))],
        out_specs=pl.BlockSpec((nb, tn), lambda j: (0, j)), out_shape=SDS((nb, n), F32),
        compiler_params=_cp("parallel"),
    )(c_all, w_ada, b_cols)


def _ada_bwd_adam(c_all, dmod_cols, w, m, v, name):
    nb, D = c_all.shape
    n = w.shape[1]
    tr = _tile(D, 512, 128)
    tn = _tile(n, 1024, 128)

    def body(c_ref, d_ref, w_ref, m_ref, v_ref, g_out, d_out, m_out, v_out):
        cv = c_ref[...]
        cond = (cv * _sigmoid(cv)).astype(BF16)
        g = _dot(cond, d_ref[...].astype(BF16), TN)
        delta, nm, nv = _adam_math(w_ref[...], g, m_ref[...], v_ref[...])
        g_out[...] = g
        d_out[...] = delta
        m_out[...] = nm
        v_out[...] = nv

    spec = pl.BlockSpec((tr, tn), lambda i, j: (i, j))
    return pl.pallas_call(
        body, name=name, grid=(D // tr, n // tn),
        in_specs=[pl.BlockSpec((nb, tr), lambda i, j: (0, i)), pl.BlockSpec((nb, tn), lambda i, j: (0, j)),
                  spec, spec, spec],
        out_specs=[spec] * 4, out_shape=[SDS((D, n), F32)] * 4,
        compiler_params=_cp("parallel", "parallel"),
    )(c_all, dmod_cols, w, m, v)


def _tok(tb, width):
    return pl.BlockSpec((None, tb, width), lambda b, i: (b, i, 0))


def _per_example(width):
    return pl.BlockSpec((None, 1, width), lambda b, i: (b, 0, 0))


def _shared_row(width):
    return pl.BlockSpec((1, width), lambda b, i: (0, 0))


def _norm_mod_fwd(x, g, sh, sc, name, o=None, ga=None):
    B, S, D = x.shape
    tb = _tile(S, ROW_BLOCK, 8)
    fused = o is not None

    def body(*refs):
        if fused:
            x_ref, o_ref, ga_ref, g_ref, sh_ref, sc_ref, x1_ref, h_ref, r_ref = refs
            xv = x_ref[...] + ga_ref[...] * o_ref[...]
            x1_ref[...] = xv
        else:
            x_ref, g_ref, sh_ref, sc_ref, h_ref, r_ref = refs
            xv = x_ref[...]
        rstd = lax.rsqrt(jnp.mean(xv * xv, axis=1, keepdims=True) + EPS)
        y = xv * rstd * g_ref[...]
        h_ref[...] = (y * (1.0 + sc_ref[...]) + sh_ref[...]).astype(BF16)
        r_ref[...] = rstd

    ins = [x] + ([o, ga] if fused else []) + [g, sh, sc]
    in_specs = [_tok(tb, D)] + ([_tok(tb, D), _per_example(D)] if fused else []) + [_shared_row(D), _per_example(D), _per_example(D)]
    out_specs = ([_tok(tb, D)] if fused else []) + [_tok(tb, D), _tok(tb, 1)]
    out_shape = ([SDS((B, S, D), F32)] if fused else []) + [SDS((B, S, D), BF16), SDS((B, S, 1), F32)]
    return pl.pallas_call(
        body, name=name, grid=(B, S // tb), in_specs=in_specs, out_specs=out_specs, out_shape=out_shape,
        compiler_params=_cp("parallel", "parallel"),
    )(*ins)


def _norm_mod_bwd(dh, xin, rstd, sc, g, dres, name, o=None, ga=None):
    B, S, D = xin.shape
    tb = _tile(S, ROW_BLOCK, 8)
    gated = o is not None

    def body(*refs):
        if gated:
            (dh_ref, x_ref, r_ref, sc_ref, g_ref, dres_ref, o_ref, ga_ref,
             dx_ref, dsh_ref, dsc_ref, gg_ref, dga_ref, do_ref) = refs
        else:
            dh_ref, x_ref, r_ref, sc_ref, g_ref, dres_ref, dx_ref, dsh_ref, dsc_ref, gg_ref = refs
        b, i = pl.program_id(0), pl.program_id(1)

        @pl.when(i == 0)
        def _():
            dsh_ref[...] = jnp.zeros_like(dsh_ref)
            dsc_ref[...] = jnp.zeros_like(dsc_ref)
            if gated:
                dga_ref[...] = jnp.zeros_like(dga_ref)

        @pl.when((i == 0) & (b == 0))
        def _():
            gg_ref[...] = jnp.zeros_like(gg_ref)

        dhv = dh_ref[...]
        rs = r_ref[...]
        gv = g_ref[...]
        xhat = x_ref[...] * rs
        dsh_ref[...] += jnp.sum(dhv, axis=0, keepdims=True)
        dsc_ref[...] += jnp.sum(dhv * (xhat * gv), axis=0, keepdims=True)
        dn = dhv * (1.0 + sc_ref[...])
        gg_ref[...] += jnp.sum(dn * xhat, axis=0, keepdims=True)
        dxhat = dn * gv
        cm = jnp.mean(dxhat * xhat, axis=1, keepdims=True)
        dx = dres_ref[...] + rs * (dxhat - xhat * cm)
        dx_ref[...] = dx
        if gated:
            dga_ref[...] += jnp.sum(dx * o_ref[...], axis=0, keepdims=True)
            do_ref[...] = (dx * ga_ref[...]).astype(BF16)

    ins = [dh, xin, rstd, sc, g, dres] + ([o, ga] if gated else [])
    in_specs = [_tok(tb, D), _tok(tb, D), _tok(tb, 1), _per_example(D), _shared_row(D), _tok(tb, D)]
    in_specs += [_tok(tb, D), _per_example(D)] if gated else []
    out_specs = [_tok(tb, D), _per_example(D), _per_example(D), _shared_row(D)]
    out_shape = [SDS((B, S, D), F32), SDS((B, 1, D), F32), SDS((B, 1, D), F32), SDS((1, D), F32)]
    if gated:
        out_specs += [_per_example(D), _tok(tb, D)]
        out_shape += [SDS((B, 1, D), F32), SDS((B, S, D), BF16)]
    return pl.pallas_call(
        body, name=name, grid=(B, S // tb), in_specs=in_specs, out_specs=out_specs, out_shape=out_shape,
        compiler_params=_cp("arbitrary", "arbitrary"),
    )(*ins)


def _loss_head(x1, ff, target, ga2, name):
    B, S, D = x1.shape
    tb = _tile(S, ROW_BLOCK, 8)
    nb, ni = B, S // tb

    def body(x_ref, f_ref, t_ref, ga_ref, loss_ref, dy_ref, dff_ref, dga_ref, acc_ref):
        b, i = pl.program_id(0), pl.program_id(1)

        @pl.when(i == 0)
        def _():
            dga_ref[...] = jnp.zeros_like(dga_ref)

        @pl.when((i == 0) & (b == 0))
        def _():
            acc_ref[...] = jnp.zeros_like(acc_ref)

        fv = f_ref[...]
        gav = ga_ref[...]
        err = x_ref[...] + gav * fv - t_ref[...]
        acc_ref[...] += jnp.sum(err * err, axis=0, keepdims=True)
        dy = err * (1.0 / D)
        dy_ref[...] = dy
        dff_ref[...] = (dy * gav).astype(BF16)
        dga_ref[...] += jnp.sum(dy * fv, axis=0, keepdims=True)

        @pl.when((i == ni - 1) & (b == nb - 1))
        def _():
            loss_ref[...] = jnp.sum(acc_ref[...], axis=1, keepdims=True) * (0.5 / D)

    return pl.pallas_call(
        body, name=name, grid=(B, S // tb),
        in_specs=[_tok(tb, D), _tok(tb, D), _tok(tb, D), _per_example(D)],
        out_specs=[pl.BlockSpec((1, 1), lambda b, i: (0, 0)), _tok(tb, D), _tok(tb, D), _per_example(D)],
        out_shape=[SDS((1, 1), F32), SDS((B, S, D), F32), SDS((B, S, D), BF16), SDS((B, 1, D), F32)],
        scratch_shapes=[pltpu.VMEM((1, D), F32)],
        compiler_params=_cp("arbitrary", "arbitrary"),
    )(x1, ff, target, ga2)


def _causal_mask():
    t = lax.broadcasted_iota(jnp.int32, (CHUNK, CHUNK), 0)
    s = lax.broadcasted_iota(jnp.int32, (CHUNK, CHUNK), 1)
    return s <= t


def _gmlp_fwd(proj, g_v, w_s, b_col, D, GW, name):
    T = proj.shape[0]
    tb = _tile(T, ROW_BLOCK, CHUNK)
    gw = GW // GROUPS
    ucol = (2 * D) // GW

    def body(u_ref, v_ref, g_ref, w_ref, b_ref, ga_ref, r_ref):
        zu = _gelu(u_ref[...])
        zv = _gelu(v_ref[...])
        rstd = lax.rsqrt(jnp.mean(zv * zv, axis=1, keepdims=True) + EPS)
        vn = (zv * rstd * g_ref[...]).astype(BF16)
        mask = _causal_mask()
        for g in range(GROUPS):
            wm = jnp.where(mask, w_ref[g], 0.0).astype(BF16)
            cols = slice(g * gw, (g + 1) * gw)
            for ci in range(tb // CHUNK):
                rows = slice(ci * CHUNK, (ci + 1) * CHUNK)
                mixed = _dot(wm, vn[rows, cols], NN) + b_ref[g]
                ga_ref[rows, cols] = (zu[rows, cols] * mixed).astype(BF16)
        r_ref[...] = rstd

    return pl.pallas_call(
        body, name=name, grid=(T // tb,),
        in_specs=[pl.BlockSpec((tb, GW), lambda i: (i, ucol)), pl.BlockSpec((tb, GW), lambda i: (i, ucol + 1)),
                  pl.BlockSpec((1, GW), lambda i: (0, 0)), pl.BlockSpec((GROUPS, CHUNK, CHUNK), lambda i: (0, 0, 0)),
                  pl.BlockSpec((GROUPS, CHUNK, 1), lambda i: (0, 0, 0))],
        out_specs=[pl.BlockSpec((tb, GW), lambda i: (i, 0)), pl.BlockSpec((tb, 1), lambda i: (i, 0))],
        out_shape=[SDS((T, GW), BF16), SDS((T, 1), F32)],
        compiler_params=_cp("parallel"),
    )(proj, proj, g_v, w_s, b_col)


def _gmlp_bwd(proj, dga, rstd_v, g_v, w_s, b_col, sel, dproj, D, GW, name):
    T = proj.shape[0]
    tb = _tile(T, ROW_BLOCK, CHUNK)
    gw = GW // GROUPS
    ucol = (2 * D) // GW
    assert (2 * D) % (2 * GW) == 0
    uvcol = (2 * D) // (2 * GW)
    nsteps = T // tb

    def body(u_ref, v_ref, dga_ref, r_ref, g_ref, w_ref, b_ref, sel_ref, _dproj_in,
             duv_ref, gws_ref, gbs_ref, gg_ref, accb_ref, dvn_ref):
        step = pl.program_id(0)

        @pl.when(step == 0)
        def _():
            gws_ref[...] = jnp.zeros_like(gws_ref)
            gg_ref[...] = jnp.zeros_like(gg_ref)
            accb_ref[...] = jnp.zeros_like(accb_ref)

        uv = u_ref[...]
        vv = v_ref[...]
        zu = _gelu(uv)
        zv = _gelu(vv)
        rs = r_ref[...]
        gv = g_ref[...]
        vhat = zv * rs
        vnb = (vhat * gv).astype(BF16)
        dgav = dga_ref[...].astype(F32)
        du_gelu = _gelu_grad(uv)
        mask = _causal_mask()
        for g in range(GROUPS):
            wm = jnp.where(mask, w_ref[g], 0.0)
            wmb = wm.astype(BF16)
            wmtb = wm.T.astype(BF16)
            cols = slice(g * gw, (g + 1) * gw)
            for ci in range(tb // CHUNK):
                rows = slice(ci * CHUNK, (ci + 1) * CHUNK)
                vn_blk = vnb[rows, cols]
                mixed = _dot(wmb, vn_blk, NN) + b_ref[g]
                duv_ref[rows, cols] = (dgav[rows, cols] * mixed * du_gelu[rows, cols]).astype(BF16)
                dmix = dgav[rows, cols] * zu[rows, cols]
                accb_ref[:, cols] += dmix
                dmb = dmix.astype(BF16)
                gws_ref[g] += _dot(dmb, vn_blk, NT)
                dvn_ref[rows, cols] = _dot(wmtb, dmb, NN)
        dvn = dvn_ref[...]
        gg_ref[...] += jnp.sum(dvn * vhat, axis=0, keepdims=True)
        dvhat = dvn * gv
        cm = jnp.mean(dvhat * vhat, axis=1, keepdims=True)
        dzv = rs * (dvhat - vhat * cm)
        duv_ref[:, GW:] = (dzv * _gelu_grad(vv)).astype(BF16)

        @pl.when(step == nsteps - 1)
        def _():
            gbs_ref[...] = jnp.dot(accb_ref[...], sel_ref[...], precision=lax.Precision.HIGHEST,
                                   preferred_element_type=F32)
            for g in range(GROUPS):
                gws_ref[g] = jnp.where(mask, gws_ref[g], 0.0)

    return pl.pallas_call(
        body, name=name, grid=(nsteps,),
        in_specs=[pl.BlockSpec((tb, GW), lambda i: (i, ucol)), pl.BlockSpec((tb, GW), lambda i: (i, ucol + 1)),
                  pl.BlockSpec((tb, GW), lambda i: (i, 0)), pl.BlockSpec((tb, 1), lambda i: (i, 0)),
                  pl.BlockSpec((1, GW), lambda i: (0, 0)), pl.BlockSpec((GROUPS, CHUNK, CHUNK), lambda i: (0, 0, 0)),
                  pl.BlockSpec((GROUPS, CHUNK, 1), lambda i: (0, 0, 0)), pl.BlockSpec((GW, LANES), lambda i: (0, 0)),
                  pl.BlockSpec(memory_space=pl.ANY)],
        out_specs=[pl.BlockSpec((tb, 2 * GW), lambda i: (i, uvcol)), pl.BlockSpec((GROUPS, CHUNK, CHUNK), lambda i: (0, 0, 0)),
                   pl.BlockSpec((CHUNK, LANES), lambda i: (0, 0)), pl.BlockSpec((1, GW), lambda i: (0, 0))],
        out_shape=[SDS(dproj.shape, BF16), SDS((GROUPS, CHUNK, CHUNK), F32), SDS((CHUNK, LANES), F32), SDS((1, GW), F32)],
        scratch_shapes=[pltpu.VMEM((CHUNK, GW), F32), pltpu.VMEM((tb, GW), F32)],
        input_output_aliases={8: 0},
        compiler_params=_cp("arbitrary"),
    )(proj, proj, dga, rstd_v, g_v, w_s, b_col, sel, dproj)


def _mix_fwd(proj, y_a, y_b, D, name):
    T = proj.shape[0]
    tb = _tile(T, 512, 8)
    td = _tile(D, 1024, 128)
    nd = D // td

    def body(ga_ref, gb_ref, ya_ref, yb_ref, o_ref):
        o_ref[...] = (_sigmoid(ga_ref[...]) * ya_ref[...] + _sigmoid(gb_ref[...]) * yb_ref[...]).astype(BF16)

    blk = pl.BlockSpec((tb, td), lambda i, j: (i, j))
    return pl.pallas_call(
        body, name=name, grid=(T // tb, nd),
        in_specs=[blk, pl.BlockSpec((tb, td), lambda i, j: (i, j + nd)), blk, blk],
        out_specs=blk, out_shape=SDS((T, D), BF16),
        compiler_params=_cp("parallel", "parallel"),
    )(proj, proj, y_a, y_b)


def _mix_bwd(proj, y_a, y_b, dmixed, D, name):
    T, width = proj.shape
    tb = _tile(T, ROW_BLOCK, 8)

    def body(g_ref, ya_ref, yb_ref, dm_ref, dya_ref, dyb_ref, dp_ref):
        dm = dm_ref[...].astype(F32)
        sa = _sigmoid(g_ref[:, :D])
        sb = _sigmoid(g_ref[:, D:])
        dya_ref[...] = (dm * sa).astype(BF16)
        dyb_ref[...] = (dm * sb).astype(BF16)
        dp_ref[:, :D] = (dm * ya_ref[...] * sa * (1.0 - sa)).astype(BF16)
        dp_ref[:, D:] = (dm * yb_ref[...] * sb * (1.0 - sb)).astype(BF16)

    blk = pl.BlockSpec((tb, D), lambda i: (i, 0))
    gates = pl.BlockSpec((tb, 2 * D), lambda i: (i, 0))
    return pl.pallas_call(
        body, name=name, grid=(T // tb,),
        in_specs=[gates, blk, blk, blk],
        out_specs=[blk, blk, gates], out_shape=[SDS((T, D), BF16), SDS((T, D), BF16), SDS((T, width), BF16)],
        compiler_params=_cp("parallel"),
    )(proj, y_a, y_b, dmixed)


def _lat_norm_fwd(proj, g_q, g_kv, D, GW, QL, KVL, name):
    T = proj.shape[0]
    tb = _tile(T, 512, 8)
    qcol = (2 * D + 2 * GW) // QL
    kvcol = (2 * D + 2 * GW + QL) // KVL

    def body(q_ref, kv_ref, gq_ref, gkv_ref, qn_ref, kvn_ref, rq_ref, rkv_ref):
        qv = q_ref[...]
        rq = lax.rsqrt(jnp.mean(qv * qv, axis=1, keepdims=True) + EPS)
        qn_ref[...] = (qv * rq * gq_ref[...]).astype(BF16)
        rq_ref[...] = rq
        kv = kv_ref[...]
        rkv = lax.rsqrt(jnp.mean(kv * kv, axis=1, keepdims=True) + EPS)
        kvn_ref[...] = (kv * rkv * gkv_ref[...]).astype(BF16)
        rkv_ref[...] = rkv

    return pl.pallas_call(
        body, name=name, grid=(T // tb,),
        in_specs=[pl.BlockSpec((tb, QL), lambda i: (i, qcol)), pl.BlockSpec((tb, KVL), lambda i: (i, kvcol)),
                  pl.BlockSpec((1, QL), lambda i: (0, 0)), pl.BlockSpec((1, KVL), lambda i: (0, 0))],
        out_specs=[pl.BlockSpec((tb, QL), lambda i: (i, 0)), pl.BlockSpec((tb, KVL), lambda i: (i, 0)),
                   pl.BlockSpec((tb, 1), lambda i: (i, 0)), pl.BlockSpec((tb, 1), lambda i: (i, 0))],
        out_shape=[SDS((T, QL), BF16), SDS((T, KVL), BF16), SDS((T, 1), F32), SDS((T, 1), F32)],
        compiler_params=_cp("parallel"),
    )(proj, proj, g_q, g_kv)


def _lat_norm_bwd(proj, dqn, dkvn, dkpe, rq, rkv, g_q, g_kv, dproj, D, GW, QL, KVL, name):
    T, width = proj.shape
    tb = _tile(T, 512, 8)
    qcol = (2 * D + 2 * GW) // QL
    kvcol = (2 * D + 2 * GW + QL) // KVL
    tail = width - (2 * D + 2 * GW)
    assert (2 * D + 2 * GW) % tail == 0 and tail >= QL + KVL + LANES
    tailcol = (2 * D + 2 * GW) // tail

    def one(xv, rs, gv, dy, gg_ref):
        xhat = xv * rs
        gg_ref[...] += jnp.sum(dy * xhat, axis=0, keepdims=True)
        dxhat = dy * gv
        cm = jnp.mean(dxhat * xhat, axis=1, keepdims=True)
        return rs * (dxhat - xhat * cm)

    def body(q_ref, kv_ref, dqn_ref, dkvn_ref, dkpe_ref, rq_ref, rkv_ref, gq_ref, gkv_ref, _dproj_in,
             tail_ref, ggq_ref, ggkv_ref):
        @pl.when(pl.program_id(0) == 0)
        def _():
            ggq_ref[...] = jnp.zeros_like(ggq_ref)
            ggkv_ref[...] = jnp.zeros_like(ggkv_ref)

        tail_ref[:, :QL] = one(q_ref[...], rq_ref[...], gq_ref[...], dqn_ref[...], ggq_ref).astype(BF16)
        tail_ref[:, QL:QL + KVL] = one(kv_ref[...], rkv_ref[...], gkv_ref[...], dkvn_ref[...], ggkv_ref).astype(BF16)
        tail_ref[:, QL + KVL:QL + KVL + LANES] = dkpe_ref[...].astype(BF16)
        if tail > QL + KVL + LANES:
            tail_ref[:, QL + KVL + LANES:] = jnp.zeros((tb, tail - (QL + KVL + LANES)), BF16)

    return pl.pallas_call(
        body, name=name, grid=(T // tb,),
        in_specs=[pl.BlockSpec((tb, QL), lambda i: (i, qcol)), pl.BlockSpec((tb, KVL), lambda i: (i, kvcol)),
                  pl.BlockSpec((tb, QL), lambda i: (i, 0)), pl.BlockSpec((tb, KVL), lambda i: (i, 0)),
                  pl.BlockSpec((tb, LANES), lambda i: (i, 0)),
                  pl.BlockSpec((tb, 1), lambda i: (i, 0)), pl.BlockSpec((tb, 1), lambda i: (i, 0)),
                  pl.BlockSpec((1, QL), lambda i: (0, 0)), pl.BlockSpec((1, KVL), lambda i: (0, 0)),
                  pl.BlockSpec(memory_space=pl.ANY)],
        out_specs=[pl.BlockSpec((tb, tail), lambda i: (i, tailcol)),
                   pl.BlockSpec((1, QL), lambda i: (0, 0)), pl.BlockSpec((1, KVL), lambda i: (0, 0))],
        out_shape=[SDS(dproj.shape, BF16), SDS((1, QL), F32), SDS((1, KVL), F32)],
        input_output_aliases={9: 0},
        compiler_params=_cp("arbitrary"),
    )(proj, proj, dqn, dkvn, dkpe, rq, rkv, g_q, g_kv, dproj)


def _swap_halves(r):
    lane = lax.broadcasted_iota(jnp.int32, r.shape, 1)
    lo = pltpu.roll(r, LANES - HALF_ROPE, 1)
    hi = pltpu.roll(r, HALF_ROPE, 1)
    return jnp.where(lane < HALF_ROPE, lo, jnp.where(lane < ROPE, hi, 0.0))


def _rope_fwd(r, cos_t, sin_t):
    return r * cos_t + _swap_halves(r) * sin_t


def _rope_bwd(d, cos_t, sin_t):
    return d * cos_t + _swap_halves(d * sin_t)


def _head_norm_bwd(xn, xr, rs, g_n, g_r, dyn, dyr):
    xhn, xhr = xn * rs, xr * rs
    dn, dr = dyn * g_n, dyr * g_r
    cm = (jnp.sum(dn * xhn, axis=1, keepdims=True) + jnp.sum(dr * xhr, axis=1, keepdims=True)) * (1.0 / QK_HEAD)
    return rs * (dn - xhn * cm), rs * (dr - xhr * cm), dyn * xhn, dyr * xhr


def _q_up_fwd(qn, w_uq_al, g_al, cos_t, sin_t, name):
    T, QL = qn.shape
    HP = w_uq_al.shape[1]
    tb = _tile(T, 512, 8)
    hw = _tile(HP, 1024, HEAD_PAD)

    def body(x_ref, w_ref, g_ref, c_ref, s_ref, o_ref):
        raw = _dot(x_ref[...], w_ref[...], NN)
        gv = g_ref[...]
        cv, sv = c_ref[...], s_ref[...]
        for h in range(hw // HEAD_PAD):
            xn = raw[:, h * HEAD_PAD:h * HEAD_PAD + LANES]
            xr = raw[:, h * HEAD_PAD + LANES:(h + 1) * HEAD_PAD]
            ss = jnp.sum(xn * xn, axis=1, keepdims=True) + jnp.sum(xr * xr, axis=1, keepdims=True)
            rs = lax.rsqrt(ss * (1.0 / QK_HEAD) + EPS)
            o_ref[:, h * HEAD_PAD:h * HEAD_PAD + LANES] = (xn * rs * gv[:, :LANES]).astype(BF16)
            o_ref[:, h * HEAD_PAD + LANES:(h + 1) * HEAD_PAD] = _rope_fwd(xr * rs * gv[:, LANES:], cv, sv).astype(BF16)

    return pl.pallas_call(
        body, name=name, grid=(T // tb, HP // hw),
        in_specs=[pl.BlockSpec((tb, QL), lambda i, j: (i, 0)), pl.BlockSpec((QL, hw), lambda i, j: (0, j)),
                  pl.BlockSpec((1, HEAD_PAD), lambda i, j: (0, 0)),
                  pl.BlockSpec((tb, LANES), lambda i, j: (i, 0)), pl.BlockSpec((tb, LANES), lambda i, j: (i, 0))],
        out_specs=pl.BlockSpec((tb, hw), lambda i, j: (i, j)), out_shape=SDS((T, HP), BF16),
        compiler_params=_cp("parallel", "parallel"),
    )(qn, w_uq_al, g_al, cos_t, sin_t)


def _q_up_bwd(qn, w_uq_al, g_al, cos_t, sin_t, dq, name):
    T, QL = qn.shape
    HP = w_uq_al.shape[1]
    tb = _tile(T, 512, 8)
    hw = _tile(HP, 1024, HEAD_PAD)

    def body(x_ref, w_ref, g_ref, c_ref, s_ref, dq_ref, o_ref, gg_ref):
        @pl.when((pl.program_id(0) == 0) & (pl.program_id(1) == 0))
        def _():
            gg_ref[...] = jnp.zeros_like(gg_ref)

        raw = _dot(x_ref[...], w_ref[...], NN)
        gv = g_ref[...]
        cv, sv = c_ref[...], s_ref[...]
        for h in range(hw // HEAD_PAD):
            lo, mid, hi = h * HEAD_PAD, h * HEAD_PAD + LANES, (h + 1) * HEAD_PAD
            xn, xr = raw[:, lo:mid], raw[:, mid:hi]
            ss = jnp.sum(xn * xn, axis=1, keepdims=True) + jnp.sum(xr * xr, axis=1, keepdims=True)
            rs = lax.rsqrt(ss * (1.0 / QK_HEAD) + EPS)
            dyn = dq_ref[:, lo:mid]
            dyr = _rope_bwd(dq_ref[:, mid:hi], cv, sv)
            dxn, dxr, ggn, ggr = _head_norm_bwd(xn, xr, rs, gv[:, :LANES], gv[:, LANES:], dyn, dyr)
            o_ref[:, lo:mid] = dxn.astype(BF16)
            o_ref[:, mid:hi] = dxr.astype(BF16)
            gg_ref[:, :LANES] += jnp.sum(ggn, axis=0, keepdims=True)
            gg_ref[:, LANES:] += jnp.sum(ggr, axis=0, keepdims=True)

    return pl.pallas_call(
        body, name=name, grid=(T // tb, HP // hw),
        in_specs=[pl.BlockSpec((tb, QL), lambda i, j: (i, 0)), pl.BlockSpec((QL, hw), lambda i, j: (0, j)),
                  pl.BlockSpec((1, HEAD_PAD), lambda i, j: (0, 0)),
                  pl.BlockSpec((tb, LANES), lambda i, j: (i, 0)), pl.BlockSpec((tb, LANES), lambda i, j: (i, 0)),
                  pl.BlockSpec((tb, hw), lambda i, j: (i, j))],
        out_specs=[pl.BlockSpec((tb, hw), lambda i, j: (i, j)), pl.BlockSpec((1, HEAD_PAD), lambda i, j: (0, 0))],
        out_shape=[SDS((T, HP), BF16), SDS((1, HEAD_PAD), F32)],
        compiler_params=_cp("arbitrary", "arbitrary"),
    )(qn, w_uq_al, g_al, cos_t, sin_t, dq)


def _kv_up_fwd(kvn, w_ukv, proj, g_al, cos_t, sin_t, kpe_col, name):
    T, KVL = kvn.shape
    n_shard = w_ukv.shape[2]
    HP = N_DEV * n_shard
    tb = _tile(T, 512, 8)
    hw = _tile(n_shard, 1024, HEAD_PAD)
    per = n_shard // hw
    nh = hw // HEAD_PAD

    def body(x_ref, w_ref, kpe_ref, g_ref, c_ref, s_ref, k_ref, v_ref):
        raw = _dot(x_ref[...], w_ref[...], NN)
        gv = g_ref[...]
        cv, sv = c_ref[...], s_ref[...]
        kpe = kpe_ref[...]
        kpe_ss = jnp.sum(kpe * kpe, axis=1, keepdims=True)
        for h in range(nh):
            lo, mid, hi = h * HEAD_PAD, h * HEAD_PAD + LANES, (h + 1) * HEAD_PAD
            xn = raw[:, lo:mid]
            rs = lax.rsqrt((jnp.sum(xn * xn, axis=1, keepdims=True) + kpe_ss) * (1.0 / QK_HEAD) + EPS)
            k_ref[:, lo:mid] = (xn * rs * gv[:, :LANES]).astype(BF16)
            k_ref[:, mid:hi] = _rope_fwd(kpe * rs * gv[:, LANES:], cv, sv).astype(BF16)
            v_ref[:, h * V_HEAD:(h + 1) * V_HEAD] = raw[:, mid:hi].astype(BF16)

    return pl.pallas_call(
        body, name=name, grid=(T // tb, HP // hw),
        in_specs=[pl.BlockSpec((tb, KVL), lambda i, j: (i, 0)),
                  pl.BlockSpec((None, KVL, hw), lambda i, j: (j // per, 0, j % per)),
                  pl.BlockSpec((tb, LANES), lambda i, j: (i, kpe_col)),
                  pl.BlockSpec((1, HEAD_PAD), lambda i, j: (0, 0)),
                  pl.BlockSpec((tb, LANES), lambda i, j: (i, 0)), pl.BlockSpec((tb, LANES), lambda i, j: (i, 0))],
        out_specs=[pl.BlockSpec((tb, hw), lambda i, j: (i, j)), pl.BlockSpec((tb, nh * V_HEAD), lambda i, j: (i, j))],
        out_shape=[SDS((T, HP), BF16), SDS((T, HP // 2), BF16)],
        compiler_params=_cp("parallel", "parallel"),
    )(kvn, w_ukv, proj, g_al, cos_t, sin_t)


def _kv_up_bwd(kvn, w_ukv, proj, g_al, cos_t, sin_t, dk, dv, kpe_col, name):
    T, KVL = kvn.shape
    n_shard = w_ukv.shape[2]
    HP = N_DEV * n_shard
    tb = _tile(T, 512, 8)
    hw = _tile(n_shard, 1024, HEAD_PAD)
    per = n_shard // hw
    nh = hw // HEAD_PAD

    def body(x_ref, w_ref, kpe_ref, g_ref, c_ref, s_ref, dk_ref, dv_ref, o_ref, dkpe_ref, gg_ref):
        i, j = pl.program_id(0), pl.program_id(1)

        @pl.when((i == 0) & (j == 0))
        def _():
            gg_ref[...] = jnp.zeros_like(gg_ref)

        @pl.when(j == 0)
        def _():
            dkpe_ref[...] = jnp.zeros_like(dkpe_ref)

        raw = _dot(x_ref[...], w_ref[...], NN)
        gv = g_ref[...]
        cv, sv = c_ref[...], s_ref[...]
        kpe = kpe_ref[...]
        kpe_ss = jnp.sum(kpe * kpe, axis=1, keepdims=True)
        for h in range(nh):
            lo, mid, hi = h * HEAD_PAD, h * HEAD_PAD + LANES, (h + 1) * HEAD_PAD
            xn = raw[:, lo:mid]
            rs = lax.rsqrt((jnp.sum(xn * xn, axis=1, keepdims=True) + kpe_ss) * (1.0 / QK_HEAD) + EPS)
            dyn = dk_ref[:, lo:mid]
            dyr = _rope_bwd(dk_ref[:, mid:hi], cv, sv)
            dxn, dxr, ggn, ggr = _head_norm_bwd(xn, kpe, rs, gv[:, :LANES], gv[:, LANES:], dyn, dyr)
            o_ref[:, lo:mid] = dxn.astype(BF16)
            o_ref[:, mid:hi] = dv_ref[:, h * V_HEAD:(h + 1) * V_HEAD].astype(BF16)
            dkpe_ref[...] += dxr
            gg_ref[:, :LANES] += jnp.sum(ggn, axis=0, keepdims=True)
            gg_ref[:, LANES:] += jnp.sum(ggr, axis=0, keepdims=True)

    return pl.pallas_call(
        body, name=name, grid=(T // tb, HP // hw),
        in_specs=[pl.BlockSpec((tb, KVL), lambda i, j: (i, 0)),
                  pl.BlockSpec((None, KVL, hw), lambda i, j: (j // per, 0, j % per)),
                  pl.BlockSpec((tb, LANES), lambda i, j: (i, kpe_col)),
                  pl.BlockSpec((1, HEAD_PAD), lambda i, j: (0, 0)),
                  pl.BlockSpec((tb, LANES), lambda i, j: (i, 0)), pl.BlockSpec((tb, LANES), lambda i, j: (i, 0)),
                  pl.BlockSpec((tb, hw), lambda i, j: (i, j)), pl.BlockSpec((tb, nh * V_HEAD), lambda i, j: (i, j))],
        out_specs=[pl.BlockSpec((tb, hw), lambda i, j: (i, j)), pl.BlockSpec((tb, LANES), lambda i, j: (i, 0)),
                   pl.BlockSpec((1, HEAD_PAD), lambda i, j: (0, 0))],
        out_shape=[SDS((T, HP), BF16), SDS((T, LANES), F32), SDS((1, HEAD_PAD), F32)],
        compiler_params=_cp("arbitrary", "arbitrary"),
    )(kvn, w_ukv, proj, g_al, cos_t, sin_t, dk, dv)


def _attn_fwd(q, k, v, H, name, comm=None):
    B, S, _ = q.shape
    tq = _tile(S, ATTN_BLOCK, 128)
    scale = QK_HEAD ** -0.5

    def body(q_ref, k_ref, v_ref, o_ref, l_ref):
        qi = pl.program_id(2)
        qv = q_ref[...]

        def step(j, carry, diagonal):
            m, l, acc = carry
            off = pl.multiple_of(j * tq, tq)
            kj = k_ref[pl.ds(off, tq), :]
            vj = v_ref[pl.ds(off, tq), :]
            s = _dot(qv, kj, NT) * scale
            if diagonal:
                row = lax.broadcasted_iota(jnp.int32, (tq, tq), 0)
                col = lax.broadcasted_iota(jnp.int32, (tq, tq), 1)
                s = jnp.where(col <= row, s, -jnp.inf)
            m_new = jnp.maximum(m, jnp.max(s, axis=1, keepdims=True))
            alpha = jnp.exp(m - m_new)
            p = jnp.exp(s - m_new)
            l = alpha * l + jnp.sum(p, axis=1, keepdims=True)
            acc = alpha * acc + _dot(p.astype(BF16), vj, NN)
            return m_new, l, acc

        init = (jnp.full((tq, 1), -1e30, F32), jnp.zeros((tq, 1), F32), jnp.zeros((tq, V_HEAD), F32))
        carry = lax.fori_loop(0, qi, functools.partial(step, diagonal=False), init)
        m, l, acc = step(qi, carry, diagonal=True)
        o_ref[...] = (acc / l).astype(BF16)
        l_ref[...] = m + jnp.log(l)

    grid = (B, H, S // tq)
    in_specs = [pl.BlockSpec((None, tq, HEAD_PAD), lambda b, h, i: (b, i, h)),
                pl.BlockSpec((None, S, HEAD_PAD), lambda b, h, i: (b, 0, h)),
                pl.BlockSpec((None, S, V_HEAD), lambda b, h, i: (b, 0, h))]
    out_specs = [pl.BlockSpec((None, tq, V_HEAD), lambda b, h, i: (b, i, h)),
                 pl.BlockSpec((None, None, tq, 1), lambda b, h, i: (b, h, i, 0))]
    out_shape = [SDS((B, S, H * V_HEAD), BF16), SDS((B, H, S, 1), F32)]
    if comm is None:
        return pl.pallas_call(
            body, name=name, grid=grid, in_specs=in_specs, out_specs=out_specs, out_shape=out_shape,
            compiler_params=_cp("parallel", "parallel", "parallel"),
        )(q, k, v)

    def first_last():
        b, h, i = pl.program_id(0), pl.program_id(1), pl.program_id(2)
        return (b == 0) & (h == 0) & (i == 0), (b == B - 1) & (h == H - 1) & (i == grid[2] - 1)

    outs = pl.pallas_call(
        _carry(body, comm, 3, 2, 0, first_last), name=name, grid=grid,
        in_specs=in_specs + [HBM_SPEC] * len(comm.ins), out_specs=out_specs + [HBM_SPEC] * len(comm.out_shape),
        out_shape=out_shape + comm.out_shape, scratch_shapes=comm.scratch,
        compiler_params=_cp("arbitrary", "arbitrary", "arbitrary"),
    )(q, k, v, *comm.ins)
    return outs[0], outs[1], comm.results(outs[2:])


def _attn_bwd(q, k, v, o, lse, do, H, name, comm=None):
    B, S, _ = q.shape
    tq = _tile(S, ATTN_BLOCK, 128)
    nq = S // tq
    scale = QK_HEAD ** -0.5

    def body(q_ref, k_ref, v_ref, o_ref, l_ref, do_ref, dq_ref, dk_ref, dv_ref, delta_ref, dq_acc):
        dq_acc[...] = jnp.zeros_like(dq_acc)
        for i in range(nq):
            rows = slice(i * tq, (i + 1) * tq)
            delta_ref[rows, :] = jnp.sum(do_ref[rows, :].astype(F32) * o_ref[rows, :].astype(F32), axis=1, keepdims=True)
        def kv_step(j, carry):
            offk = pl.multiple_of(j * tq, tq)
            kj = k_ref[pl.ds(offk, tq), :]
            vj = v_ref[pl.ds(offk, tq), :]

            def q_step(i, acc, diagonal):
                dk_acc, dv_acc = acc
                offq = pl.multiple_of(i * tq, tq)
                qi = q_ref[pl.ds(offq, tq), :]
                doi = do_ref[pl.ds(offq, tq), :]
                s = _dot(qi, kj, NT) * scale
                p = jnp.exp(s - l_ref[pl.ds(offq, tq), :])
                if diagonal:
                    row = lax.broadcasted_iota(jnp.int32, (tq, tq), 0)
                    col = lax.broadcasted_iota(jnp.int32, (tq, tq), 1)
                    p = jnp.where(col <= row, p, 0.0)
                dv_acc = dv_acc + _dot(p.astype(BF16), doi, TN)
                dp = _dot(doi, vj, NT)
                ds = (p * (dp - delta_ref[pl.ds(offq, tq), :]) * scale).astype(BF16)
                dk_acc = dk_acc + _dot(ds, qi, TN)
                dq_acc[pl.ds(offq, tq), :] += _dot(ds, kj, NN)
                return dk_acc, dv_acc

            acc = q_step(j, (jnp.zeros((tq, HEAD_PAD), F32), jnp.zeros((tq, V_HEAD), F32)), diagonal=True)
            dk_acc, dv_acc = lax.fori_loop(j + 1, nq, functools.partial(q_step, diagonal=False), acc)
            dk_ref[pl.ds(offk, tq), :] = dk_acc.astype(BF16)
            dv_ref[pl.ds(offk, tq), :] = dv_acc.astype(BF16)
            return carry

        lax.fori_loop(0, nq, kv_step, 0)
        dq_ref[...] = dq_acc[...].astype(BF16)

    qk_spec = pl.BlockSpec((None, S, HEAD_PAD), lambda b, h: (b, 0, h))
    v_spec = pl.BlockSpec((None, S, V_HEAD), lambda b, h: (b, 0, h))
    in_specs = [qk_spec, qk_spec, v_spec, v_spec, pl.BlockSpec((None, None, S, 1), lambda b, h: (b, h, 0, 0)), v_spec]
    out_specs = [qk_spec, qk_spec, v_spec]
    out_shape = [SDS((B, S, H * HEAD_PAD), BF16), SDS((B, S, H * HEAD_PAD), BF16), SDS((B, S, H * V_HEAD), BF16)]
    scratch = [pltpu.VMEM((S, 1), F32), pltpu.VMEM((S, HEAD_PAD), F32)]
    if comm is None:
        return pl.pallas_call(
            body, name=name, grid=(B, H), in_specs=in_specs, out_specs=out_specs, out_shape=out_shape,
            scratch_shapes=scratch, compiler_params=_cp("parallel", "parallel"),
        )(q, k, v, o, lse, do)

    def first_last():
        b, h = pl.program_id(0), pl.program_id(1)
        return (b == 0) & (h == 0), (b == B - 1) & (h == H - 1)

    outs = pl.pallas_call(
        _carry(body, comm, 6, 3, 2, first_last), name=name, grid=(B, H),
        in_specs=in_specs + [HBM_SPEC] * len(comm.ins), out_specs=out_specs + [HBM_SPEC] * len(comm.out_shape),
        out_shape=out_shape + comm.out_shape, scratch_shapes=scratch + comm.scratch,
        compiler_params=_cp("arbitrary", "arbitrary"),
    )(q, k, v, o, lse, do, *comm.ins)
    return outs[0], outs[1], outs[2], comm.results(outs[3:])


def _natural(sm):
    nd, R, n = sm.shape
    return jnp.transpose(sm, (1, 0, 2)).reshape(R, nd * n)


def _col_shards(full):
    R, N = full.shape
    return jnp.transpose(full.reshape(R, N_DEV, N // N_DEV), (1, 0, 2))


def _shard_rows(sm, lo, hi, n):
    out = []
    for j in range(N_DEV):
        a, b = max(lo, j * n), min(hi, (j + 1) * n)
        if a < b:
            out.append(sm[j, a - j * n:b - j * n])
    return out


def _al_rows(g_t, lo, hi, off_gate, n_gate):
    parts = []
    if lo < off_gate:
        parts.append(g_t[n_gate + lo:n_gate + min(hi, off_gate)])
    if hi > off_gate:
        parts.append(g_t[max(lo, off_gate) - off_gate:hi - off_gate])
    return parts[0] if len(parts) == 1 else jnp.concatenate(parts, axis=0)


def _pad_heads(w, H):
    R = w.shape[0]
    return jnp.pad(w.reshape(R, H, QK_HEAD), ((0, 0), (0, 0), (0, HEAD_PAD - QK_HEAD))).reshape(R, H * HEAD_PAD)


def _unpad_heads(w, H):
    R = w.shape[0]
    return w.reshape(R, H, HEAD_PAD)[:, :, :QK_HEAD].reshape(R, H * QK_HEAD)


def _pad_gain(g):
    return jnp.pad(g, ((0, 0), (0, HEAD_PAD - QK_HEAD)))


def _rope_tables(positions):
    inv_freq = 1.0 / (ROPE_THETA ** (jnp.arange(0, ROPE, 2, dtype=F32) / ROPE))
    ang = positions.astype(F32).reshape(-1, 1) * inv_freq
    cos, sin = jnp.cos(ang), jnp.sin(ang)
    zeros = jnp.zeros((ang.shape[0], LANES - ROPE), F32)
    return jnp.concatenate([cos, cos, zeros], axis=1), jnp.concatenate([-sin, sin, zeros], axis=1)


def kernel(x, c, positions, w_ada, b_ada, g_norm1, w_in, g_v, w_s, b_s, g_q_lat, g_kv_lat, w_uq, w_ukv, g_qn, g_kn, w_branch_a, w_branch_b, w_out, g_norm2, w_ff1, w_ff2, loss_target, m_w_ada, m_b_ada, m_g_norm1, m_w_in, m_g_v, m_w_s, m_b_s, m_g_q_lat, m_g_kv_lat, m_w_uq, m_w_ukv, m_g_qn, m_g_kn, m_w_branch_a, m_w_branch_b, m_w_out, m_g_norm2, m_w_ff1, m_w_ff2, v_w_ada, v_b_ada, v_g_norm1, v_w_in, v_g_v, v_w_s, v_b_s, v_g_q_lat, v_g_kv_lat, v_w_uq, v_w_ukv, v_g_qn, v_g_kn, v_w_branch_a, v_w_branch_b, v_w_out, v_g_norm2, v_w_ff1, v_w_ff2):
    B, S, D = x.shape
    T = B * S
    GW = g_v.shape[-1]
    QL = g_q_lat.shape[-1]
    KVL = g_kv_lat.shape[-1]
    H = w_uq.shape[-1] * N_DEV // QK_HEAD
    IN = w_in.shape[-1] * N_DEV
    OFF_GATE = IN - 2 * D
    IN_AL = _round_up(2 * D + OFF_GATE + (LANES - ROPE), 512)
    assert OFF_GATE == 2 * GW + QL + KVL + ROPE
    assert (2 * D) % GW == 0 and (2 * D + 2 * GW) % QL == 0 and (2 * D + 2 * GW + QL) % KVL == 0
    kpe_col = (2 * D + 2 * GW + QL + KVL) // LANES

    xi, yi, ci = _here()
    dev = 4 * xi + 2 * yi + ci
    c_idx = jnp.reshape(ci, (1,)).astype(jnp.int32)

    big = [w_in, w_uq, w_ukv, w_branch_a, w_branch_b, w_out, w_ff1, w_ff2]
    s_uq, s_ukv, s_ba, s_bb, s_out, s_ff1, s_ff2 = [w[0].astype(BF16) for w in big[1:]]
    n_in = IN // N_DEV
    s_in_t = jnp.transpose(w_in[0]).astype(BF16)
    ((g_in_t,),) = _run_comm([_GatherJob([s_in_t])], "ag_w_in")
    w_al_t = jnp.concatenate(_shard_rows(g_in_t, OFF_GATE, IN, n_in) + _shard_rows(g_in_t, 0, OFF_GATE, n_in)
                             + [jnp.zeros((IN_AL - IN, D), BF16)], axis=0)

    n_ada = w_ada.shape[-1]
    c_all = _all_gather_small(c, "ag_c").reshape(N_DEV * B, D)
    b_cols = lax.dynamic_slice(b_ada, (0, dev * n_ada), (1, n_ada))
    mod_cols = _ada_fwd(c_all, w_ada[0], b_cols, "ada_fwd")
    mod_all = _all_gather_small(mod_cols, "ag_mod")
    mod_mine = lax.dynamic_slice(mod_all, (0, dev * B, 0), (N_DEV, B, n_ada))
    mod_mine = jnp.transpose(mod_mine, (1, 0, 2)).reshape(B, 6, 1, D)
    sh1, sc1, ga1, sh2, sc2, ga2 = [mod_mine[:, k] for k in range(6)]

    cos_t, sin_t = _rope_tables(positions)
    g_qn_al, g_kn_al = _pad_gain(g_qn), _pad_gain(g_kn)
    b_col = b_s[0].reshape(GROUPS, CHUNK, 1)
    gw = GW // GROUPS
    sel = (jnp.arange(GW)[:, None] // gw == jnp.arange(LANES)[None, :]).astype(F32)

    h1, rstd1 = _norm_mod_fwd(x, g_norm1, sh1, sc1, "norm1_fwd")
    h1f = h1.reshape(T, D)
    proj, ((g_uq, g_ukv, g_ba, g_bb, g_out),) = _matmul(
        h1f, w_al_t, mode="nt", out_dtypes=[F32], name="mm_proj", comm=_Comm([_GatherJob([s_uq, s_ukv, s_ba, s_bb, s_out])]))
    w_uq_al = _pad_heads(_natural(g_uq), H)
    w_bb_f = g_bb.reshape(-1, D)
    w_out_f = g_out.reshape(-1, D)
    ga_act, rstd_v = _gmlp_fwd(proj, g_v, w_s[0], b_col, D, GW, "gmlp_fwd")
    y_a = _matmul(ga_act, g_ba, mode="nn", out_dtypes=[BF16], name="mm_ya", b_shards=True)
    qn, kvn, rstd_q, rstd_kv = _lat_norm_fwd(proj, g_q_lat, g_kv_lat, D, GW, QL, KVL, "latnorm_fwd")
    q_al = _q_up_fwd(qn, w_uq_al, g_qn_al, cos_t, sin_t, "q_up_fwd")
    k_al, v_al = _kv_up_fwd(kvn, g_ukv, proj, g_kn_al, cos_t, sin_t, kpe_col, "kv_up_fwd")
    q3, k3, v3 = q_al.reshape(B, S, -1), k_al.reshape(B, S, -1), v_al.reshape(B, S, -1)
    attn, lse, ((g_ff1,),) = _attn_fwd(q3, k3, v3, H, "attn_fwd", comm=_Comm([_GatherJob([s_ff1])]))
    attn_f = attn.reshape(T, H * V_HEAD)
    y_b = _matmul(attn_f, w_bb_f, mode="nn", out_dtypes=[BF16], name="mm_yb")
    mixed = _mix_fwd(proj, y_a, y_b, D, "mix_fwd")
    o = _matmul(mixed, w_out_f, mode="nn", out_dtypes=[F32], name="mm_o")
    x1, h2, rstd2 = _norm_mod_fwd(x, g_norm2, sh2, sc2, "norm2_fwd", o=o.reshape(B, S, D), ga=ga1)

    def relu_sq(acc):
        r = jnp.maximum(acc, 0.0)
        return r * r, r

    (a_act, r_act), ((g_ff2,),) = _matmul(h2.reshape(T, D), g_ff1, mode="nn", out_dtypes=[BF16, BF16], name="mm_ff1",
                                          epi=relu_sq, comm=_Comm([_GatherJob([s_ff2])]), b_shards=True)
    w_ff2_f = g_ff2.reshape(-1, D)
    ff = _matmul(a_act, w_ff2_f, mode="nn", out_dtypes=[F32], name="mm_ff2")
    loss_part, dy, dff, d_ga2 = _loss_head(x1, ff.reshape(B, S, D), loss_target, ga2, "loss_head")
    loss = lax.psum(loss_part[0, 0], ("x", "y", "c"))

    dff_f = dff.reshape(T, D)
    df1 = _matmul(dff_f, w_ff2_f, mode="nt", out_dtypes=[BF16], name="mm_da", epi=lambda acc, r: (acc * (2.0 * r.astype(F32)),),
                  extras=(r_act,))
    gs_ff2 = _matmul(a_act, dff_f, mode="tn", out_dtypes=[BF16], name="mm_gw_ff2").reshape(N_DEV, -1, D)
    gs_ff1, ((sib_ff2,),) = _matmul(h2.reshape(T, D), df1, mode="tn", out_dtypes=[BF16], name="mm_gw_ff1",
                                    comm=_Comm([_SiblingJob([gs_ff2])]), out_shards=True)
    cs_ff2 = _chip_sum(gs_ff2, sib_ff2, c_idx, "chip_sum_w_ff2")
    dh2, ((parts_ff2,), (sib_ff1,)) = _matmul(df1, g_ff1, mode="nt", out_dtypes=[F32], name="mm_dh2", b_shards=True,
                                              comm=_Comm([_ChipsJob([cs_ff2]), _SiblingJob([gs_ff1])]))
    cs_ff1 = _chip_sum(gs_ff1, sib_ff1, c_idx, "chip_sum_w_ff1")
    dx1, d_sh2, d_sc2, gg_norm2, d_ga1, do = _norm_mod_bwd(
        dh2.reshape(B, S, D), x1, rstd2, sc2, g_norm2, dy, "norm2_bwd", o=o.reshape(B, S, D), ga=ga1)

    do_f = do.reshape(T, D)
    dmixed = _matmul(do_f, w_out_f, mode="nt", out_dtypes=[BF16], name="mm_dmixed")
    gw_out = _matmul(mixed, do_f, mode="tn", out_dtypes=[BF16], name="mm_gw_out")
    dy_a, dy_b, dproj = _mix_bwd(proj, y_a, y_b, dmixed, D, "mix_bwd")
    gs_ba = _matmul(ga_act, dy_a, mode="tn", out_dtypes=[BF16], name="mm_gw_ba", out_shards=True)
    dga_act = _matmul(dy_a, g_ba, mode="nt", out_dtypes=[BF16], name="mm_dga", b_shards=True)
    gw_bb = _matmul(attn_f, dy_b, mode="tn", out_dtypes=[BF16], name="mm_gw_bb")
    dattn = _matmul(dy_b, w_bb_f, mode="nt", out_dtypes=[BF16], name="mm_dattn")
    dproj, gg_ws, gg_bs_t, gg_gv = _gmlp_bwd(proj, dga_act, rstd_v, g_v, w_s[0], b_col, sel, dproj, D, GW, "gmlp_bwd")
    gs_early = [gs_ba, gw_bb.reshape(N_DEV, -1, D), gw_out.reshape(N_DEV, -1, D)]
    dq, dk, dv, ((parts_ff1,), sib_early) = _attn_bwd(q3, k3, v3, attn, lse, dattn.reshape(B, S, -1), H, "attn_bwd",
                                                      comm=_Comm([_ChipsJob([cs_ff1]), _SiblingJob(gs_early)]))
    cs_early = [_chip_sum(g, r, c_idx, "chip_sum_" + nm)
                for g, r, nm in zip(gs_early, sib_early, ["w_branch_a", "w_branch_b", "w_out"])]
    dq_raw, gg_qn = _q_up_bwd(qn, w_uq_al, g_qn_al, cos_t, sin_t, dq.reshape(T, -1), "q_up_bwd")
    dkv_raw, dkpe, gg_kn = _kv_up_bwd(kvn, g_ukv, proj, g_kn_al, cos_t, sin_t, dk.reshape(T, -1), dv.reshape(T, -1),
                                      kpe_col, "kv_up_bwd")
    gw_uq_al = _matmul(qn, dq_raw, mode="tn", out_dtypes=[BF16], name="mm_gw_uq")
    dqn = _matmul(dq_raw, w_uq_al, mode="nt", out_dtypes=[F32], name="mm_dqn")
    gs_ukv = _matmul(kvn, dkv_raw, mode="tn", out_dtypes=[BF16], name="mm_gw_ukv", out_shards=True)
    dkvn = _matmul(dkv_raw, g_ukv, mode="nt", out_dtypes=[F32], name="mm_dkvn", b_shards=True)
    dproj, gg_qlat, gg_kvlat = _lat_norm_bwd(
        proj, dqn, dkvn, dkpe, rstd_q, rstd_kv, g_q_lat, g_kv_lat, dproj, D, GW, QL, KVL, "latnorm_bwd")
    gs_late = [_col_shards(_unpad_heads(gw_uq_al, H)), gs_ukv]
    (sib_late,) = _run_comm([_SiblingJob(gs_late)], "rs_sibling_late")
    cs_mid = [_chip_sum(g, r, c_idx, "chip_sum_" + nm) for g, r, nm in zip(gs_late, sib_late, ["w_uq", "w_ukv"])] + cs_early
    gw_al_t, (parts_mid,) = _matmul(dproj, h1f, mode="tn", out_dtypes=[BF16], name="mm_gw_in", comm=_Comm([_ChipsJob(cs_mid)]))
    gs_in = jnp.stack([_al_rows(gw_al_t, j * n_in, (j + 1) * n_in, OFF_GATE, 2 * D) for j in range(N_DEV)])
    ((sib_in,),) = _run_comm([_SiblingJob([gs_in])], "rs_sibling_in")
    cs_in = _chip_sum(gs_in, sib_in, c_idx, "chip_sum_w_in")
    dh1, ((parts_in,),) = _matmul(dproj, w_al_t, mode="nn", out_dtypes=[F32], name="mm_dh1", comm=_Comm([_ChipsJob([cs_in])]))
    grad_x, d_sh1, d_sc1, gg_norm1 = _norm_mod_bwd(dh1.reshape(B, S, D), x, rstd1, sc1, g_norm1, dx1, "norm1_bwd")

    dmod_mine = jnp.concatenate([d_sh1, d_sc1, d_ga1, d_sh2, d_sc2, d_ga2], axis=2).reshape(B, 6 * D)
    dmod_all = _all_gather_small(dmod_mine, "ag_dmod").reshape(N_DEV * B, 6 * D)
    dmod_cols = lax.dynamic_slice(dmod_all, (0, dev * n_ada), (N_DEV * B, n_ada))
    ada_out = _ada_bwd_adam(c_all, dmod_cols, w_ada[0], m_w_ada[0], v_w_ada[0], "ada_bwd_adam")
    nb_rows = 8
    bada_out = _adam_from_parts(b_ada.reshape(nb_rows, -1), m_b_ada.reshape(nb_rows, -1), v_b_ada.reshape(nb_rows, -1),
                                dmod_all.reshape(N_DEV * B, nb_rows, -1), "adam_b_ada")

    names = ["w_in", "w_uq", "w_ukv", "w_branch_a", "w_branch_b", "w_out", "w_ff1", "w_ff2"]
    parts = [parts_in] + list(parts_mid) + [parts_ff1, parts_ff2]
    ms = [m_w_in, m_w_uq, m_w_ukv, m_w_branch_a, m_w_branch_b, m_w_out, m_w_ff1, m_w_ff2]
    vs = [v_w_in, v_w_uq, v_w_ukv, v_w_branch_a, v_w_branch_b, v_w_out, v_w_ff1, v_w_ff2]
    big_out = {}
    for nm, w, m, v, p in zip(names, big, ms, vs, parts):
        if nm == "w_in":
            res = _adam_from_parts(jnp.transpose(w[0]), jnp.transpose(m[0]), jnp.transpose(v[0]), p, "adam_" + nm)
            big_out[nm] = [jnp.transpose(r) for r in res]
        else:
            big_out[nm] = _adam_from_parts(w[0], m[0], v[0], p, "adam_" + nm)

    small = [("g_norm1", g_norm1, m_g_norm1, v_g_norm1, gg_norm1),
             ("g_v", g_v, m_g_v, v_g_v, gg_gv),
             ("w_s", w_s, m_w_s, v_w_s, gg_ws),
             ("b_s", b_s, m_b_s, v_b_s, jnp.transpose(gg_bs_t[:, :GROUPS])),
             ("g_q_lat", g_q_lat, m_g_q_lat, v_g_q_lat, gg_qlat),
             ("g_kv_lat", g_kv_lat, m_g_kv_lat, v_g_kv_lat, gg_kvlat),
             ("g_qn", g_qn, m_g_qn, v_g_qn, gg_qn[:, :QK_HEAD]),
             ("g_kn", g_kn, m_g_kn, v_g_kn, gg_kn[:, :QK_HEAD]),
             ("g_norm2", g_norm2, m_g_norm2, v_g_norm2, gg_norm2)]
    sizes = [w.size for _, w, _, _, _ in small]
    n_small = sum(sizes)
    n_small_pad = _round_up(n_small, 8 * LANES)

    def flat_cat(arrs):
        return jnp.pad(jnp.concatenate([a.reshape(-1) for a in arrs]), (0, n_small_pad - n_small))

    part_small = _all_gather_small(flat_cat([t[4] for t in small]), "ag_small_grads")
    small_out = _adam_from_parts(
        flat_cat([t[1] for t in small]).reshape(8, -1), flat_cat([t[2] for t in small]).reshape(8, -1),
        flat_cat([t[3] for t in small]).reshape(8, -1), part_small.reshape(N_DEV, 8, -1), "adam_small")
    offs = [sum(sizes[:i]) for i in range(len(sizes))]

    def small_piece(kind, i):
        return small_out[kind].reshape(-1)[offs[i]:offs[i] + sizes[i]].reshape(small[i][1].shape)

    small_idx = {t[0]: i for i, t in enumerate(small)}
    order = ["w_ada", "b_ada", "g_norm1", "w_in", "g_v", "w_s", "b_s", "g_q_lat", "g_kv_lat", "w_uq", "w_ukv", "g_qn", "g_kn",
             "w_branch_a", "w_branch_b", "w_out", "g_norm2", "w_ff1", "w_ff2"]

    def result(kind, nm):
        if nm == "w_ada":
            return ada_out[kind][None]
        if nm == "b_ada":
            return bada_out[kind].reshape(b_ada.shape)
        if nm in small_idx:
            return small_piece(kind, small_idx[nm])
        return big_out[nm][kind][None]

    outs = [loss, grad_x]
    for kind in range(4):
        outs += [result(kind, nm) for nm in order]
    return tuple(outs)
```

```python
import functools
import math

import jax
import jax.numpy as jnp
from jax import lax
from jax.experimental import pallas as pl
from jax.experimental.pallas import tpu as pltpu

F32 = jnp.float32
BF16 = jnp.bfloat16
SDS = jax.ShapeDtypeStruct
MESH = pl.DeviceIdType.MESH

N_DEV = 8
N_CHIP = 4
CHUNK = 128
GROUPS = 8
NOPE = 128
ROPE = 64
HALF_ROPE = ROPE // 2
QK_HEAD = NOPE + ROPE
V_HEAD = 128
HEAD_PAD = 256
LANES = 128
ROPE_THETA = 10000.0
EPS = 1e-6
INV_SQRT2 = 1.0 / math.sqrt(2.0)
INV_SQRT_2PI = 1.0 / math.sqrt(2.0 * math.pi)

ADAM_LR = 0.001
ADAM_B1 = 0.9
ADAM_B2 = 0.999
ADAM_EPS = 1e-08
ADAM_WD = 0.01
ADAM_STEP = 10

V7X_VMEM_LIMIT_BYTES = 56 * 1024 * 1024
MM_TILE = 1024
MM_TILE_K = 4096
ATTN_BLOCK = 512
ATTN_BWD_BLOCK = 1024
ROW_BLOCK = 128

NN = (((1,), (0,)), ((), ()))
NT = (((1,), (1,)), ((), ()))
TN = (((0,), (0,)), ((), ()))


def _tile(n, pref, mult):
    t = min(pref, n)
    t -= t % mult
    while t >= mult:
        if n % t == 0:
            return t
        t -= mult
    return n


def _round_up(n, m):
    return (n + m - 1) // m * m


BF16_SUBLANES = 16


def _block_2d(R, C, elems):
    if R % BF16_SUBLANES == 0:
        return _tile(R, max(BF16_SUBLANES, elems // C // BF16_SUBLANES * BF16_SUBLANES), BF16_SUBLANES), C
    return R, _tile(C, max(LANES, elems // R // LANES * LANES), LANES)


def _cp(*sem):
    return pltpu.CompilerParams(dimension_semantics=sem, vmem_limit_bytes=V7X_VMEM_LIMIT_BYTES)


def _dot(a, b, dims):
    return lax.dot_general(a, b, dims, preferred_element_type=F32)


def _gelu(x):
    return 0.5 * x * (1.0 + lax.erf(x * INV_SQRT2))


def _gelu_grad(x):
    return 0.5 * (1.0 + lax.erf(x * INV_SQRT2)) + x * jnp.exp(-0.5 * x * x) * INV_SQRT_2PI


def _sigmoid(x):
    return 1.0 / (1.0 + jnp.exp(-x))


def _matmul(a, b, *, mode, out_dtypes, name, epi=None, extras=(), comm=None, b_shards=False, out_shards=False):
    if mode == "tn":
        K, M = a.shape
    else:
        M, K = a.shape
    n_shard = None
    if b_shards:
        _, R, n_shard = b.shape
        N, Kb = (R, N_DEV * n_shard) if mode == "nt" else (N_DEV * n_shard, R)
    elif mode == "nt":
        N, Kb = b.shape
    else:
        Kb, N = b.shape
    assert K == Kb, (name, a.shape, b.shape)
    tm = _tile(M, MM_TILE, 128)
    tn = _tile(n_shard if (b_shards and mode != "nt") else N, MM_TILE, 128)
    tk = _tile(n_shard if (b_shards and mode == "nt") else K, MM_TILE_K, 128)
    if out_shards:
        assert mode == "tn" and not extras and len(out_dtypes) == 1
        n_shard = N // N_DEV
        tn = _tile(n_shard, MM_TILE, 128)
    nk = K // tk
    n_extra = len(extras)
    n_out = len(out_dtypes)
    dims = {"nn": NN, "nt": NT, "tn": TN}[mode]

    def body(a_ref, b_ref, *rest):
        extra_refs = rest[:n_extra]
        out_refs = rest[n_extra:n_extra + n_out]
        acc_ref = rest[n_extra + n_out] if nk > 1 else None
        k = pl.program_id(2)

        def product():
            return _dot(a_ref[...].astype(BF16), b_ref[...].astype(BF16), dims)

        def finish(acc):
            res = (acc,) if epi is None else epi(acc, *[e[...] for e in extra_refs])
            for o_ref, r in zip(out_refs, res):
                o_ref[...] = r.astype(o_ref.dtype)

        if nk == 1:
            finish(product())
            return

        @pl.when(k == 0)
        def _():
            acc_ref[...] = product()

        if nk > 2:
            @pl.when((k > 0) & (k < nk - 1))
            def _():
                acc_ref[...] += product()

        @pl.when(k == nk - 1)
        def _():
            finish(acc_ref[...] + product())

    if mode == "tn":
        a_spec = pl.BlockSpec((tk, tm), lambda i, j, k: (k, i))
    else:
        a_spec = pl.BlockSpec((tm, tk), lambda i, j, k: (i, k))
    if b_shards and mode == "nt":
        per = n_shard // tk
        b_spec = pl.BlockSpec((None, tn, tk), lambda i, j, k: (k // per, j, k % per))
    elif b_shards:
        per = n_shard // tn
        b_spec = pl.BlockSpec((None, tk, tn), lambda i, j, k: (j // per, k, j % per))
    elif mode == "nt":
        b_spec = pl.BlockSpec((tn, tk), lambda i, j, k: (j, k))
    else:
        b_spec = pl.BlockSpec((tk, tn), lambda i, j, k: (k, j))
    mn_spec = pl.BlockSpec((tm, tn), lambda i, j, k: (i, j))
    grid = (M // tm, N // tn, nk)
    in_specs = [a_spec, b_spec] + [mn_spec] * n_extra
    if out_shards:
        per_out = n_shard // tn
        out_specs = [pl.BlockSpec((None, tm, tn), lambda i, j, k: (j // per_out, i, j % per_out))]
        out_shape = [SDS((N_DEV, M, n_shard), out_dtypes[0])]
    else:
        out_specs = [mn_spec] * n_out
        out_shape = [SDS((M, N), dt) for dt in out_dtypes]
    scratch = [pltpu.VMEM((tm, tn), F32)] if nk > 1 else []
    if comm is None:
        outs = pl.pallas_call(
            body, name=name, grid=grid, in_specs=in_specs, out_specs=out_specs, out_shape=out_shape,
            scratch_shapes=scratch, compiler_params=_cp("parallel", "parallel", "arbitrary"),
        )(a, b, *extras)
        return outs[0] if n_out == 1 else outs

    def first_last():
        i, j, k = pl.program_id(0), pl.program_id(1), pl.program_id(2)
        return ((i == 0) & (j == 0) & (k == 0),
                (i == grid[0] - 1) & (j == grid[1] - 1) & (k == nk - 1))

    outs = pl.pallas_call(
        _carry(body, comm, 2 + n_extra, n_out, len(scratch), first_last), name=name, grid=grid,
        in_specs=in_specs + [HBM_SPEC] * len(comm.ins), out_specs=out_specs + [HBM_SPEC] * len(comm.out_shape),
        out_shape=out_shape + comm.out_shape, scratch_shapes=scratch + comm.scratch,
        compiler_params=_cp("arbitrary", "arbitrary", "arbitrary"),
    )(a, b, *extras, *comm.ins)
    res = outs[0] if n_out == 1 else list(outs[:n_out])
    return res, comm.results(outs[n_out:])


def _here():
    return lax.axis_index("x"), lax.axis_index("y"), lax.axis_index("c")


def _other_chips(x, y):
    return [(1 - x, y), (x, 1 - y), (1 - x, 1 - y)]


def _all_gather_small(v, name):
    shape = v.shape
    n = v.size
    n_pad = _round_up(n, 8 * LANES)
    flat = jnp.pad(v.reshape(-1), (0, n_pad - n)).reshape(8, n_pad // 8)
    m_per, cols = flat.shape

    def body(x_ref, out_ref, send_sems, recv_sems, local_sem):
        x, y, c = _here()
        me, sibling = (x, y, c), (x, y, 1 - c)
        chips = _other_chips(x, y)

        def rows(px, py, pc):
            return out_ref.at[pl.ds((4 * px + 2 * py + pc) * m_per, m_per), :]

        def copy(k, block, to, src=None):
            return pltpu.make_async_remote_copy(
                src_ref=rows(*block) if src is None else src, dst_ref=rows(*block),
                send_sem=send_sems.at[k], recv_sem=recv_sems.at[k], device_id=to, device_id_type=MESH)

        mine = pltpu.make_async_copy(x_ref, rows(*me), local_sem)
        mine.start()
        first = [copy(0, me, sibling, src=x_ref)]
        first += [copy(1 + j, me, (*chip, c), src=x_ref) for j, chip in enumerate(chips)]
        for cp in first:
            cp.start()
        passed = [copy(4 + j, (*chip, c), sibling) for j, chip in enumerate(chips)]
        for j, chip in enumerate(chips):
            copy(1 + j, (*chip, c), me).wait_recv()
            passed[j].start()
        copy(0, sibling, me).wait_recv()
        for j, chip in enumerate(chips):
            copy(4 + j, (*chip, 1 - c), me).wait_recv()
        for cp in first + passed:
            cp.wait_send()
        mine.wait()

    out = pl.pallas_call(
        body, name=name,
        out_shape=SDS((N_DEV * m_per, cols), F32),
        in_specs=[pl.BlockSpec(memory_space=pltpu.VMEM)],
        out_specs=pl.BlockSpec(memory_space=pltpu.VMEM),
        scratch_shapes=[pltpu.SemaphoreType.DMA((7,)), pltpu.SemaphoreType.DMA((7,)), pltpu.SemaphoreType.DMA],
        compiler_params=pltpu.CompilerParams(vmem_limit_bytes=V7X_VMEM_LIMIT_BYTES),
    )(flat)
    return out.reshape(N_DEV, n_pad)[:, :n].reshape((N_DEV,) + shape)


HBM_SPEC = pl.BlockSpec(memory_space=pltpu.HBM)


class _GatherJob:
    def __init__(self, shards):
        self.ins = list(shards)
        self.nw = len(shards)
        self.out_shape = [SDS((N_DEV,) + s.shape, s.dtype) for s in shards]
        self.scratch = [pltpu.SemaphoreType.DMA((7 * self.nw,)), pltpu.SemaphoreType.DMA((7 * self.nw,)),
                        pltpu.SemaphoreType.DMA((self.nw,))]

    def _parts(self, xs, outs, sems):
        send_sems, recv_sems, local_sems = sems
        x, y, c = _here()

        def blk(w, px, py, pc):
            return outs[w].at[4 * px + 2 * py + pc]

        def copy(w, k, block, to, src=None):
            return pltpu.make_async_remote_copy(
                src_ref=blk(w, *block) if src is None else src, dst_ref=blk(w, *block),
                send_sem=send_sems.at[7 * w + k], recv_sem=recv_sems.at[7 * w + k], device_id=to, device_id_type=MESH)

        me, sibling = (x, y, c), (x, y, 1 - c)
        chips = _other_chips(x, y)
        mine = [pltpu.make_async_copy(xs[w], blk(w, *me), local_sems.at[w]) for w in range(self.nw)]
        first = []
        for w in range(self.nw):
            first.append(copy(w, 0, me, sibling, src=xs[w]))
            first += [copy(w, 1 + j, me, (*chip, c), src=xs[w]) for j, chip in enumerate(chips)]
        return copy, me, sibling, chips, c, mine, first

    def start(self, xs, outs, sems):
        _, _, _, _, _, mine, first = self._parts(xs, outs, sems)
        for cp in mine + first:
            cp.start()

    def finish(self, xs, outs, sems):
        copy, me, sibling, chips, c, mine, first = self._parts(xs, outs, sems)
        passed = []
        for w in range(self.nw):
            for j, chip in enumerate(chips):
                copy(w, 1 + j, (*chip, c), me).wait_recv()
                fwd = copy(w, 4 + j, (*chip, c), sibling)
                fwd.start()
                passed.append(fwd)
        for w in range(self.nw):
            copy(w, 0, sibling, me).wait_recv()
            for j, chip in enumerate(chips):
                copy(w, 4 + j, (*chip, 1 - c), me).wait_recv()
        for cp in first + passed:
            cp.wait_send()
        for cp in mine:
            cp.wait()


class _SiblingJob:
    def __init__(self, grads):
        self.ins = list(grads)
        self.nw = len(grads)
        self.out_shape = [SDS((N_CHIP,) + g.shape[1:], g.dtype) for g in grads]
        self.scratch = [pltpu.SemaphoreType.DMA((N_CHIP * self.nw,)), pltpu.SemaphoreType.DMA((N_CHIP * self.nw,))]

    def _copies(self, gs, outs, sems):
        send_sems, recv_sems = sems
        x, y, c = _here()
        return [pltpu.make_async_remote_copy(
            src_ref=gs[w].at[2 * k + (1 - c)], dst_ref=outs[w].at[k],
            send_sem=send_sems.at[N_CHIP * w + k], recv_sem=recv_sems.at[N_CHIP * w + k],
            device_id=(x, y, 1 - c), device_id_type=MESH) for w in range(self.nw) for k in range(N_CHIP)]

    def start(self, gs, outs, sems):
        for cp in self._copies(gs, outs, sems):
            cp.start()

    def finish(self, gs, outs, sems):
        for cp in self._copies(gs, outs, sems):
            cp.wait()


class _ChipsJob:
    def __init__(self, chip_sums):
        self.ins = list(chip_sums)
        self.nw = len(chip_sums)
        self.out_shape = [SDS(s.shape, s.dtype) for s in chip_sums]
        self.scratch = [pltpu.SemaphoreType.DMA((3 * self.nw,)), pltpu.SemaphoreType.DMA((3 * self.nw,)),
                        pltpu.SemaphoreType.DMA((self.nw,))]

    def _parts(self, srcs, outs, sems):
        send_sems, recv_sems, local_sems = sems
        x, y, c = _here()
        my_chip = 2 * x + y
        local = [pltpu.make_async_copy(srcs[w].at[my_chip], outs[w].at[my_chip], local_sems.at[w]) for w in range(self.nw)]
        sends, landed = [], []
        for w in range(self.nw):
            for j, (px, py) in enumerate(_other_chips(x, y)):
                sems_j = dict(send_sem=send_sems.at[3 * w + j], recv_sem=recv_sems.at[3 * w + j],
                              device_id=(px, py, c), device_id_type=MESH)
                sends.append(pltpu.make_async_remote_copy(
                    src_ref=srcs[w].at[2 * px + py], dst_ref=outs[w].at[my_chip], **sems_j))
                landed.append(pltpu.make_async_remote_copy(
                    src_ref=srcs[w].at[2 * px + py], dst_ref=outs[w].at[2 * px + py], **sems_j))
        return local, sends, landed

    def start(self, srcs, outs, sems):
        local, sends, _ = self._parts(srcs, outs, sems)
        for cp in local + sends:
            cp.start()

    def finish(self, srcs, outs, sems):
        local, sends, landed = self._parts(srcs, outs, sems)
        for cp in landed:
            cp.wait_recv()
        for cp in sends:
            cp.wait_send()
        for cp in local:
            cp.wait()


class _Comm:
    def __init__(self, jobs):
        self.jobs = list(jobs)
        self.ins = [a for j in self.jobs for a in j.ins]
        self.out_shape = [s for j in self.jobs for s in j.out_shape]
        self.scratch = [s for j in self.jobs for s in j.scratch]

    def _split(self, flat, counts):
        out, pos = [], 0
        for n in counts:
            out.append(flat[pos:pos + n])
            pos += n
        return out

    def _each(self, ins, outs, sems):
        return zip(self.jobs, self._split(ins, [len(j.ins) for j in self.jobs]),
                   self._split(outs, [len(j.out_shape) for j in self.jobs]),
                   self._split(sems, [len(j.scratch) for j in self.jobs]))

    def start(self, ins, outs, sems):
        for job, i, o, s in self._each(ins, outs, sems):
            job.start(i, o, s)

    def finish(self, ins, outs, sems):
        for job, i, o, s in self._each(ins, outs, sems):
            job.finish(i, o, s)

    def results(self, flat):
        return [list(r) for r in self._split(list(flat), [len(j.out_shape) for j in self.jobs])]


def _carry(body, comm, n_in, n_out, n_scratch, first_last):
    ci, co = len(comm.ins), len(comm.out_shape)

    def wrapped(*refs):
        ins, rest = refs[:n_in + ci], refs[n_in + ci:]
        outs, scr = rest[:n_out + co], rest[n_out + co:]
        c_ins, c_outs, c_sems = ins[n_in:], outs[n_out:], scr[n_scratch:]
        first, last = first_last()

        @pl.when(first)
        def _():
            comm.start(c_ins, c_outs, c_sems)

        body(*ins[:n_in], *outs[:n_out], *scr[:n_scratch])

        @pl.when(last)
        def _():
            comm.finish(c_ins, c_outs, c_sems)

    return wrapped


def _run_comm(jobs, name):
    comm = _Comm(jobs)

    def body(*refs):
        ci, co = len(comm.ins), len(comm.out_shape)
        comm.start(refs[:ci], refs[ci:ci + co], refs[ci + co:])
        comm.finish(refs[:ci], refs[ci:ci + co], refs[ci + co:])

    outs = pl.pallas_call(
        body, name=name, out_shape=comm.out_shape,
        in_specs=[HBM_SPEC] * len(comm.ins), out_specs=[HBM_SPEC] * len(comm.out_shape),
        scratch_shapes=comm.scratch,
    )(*comm.ins)
    return comm.results(outs)


def _chip_sum(g, recv, c_idx, name):
    _, m, n = g.shape
    tr, tc = _block_2d(m, n, 1 << 20)

    def body(c_ref, g_ref, r_ref, o_ref):
        o_ref[...] = (g_ref[...].astype(F32) + r_ref[...].astype(F32)).astype(o_ref.dtype)

    grid_spec = pltpu.PrefetchScalarGridSpec(
        num_scalar_prefetch=1, grid=(N_CHIP, m // tr, n // tc),
        in_specs=[pl.BlockSpec((None, tr, tc), lambda k, i, j, c_ref: (2 * k + c_ref[0], i, j)),
                  pl.BlockSpec((None, tr, tc), lambda k, i, j, c_ref: (k, i, j))],
        out_specs=pl.BlockSpec((None, tr, tc), lambda k, i, j, c_ref: (k, i, j)))
    return pl.pallas_call(
        body, name=name, grid_spec=grid_spec, out_shape=SDS((N_CHIP, m, n), BF16),
        compiler_params=_cp("parallel", "parallel", "parallel"),
    )(c_idx, g, recv)


def _adam_math(w, g, m, v):
    m = ADAM_B1 * m + (1.0 - ADAM_B1) * g
    v = ADAM_B2 * v + (1.0 - ADAM_B2) * (g * g)
    m_hat = m / (1.0 - ADAM_B1 ** ADAM_STEP)
    v_hat = v / (1.0 - ADAM_B2 ** ADAM_STEP)
    delta = -ADAM_LR * (m_hat / (jnp.sqrt(v_hat) + ADAM_EPS) + ADAM_WD * w)
    return delta, m, v


def _adam_from_parts(w, m, v, parts, name):
    R, C = w.shape
    P = parts.shape[0]
    tr, tc = _block_2d(R, C, 1 << 19)

    def body(w_ref, m_ref, v_ref, p_ref, g_out, d_out, m_out, v_out):
        g = p_ref[0].astype(F32)
        for k in range(1, P):
            g = g + p_ref[k].astype(F32)
        delta, nm, nv = _adam_math(w_ref[...], g, m_ref[...], v_ref[...])
        g_out[...] = g
        d_out[...] = delta
        m_out[...] = nm
        v_out[...] = nv

    spec = pl.BlockSpec((tr, tc), lambda i, j: (i, j))
    return pl.pallas_call(
        body, name=name, grid=(R // tr, C // tc),
        in_specs=[spec, spec, spec, pl.BlockSpec((P, tr, tc), lambda i, j: (0, i, j))],
        out_specs=[spec] * 4, out_shape=[SDS((R, C), F32)] * 4,
        compiler_params=_cp("parallel", "parallel"),
    )(w, m, v, parts)


def _ada_fwd(c_all, w_ada, b_cols, name):
    nb, D = c_all.shape
    n = w_ada.shape[1]
    tn = _tile(n, 512, 128)

    def body(c_ref, w_ref, b_ref, o_ref):
        cv = c_ref[...]
        cond = (cv * _sigmoid(cv)).astype(BF16)
        o_ref[...] = _dot(cond, w_ref[...].astype(BF16), NN) + b_ref[...]

    return pl.pallas_call(
        body, name=name, grid=(n // tn,),
        in_specs=[pl.BlockSpec((nb, D), lambda j: (0, 0)), pl.BlockSpec((D, tn), lambda j: (0, j)),
                  pl.BlockSpec((1, tn), lambda j: (0, j))],
        out_specs=pl.BlockSpec((nb, tn), lambda j: (0, j)), out_shape=SDS((nb, n), F32),
        compiler_params=_cp("parallel"),
    )(c_all, w_ada, b_cols)


def _ada_bwd_adam(c_all, dmod_cols, w, m, v, name):
    nb, D = c_all.shape
    n = w.shape[1]
    tr = _tile(D, 512, 128)
    tn = _tile(n, 1024, 128)

    def body(c_ref, d_ref, w_ref, m_ref, v_ref, g_out, d_out, m_out, v_out):
        cv = c_ref[...]
        cond = (cv * _sigmoid(cv)).astype(BF16)
        g = _dot(cond, d_ref[...].astype(BF16), TN)
        delta, nm, nv = _adam_math(w_ref[...], g, m_ref[...], v_ref[...])
        g_out[...] = g
        d_out[...] = delta
        m_out[...] = nm
        v_out[...] = nv

    spec = pl.BlockSpec((tr, tn), lambda i, j: (i, j))
    return pl.pallas_call(
        body, name=name, grid=(D // tr, n // tn),
        in_specs=[pl.BlockSpec((nb, tr), lambda i, j: (0, i)), pl.BlockSpec((nb, tn), lambda i, j: (0, j)),
                  spec, spec, spec],
        out_specs=[spec] * 4, out_shape=[SDS((D, n), F32)] * 4,
        compiler_params=_cp("parallel", "parallel"),
    )(c_all, dmod_cols, w, m, v)


def _tok(tb, width):
    return pl.BlockSpec((None, tb, width), lambda b, i: (b, i, 0))


def _per_example(width):
    return pl.BlockSpec((None, 1, width), lambda b, i: (b, 0, 0))


def _shared_row(width):
    return pl.BlockSpec((1, width), lambda b, i: (0, 0))


def _norm_mod_fwd(x, g, sh, sc, name, o=None, ga=None):
    B, S, D = x.shape
    tb = _tile(S, ROW_BLOCK, 8)
    fused = o is not None

    def body(*refs):
        if fused:
            x_ref, o_ref, ga_ref, g_ref, sh_ref, sc_ref, x1_ref, h_ref, r_ref = refs
            xv = x_ref[...] + ga_ref[...] * o_ref[...]
            x1_ref[...] = xv
        else:
            x_ref, g_ref, sh_ref, sc_ref, h_ref, r_ref = refs
            xv = x_ref[...]
        rstd = lax.rsqrt(jnp.mean(xv * xv, axis=1, keepdims=True) + EPS)
        y = xv * rstd * g_ref[...]
        h_ref[...] = (y * (1.0 + sc_ref[...]) + sh_ref[...]).astype(BF16)
        r_ref[...] = rstd

    ins = [x] + ([o, ga] if fused else []) + [g, sh, sc]
    in_specs = [_tok(tb, D)] + ([_tok(tb, D), _per_example(D)] if fused else []) + [_shared_row(D), _per_example(D), _per_example(D)]
    out_specs = ([_tok(tb, D)] if fused else []) + [_tok(tb, D), _tok(tb, 1)]
    out_shape = ([SDS((B, S, D), F32)] if fused else []) + [SDS((B, S, D), BF16), SDS((B, S, 1), F32)]
    return pl.pallas_call(
        body, name=name, grid=(B, S // tb), in_specs=in_specs, out_specs=out_specs, out_shape=out_shape,
        compiler_params=_cp("parallel", "parallel"),
    )(*ins)


def _norm_mod_bwd(dh, xin, rstd, sc, g, dres, name, o=None, ga=None):
    B, S, D = xin.shape
    tb = _tile(S, ROW_BLOCK, 8)
    gated = o is not None

    def body(*refs):
        if gated:
            (dh_ref, x_ref, r_ref, sc_ref, g_ref, dres_ref, o_ref, ga_ref,
             dx_ref, dsh_ref, dsc_ref, gg_ref, dga_ref, do_ref) = refs
        else:
            dh_ref, x_ref, r_ref, sc_ref, g_ref, dres_ref, dx_ref, dsh_ref, dsc_ref, gg_ref = refs
        b, i = pl.program_id(0), pl.program_id(1)

        @pl.when(i == 0)
        def _():
            dsh_ref[...] = jnp.zeros_like(dsh_ref)
            dsc_ref[...] = jnp.zeros_like(dsc_ref)
            if gated:
                dga_ref[...] = jnp.zeros_like(dga_ref)

        @pl.when((i == 0) & (b == 0))
        def _():
            gg_ref[...] = jnp.zeros_like(gg_ref)

        dhv = dh_ref[...]
        rs = r_ref[...]
        gv = g_ref[...]
        xhat = x_ref[...] * rs
        dsh_ref[...] += jnp.sum(dhv, axis=0, keepdims=True)
        dsc_ref[...] += jnp.sum(dhv * (xhat * gv), axis=0, keepdims=True)
        dn = dhv * (1.0 + sc_ref[...])
        gg_ref[...] += jnp.sum(dn * xhat, axis=0, keepdims=True)
        dxhat = dn * gv
        cm = jnp.mean(dxhat * xhat, axis=1, keepdims=True)
        dx = dres_ref[...] + rs * (dxhat - xhat * cm)
        dx_ref[...] = dx
        if gated:
            dga_ref[...] += jnp.sum(dx * o_ref[...], axis=0, keepdims=True)
            do_ref[...] = (dx * ga_ref[...]).astype(BF16)

    ins = [dh, xin, rstd, sc, g, dres] + ([o, ga] if gated else [])
    in_specs = [_tok(tb, D), _tok(tb, D), _tok(tb, 1), _per_example(D), _shared_row(D), _tok(tb, D)]
    in_specs += [_tok(tb, D), _per_example(D)] if gated else []
    out_specs = [_tok(tb, D), _per_example(D), _per_example(D), _shared_row(D)]
    out_shape = [SDS((B, S, D), F32), SDS((B, 1, D), F32), SDS((B, 1, D), F32), SDS((1, D), F32)]
    if gated:
        out_specs += [_per_example(D), _tok(tb, D)]
        out_shape += [SDS((B, 1, D), F32), SDS((B, S, D), BF16)]
    return pl.pallas_call(
        body, name=name, grid=(B, S // tb), in_specs=in_specs, out_specs=out_specs, out_shape=out_shape,
        compiler_params=_cp("arbitrary", "arbitrary"),
    )(*ins)


def _loss_head(x1, ff, target, ga2, name):
    B, S, D = x1.shape
    tb = _tile(S, ROW_BLOCK, 8)
    nb, ni = B, S // tb

    def body(x_ref, f_ref, t_ref, ga_ref, loss_ref, dy_ref, dff_ref, dga_ref, acc_ref):
        b, i = pl.program_id(0), pl.program_id(1)

        @pl.when(i == 0)
        def _():
            dga_ref[...] = jnp.zeros_like(dga_ref)

        @pl.when((i == 0) & (b == 0))
        def _():
            acc_ref[...] = jnp.zeros_like(acc_ref)

        fv = f_ref[...]
        gav = ga_ref[...]
        err = x_ref[...] + gav * fv - t_ref[...]
        acc_ref[...] += jnp.sum(err * err, axis=0, keepdims=True)
        dy = err * (1.0 / D)
        dy_ref[...] = dy
        dff_ref[...] = (dy * gav).astype(BF16)
        dga_ref[...] += jnp.sum(dy * fv, axis=0, keepdims=True)

        @pl.when((i == ni - 1) & (b == nb - 1))
        def _():
            loss_ref[...] = jnp.sum(acc_ref[...], axis=1, keepdims=True) * (0.5 / D)

    return pl.pallas_call(
        body, name=name, grid=(B, S // tb),
        in_specs=[_tok(tb, D), _tok(tb, D), _tok(tb, D), _per_example(D)],
        out_specs=[pl.BlockSpec((1, 1), lambda b, i: (0, 0)), _tok(tb, D), _tok(tb, D), _per_example(D)],
        out_shape=[SDS((1, 1), F32), SDS((B, S, D), F32), SDS((B, S, D), BF16), SDS((B, 1, D), F32)],
        scratch_shapes=[pltpu.VMEM((1, D), F32)],
        compiler_params=_cp("arbitrary", "arbitrary"),
    )(x1, ff, target, ga2)


def _causal_mask():
    t = lax.broadcasted_iota(jnp.int32, (CHUNK, CHUNK), 0)
    s = lax.broadcasted_iota(jnp.int32, (CHUNK, CHUNK), 1)
    return s <= t


def _gmlp_fwd(proj, g_v, w_s, b_col, D, GW, name):
    T = proj.shape[0]
    tb = _tile(T, ROW_BLOCK, CHUNK)
    gw = GW // GROUPS
    ucol = (2 * D) // GW

    def body(u_ref, v_ref, g_ref, w_ref, b_ref, ga_ref, r_ref):
        zu = _gelu(u_ref[...])
        zv = _gelu(v_ref[...])
        rstd = lax.rsqrt(jnp.mean(zv * zv, axis=1, keepdims=True) + EPS)
        vn = (zv * rstd * g_ref[...]).astype(BF16)
        mask = _causal_mask()
        for g in range(GROUPS):
            wm = jnp.where(mask, w_ref[g], 0.0).astype(BF16)
            cols = slice(g * gw, (g + 1) * gw)
            for ci in range(tb // CHUNK):
                rows = slice(ci * CHUNK, (ci + 1) * CHUNK)
                mixed = _dot(wm, vn[rows, cols], NN) + b_ref[g]
                ga_ref[rows, cols] = (zu[rows, cols] * mixed).astype(BF16)
        r_ref[...] = rstd

    return pl.pallas_call(
        body, name=name, grid=(T // tb,),
        in_specs=[pl.BlockSpec((tb, GW), lambda i: (i, ucol)), pl.BlockSpec((tb, GW), lambda i: (i, ucol + 1)),
                  pl.BlockSpec((1, GW), lambda i: (0, 0)), pl.BlockSpec((GROUPS, CHUNK, CHUNK), lambda i: (0, 0, 0)),
                  pl.BlockSpec((GROUPS, CHUNK, 1), lambda i: (0, 0, 0))],
        out_specs=[pl.BlockSpec((tb, GW), lambda i: (i, 0)), pl.BlockSpec((tb, 1), lambda i: (i, 0))],
        out_shape=[SDS((T, GW), BF16), SDS((T, 1), F32)],
        compiler_params=_cp("parallel"),
    )(proj, proj, g_v, w_s, b_col)


def _gmlp_bwd(proj, dga, rstd_v, g_v, w_s, b_col, sel, dproj, D, GW, name):
    T = proj.shape[0]
    tb = _tile(T, ROW_BLOCK, CHUNK)
    gw = GW // GROUPS
    ucol = (2 * D) // GW
    assert (2 * D) % (2 * GW) == 0
    uvcol = (2 * D) // (2 * GW)
    nsteps = T // tb

    def body(u_ref, v_ref, dga_ref, r_ref, g_ref, w_ref, b_ref, sel_ref, _dproj_in,
             duv_ref, gws_ref, gbs_ref, gg_ref, accb_ref, dvn_ref):
        step = pl.program_id(0)

        @pl.when(step == 0)
        def _():
            gws_ref[...] = jnp.zeros_like(gws_ref)
            gg_ref[...] = jnp.zeros_like(gg_ref)
            accb_ref[...] = jnp.zeros_like(accb_ref)

        uv = u_ref[...]
        vv = v_ref[...]
        zu = _gelu(uv)
        zv = _gelu(vv)
        rs = r_ref[...]
        gv = g_ref[...]
        vhat = zv * rs
        vnb = (vhat * gv).astype(BF16)
        dgav = dga_ref[...].astype(F32)
        du_gelu = _gelu_grad(uv)
        mask = _causal_mask()
        for g in range(GROUPS):
            wm = jnp.where(mask, w_ref[g], 0.0)
            wmb = wm.astype(BF16)
            wmtb = wm.T.astype(BF16)
            cols = slice(g * gw, (g + 1) * gw)
            for ci in range(tb // CHUNK):
                rows = slice(ci * CHUNK, (ci + 1) * CHUNK)
                vn_blk = vnb[rows, cols]
                mixed = _dot(wmb, vn_blk, NN) + b_ref[g]
                duv_ref[rows, cols] = (dgav[rows, cols] * mixed * du_gelu[rows, cols]).astype(BF16)
                dmix = dgav[rows, cols] * zu[rows, cols]
                accb_ref[:, cols] += dmix
                dmb = dmix.astype(BF16)
                gws_ref[g] += _dot(dmb, vn_blk, NT)
                dvn_ref[rows, cols] = _dot(wmtb, dmb, NN)
        dvn = dvn_ref[...]
        gg_ref[...] += jnp.sum(dvn * vhat, axis=0, keepdims=True)
        dvhat = dvn * gv
        cm = jnp.mean(dvhat * vhat, axis=1, keepdims=True)
        dzv = rs * (dvhat - vhat * cm)
        duv_ref[:, GW:] = (dzv * _gelu_grad(vv)).astype(BF16)

        @pl.when(step == nsteps - 1)
        def _():
            gbs_ref[...] = jnp.dot(accb_ref[...], sel_ref[...], precision=lax.Precision.HIGHEST,
                                   preferred_element_type=F32)
            for g in range(GROUPS):
                gws_ref[g] = jnp.where(mask, gws_ref[g], 0.0)

    return pl.pallas_call(
        body, name=name, grid=(nsteps,),
        in_specs=[pl.BlockSpec((tb, GW), lambda i: (i, ucol)), pl.BlockSpec((tb, GW), lambda i: (i, ucol + 1)),
                  pl.BlockSpec((tb, GW), lambda i: (i, 0)), pl.BlockSpec((tb, 1), lambda i: (i, 0)),
                  pl.BlockSpec((1, GW), lambda i: (0, 0)), pl.BlockSpec((GROUPS, CHUNK, CHUNK), lambda i: (0, 0, 0)),
                  pl.BlockSpec((GROUPS, CHUNK, 1), lambda i: (0, 0, 0)), pl.BlockSpec((GW, LANES), lambda i: (0, 0)),
                  pl.BlockSpec(memory_space=pl.ANY)],
        out_specs=[pl.BlockSpec((tb, 2 * GW), lambda i: (i, uvcol)), pl.BlockSpec((GROUPS, CHUNK, CHUNK), lambda i: (0, 0, 0)),
                   pl.BlockSpec((CHUNK, LANES), lambda i: (0, 0)), pl.BlockSpec((1, GW), lambda i: (0, 0))],
        out_shape=[SDS(dproj.shape, BF16), SDS((GROUPS, CHUNK, CHUNK), F32), SDS((CHUNK, LANES), F32), SDS((1, GW), F32)],
        scratch_shapes=[pltpu.VMEM((CHUNK, GW), F32), pltpu.VMEM((tb, GW), F32)],
        input_output_aliases={8: 0},
        compiler_params=_cp("arbitrary"),
    )(proj, proj, dga, rstd_v, g_v, w_s, b_col, sel, dproj)


def _mix_fwd(proj, y_a, y_b, D, name):
    T = proj.shape[0]
    tb = _tile(T, 512, 8)
    td = _tile(D, 1024, 128)
    nd = D // td

    def body(ga_ref, gb_ref, ya_ref, yb_ref, o_ref):
        o_ref[...] = (_sigmoid(ga_ref[...]) * ya_ref[...] + _sigmoid(gb_ref[...]) * yb_ref[...]).astype(BF16)

    blk = pl.BlockSpec((tb, td), lambda i, j: (i, j))
    return pl.pallas_call(
        body, name=name, grid=(T // tb, nd),
        in_specs=[blk, pl.BlockSpec((tb, td), lambda i, j: (i, j + nd)), blk, blk],
        out_specs=blk, out_shape=SDS((T, D), BF16),
        compiler_params=_cp("parallel", "parallel"),
    )(proj, proj, y_a, y_b)


def _mix_bwd(proj, y_a, y_b, dmixed, D, name):
    T, width = proj.shape
    tb = _tile(T, ROW_BLOCK, 8)

    def body(g_ref, ya_ref, yb_ref, dm_ref, dya_ref, dyb_ref, dp_ref):
        dm = dm_ref[...].astype(F32)
        sa = _sigmoid(g_ref[:, :D])
        sb = _sigmoid(g_ref[:, D:])
        dya_ref[...] = (dm * sa).astype(BF16)
        dyb_ref[...] = (dm * sb).astype(BF16)
        dp_ref[:, :D] = (dm * ya_ref[...] * sa * (1.0 - sa)).astype(BF16)
        dp_ref[:, D:] = (dm * yb_ref[...] * sb * (1.0 - sb)).astype(BF16)

    blk = pl.BlockSpec((tb, D), lambda i: (i, 0))
    gates = pl.BlockSpec((tb, 2 * D), lambda i: (i, 0))
    return pl.pallas_call(
        body, name=name, grid=(T // tb,),
        in_specs=[gates, blk, blk, blk],
        out_specs=[blk, blk, gates], out_shape=[SDS((T, D), BF16), SDS((T, D), BF16), SDS((T, width), BF16)],
        compiler_params=_cp("parallel"),
    )(proj, y_a, y_b, dmixed)


def _lat_norm_fwd(proj, g_q, g_kv, D, GW, QL, KVL, name):
    T = proj.shape[0]
    tb = _tile(T, 512, 8)
    qcol = (2 * D + 2 * GW) // QL
    kvcol = (2 * D + 2 * GW + QL) // KVL

    def body(q_ref, kv_ref, gq_ref, gkv_ref, qn_ref, kvn_ref, rq_ref, rkv_ref):
        qv = q_ref[...]
        rq = lax.rsqrt(jnp.mean(qv * qv, axis=1, keepdims=True) + EPS)
        qn_ref[...] = (qv * rq * gq_ref[...]).astype(BF16)
        rq_ref[...] = rq
        kv = kv_ref[...]
        rkv = lax.rsqrt(jnp.mean(kv * kv, axis=1, keepdims=True) + EPS)
        kvn_ref[...] = (kv * rkv * gkv_ref[...]).astype(BF16)
        rkv_ref[...] = rkv

    return pl.pallas_call(
        body, name=name, grid=(T // tb,),
        in_specs=[pl.BlockSpec((tb, QL), lambda i: (i, qcol)), pl.BlockSpec((tb, KVL), lambda i: (i, kvcol)),
                  pl.BlockSpec((1, QL), lambda i: (0, 0)), pl.BlockSpec((1, KVL), lambda i: (0, 0))],
        out_specs=[pl.BlockSpec((tb, QL), lambda i: (i, 0)), pl.BlockSpec((tb, KVL), lambda i: (i, 0)),
                   pl.BlockSpec((tb, 1), lambda i: (i, 0)), pl.BlockSpec((tb, 1), lambda i: (i, 0))],
        out_shape=[SDS((T, QL), BF16), SDS((T, KVL), BF16), SDS((T, 1), F32), SDS((T, 1), F32)],
        compiler_params=_cp("parallel"),
    )(proj, proj, g_q, g_kv)


def _lat_norm_bwd(proj, dqn, dkvn, dkpe, rq, rkv, g_q, g_kv, dproj, D, GW, QL, KVL, name):
    T, width = proj.shape
    tb = _tile(T, 512, 8)
    qcol = (2 * D + 2 * GW) // QL
    kvcol = (2 * D + 2 * GW + QL) // KVL
    tail = width - (2 * D + 2 * GW)
    assert (2 * D + 2 * GW) % tail == 0 and tail >= QL + KVL + LANES
    tailcol = (2 * D + 2 * GW) // tail

    def one(xv, rs, gv, dy, gg_ref):
        xhat = xv * rs
        gg_ref[...] += jnp.sum(dy * xhat, axis=0, keepdims=True)
        dxhat = dy * gv
        cm = jnp.mean(dxhat * xhat, axis=1, keepdims=True)
        return rs * (dxhat - xhat * cm)

    def body(q_ref, kv_ref, dqn_ref, dkvn_ref, dkpe_ref, rq_ref, rkv_ref, gq_ref, gkv_ref, _dproj_in,
             tail_ref, ggq_ref, ggkv_ref):
        @pl.when(pl.program_id(0) == 0)
        def _():
            ggq_ref[...] = jnp.zeros_like(ggq_ref)
            ggkv_ref[...] = jnp.zeros_like(ggkv_ref)

        tail_ref[:, :QL] = one(q_ref[...], rq_ref[...], gq_ref[...], dqn_ref[...], ggq_ref).astype(BF16)
        tail_ref[:, QL:QL + KVL] = one(kv_ref[...], rkv_ref[...], gkv_ref[...], dkvn_ref[...], ggkv_ref).astype(BF16)
        tail_ref[:, QL + KVL:QL + KVL + LANES] = dkpe_ref[...].astype(BF16)
        if tail > QL + KVL + LANES:
            tail_ref[:, QL + KVL + LANES:] = jnp.zeros((tb, tail - (QL + KVL + LANES)), BF16)

    return pl.pallas_call(
        body, name=name, grid=(T // tb,),
        in_specs=[pl.BlockSpec((tb, QL), lambda i: (i, qcol)), pl.BlockSpec((tb, KVL), lambda i: (i, kvcol)),
                  pl.BlockSpec((tb, QL), lambda i: (i, 0)), pl.BlockSpec((tb, KVL), lambda i: (i, 0)),
                  pl.BlockSpec((tb, LANES), lambda i: (i, 0)),
                  pl.BlockSpec((tb, 1), lambda i: (i, 0)), pl.BlockSpec((tb, 1), lambda i: (i, 0)),
                  pl.BlockSpec((1, QL), lambda i: (0, 0)), pl.BlockSpec((1, KVL), lambda i: (0, 0)),
                  pl.BlockSpec(memory_space=pl.ANY)],
        out_specs=[pl.BlockSpec((tb, tail), lambda i: (i, tailcol)),
                   pl.BlockSpec((1, QL), lambda i: (0, 0)), pl.BlockSpec((1, KVL), lambda i: (0, 0))],
        out_shape=[SDS(dproj.shape, BF16), SDS((1, QL), F32), SDS((1, KVL), F32)],
        input_output_aliases={9: 0},
        compiler_params=_cp("arbitrary"),
    )(proj, proj, dqn, dkvn, dkpe, rq, rkv, g_q, g_kv, dproj)


def _swap_halves(r):
    lane = lax.broadcasted_iota(jnp.int32, r.shape, 1)
    lo = pltpu.roll(r, LANES - HALF_ROPE, 1)
    hi = pltpu.roll(r, HALF_ROPE, 1)
    return jnp.where(lane < HALF_ROPE, lo, jnp.where(lane < ROPE, hi, 0.0))


def _rope_fwd(r, cos_t, sin_t):
    return r * cos_t + _swap_halves(r) * sin_t


def _rope_bwd(d, cos_t, sin_t):
    return d * cos_t + _swap_halves(d * sin_t)


def _head_norm_bwd(xn, xr, rs, g_n, g_r, dyn, dyr):
    xhn, xhr = xn * rs, xr * rs
    dn, dr = dyn * g_n, dyr * g_r
    cm = (jnp.sum(dn * xhn, axis=1, keepdims=True) + jnp.sum(dr * xhr, axis=1, keepdims=True)) * (1.0 / QK_HEAD)
    return rs * (dn - xhn * cm), rs * (dr - xhr * cm), dyn * xhn, dyr * xhr


def _q_up_fwd(qn, w_uq_al, g_al, cos_t, sin_t, name):
    T, QL = qn.shape
    HP = w_uq_al.shape[1]
    tb = _tile(T, 512, 8)
    hw = _tile(HP, 1024, HEAD_PAD)

    def body(x_ref, w_ref, g_ref, c_ref, s_ref, o_ref):
        raw = _dot(x_ref[...], w_ref[...], NN)
        gv = g_ref[...]
        cv, sv = c_ref[...], s_ref[...]
        for h in range(hw // HEAD_PAD):
            xn = raw[:, h * HEAD_PAD:h * HEAD_PAD + LANES]
            xr = raw[:, h * HEAD_PAD + LANES:(h + 1) * HEAD_PAD]
            ss = jnp.sum(xn * xn, axis=1, keepdims=True) + jnp.sum(xr * xr, axis=1, keepdims=True)
            rs = lax.rsqrt(ss * (1.0 / QK_HEAD) + EPS)
            o_ref[:, h * HEAD_PAD:h * HEAD_PAD + LANES] = (xn * rs * gv[:, :LANES]).astype(BF16)
            o_ref[:, h * HEAD_PAD + LANES:(h + 1) * HEAD_PAD] = _rope_fwd(xr * rs * gv[:, LANES:], cv, sv).astype(BF16)

    return pl.pallas_call(
        body, name=name, grid=(T // tb, HP // hw),
        in_specs=[pl.BlockSpec((tb, QL), lambda i, j: (i, 0)), pl.BlockSpec((QL, hw), lambda i, j: (0, j)),
                  pl.BlockSpec((1, HEAD_PAD), lambda i, j: (0, 0)),
                  pl.BlockSpec((tb, LANES), lambda i, j: (i, 0)), pl.BlockSpec((tb, LANES), lambda i, j: (i, 0))],
        out_specs=pl.BlockSpec((tb, hw), lambda i, j: (i, j)), out_shape=SDS((T, HP), BF16),
        compiler_params=_cp("parallel", "parallel"),
    )(qn, w_uq_al, g_al, cos_t, sin_t)


def _q_up_bwd(qn, w_uq_al, g_al, cos_t, sin_t, dq, name):
    T, QL = qn.shape
    HP = w_uq_al.shape[1]
    tb = _tile(T, 512, 8)
    hw = _tile(HP, 1024, HEAD_PAD)

    def body(x_ref, w_ref, g_ref, c_ref, s_ref, dq_ref, o_ref, gg_ref):
        @pl.when((pl.program_id(0) == 0) & (pl.program_id(1) == 0))
        def _():
            gg_ref[...] = jnp.zeros_like(gg_ref)

        raw = _dot(x_ref[...], w_ref[...], NN)
        gv = g_ref[...]
        cv, sv = c_ref[...], s_ref[...]
        for h in range(hw // HEAD_PAD):
            lo, mid, hi = h * HEAD_PAD, h * HEAD_PAD + LANES, (h + 1) * HEAD_PAD
            xn, xr = raw[:, lo:mid], raw[:, mid:hi]
            ss = jnp.sum(xn * xn, axis=1, keepdims=True) + jnp.sum(xr * xr, axis=1, keepdims=True)
            rs = lax.rsqrt(ss * (1.0 / QK_HEAD) + EPS)
            dyn = dq_ref[:, lo:mid]
            dyr = _rope_bwd(dq_ref[:, mid:hi], cv, sv)
            dxn, dxr, ggn, ggr = _head_norm_bwd(xn, xr, rs, gv[:, :LANES], gv[:, LANES:], dyn, dyr)
            o_ref[:, lo:mid] = dxn.astype(BF16)
            o_ref[:, mid:hi] = dxr.astype(BF16)
            gg_ref[:, :LANES] += jnp.sum(ggn, axis=0, keepdims=True)
            gg_ref[:, LANES:] += jnp.sum(ggr, axis=0, keepdims=True)

    return pl.pallas_call(
        body, name=name, grid=(T // tb, HP // hw),
        in_specs=[pl.BlockSpec((tb, QL), lambda i, j: (i, 0)), pl.BlockSpec((QL, hw), lambda i, j: (0, j)),
                  pl.BlockSpec((1, HEAD_PAD), lambda i, j: (0, 0)),
                  pl.BlockSpec((tb, LANES), lambda i, j: (i, 0)), pl.BlockSpec((tb, LANES), lambda i, j: (i, 0)),
                  pl.BlockSpec((tb, hw), lambda i, j: (i, j))],
        out_specs=[pl.BlockSpec((tb, hw), lambda i, j: (i, j)), pl.BlockSpec((1, HEAD_PAD), lambda i, j: (0, 0))],
        out_shape=[SDS((T, HP), BF16), SDS((1, HEAD_PAD), F32)],
        compiler_params=_cp("arbitrary", "arbitrary"),
    )(qn, w_uq_al, g_al, cos_t, sin_t, dq)


def _kv_up_fwd(kvn, w_ukv, proj, g_al, cos_t, sin_t, kpe_col, name):
    T, KVL = kvn.shape
    n_shard = w_ukv.shape[2]
    HP = N_DEV * n_shard
    tb = _tile(T, 512, 8)
    hw = _tile(n_shard, 1024, HEAD_PAD)
    per = n_shard // hw
    nh = hw // HEAD_PAD

    def body(x_ref, w_ref, kpe_ref, g_ref, c_ref, s_ref, k_ref, v_ref):
        raw = _dot(x_ref[...], w_ref[...], NN)
        gv = g_ref[...]
        cv, sv = c_ref[...], s_ref[...]
        kpe = kpe_ref[...]
        kpe_ss = jnp.sum(kpe * kpe, axis=1, keepdims=True)
        for h in range(nh):
            lo, mid, hi = h * HEAD_PAD, h * HEAD_PAD + LANES, (h + 1) * HEAD_PAD
            xn = raw[:, lo:mid]
            rs = lax.rsqrt((jnp.sum(xn * xn, axis=1, keepdims=True) + kpe_ss) * (1.0 / QK_HEAD) + EPS)
            k_ref[:, lo:mid] = (xn * rs * gv[:, :LANES]).astype(BF16)
            k_ref[:, mid:hi] = _rope_fwd(kpe * rs * gv[:, LANES:], cv, sv).astype(BF16)
            v_ref[:, h * V_HEAD:(h + 1) * V_HEAD] = raw[:, mid:hi].astype(BF16)

    return pl.pallas_call(
        body, name=name, grid=(T // tb, HP // hw),
        in_specs=[pl.BlockSpec((tb, KVL), lambda i, j: (i, 0)),
                  pl.BlockSpec((None, KVL, hw), lambda i, j: (j // per, 0, j % per)),
                  pl.BlockSpec((tb, LANES), lambda i, j: (i, kpe_col)),
                  pl.BlockSpec((1, HEAD_PAD), lambda i, j: (0, 0)),
                  pl.BlockSpec((tb, LANES), lambda i, j: (i, 0)), pl.BlockSpec((tb, LANES), lambda i, j: (i, 0))],
        out_specs=[pl.BlockSpec((tb, hw), lambda i, j: (i, j)), pl.BlockSpec((tb, nh * V_HEAD), lambda i, j: (i, j))],
        out_shape=[SDS((T, HP), BF16), SDS((T, HP // 2), BF16)],
        compiler_params=_cp("parallel", "parallel"),
    )(kvn, w_ukv, proj, g_al, cos_t, sin_t)


def _kv_up_bwd(kvn, w_ukv, proj, g_al, cos_t, sin_t, dk, dv, kpe_col, name):
    T, KVL = kvn.shape
    n_shard = w_ukv.shape[2]
    HP = N_DEV * n_shard
    tb = _tile(T, 512, 8)
    hw = _tile(n_shard, 1024, HEAD_PAD)
    per = n_shard // hw
    nh = hw // HEAD_PAD

    def body(x_ref, w_ref, kpe_ref, g_ref, c_ref, s_ref, dk_ref, dv_ref, o_ref, dkpe_ref, gg_ref):
        i, j = pl.program_id(0), pl.program_id(1)

        @pl.when((i == 0) & (j == 0))
        def _():
            gg_ref[...] = jnp.zeros_like(gg_ref)

        @pl.when(j == 0)
        def _():
            dkpe_ref[...] = jnp.zeros_like(dkpe_ref)

        raw = _dot(x_ref[...], w_ref[...], NN)
        gv = g_ref[...]
        cv, sv = c_ref[...], s_ref[...]
        kpe = kpe_ref[...]
        kpe_ss = jnp.sum(kpe * kpe, axis=1, keepdims=True)
        for h in range(nh):
            lo, mid, hi = h * HEAD_PAD, h * HEAD_PAD + LANES, (h + 1) * HEAD_PAD
            xn = raw[:, lo:mid]
            rs = lax.rsqrt((jnp.sum(xn * xn, axis=1, keepdims=True) + kpe_ss) * (1.0 / QK_HEAD) + EPS)
            dyn = dk_ref[:, lo:mid]
            dyr = _rope_bwd(dk_ref[:, mid:hi], cv, sv)
            dxn, dxr, ggn, ggr = _head_norm_bwd(xn, kpe, rs, gv[:, :LANES], gv[:, LANES:], dyn, dyr)
            o_ref[:, lo:mid] = dxn.astype(BF16)
            o_ref[:, mid:hi] = dv_ref[:, h * V_HEAD:(h + 1) * V_HEAD].astype(BF16)
            dkpe_ref[...] += dxr
            gg_ref[:, :LANES] += jnp.sum(ggn, axis=0, keepdims=True)
            gg_ref[:, LANES:] += jnp.sum(ggr, axis=0, keepdims=True)

    return pl.pallas_call(
        body, name=name, grid=(T // tb, HP // hw),
        in_specs=[pl.BlockSpec((tb, KVL), lambda i, j: (i, 0)),
                  pl.BlockSpec((None, KVL, hw), lambda i, j: (j // per, 0, j % per)),
                  pl.BlockSpec((tb, LANES), lambda i, j: (i, kpe_col)),
                  pl.BlockSpec((1, HEAD_PAD), lambda i, j: (0, 0)),
                  pl.BlockSpec((tb, LANES), lambda i, j: (i, 0)), pl.BlockSpec((tb, LANES), lambda i, j: (i, 0)),
                  pl.BlockSpec((tb, hw), lambda i, j: (i, j)), pl.BlockSpec((tb, nh * V_HEAD), lambda i, j: (i, j))],
        out_specs=[pl.BlockSpec((tb, hw), lambda i, j: (i, j)), pl.BlockSpec((tb, LANES), lambda i, j: (i, 0)),
                   pl.BlockSpec((1, HEAD_PAD), lambda i, j: (0, 0))],
        out_shape=[SDS((T, HP), BF16), SDS((T, LANES), F32), SDS((1, HEAD_PAD), F32)],
        compiler_params=_cp("arbitrary", "arbitrary"),
    )(kvn, w_ukv, proj, g_al, cos_t, sin_t, dk, dv)


def _attn_fwd(q, k, v, H, name, comm=None):
    B, S, _ = q.shape
    tq = _tile(S, ATTN_BLOCK, 128)
    scale = QK_HEAD ** -0.5

    def body(q_ref, k_ref, v_ref, o_ref, l_ref):
        qi = pl.program_id(2)
        qv = q_ref[...]

        def step(j, carry, diagonal):
            m, l, acc = carry
            off = pl.multiple_of(j * tq, tq)
            kj = k_ref[pl.ds(off, tq), :]
            vj = v_ref[pl.ds(off, tq), :]
            s = _dot(qv, kj, NT) * scale
            if diagonal:
                row = lax.broadcasted_iota(jnp.int32, (tq, tq), 0)
                col = lax.broadcasted_iota(jnp.int32, (tq, tq), 1)
                s = jnp.where(col <= row, s, -jnp.inf)
            m_new = jnp.maximum(m, jnp.max(s, axis=1, keepdims=True))
            alpha = jnp.exp(m - m_new)
            p = jnp.exp(s - m_new)
            l = alpha * l + jnp.sum(p, axis=1, keepdims=True)
            acc = alpha * acc + _dot(p.astype(BF16), vj, NN)
            return m_new, l, acc

        init = (jnp.full((tq, 1), -1e30, F32), jnp.zeros((tq, 1), F32), jnp.zeros((tq, V_HEAD), F32))
        carry = lax.fori_loop(0, qi, functools.partial(step, diagonal=False), init)
        m, l, acc = step(qi, carry, diagonal=True)
        o_ref[...] = (acc / l).astype(BF16)
        l_ref[...] = m + jnp.log(l)

    grid = (B, H, S // tq)
    in_specs = [pl.BlockSpec((None, tq, HEAD_PAD), lambda b, h, i: (b, i, h)),
                pl.BlockSpec((None, S, HEAD_PAD), lambda b, h, i: (b, 0, h)),
                pl.BlockSpec((None, S, V_HEAD), lambda b, h, i: (b, 0, h))]
    out_specs = [pl.BlockSpec((None, tq, V_HEAD), lambda b, h, i: (b, i, h)),
                 pl.BlockSpec((None, None, tq, 1), lambda b, h, i: (b, h, i, 0))]
    out_shape = [SDS((B, S, H * V_HEAD), BF16), SDS((B, H, S, 1), F32)]
    if comm is None:
        return pl.pallas_call(
            body, name=name, grid=grid, in_specs=in_specs, out_specs=out_specs, out_shape=out_shape,
            compiler_params=_cp("parallel", "parallel", "parallel"),
        )(q, k, v)

    def first_last():
        b, h, i = pl.program_id(0), pl.program_id(1), pl.program_id(2)
        return (b == 0) & (h == 0) & (i == 0), (b == B - 1) & (h == H - 1) & (i == grid[2] - 1)

    outs = pl.pallas_call(
        _carry(body, comm, 3, 2, 0, first_last), name=name, grid=grid,
        in_specs=in_specs + [HBM_SPEC] * len(comm.ins), out_specs=out_specs + [HBM_SPEC] * len(comm.out_shape),
        out_shape=out_shape + comm.out_shape, scratch_shapes=comm.scratch,
        compiler_params=_cp("arbitrary", "arbitrary", "arbitrary"),
    )(q, k, v, *comm.ins)
    return outs[0], outs[1], comm.results(outs[2:])


def _attn_bwd(q, k, v, o, lse, do, H, name, comm=None):
    B, S, _ = q.shape
    tq = _tile(S, ATTN_BWD_BLOCK, 128)
    nq = S // tq
    scale = QK_HEAD ** -0.5

    def body(q_ref, k_ref, v_ref, o_ref, l_ref, do_ref, dq_ref, dk_ref, dv_ref, delta_ref, dq_acc):
        dq_acc[...] = jnp.zeros_like(dq_acc)
        for i in range(nq):
            rows = slice(i * tq, (i + 1) * tq)
            delta_ref[rows, :] = jnp.sum(do_ref[rows, :].astype(F32) * o_ref[rows, :].astype(F32), axis=1, keepdims=True)
        def kv_step(j, carry):
            offk = pl.multiple_of(j * tq, tq)
            kj = k_ref[pl.ds(offk, tq), :]
            vj = v_ref[pl.ds(offk, tq), :]

            def q_step(i, acc, diagonal):
                dk_acc, dv_acc = acc
                offq = pl.multiple_of(i * tq, tq)
                qi = q_ref[pl.ds(offq, tq), :]
                doi = do_ref[pl.ds(offq, tq), :]
                s = _dot(qi, kj, NT) * scale
                p = jnp.exp(s - l_ref[pl.ds(offq, tq), :])
                if diagonal:
                    row = lax.broadcasted_iota(jnp.int32, (tq, tq), 0)
                    col = lax.broadcasted_iota(jnp.int32, (tq, tq), 1)
                    p = jnp.where(col <= row, p, 0.0)
                dv_acc = dv_acc + _dot(p.astype(BF16), doi, TN)
                dp = _dot(doi, vj, NT)
                ds = (p * (dp - delta_ref[pl.ds(offq, tq), :]) * scale).astype(BF16)
                dk_acc = dk_acc + _dot(ds, qi, TN)
                dq_acc[pl.ds(offq, tq), :] += _dot(ds, kj, NN)
                return dk_acc, dv_acc

            acc = q_step(j, (jnp.zeros((tq, HEAD_PAD), F32), jnp.zeros((tq, V_HEAD), F32)), diagonal=True)
            dk_acc, dv_acc = lax.fori_loop(j + 1, nq, functools.partial(q_step, diagonal=False), acc)
            dk_ref[pl.ds(offk, tq), :] = dk_acc.astype(BF16)
            dv_ref[pl.ds(offk, tq), :] = dv_acc.astype(BF16)
            return carry

        lax.fori_loop(0, nq, kv_step, 0)
        dq_ref[...] = dq_acc[...].astype(BF16)

    qk_spec = pl.BlockSpec((None, S, HEAD_PAD), lambda b, h: (b, 0, h))
    v_spec = pl.BlockSpec((None, S, V_HEAD), lambda b, h: (b, 0, h))
    in_specs = [qk_spec, qk_spec, v_spec, v_spec, pl.BlockSpec((None, None, S, 1), lambda b, h: (b, h, 0, 0)), v_spec]
    out_specs = [qk_spec, qk_spec, v_spec]
    out_shape = [SDS((B, S, H * HEAD_PAD), BF16), SDS((B, S, H * HEAD_PAD), BF16), SDS((B, S, H * V_HEAD), BF16)]
    scratch = [pltpu.VMEM((S, 1), F32), pltpu.VMEM((S, HEAD_PAD), F32)]
    if comm is None:
        return pl.pallas_call(
            body, name=name, grid=(B, H), in_specs=in_specs, out_specs=out_specs, out_shape=out_shape,
            scratch_shapes=scratch, compiler_params=_cp("parallel", "parallel"),
        )(q, k, v, o, lse, do)

    def first_last():
        b, h = pl.program_id(0), pl.program_id(1)
        return (b == 0) & (h == 0), (b == B - 1) & (h == H - 1)

    outs = pl.pallas_call(
        _carry(body, comm, 6, 3, 2, first_last), name=name, grid=(B, H),
        in_specs=in_specs + [HBM_SPEC] * len(comm.ins), out_specs=out_specs + [HBM_SPEC] * len(comm.out_shape),
        out_shape=out_shape + comm.out_shape, scratch_shapes=scratch + comm.scratch,
        compiler_params=_cp("arbitrary", "arbitrary"),
    )(q, k, v, o, lse, do, *comm.ins)
    return outs[0], outs[1], outs[2], comm.results(outs[3:])


def _natural(sm):
    nd, R, n = sm.shape
    return jnp.transpose(sm, (1, 0, 2)).reshape(R, nd * n)


def _col_shards(full):
    R, N = full.shape
    return jnp.transpose(full.reshape(R, N_DEV, N // N_DEV), (1, 0, 2))


def _shard_rows(sm, lo, hi, n):
    out = []
    for j in range(N_DEV):
        a, b = max(lo, j * n), min(hi, (j + 1) * n)
        if a < b:
            out.append(sm[j, a - j * n:b - j * n])
    return out


def _al_rows(g_t, lo, hi, off_gate, n_gate):
    parts = []
    if lo < off_gate:
        parts.append(g_t[n_gate + lo:n_gate + min(hi, off_gate)])
    if hi > off_gate:
        parts.append(g_t[max(lo, off_gate) - off_gate:hi - off_gate])
    return parts[0] if len(parts) == 1 else jnp.concatenate(parts, axis=0)


def _pad_heads(w, H):
    R = w.shape[0]
    return jnp.pad(w.reshape(R, H, QK_HEAD), ((0, 0), (0, 0), (0, HEAD_PAD - QK_HEAD))).reshape(R, H * HEAD_PAD)


def _unpad_heads(w, H):
    R = w.shape[0]
    return w.reshape(R, H, HEAD_PAD)[:, :, :QK_HEAD].reshape(R, H * QK_HEAD)


def _pad_gain(g):
    return jnp.pad(g, ((0, 0), (0, HEAD_PAD - QK_HEAD)))


def _rope_tables(positions):
    inv_freq = 1.0 / (ROPE_THETA ** (jnp.arange(0, ROPE, 2, dtype=F32) / ROPE))
    ang = positions.astype(F32).reshape(-1, 1) * inv_freq
    cos, sin = jnp.cos(ang), jnp.sin(ang)
    zeros = jnp.zeros((ang.shape[0], LANES - ROPE), F32)
    return jnp.concatenate([cos, cos, zeros], axis=1), jnp.concatenate([-sin, sin, zeros], axis=1)


def kernel(x, c, positions, w_ada, b_ada, g_norm1, w_in, g_v, w_s, b_s, g_q_lat, g_kv_lat, w_uq, w_ukv, g_qn, g_kn, w_branch_a, w_branch_b, w_out, g_norm2, w_ff1, w_ff2, loss_target, m_w_ada, m_b_ada, m_g_norm1, m_w_in, m_g_v, m_w_s, m_b_s, m_g_q_lat, m_g_kv_lat, m_w_uq, m_w_ukv, m_g_qn, m_g_kn, m_w_branch_a, m_w_branch_b, m_w_out, m_g_norm2, m_w_ff1, m_w_ff2, v_w_ada, v_b_ada, v_g_norm1, v_w_in, v_g_v, v_w_s, v_b_s, v_g_q_lat, v_g_kv_lat, v_w_uq, v_w_ukv, v_g_qn, v_g_kn, v_w_branch_a, v_w_branch_b, v_w_out, v_g_norm2, v_w_ff1, v_w_ff2):
    B, S, D = x.shape
    T = B * S
    GW = g_v.shape[-1]
    QL = g_q_lat.shape[-1]
    KVL = g_kv_lat.shape[-1]
    H = w_uq.shape[-1] * N_DEV // QK_HEAD
    IN = w_in.shape[-1] * N_DEV
    OFF_GATE = IN - 2 * D
    IN_AL = _round_up(2 * D + OFF_GATE + (LANES - ROPE), 512)
    assert OFF_GATE == 2 * GW + QL + KVL + ROPE
    assert (2 * D) % GW == 0 and (2 * D + 2 * GW) % QL == 0 and (2 * D + 2 * GW + QL) % KVL == 0
    kpe_col = (2 * D + 2 * GW + QL + KVL) // LANES

    xi, yi, ci = _here()
    dev = 4 * xi + 2 * yi + ci
    c_idx = jnp.reshape(ci, (1,)).astype(jnp.int32)

    big = [w_in, w_uq, w_ukv, w_branch_a, w_branch_b, w_out, w_ff1, w_ff2]
    s_uq, s_ukv, s_ba, s_bb, s_out, s_ff1, s_ff2 = [w[0].astype(BF16) for w in big[1:]]
    n_in = IN // N_DEV
    s_in_t = jnp.transpose(w_in[0]).astype(BF16)
    ((g_in_t,),) = _run_comm([_GatherJob([s_in_t])], "ag_w_in")
    w_al_t = jnp.concatenate(_shard_rows(g_in_t, OFF_GATE, IN, n_in) + _shard_rows(g_in_t, 0, OFF_GATE, n_in)
                             + [jnp.zeros((IN_AL - IN, D), BF16)], axis=0)

    n_ada = w_ada.shape[-1]
    c_all = _all_gather_small(c, "ag_c").reshape(N_DEV * B, D)
    b_cols = lax.dynamic_slice(b_ada, (0, dev * n_ada), (1, n_ada))
    mod_cols = _ada_fwd(c_all, w_ada[0], b_cols, "ada_fwd")
    mod_all = _all_gather_small(mod_cols, "ag_mod")
    mod_mine = lax.dynamic_slice(mod_all, (0, dev * B, 0), (N_DEV, B, n_ada))
    mod_mine = jnp.transpose(mod_mine, (1, 0, 2)).reshape(B, 6, 1, D)
    sh1, sc1, ga1, sh2, sc2, ga2 = [mod_mine[:, k] for k in range(6)]

    cos_t, sin_t = _rope_tables(positions)
    g_qn_al, g_kn_al = _pad_gain(g_qn), _pad_gain(g_kn)
    b_col = b_s[0].reshape(GROUPS, CHUNK, 1)
    gw = GW // GROUPS
    sel = (jnp.arange(GW)[:, None] // gw == jnp.arange(LANES)[None, :]).astype(F32)

    h1, rstd1 = _norm_mod_fwd(x, g_norm1, sh1, sc1, "norm1_fwd")
    h1f = h1.reshape(T, D)
    proj, ((g_uq, g_ukv, g_ba, g_bb, g_out),) = _matmul(
        h1f, w_al_t, mode="nt", out_dtypes=[F32], name="mm_proj", comm=_Comm([_GatherJob([s_uq, s_ukv, s_ba, s_bb, s_out])]))
    w_uq_al = _pad_heads(_natural(g_uq), H)
    w_bb_f = g_bb.reshape(-1, D)
    w_out_f = g_out.reshape(-1, D)
    ga_act, rstd_v = _gmlp_fwd(proj, g_v, w_s[0], b_col, D, GW, "gmlp_fwd")
    y_a = _matmul(ga_act, g_ba, mode="nn", out_dtypes=[BF16], name="mm_ya", b_shards=True)
    qn, kvn, rstd_q, rstd_kv = _lat_norm_fwd(proj, g_q_lat, g_kv_lat, D, GW, QL, KVL, "latnorm_fwd")
    q_al = _q_up_fwd(qn, w_uq_al, g_qn_al, cos_t, sin_t, "q_up_fwd")
    k_al, v_al = _kv_up_fwd(kvn, g_ukv, proj, g_kn_al, cos_t, sin_t, kpe_col, "kv_up_fwd")
    q3, k3, v3 = q_al.reshape(B, S, -1), k_al.reshape(B, S, -1), v_al.reshape(B, S, -1)
    attn, lse, ((g_ff1,),) = _attn_fwd(q3, k3, v3, H, "attn_fwd", comm=_Comm([_GatherJob([s_ff1])]))
    attn_f = attn.reshape(T, H * V_HEAD)
    y_b = _matmul(attn_f, w_bb_f, mode="nn", out_dtypes=[BF16], name="mm_yb")
    mixed = _mix_fwd(proj, y_a, y_b, D, "mix_fwd")
    o = _matmul(mixed, w_out_f, mode="nn", out_dtypes=[F32], name="mm_o")
    x1, h2, rstd2 = _norm_mod_fwd(x, g_norm2, sh2, sc2, "norm2_fwd", o=o.reshape(B, S, D), ga=ga1)

    def relu_sq(acc):
        r = jnp.maximum(acc, 0.0)
        return r * r, r

    (a_act, r_act), ((g_ff2,),) = _matmul(h2.reshape(T, D), g_ff1, mode="nn", out_dtypes=[BF16, BF16], name="mm_ff1",
                                          epi=relu_sq, comm=_Comm([_GatherJob([s_ff2])]), b_shards=True)
    w_ff2_f = g_ff2.reshape(-1, D)
    ff = _matmul(a_act, w_ff2_f, mode="nn", out_dtypes=[F32], name="mm_ff2")
    loss_part, dy, dff, d_ga2 = _loss_head(x1, ff.reshape(B, S, D), loss_target, ga2, "loss_head")
    loss = lax.psum(loss_part[0, 0], ("x", "y", "c"))

    dff_f = dff.reshape(T, D)
    df1 = _matmul(dff_f, w_ff2_f, mode="nt", out_dtypes=[BF16], name="mm_da", epi=lambda acc, r: (acc * (2.0 * r.astype(F32)),),
                  extras=(r_act,))
    gs_ff2 = _matmul(a_act, dff_f, mode="tn", out_dtypes=[BF16], name="mm_gw_ff2").reshape(N_DEV, -1, D)
    gs_ff1, ((sib_ff2,),) = _matmul(h2.reshape(T, D), df1, mode="tn", out_dtypes=[BF16], name="mm_gw_ff1",
                                    comm=_Comm([_SiblingJob([gs_ff2])]), out_shards=True)
    cs_ff2 = _chip_sum(gs_ff2, sib_ff2, c_idx, "chip_sum_w_ff2")
    dh2, ((parts_ff2,), (sib_ff1,)) = _matmul(df1, g_ff1, mode="nt", out_dtypes=[F32], name="mm_dh2", b_shards=True,
                                              comm=_Comm([_ChipsJob([cs_ff2]), _SiblingJob([gs_ff1])]))
    cs_ff1 = _chip_sum(gs_ff1, sib_ff1, c_idx, "chip_sum_w_ff1")
    dx1, d_sh2, d_sc2, gg_norm2, d_ga1, do = _norm_mod_bwd(
        dh2.reshape(B, S, D), x1, rstd2, sc2, g_norm2, dy, "norm2_bwd", o=o.reshape(B, S, D), ga=ga1)

    do_f = do.reshape(T, D)
    dmixed = _matmul(do_f, w_out_f, mode="nt", out_dtypes=[BF16], name="mm_dmixed")
    gw_out = _matmul(mixed, do_f, mode="tn", out_dtypes=[BF16], name="mm_gw_out")
    dy_a, dy_b, dproj = _mix_bwd(proj, y_a, y_b, dmixed, D, "mix_bwd")
    gs_ba = _matmul(ga_act, dy_a, mode="tn", out_dtypes=[BF16], name="mm_gw_ba", out_shards=True)
    dga_act = _matmul(dy_a, g_ba, mode="nt", out_dtypes=[BF16], name="mm_dga", b_shards=True)
    gw_bb = _matmul(attn_f, dy_b, mode="tn", out_dtypes=[BF16], name="mm_gw_bb")
    dattn = _matmul(dy_b, w_bb_f, mode="nt", out_dtypes=[BF16], name="mm_dattn")
    dproj, gg_ws, gg_bs_t, gg_gv = _gmlp_bwd(proj, dga_act, rstd_v, g_v, w_s[0], b_col, sel, dproj, D, GW, "gmlp_bwd")
    gs_early = [gs_ba, gw_bb.reshape(N_DEV, -1, D), gw_out.reshape(N_DEV, -1, D)]
    dq, dk, dv, ((parts_ff1,), sib_early) = _attn_bwd(q3, k3, v3, attn, lse, dattn.reshape(B, S, -1), H, "attn_bwd",
                                                      comm=_Comm([_ChipsJob([cs_ff1]), _SiblingJob(gs_early)]))
    cs_early = [_chip_sum(g, r, c_idx, "chip_sum_" + nm)
                for g, r, nm in zip(gs_early, sib_early, ["w_branch_a", "w_branch_b", "w_out"])]
    dq_raw, gg_qn = _q_up_bwd(qn, w_uq_al, g_qn_al, cos_t, sin_t, dq.reshape(T, -1), "q_up_bwd")
    dkv_raw, dkpe, gg_kn = _kv_up_bwd(kvn, g_ukv, proj, g_kn_al, cos_t, sin_t, dk.reshape(T, -1), dv.reshape(T, -1),
                                      kpe_col, "kv_up_bwd")
    gw_uq_al = _matmul(qn, dq_raw, mode="tn", out_dtypes=[BF16], name="mm_gw_uq")
    dqn = _matmul(dq_raw, w_uq_al, mode="nt", out_dtypes=[F32], name="mm_dqn")
    gs_ukv = _matmul(kvn, dkv_raw, mode="tn", out_dtypes=[BF16], name="mm_gw_ukv", out_shards=True)
    dkvn = _matmul(dkv_raw, g_ukv, mode="nt", out_dtypes=[F32], name="mm_dkvn", b_shards=True)
    dproj, gg_qlat, gg_kvlat = _lat_norm_bwd(
        proj, dqn, dkvn, dkpe, rstd_q, rstd_kv, g_q_lat, g_kv_lat, dproj, D, GW, QL, KVL, "latnorm_bwd")
    gs_late = [_col_shards(_unpad_heads(gw_uq_al, H)), gs_ukv]
    (sib_late,) = _run_comm([_SiblingJob(gs_late)], "rs_sibling_late")
    cs_mid = [_chip_sum(g, r, c_idx, "chip_sum_" + nm) for g, r, nm in zip(gs_late, sib_late, ["w_uq", "w_ukv"])] + cs_early
    gw_al_t, (parts_mid,) = _matmul(dproj, h1f, mode="tn", out_dtypes=[BF16], name="mm_gw_in", comm=_Comm([_ChipsJob(cs_mid)]))
    gs_in = jnp.stack([_al_rows(gw_al_t, j * n_in, (j + 1) * n_in, OFF_GATE, 2 * D) for j in range(N_DEV)])
    ((sib_in,),) = _run_comm([_SiblingJob([gs_in])], "rs_sibling_in")
    cs_in = _chip_sum(gs_in, sib_in, c_idx, "chip_sum_w_in")
    dh1, ((parts_in,),) = _matmul(dproj, w_al_t, mode="nn", out_dtypes=[F32], name="mm_dh1", comm=_Comm([_ChipsJob([cs_in])]))
    grad_x, d_sh1, d_sc1, gg_norm1 = _norm_mod_bwd(dh1.reshape(B, S, D), x, rstd1, sc1, g_norm1, dx1, "norm1_bwd")

    dmod_mine = jnp.concatenate([d_sh1, d_sc1, d_ga1, d_sh2, d_sc2, d_ga2], axis=2).reshape(B, 6 * D)
    dmod_all = _all_gather_small(dmod_mine, "ag_dmod").reshape(N_DEV * B, 6 * D)
    dmod_cols = lax.dynamic_slice(dmod_all, (0, dev * n_ada), (N_DEV * B, n_ada))
    ada_out = _ada_bwd_adam(c_all, dmod_cols, w_ada[0], m_w_ada[0], v_w_ada[0], "ada_bwd_adam")
    nb_rows = 8
    bada_out = _adam_from_parts(b_ada.reshape(nb_rows, -1), m_b_ada.reshape(nb_rows, -1), v_b_ada.reshape(nb_rows, -1),
                                dmod_all.reshape(N_DEV * B, nb_rows, -1), "adam_b_ada")

    names = ["w_in", "w_uq", "w_ukv", "w_branch_a", "w_branch_b", "w_out", "w_ff1", "w_ff2"]
    parts = [parts_in] + list(parts_mid) + [parts_ff1, parts_ff2]
    ms = [m_w_in, m_w_uq, m_w_ukv, m_w_branch_a, m_w_branch_b, m_w_out, m_w_ff1, m_w_ff2]
    vs = [v_w_in, v_w_uq, v_w_ukv, v_w_branch_a, v_w_branch_b, v_w_out, v_w_ff1, v_w_ff2]
    big_out = {}
    for nm, w, m, v, p in zip(names, big, ms, vs, parts):
        if nm == "w_in":
            res = _adam_from_parts(jnp.transpose(w[0]), jnp.transpose(m[0]), jnp.transpose(v[0]), p, "adam_" + nm)
            big_out[nm] = [jnp.transpose(r) for r in res]
        else:
            big_out[nm] = _adam_from_parts(w[0], m[0], v[0], p, "adam_" + nm)

    small = [("g_norm1", g_norm1, m_g_norm1, v_g_norm1, gg_norm1),
             ("g_v", g_v, m_g_v, v_g_v, gg_gv),
             ("w_s", w_s, m_w_s, v_w_s, gg_ws),
             ("b_s", b_s, m_b_s, v_b_s, jnp.transpose(gg_bs_t[:, :GROUPS])),
             ("g_q_lat", g_q_lat, m_g_q_lat, v_g_q_lat, gg_qlat),
             ("g_kv_lat", g_kv_lat, m_g_kv_lat, v_g_kv_lat, gg_kvlat),
             ("g_qn", g_qn, m_g_qn, v_g_qn, gg_qn[:, :QK_HEAD]),
             ("g_kn", g_kn, m_g_kn, v_g_kn, gg_kn[:, :QK_HEAD]),
             ("g_norm2", g_norm2, m_g_norm2, v_g_norm2, gg_norm2)]
    sizes = [w.size for _, w, _, _, _ in small]
    n_small = sum(sizes)
    n_small_pad = _round_up(n_small, 8 * LANES)

    def flat_cat(arrs):
        return jnp.pad(jnp.concatenate([a.reshape(-1) for a in arrs]), (0, n_small_pad - n_small))

    part_small = _all_gather_small(flat_cat([t[4] for t in small]), "ag_small_grads")
    small_out = _adam_from_parts(
        flat_cat([t[1] for t in small]).reshape(8, -1), flat_cat([t[2] for t in small]).reshape(8, -1),
        flat_cat([t[3] for t in small]).reshape(8, -1), part_small.reshape(N_DEV, 8, -1), "adam_small")
    offs = [sum(sizes[:i]) for i in range(len(sizes))]

    def small_piece(kind, i):
        return small_out[kind].reshape(-1)[offs[i]:offs[i] + sizes[i]].reshape(small[i][1].shape)

    small_idx = {t[0]: i for i, t in enumerate(small)}
    order = ["w_ada", "b_ada", "g_norm1", "w_in", "g_v", "w_s", "b_s", "g_q_lat", "g_kv_lat", "w_uq", "w_ukv", "g_qn", "g_kn",
             "w_branch_a", "w_branch_b", "w_out", "g_norm2", "w_ff1", "w_ff2"]

    def result(kind, nm):
        if nm == "w_ada":
            return ada_out[kind][None]
        if nm == "b_ada":
            return bada_out[kind].reshape(b_ada.shape)
        if nm in small_idx:
            return small_piece(kind, small_idx[nm])
        return big_out[nm][kind][None]

    outs = [loss, grad_x]
    for kind in range(4):
        outs += [result(kind, nm) for nm in order]
    return tuple(outs)
```

```python
import functools
import math

import jax
import jax.numpy as jnp
from jax import lax
from jax.experimental import pallas as pl
from jax.experimental.pallas import tpu as pltpu

F32 = jnp.float32
BF16 = jnp.bfloat16
SDS = jax.ShapeDtypeStruct
MESH = pl.DeviceIdType.MESH

N_DEV = 8
N_CHIP = 4
CHUNK = 128
GROUPS = 8
NOPE = 128
ROPE = 64
HALF_ROPE = ROPE // 2
QK_HEAD = NOPE + ROPE
V_HEAD = 128
HEAD_PAD = 256
LANES = 128
ROPE_THETA = 10000.0
EPS = 1e-6
INV_SQRT2 = 1.0 / math.sqrt(2.0)
INV_SQRT_2PI = 1.0 / math.sqrt(2.0 * math.pi)

ADAM_LR = 0.001
ADAM_B1 = 0.9
ADAM_B2 = 0.999
ADAM_EPS = 1e-08
ADAM_WD = 0.01
ADAM_STEP = 10

V7X_VMEM_LIMIT_BYTES = 56 * 1024 * 1024
MM_TILE = 1024
MM_TILE_K = 4096
ATTN_BLOCK = 512
ATTN_BWD_BLOCK = 1024
ROW_BLOCK = 128
TOKEN_BLOCK = 512
COL_BLOCK = 1024
ADA_COL_BLOCK = 512
SUBLANES = 8
SUM_BLOCK_ELEMS = 1 << 20
ADAM_BLOCK_ELEMS = 1 << 19

NN = (((1,), (0,)), ((), ()))
NT = (((1,), (1,)), ((), ()))
TN = (((0,), (0,)), ((), ()))


def _tile(n, pref, mult):
    t = min(pref, n)
    t -= t % mult
    while t >= mult:
        if n % t == 0:
            return t
        t -= mult
    return n


def _round_up(n, m):
    return (n + m - 1) // m * m


BF16_SUBLANES = 16


def _block_2d(R, C, elems):
    if R % BF16_SUBLANES == 0:
        return _tile(R, max(BF16_SUBLANES, elems // C // BF16_SUBLANES * BF16_SUBLANES), BF16_SUBLANES), C
    return R, _tile(C, max(LANES, elems // R // LANES * LANES), LANES)


def _cp(*sem):
    return pltpu.CompilerParams(dimension_semantics=sem, vmem_limit_bytes=V7X_VMEM_LIMIT_BYTES)


def _dot(a, b, dims):
    return lax.dot_general(a, b, dims, preferred_element_type=F32)


def _gelu(x):
    return 0.5 * x * (1.0 + lax.erf(x * INV_SQRT2))


def _gelu_grad(x):
    return 0.5 * (1.0 + lax.erf(x * INV_SQRT2)) + x * jnp.exp(-0.5 * x * x) * INV_SQRT_2PI


def _sigmoid(x):
    return 1.0 / (1.0 + jnp.exp(-x))


def _matmul(a, b, *, mode, out_dtypes, name, epi=None, extras=(), comm=None, b_shards=False, out_shards=False):
    if mode == "tn":
        K, M = a.shape
    else:
        M, K = a.shape
    n_shard = None
    if b_shards:
        _, R, n_shard = b.shape
        N, Kb = (R, N_DEV * n_shard) if mode == "nt" else (N_DEV * n_shard, R)
    elif mode == "nt":
        N, Kb = b.shape
    else:
        Kb, N = b.shape
    assert K == Kb, (name, a.shape, b.shape)
    tm = _tile(M, MM_TILE, LANES)
    tn = _tile(n_shard if (b_shards and mode != "nt") else N, MM_TILE, LANES)
    tk = _tile(n_shard if (b_shards and mode == "nt") else K, MM_TILE_K, LANES)
    if out_shards:
        assert mode == "tn" and not extras and len(out_dtypes) == 1
        n_shard = N // N_DEV
        tn = _tile(n_shard, MM_TILE, LANES)
    nk = K // tk
    n_extra = len(extras)
    n_out = len(out_dtypes)
    dims = {"nn": NN, "nt": NT, "tn": TN}[mode]

    def body(a_ref, b_ref, *rest):
        extra_refs = rest[:n_extra]
        out_refs = rest[n_extra:n_extra + n_out]
        acc_ref = rest[n_extra + n_out] if nk > 1 else None
        k = pl.program_id(2)

        def product():
            return _dot(a_ref[...].astype(BF16), b_ref[...].astype(BF16), dims)

        def finish(acc):
            res = (acc,) if epi is None else epi(acc, *[e[...] for e in extra_refs])
            for o_ref, r in zip(out_refs, res):
                o_ref[...] = r.astype(o_ref.dtype)

        if nk == 1:
            finish(product())
            return

        @pl.when(k == 0)
        def _():
            acc_ref[...] = product()

        if nk > 2:
            @pl.when((k > 0) & (k < nk - 1))
            def _():
                acc_ref[...] += product()

        @pl.when(k == nk - 1)
        def _():
            finish(acc_ref[...] + product())

    if mode == "tn":
        a_spec = pl.BlockSpec((tk, tm), lambda i, j, k: (k, i))
    else:
        a_spec = pl.BlockSpec((tm, tk), lambda i, j, k: (i, k))
    if b_shards and mode == "nt":
        per = n_shard // tk
        b_spec = pl.BlockSpec((None, tn, tk), lambda i, j, k: (k // per, j, k % per))
    elif b_shards:
        per = n_shard // tn
        b_spec = pl.BlockSpec((None, tk, tn), lambda i, j, k: (j // per, k, j % per))
    elif mode == "nt":
        b_spec = pl.BlockSpec((tn, tk), lambda i, j, k: (j, k))
    else:
        b_spec = pl.BlockSpec((tk, tn), lambda i, j, k: (k, j))
    mn_spec = pl.BlockSpec((tm, tn), lambda i, j, k: (i, j))
    grid = (M // tm, N // tn, nk)
    in_specs = [a_spec, b_spec] + [mn_spec] * n_extra
    if out_shards:
        per_out = n_shard // tn
        out_specs = [pl.BlockSpec((None, tm, tn), lambda i, j, k: (j // per_out, i, j % per_out))]
        out_shape = [SDS((N_DEV, M, n_shard), out_dtypes[0])]
    else:
        out_specs = [mn_spec] * n_out
        out_shape = [SDS((M, N), dt) for dt in out_dtypes]
    scratch = [pltpu.VMEM((tm, tn), F32)] if nk > 1 else []
    if comm is None:
        outs = pl.pallas_call(
            body, name=name, grid=grid, in_specs=in_specs, out_specs=out_specs, out_shape=out_shape,
            scratch_shapes=scratch, compiler_params=_cp("parallel", "parallel", "arbitrary"),
        )(a, b, *extras)
        return outs[0] if n_out == 1 else outs

    def first_last():
        i, j, k = pl.program_id(0), pl.program_id(1), pl.program_id(2)
        return ((i == 0) & (j == 0) & (k == 0),
                (i == grid[0] - 1) & (j == grid[1] - 1) & (k == nk - 1))

    outs = pl.pallas_call(
        _carry(body, comm, 2 + n_extra, n_out, len(scratch), first_last), name=name, grid=grid,
        in_specs=in_specs + [HBM_SPEC] * len(comm.ins), out_specs=out_specs + [HBM_SPEC] * len(comm.out_shape),
        out_shape=out_shape + comm.out_shape, scratch_shapes=scratch + comm.scratch,
        compiler_params=_cp("arbitrary", "arbitrary", "arbitrary"),
    )(a, b, *extras, *comm.ins)
    res = outs[0] if n_out == 1 else list(outs[:n_out])
    return res, comm.results(outs[n_out:])


def _here():
    return lax.axis_index("x"), lax.axis_index("y"), lax.axis_index("c")


def _other_chips(x, y):
    return [(1 - x, y), (x, 1 - y), (1 - x, 1 - y)]


def _all_gather_small(v, name):
    shape = v.shape
    n = v.size
    n_pad = _round_up(n, 8 * LANES)
    flat = jnp.pad(v.reshape(-1), (0, n_pad - n)).reshape(8, n_pad // 8)
    m_per, cols = flat.shape

    def body(x_ref, out_ref, send_sems, recv_sems, local_sem):
        x, y, c = _here()
        me, sibling = (x, y, c), (x, y, 1 - c)
        chips = _other_chips(x, y)

        def rows(px, py, pc):
            return out_ref.at[pl.ds((4 * px + 2 * py + pc) * m_per, m_per), :]

        def copy(k, block, to, src=None):
            return pltpu.make_async_remote_copy(
                src_ref=rows(*block) if src is None else src, dst_ref=rows(*block),
                send_sem=send_sems.at[k], recv_sem=recv_sems.at[k], device_id=to, device_id_type=MESH)

        mine = pltpu.make_async_copy(x_ref, rows(*me), local_sem)
        mine.start()
        first = [copy(0, me, sibling, src=x_ref)]
        first += [copy(1 + j, me, (*chip, c), src=x_ref) for j, chip in enumerate(chips)]
        for cp in first:
            cp.start()
        passed = [copy(4 + j, (*chip, c), sibling) for j, chip in enumerate(chips)]
        for j, chip in enumerate(chips):
            copy(1 + j, (*chip, c), me).wait_recv()
            passed[j].start()
        copy(0, sibling, me).wait_recv()
        for j, chip in enumerate(chips):
            copy(4 + j, (*chip, 1 - c), me).wait_recv()
        for cp in first + passed:
            cp.wait_send()
        mine.wait()

    out = pl.pallas_call(
        body, name=name,
        out_shape=SDS((N_DEV * m_per, cols), F32),
        in_specs=[pl.BlockSpec(memory_space=pltpu.VMEM)],
        out_specs=pl.BlockSpec(memory_space=pltpu.VMEM),
        scratch_shapes=[pltpu.SemaphoreType.DMA((7,)), pltpu.SemaphoreType.DMA((7,)), pltpu.SemaphoreType.DMA],
        compiler_params=pltpu.CompilerParams(vmem_limit_bytes=V7X_VMEM_LIMIT_BYTES),
    )(flat)
    return out.reshape(N_DEV, n_pad)[:, :n].reshape((N_DEV,) + shape)


HBM_SPEC = pl.BlockSpec(memory_space=pltpu.HBM)


class _GatherJob:
    def __init__(self, shards):
        self.ins = list(shards)
        self.nw = len(shards)
        self.out_shape = [SDS((N_DEV,) + s.shape, s.dtype) for s in shards]
        self.scratch = [pltpu.SemaphoreType.DMA((7 * self.nw,)), pltpu.SemaphoreType.DMA((7 * self.nw,)),
                        pltpu.SemaphoreType.DMA((self.nw,))]

    def _parts(self, xs, outs, sems):
        send_sems, recv_sems, local_sems = sems
        x, y, c = _here()

        def blk(w, px, py, pc):
            return outs[w].at[4 * px + 2 * py + pc]

        def copy(w, k, block, to, src=None):
            return pltpu.make_async_remote_copy(
                src_ref=blk(w, *block) if src is None else src, dst_ref=blk(w, *block),
                send_sem=send_sems.at[7 * w + k], recv_sem=recv_sems.at[7 * w + k], device_id=to, device_id_type=MESH)

        me, sibling = (x, y, c), (x, y, 1 - c)
        chips = _other_chips(x, y)
        mine = [pltpu.make_async_copy(xs[w], blk(w, *me), local_sems.at[w]) for w in range(self.nw)]
        first = []
        for w in range(self.nw):
            first.append(copy(w, 0, me, sibling, src=xs[w]))
            first += [copy(w, 1 + j, me, (*chip, c), src=xs[w]) for j, chip in enumerate(chips)]
        return copy, me, sibling, chips, c, mine, first

    def start(self, xs, outs, sems):
        _, _, _, _, _, mine, first = self._parts(xs, outs, sems)
        for cp in mine + first:
            cp.start()

    def finish(self, xs, outs, sems):
        copy, me, sibling, chips, c, mine, first = self._parts(xs, outs, sems)
        passed = []
        for w in range(self.nw):
            for j, chip in enumerate(chips):
                copy(w, 1 + j, (*chip, c), me).wait_recv()
                fwd = copy(w, 4 + j, (*chip, c), sibling)
                fwd.start()
                passed.append(fwd)
        for w in range(self.nw):
            copy(w, 0, sibling, me).wait_recv()
            for j, chip in enumerate(chips):
                copy(w, 4 + j, (*chip, 1 - c), me).wait_recv()
        for cp in first + passed:
            cp.wait_send()
        for cp in mine:
            cp.wait()


class _SiblingJob:
    def __init__(self, grads):
        self.ins = list(grads)
        self.nw = len(grads)
        self.out_shape = [SDS((N_CHIP,) + g.shape[1:], g.dtype) for g in grads]
        self.scratch = [pltpu.SemaphoreType.DMA((N_CHIP * self.nw,)), pltpu.SemaphoreType.DMA((N_CHIP * self.nw,))]

    def _copies(self, gs, outs, sems):
        send_sems, recv_sems = sems
        x, y, c = _here()
        return [pltpu.make_async_remote_copy(
            src_ref=gs[w].at[2 * k + (1 - c)], dst_ref=outs[w].at[k],
            send_sem=send_sems.at[N_CHIP * w + k], recv_sem=recv_sems.at[N_CHIP * w + k],
            device_id=(x, y, 1 - c), device_id_type=MESH) for w in range(self.nw) for k in range(N_CHIP)]

    def start(self, gs, outs, sems):
        for cp in self._copies(gs, outs, sems):
            cp.start()

    def finish(self, gs, outs, sems):
        for cp in self._copies(gs, outs, sems):
            cp.wait()


class _ChipsJob:
    def __init__(self, chip_sums):
        self.ins = list(chip_sums)
        self.nw = len(chip_sums)
        self.out_shape = [SDS(s.shape, s.dtype) for s in chip_sums]
        self.scratch = [pltpu.SemaphoreType.DMA((3 * self.nw,)), pltpu.SemaphoreType.DMA((3 * self.nw,)),
                        pltpu.SemaphoreType.DMA((self.nw,))]

    def _parts(self, srcs, outs, sems):
        send_sems, recv_sems, local_sems = sems
        x, y, c = _here()
        my_chip = 2 * x + y
        local = [pltpu.make_async_copy(srcs[w].at[my_chip], outs[w].at[my_chip], local_sems.at[w]) for w in range(self.nw)]
        sends, landed = [], []
        for w in range(self.nw):
            for j, (px, py) in enumerate(_other_chips(x, y)):
                sems_j = dict(send_sem=send_sems.at[3 * w + j], recv_sem=recv_sems.at[3 * w + j],
                              device_id=(px, py, c), device_id_type=MESH)
                sends.append(pltpu.make_async_remote_copy(
                    src_ref=srcs[w].at[2 * px + py], dst_ref=outs[w].at[my_chip], **sems_j))
                landed.append(pltpu.make_async_remote_copy(
                    src_ref=srcs[w].at[2 * px + py], dst_ref=outs[w].at[2 * px + py], **sems_j))
        return local, sends, landed

    def start(self, srcs, outs, sems):
        local, sends, _ = self._parts(srcs, outs, sems)
        for cp in local + sends:
            cp.start()

    def finish(self, srcs, outs, sems):
        local, sends, landed = self._parts(srcs, outs, sems)
        for cp in landed:
            cp.wait_recv()
        for cp in sends:
            cp.wait_send()
        for cp in local:
            cp.wait()


class _Comm:
    def __init__(self, jobs):
        self.jobs = list(jobs)
        self.ins = [a for j in self.jobs for a in j.ins]
        self.out_shape = [s for j in self.jobs for s in j.out_shape]
        self.scratch = [s for j in self.jobs for s in j.scratch]

    def _split(self, flat, counts):
        out, pos = [], 0
        for n in counts:
            out.append(flat[pos:pos + n])
            pos += n
        return out

    def _each(self, ins, outs, sems):
        return zip(self.jobs, self._split(ins, [len(j.ins) for j in self.jobs]),
                   self._split(outs, [len(j.out_shape) for j in self.jobs]),
                   self._split(sems, [len(j.scratch) for j in self.jobs]))

    def start(self, ins, outs, sems):
        for job, i, o, s in self._each(ins, outs, sems):
            job.start(i, o, s)

    def finish(self, ins, outs, sems):
        for job, i, o, s in self._each(ins, outs, sems):
            job.finish(i, o, s)

    def results(self, flat):
        return [list(r) for r in self._split(list(flat), [len(j.out_shape) for j in self.jobs])]


def _carry(body, comm, n_in, n_out, n_scratch, first_last):
    ci, co = len(comm.ins), len(comm.out_shape)

    def wrapped(*refs):
        ins, rest = refs[:n_in + ci], refs[n_in + ci:]
        outs, scr = rest[:n_out + co], rest[n_out + co:]
        c_ins, c_outs, c_sems = ins[n_in:], outs[n_out:], scr[n_scratch:]
        first, last = first_last()

        @pl.when(first)
        def _():
            comm.start(c_ins, c_outs, c_sems)

        body(*ins[:n_in], *outs[:n_out], *scr[:n_scratch])

        @pl.when(last)
        def _():
            comm.finish(c_ins, c_outs, c_sems)

    return wrapped


def _run_comm(jobs, name):
    comm = _Comm(jobs)

    def body(*refs):
        ci, co = len(comm.ins), len(comm.out_shape)
        comm.start(refs[:ci], refs[ci:ci + co], refs[ci + co:])
        comm.finish(refs[:ci], refs[ci:ci + co], refs[ci + co:])

    outs = pl.pallas_call(
        body, name=name, out_shape=comm.out_shape,
        in_specs=[HBM_SPEC] * len(comm.ins), out_specs=[HBM_SPEC] * len(comm.out_shape),
        scratch_shapes=comm.scratch,
    )(*comm.ins)
    return comm.results(outs)


def _chip_sum(g, recv, c_idx, name):
    _, m, n = g.shape
    tr, tc = _block_2d(m, n, SUM_BLOCK_ELEMS)

    def body(c_ref, g_ref, r_ref, o_ref):
        o_ref[...] = (g_ref[...].astype(F32) + r_ref[...].astype(F32)).astype(o_ref.dtype)

    grid_spec = pltpu.PrefetchScalarGridSpec(
        num_scalar_prefetch=1, grid=(N_CHIP, m // tr, n // tc),
        in_specs=[pl.BlockSpec((None, tr, tc), lambda k, i, j, c_ref: (2 * k + c_ref[0], i, j)),
                  pl.BlockSpec((None, tr, tc), lambda k, i, j, c_ref: (k, i, j))],
        out_specs=pl.BlockSpec((None, tr, tc), lambda k, i, j, c_ref: (k, i, j)))
    return pl.pallas_call(
        body, name=name, grid_spec=grid_spec, out_shape=SDS((N_CHIP, m, n), BF16),
        compiler_params=_cp("parallel", "parallel", "parallel"),
    )(c_idx, g, recv)


def _adam_math(w, g, m, v):
    m = ADAM_B1 * m + (1.0 - ADAM_B1) * g
    v = ADAM_B2 * v + (1.0 - ADAM_B2) * (g * g)
    m_hat = m / (1.0 - ADAM_B1 ** ADAM_STEP)
    v_hat = v / (1.0 - ADAM_B2 ** ADAM_STEP)
    delta = -ADAM_LR * (m_hat / (jnp.sqrt(v_hat) + ADAM_EPS) + ADAM_WD * w)
    return delta, m, v


def _adam_from_parts(w, m, v, parts, name):
    R, C = w.shape
    P = parts.shape[0]
    tr, tc = _block_2d(R, C, ADAM_BLOCK_ELEMS)

    def body(w_ref, m_ref, v_ref, p_ref, g_out, d_out, m_out, v_out):
        g = p_ref[0].astype(F32)
        for k in range(1, P):
            g = g + p_ref[k].astype(F32)
        delta, nm, nv = _adam_math(w_ref[...], g, m_ref[...], v_ref[...])
        g_out[...] = g
        d_out[...] = delta
        m_out[...] = nm
        v_out[...] = nv

    spec = pl.BlockSpec((tr, tc), lambda i, j: (i, j))
    return pl.pallas_call(
        body, name=name, grid=(R // tr, C // tc),
        in_specs=[spec, spec, spec, pl.BlockSpec((P, tr, tc), lambda i, j: (0, i, j))],
        out_specs=[spec] * 4, out_shape=[SDS((R, C), F32)] * 4,
        compiler_params=_cp("parallel", "parallel"),
    )(w, m, v, parts)


def _ada_fwd(c_all, w_ada, b_cols, name):
    nb, D = c_all.shape
    n = w_ada.shape[1]
    tn = _tile(n, ADA_COL_BLOCK, LANES)

    def body(c_ref, w_ref, b_ref, o_ref):
        cv = c_ref[...]
        cond = (cv * _sigmoid(cv)).astype(BF16)
        o_ref[...] = _dot(cond, w_ref[...].astype(BF16), NN) + b_ref[...]

    return pl.pallas_call(
        body, name=name, grid=(n // tn,),
        in_specs=[pl.BlockSpec((nb, D), lambda j: (0, 0)), pl.BlockSpec((D, tn), lambda j: (0, j)),
                  pl.BlockSpec((1, tn), lambda j: (0, j))],
        out_specs=pl.BlockSpec((nb, tn), lambda j: (0, j)), out_shape=SDS((nb, n), F32),
        compiler_params=_cp("parallel"),
    )(c_all, w_ada, b_cols)


def _ada_bwd_adam(c_all, dmod_cols, w, m, v, name):
    nb, D = c_all.shape
    n = w.shape[1]
    tr = _tile(D, TOKEN_BLOCK, LANES)
    tn = _tile(n, COL_BLOCK, LANES)

    def body(c_ref, d_ref, w_ref, m_ref, v_ref, g_out, d_out, m_out, v_out):
        cv = c_ref[...]
        cond = (cv * _sigmoid(cv)).astype(BF16)
        g = _dot(cond, d_ref[...].astype(BF16), TN)
        delta, nm, nv = _adam_math(w_ref[...], g, m_ref[...], v_ref[...])
        g_out[...] = g
        d_out[...] = delta
        m_out[...] = nm
        v_out[...] = nv

    spec = pl.BlockSpec((tr, tn), lambda i, j: (i, j))
    return pl.pallas_call(
        body, name=name, grid=(D // tr, n // tn),
        in_specs=[pl.BlockSpec((nb, tr), lambda i, j: (0, i)), pl.BlockSpec((nb, tn), lambda i, j: (0, j)),
                  spec, spec, spec],
        out_specs=[spec] * 4, out_shape=[SDS((D, n), F32)] * 4,
        compiler_params=_cp("parallel", "parallel"),
    )(c_all, dmod_cols, w, m, v)


def _tok(tb, width):
    return pl.BlockSpec((None, tb, width), lambda b, i: (b, i, 0))


def _per_example(width):
    return pl.BlockSpec((None, 1, width), lambda b, i: (b, 0, 0))


def _shared_row(width):
    return pl.BlockSpec((1, width), lambda b, i: (0, 0))


def _norm_mod_fwd(x, g, sh, sc, name, o=None, ga=None, comm=None):
    B, S, D = x.shape
    tb = _tile(S, ROW_BLOCK, SUBLANES)
    fused = o is not None

    def body(*refs):
        if fused:
            x_ref, o_ref, ga_ref, g_ref, sh_ref, sc_ref, x1_ref, h_ref, r_ref = refs
            xv = x_ref[...] + ga_ref[...] * o_ref[...]
            x1_ref[...] = xv
        else:
            x_ref, g_ref, sh_ref, sc_ref, h_ref, r_ref = refs
            xv = x_ref[...]
        rstd = lax.rsqrt(jnp.mean(xv * xv, axis=1, keepdims=True) + EPS)
        y = xv * rstd * g_ref[...]
        h_ref[...] = (y * (1.0 + sc_ref[...]) + sh_ref[...]).astype(BF16)
        r_ref[...] = rstd

    ins = [x] + ([o, ga] if fused else []) + [g, sh, sc]
    in_specs = [_tok(tb, D)] + ([_tok(tb, D), _per_example(D)] if fused else []) + [_shared_row(D), _per_example(D), _per_example(D)]
    out_specs = ([_tok(tb, D)] if fused else []) + [_tok(tb, D), _tok(tb, 1)]
    out_shape = ([SDS((B, S, D), F32)] if fused else []) + [SDS((B, S, D), BF16), SDS((B, S, 1), F32)]
    if comm is None:
        return pl.pallas_call(
            body, name=name, grid=(B, S // tb), in_specs=in_specs, out_specs=out_specs, out_shape=out_shape,
            compiler_params=_cp("parallel", "parallel"),
        )(*ins)

    def first_last():
        b, i = pl.program_id(0), pl.program_id(1)
        return (b == 0) & (i == 0), (b == B - 1) & (i == S // tb - 1)

    outs = pl.pallas_call(
        _carry(body, comm, len(ins), len(out_shape), 0, first_last), name=name, grid=(B, S // tb),
        in_specs=in_specs + [HBM_SPEC] * len(comm.ins), out_specs=out_specs + [HBM_SPEC] * len(comm.out_shape),
        out_shape=out_shape + comm.out_shape, scratch_shapes=comm.scratch,
        compiler_params=_cp("arbitrary", "arbitrary"),
    )(*ins, *comm.ins)
    return list(outs[:len(out_shape)]) + [comm.results(outs[len(out_shape):])]


def _norm_mod_bwd(dh, xin, rstd, sc, g, dres, name, o=None, ga=None):
    B, S, D = xin.shape
    tb = _tile(S, ROW_BLOCK, SUBLANES)
    gated = o is not None

    def body(*refs):
        if gated:
            (dh_ref, x_ref, r_ref, sc_ref, g_ref, dres_ref, o_ref, ga_ref,
             dx_ref, dsh_ref, dsc_ref, gg_ref, dga_ref, do_ref) = refs
        else:
            dh_ref, x_ref, r_ref, sc_ref, g_ref, dres_ref, dx_ref, dsh_ref, dsc_ref, gg_ref = refs
        b, i = pl.program_id(0), pl.program_id(1)

        @pl.when(i == 0)
        def _():
            dsh_ref[...] = jnp.zeros_like(dsh_ref)
            dsc_ref[...] = jnp.zeros_like(dsc_ref)
            if gated:
                dga_ref[...] = jnp.zeros_like(dga_ref)

        @pl.when((i == 0) & (b == 0))
        def _():
            gg_ref[...] = jnp.zeros_like(gg_ref)

        dhv = dh_ref[...]
        rs = r_ref[...]
        gv = g_ref[...]
        xhat = x_ref[...] * rs
        dsh_ref[...] += jnp.sum(dhv, axis=0, keepdims=True)
        dsc_ref[...] += jnp.sum(dhv * (xhat * gv), axis=0, keepdims=True)
        dn = dhv * (1.0 + sc_ref[...])
        gg_ref[...] += jnp.sum(dn * xhat, axis=0, keepdims=True)
        dxhat = dn * gv
        cm = jnp.mean(dxhat * xhat, axis=1, keepdims=True)
        dx = dres_ref[...] + rs * (dxhat - xhat * cm)
        dx_ref[...] = dx
        if gated:
            dga_ref[...] += jnp.sum(dx * o_ref[...], axis=0, keepdims=True)
            do_ref[...] = (dx * ga_ref[...]).astype(BF16)

    ins = [dh, xin, rstd, sc, g, dres] + ([o, ga] if gated else [])
    in_specs = [_tok(tb, D), _tok(tb, D), _tok(tb, 1), _per_example(D), _shared_row(D), _tok(tb, D)]
    in_specs += [_tok(tb, D), _per_example(D)] if gated else []
    out_specs = [_tok(tb, D), _per_example(D), _per_example(D), _shared_row(D)]
    out_shape = [SDS((B, S, D), F32), SDS((B, 1, D), F32), SDS((B, 1, D), F32), SDS((1, D), F32)]
    if gated:
        out_specs += [_per_example(D), _tok(tb, D)]
        out_shape += [SDS((B, 1, D), F32), SDS((B, S, D), BF16)]
    return pl.pallas_call(
        body, name=name, grid=(B, S // tb), in_specs=in_specs, out_specs=out_specs, out_shape=out_shape,
        compiler_params=_cp("arbitrary", "arbitrary"),
    )(*ins)


def _loss_head(x1, ff, target, ga2, name):
    B, S, D = x1.shape
    tb = _tile(S, ROW_BLOCK, SUBLANES)
    nb, ni = B, S // tb

    def body(x_ref, f_ref, t_ref, ga_ref, loss_ref, dy_ref, dff_ref, dga_ref, acc_ref):
        b, i = pl.program_id(0), pl.program_id(1)

        @pl.when(i == 0)
        def _():
            dga_ref[...] = jnp.zeros_like(dga_ref)

        @pl.when((i == 0) & (b == 0))
        def _():
            acc_ref[...] = jnp.zeros_like(acc_ref)

        fv = f_ref[...]
        gav = ga_ref[...]
        err = x_ref[...] + gav * fv - t_ref[...]
        acc_ref[...] += jnp.sum(err * err, axis=0, keepdims=True)
        dy = err * (1.0 / D)
        dy_ref[...] = dy
        dff_ref[...] = (dy * gav).astype(BF16)
        dga_ref[...] += jnp.sum(dy * fv, axis=0, keepdims=True)

        @pl.when((i == ni - 1) & (b == nb - 1))
        def _():
            loss_ref[...] = jnp.sum(acc_ref[...], axis=1, keepdims=True) * (0.5 / D)

    return pl.pallas_call(
        body, name=name, grid=(B, S // tb),
        in_specs=[_tok(tb, D), _tok(tb, D), _tok(tb, D), _per_example(D)],
        out_specs=[pl.BlockSpec((1, 1), lambda b, i: (0, 0)), _tok(tb, D), _tok(tb, D), _per_example(D)],
        out_shape=[SDS((1, 1), F32), SDS((B, S, D), F32), SDS((B, S, D), BF16), SDS((B, 1, D), F32)],
        scratch_shapes=[pltpu.VMEM((1, D), F32)],
        compiler_params=_cp("arbitrary", "arbitrary"),
    )(x1, ff, target, ga2)


def _causal_mask():
    t = lax.broadcasted_iota(jnp.int32, (CHUNK, CHUNK), 0)
    s = lax.broadcasted_iota(jnp.int32, (CHUNK, CHUNK), 1)
    return s <= t


def _gmlp_fwd(proj, g_v, w_s, b_col, D, GW, name):
    T = proj.shape[0]
    tb = _tile(T, ROW_BLOCK, CHUNK)
    gw = GW // GROUPS
    ucol = (2 * D) // GW

    def body(u_ref, v_ref, g_ref, w_ref, b_ref, ga_ref, r_ref):
        zu = _gelu(u_ref[...])
        zv = _gelu(v_ref[...])
        rstd = lax.rsqrt(jnp.mean(zv * zv, axis=1, keepdims=True) + EPS)
        vn = (zv * rstd * g_ref[...]).astype(BF16)
        mask = _causal_mask()
        for g in range(GROUPS):
            wm = jnp.where(mask, w_ref[g], 0.0).astype(BF16)
            cols = slice(g * gw, (g + 1) * gw)
            for ci in range(tb // CHUNK):
                rows = slice(ci * CHUNK, (ci + 1) * CHUNK)
                mixed = _dot(wm, vn[rows, cols], NN) + b_ref[g]
                ga_ref[rows, cols] = (zu[rows, cols] * mixed).astype(BF16)
        r_ref[...] = rstd

    return pl.pallas_call(
        body, name=name, grid=(T // tb,),
        in_specs=[pl.BlockSpec((tb, GW), lambda i: (i, ucol)), pl.BlockSpec((tb, GW), lambda i: (i, ucol + 1)),
                  pl.BlockSpec((1, GW), lambda i: (0, 0)), pl.BlockSpec((GROUPS, CHUNK, CHUNK), lambda i: (0, 0, 0)),
                  pl.BlockSpec((GROUPS, CHUNK, 1), lambda i: (0, 0, 0))],
        out_specs=[pl.BlockSpec((tb, GW), lambda i: (i, 0)), pl.BlockSpec((tb, 1), lambda i: (i, 0))],
        out_shape=[SDS((T, GW), BF16), SDS((T, 1), F32)],
        compiler_params=_cp("parallel"),
    )(proj, proj, g_v, w_s, b_col)


def _gmlp_bwd(proj, dga, rstd_v, g_v, w_s, b_col, sel, dproj, D, GW, name):
    T = proj.shape[0]
    tb = _tile(T, ROW_BLOCK, CHUNK)
    gw = GW // GROUPS
    ucol = (2 * D) // GW
    assert (2 * D) % (2 * GW) == 0
    uvcol = (2 * D) // (2 * GW)
    nsteps = T // tb

    def body(u_ref, v_ref, dga_ref, r_ref, g_ref, w_ref, b_ref, sel_ref, _dproj_in,
             duv_ref, gws_ref, gbs_ref, gg_ref, accb_ref, dvn_ref):
        step = pl.program_id(0)

        @pl.when(step == 0)
        def _():
            gws_ref[...] = jnp.zeros_like(gws_ref)
            gg_ref[...] = jnp.zeros_like(gg_ref)
            accb_ref[...] = jnp.zeros_like(accb_ref)

        uv = u_ref[...]
        vv = v_ref[...]
        zu = _gelu(uv)
        zv = _gelu(vv)
        rs = r_ref[...]
        gv = g_ref[...]
        vhat = zv * rs
        vnb = (vhat * gv).astype(BF16)
        dgav = dga_ref[...].astype(F32)
        du_gelu = _gelu_grad(uv)
        mask = _causal_mask()
        for g in range(GROUPS):
            wm = jnp.where(mask, w_ref[g], 0.0)
            wmb = wm.astype(BF16)
            wmtb = wm.T.astype(BF16)
            cols = slice(g * gw, (g + 1) * gw)
            for ci in range(tb // CHUNK):
                rows = slice(ci * CHUNK, (ci + 1) * CHUNK)
                vn_blk = vnb[rows, cols]
                mixed = _dot(wmb, vn_blk, NN) + b_ref[g]
                duv_ref[rows, cols] = (dgav[rows, cols] * mixed * du_gelu[rows, cols]).astype(BF16)
                dmix = dgav[rows, cols] * zu[rows, cols]
                accb_ref[:, cols] += dmix
                dmb = dmix.astype(BF16)
                gws_ref[g] += _dot(dmb, vn_blk, NT)
                dvn_ref[rows, cols] = _dot(wmtb, dmb, NN)
        dvn = dvn_ref[...]
        gg_ref[...] += jnp.sum(dvn * vhat, axis=0, keepdims=True)
        dvhat = dvn * gv
        cm = jnp.mean(dvhat * vhat, axis=1, keepdims=True)
        dzv = rs * (dvhat - vhat * cm)
        duv_ref[:, GW:] = (dzv * _gelu_grad(vv)).astype(BF16)

        @pl.when(step == nsteps - 1)
        def _():
            gbs_ref[...] = jnp.dot(accb_ref[...], sel_ref[...], precision=lax.Precision.HIGHEST,
                                   preferred_element_type=F32)
            for g in range(GROUPS):
                gws_ref[g] = jnp.where(mask, gws_ref[g], 0.0)

    return pl.pallas_call(
        body, name=name, grid=(nsteps,),
        in_specs=[pl.BlockSpec((tb, GW), lambda i: (i, ucol)), pl.BlockSpec((tb, GW), lambda i: (i, ucol + 1)),
                  pl.BlockSpec((tb, GW), lambda i: (i, 0)), pl.BlockSpec((tb, 1), lambda i: (i, 0)),
                  pl.BlockSpec((1, GW), lambda i: (0, 0)), pl.BlockSpec((GROUPS, CHUNK, CHUNK), lambda i: (0, 0, 0)),
                  pl.BlockSpec((GROUPS, CHUNK, 1), lambda i: (0, 0, 0)), pl.BlockSpec((GW, LANES), lambda i: (0, 0)),
                  pl.BlockSpec(memory_space=pl.ANY)],
        out_specs=[pl.BlockSpec((tb, 2 * GW), lambda i: (i, uvcol)), pl.BlockSpec((GROUPS, CHUNK, CHUNK), lambda i: (0, 0, 0)),
                   pl.BlockSpec((CHUNK, LANES), lambda i: (0, 0)), pl.BlockSpec((1, GW), lambda i: (0, 0))],
        out_shape=[SDS(dproj.shape, BF16), SDS((GROUPS, CHUNK, CHUNK), F32), SDS((CHUNK, LANES), F32), SDS((1, GW), F32)],
        scratch_shapes=[pltpu.VMEM((CHUNK, GW), F32), pltpu.VMEM((tb, GW), F32)],
        input_output_aliases={8: 0},
        compiler_params=_cp("arbitrary"),
    )(proj, proj, dga, rstd_v, g_v, w_s, b_col, sel, dproj)


def _mix_fwd(proj, y_a, y_b, D, name):
    T = proj.shape[0]
    tb = _tile(T, TOKEN_BLOCK, SUBLANES)
    td = _tile(D, COL_BLOCK, LANES)
    nd = D // td

    def body(ga_ref, gb_ref, ya_ref, yb_ref, o_ref):
        o_ref[...] = (_sigmoid(ga_ref[...]) * ya_ref[...] + _sigmoid(gb_ref[...]) * yb_ref[...]).astype(BF16)

    blk = pl.BlockSpec((tb, td), lambda i, j: (i, j))
    return pl.pallas_call(
        body, name=name, grid=(T // tb, nd),
        in_specs=[blk, pl.BlockSpec((tb, td), lambda i, j: (i, j + nd)), blk, blk],
        out_specs=blk, out_shape=SDS((T, D), BF16),
        compiler_params=_cp("parallel", "parallel"),
    )(proj, proj, y_a, y_b)


def _mix_bwd(proj, y_a, y_b, dmixed, D, name):
    T, width = proj.shape
    tb = _tile(T, ROW_BLOCK, SUBLANES)

    def body(g_ref, ya_ref, yb_ref, dm_ref, dya_ref, dyb_ref, dp_ref):
        dm = dm_ref[...].astype(F32)
        sa = _sigmoid(g_ref[:, :D])
        sb = _sigmoid(g_ref[:, D:])
        dya_ref[...] = (dm * sa).astype(BF16)
        dyb_ref[...] = (dm * sb).astype(BF16)
        dp_ref[:, :D] = (dm * ya_ref[...] * sa * (1.0 - sa)).astype(BF16)
        dp_ref[:, D:] = (dm * yb_ref[...] * sb * (1.0 - sb)).astype(BF16)

    blk = pl.BlockSpec((tb, D), lambda i: (i, 0))
    gates = pl.BlockSpec((tb, 2 * D), lambda i: (i, 0))
    return pl.pallas_call(
        body, name=name, grid=(T // tb,),
        in_specs=[gates, blk, blk, blk],
        out_specs=[blk, blk, gates], out_shape=[SDS((T, D), BF16), SDS((T, D), BF16), SDS((T, width), BF16)],
        compiler_params=_cp("parallel"),
    )(proj, y_a, y_b, dmixed)


def _lat_norm_fwd(proj, g_q, g_kv, D, GW, QL, KVL, name):
    T = proj.shape[0]
    tb = _tile(T, TOKEN_BLOCK, SUBLANES)
    qcol = (2 * D + 2 * GW) // QL
    kvcol = (2 * D + 2 * GW + QL) // KVL

    def body(q_ref, kv_ref, gq_ref, gkv_ref, qn_ref, kvn_ref, rq_ref, rkv_ref):
        qv = q_ref[...]
        rq = lax.rsqrt(jnp.mean(qv * qv, axis=1, keepdims=True) + EPS)
        qn_ref[...] = (qv * rq * gq_ref[...]).astype(BF16)
        rq_ref[...] = rq
        kv = kv_ref[...]
        rkv = lax.rsqrt(jnp.mean(kv * kv, axis=1, keepdims=True) + EPS)
        kvn_ref[...] = (kv * rkv * gkv_ref[...]).astype(BF16)
        rkv_ref[...] = rkv

    return pl.pallas_call(
        body, name=name, grid=(T // tb,),
        in_specs=[pl.BlockSpec((tb, QL), lambda i: (i, qcol)), pl.BlockSpec((tb, KVL), lambda i: (i, kvcol)),
                  pl.BlockSpec((1, QL), lambda i: (0, 0)), pl.BlockSpec((1, KVL), lambda i: (0, 0))],
        out_specs=[pl.BlockSpec((tb, QL), lambda i: (i, 0)), pl.BlockSpec((tb, KVL), lambda i: (i, 0)),
                   pl.BlockSpec((tb, 1), lambda i: (i, 0)), pl.BlockSpec((tb, 1), lambda i: (i, 0))],
        out_shape=[SDS((T, QL), BF16), SDS((T, KVL), BF16), SDS((T, 1), F32), SDS((T, 1), F32)],
        compiler_params=_cp("parallel"),
    )(proj, proj, g_q, g_kv)


def _lat_norm_bwd(proj, dqn, dkvn, dkpe, rq, rkv, g_q, g_kv, dproj, D, GW, QL, KVL, name):
    T, width = proj.shape
    tb = _tile(T, TOKEN_BLOCK, SUBLANES)
    qcol = (2 * D + 2 * GW) // QL
    kvcol = (2 * D + 2 * GW + QL) // KVL
    tail = width - (2 * D + 2 * GW)
    assert (2 * D + 2 * GW) % tail == 0 and tail >= QL + KVL + LANES
    tailcol = (2 * D + 2 * GW) // tail

    def one(xv, rs, gv, dy, gg_ref):
        xhat = xv * rs
        gg_ref[...] += jnp.sum(dy * xhat, axis=0, keepdims=True)
        dxhat = dy * gv
        cm = jnp.mean(dxhat * xhat, axis=1, keepdims=True)
        return rs * (dxhat - xhat * cm)

    def body(q_ref, kv_ref, dqn_ref, dkvn_ref, dkpe_ref, rq_ref, rkv_ref, gq_ref, gkv_ref, _dproj_in,
             tail_ref, ggq_ref, ggkv_ref):
        @pl.when(pl.program_id(0) == 0)
        def _():
            ggq_ref[...] = jnp.zeros_like(ggq_ref)
            ggkv_ref[...] = jnp.zeros_like(ggkv_ref)

        tail_ref[:, :QL] = one(q_ref[...], rq_ref[...], gq_ref[...], dqn_ref[...], ggq_ref).astype(BF16)
        tail_ref[:, QL:QL + KVL] = one(kv_ref[...], rkv_ref[...], gkv_ref[...], dkvn_ref[...], ggkv_ref).astype(BF16)
        tail_ref[:, QL + KVL:QL + KVL + LANES] = dkpe_ref[...].astype(BF16)
        if tail > QL + KVL + LANES:
            tail_ref[:, QL + KVL + LANES:] = jnp.zeros((tb, tail - (QL + KVL + LANES)), BF16)

    return pl.pallas_call(
        body, name=name, grid=(T // tb,),
        in_specs=[pl.BlockSpec((tb, QL), lambda i: (i, qcol)), pl.BlockSpec((tb, KVL), lambda i: (i, kvcol)),
                  pl.BlockSpec((tb, QL), lambda i: (i, 0)), pl.BlockSpec((tb, KVL), lambda i: (i, 0)),
                  pl.BlockSpec((tb, LANES), lambda i: (i, 0)),
                  pl.BlockSpec((tb, 1), lambda i: (i, 0)), pl.BlockSpec((tb, 1), lambda i: (i, 0)),
                  pl.BlockSpec((1, QL), lambda i: (0, 0)), pl.BlockSpec((1, KVL), lambda i: (0, 0)),
                  pl.BlockSpec(memory_space=pl.ANY)],
        out_specs=[pl.BlockSpec((tb, tail), lambda i: (i, tailcol)),
                   pl.BlockSpec((1, QL), lambda i: (0, 0)), pl.BlockSpec((1, KVL), lambda i: (0, 0))],
        out_shape=[SDS(dproj.shape, BF16), SDS((1, QL), F32), SDS((1, KVL), F32)],
        input_output_aliases={9: 0},
        compiler_params=_cp("arbitrary"),
    )(proj, proj, dqn, dkvn, dkpe, rq, rkv, g_q, g_kv, dproj)


def _swap_halves(r):
    lane = lax.broadcasted_iota(jnp.int32, r.shape, 1)
    lo = pltpu.roll(r, LANES - HALF_ROPE, 1)
    hi = pltpu.roll(r, HALF_ROPE, 1)
    return jnp.where(lane < HALF_ROPE, lo, jnp.where(lane < ROPE, hi, 0.0))


def _rope_fwd(r, cos_t, sin_t):
    return r * cos_t + _swap_halves(r) * sin_t


def _rope_bwd(d, cos_t, sin_t):
    return d * cos_t + _swap_halves(d * sin_t)


def _head_norm_bwd(xn, xr, rs, g_n, g_r, dyn, dyr):
    xhn, xhr = xn * rs, xr * rs
    dn, dr = dyn * g_n, dyr * g_r
    cm = (jnp.sum(dn * xhn, axis=1, keepdims=True) + jnp.sum(dr * xhr, axis=1, keepdims=True)) * (1.0 / QK_HEAD)
    return rs * (dn - xhn * cm), rs * (dr - xhr * cm), dyn * xhn, dyr * xhr


def _q_up_fwd(qn, w_uq_al, g_al, cos_t, sin_t, name):
    T, QL = qn.shape
    HP = w_uq_al.shape[1]
    tb = _tile(T, TOKEN_BLOCK, SUBLANES)
    hw = _tile(HP, COL_BLOCK, HEAD_PAD)

    def body(x_ref, w_ref, g_ref, c_ref, s_ref, o_ref):
        raw = _dot(x_ref[...], w_ref[...], NN)
        gv = g_ref[...]
        cv, sv = c_ref[...], s_ref[...]
        for h in range(hw // HEAD_PAD):
            xn = raw[:, h * HEAD_PAD:h * HEAD_PAD + LANES]
            xr = raw[:, h * HEAD_PAD + LANES:(h + 1) * HEAD_PAD]
            ss = jnp.sum(xn * xn, axis=1, keepdims=True) + jnp.sum(xr * xr, axis=1, keepdims=True)
            rs = lax.rsqrt(ss * (1.0 / QK_HEAD) + EPS)
            o_ref[:, h * HEAD_PAD:h * HEAD_PAD + LANES] = (xn * rs * gv[:, :LANES]).astype(BF16)
            o_ref[:, h * HEAD_PAD + LANES:(h + 1) * HEAD_PAD] = _rope_fwd(xr * rs * gv[:, LANES:], cv, sv).astype(BF16)

    return pl.pallas_call(
        body, name=name, grid=(T // tb, HP // hw),
        in_specs=[pl.BlockSpec((tb, QL), lambda i, j: (i, 0)), pl.BlockSpec((QL, hw), lambda i, j: (0, j)),
                  pl.BlockSpec((1, HEAD_PAD), lambda i, j: (0, 0)),
                  pl.BlockSpec((tb, LANES), lambda i, j: (i, 0)), pl.BlockSpec((tb, LANES), lambda i, j: (i, 0))],
        out_specs=pl.BlockSpec((tb, hw), lambda i, j: (i, j)), out_shape=SDS((T, HP), BF16),
        compiler_params=_cp("parallel", "parallel"),
    )(qn, w_uq_al, g_al, cos_t, sin_t)


def _q_up_bwd(qn, w_uq_al, g_al, cos_t, sin_t, dq, name):
    T, QL = qn.shape
    HP = w_uq_al.shape[1]
    tb = _tile(T, TOKEN_BLOCK, SUBLANES)
    hw = _tile(HP, COL_BLOCK, HEAD_PAD)

    def body(x_ref, w_ref, g_ref, c_ref, s_ref, dq_ref, o_ref, gg_ref):
        @pl.when((pl.program_id(0) == 0) & (pl.program_id(1) == 0))
        def _():
            gg_ref[...] = jnp.zeros_like(gg_ref)

        raw = _dot(x_ref[...], w_ref[...], NN)
        gv = g_ref[...]
        cv, sv = c_ref[...], s_ref[...]
        for h in range(hw // HEAD_PAD):
            lo, mid, hi = h * HEAD_PAD, h * HEAD_PAD + LANES, (h + 1) * HEAD_PAD
            xn, xr = raw[:, lo:mid], raw[:, mid:hi]
            ss = jnp.sum(xn * xn, axis=1, keepdims=True) + jnp.sum(xr * xr, axis=1, keepdims=True)
            rs = lax.rsqrt(ss * (1.0 / QK_HEAD) + EPS)
            dyn = dq_ref[:, lo:mid]
            dyr = _rope_bwd(dq_ref[:, mid:hi], cv, sv)
            dxn, dxr, ggn, ggr = _head_norm_bwd(xn, xr, rs, gv[:, :LANES], gv[:, LANES:], dyn, dyr)
            o_ref[:, lo:mid] = dxn.astype(BF16)
            o_ref[:, mid:hi] = dxr.astype(BF16)
            gg_ref[:, :LANES] += jnp.sum(ggn, axis=0, keepdims=True)
            gg_ref[:, LANES:] += jnp.sum(ggr, axis=0, keepdims=True)

    return pl.pallas_call(
        body, name=name, grid=(T // tb, HP // hw),
        in_specs=[pl.BlockSpec((tb, QL), lambda i, j: (i, 0)), pl.BlockSpec((QL, hw), lambda i, j: (0, j)),
                  pl.BlockSpec((1, HEAD_PAD), lambda i, j: (0, 0)),
                  pl.BlockSpec((tb, LANES), lambda i, j: (i, 0)), pl.BlockSpec((tb, LANES), lambda i, j: (i, 0)),
                  pl.BlockSpec((tb, hw), lambda i, j: (i, j))],
        out_specs=[pl.BlockSpec((tb, hw), lambda i, j: (i, j)), pl.BlockSpec((1, HEAD_PAD), lambda i, j: (0, 0))],
        out_shape=[SDS((T, HP), BF16), SDS((1, HEAD_PAD), F32)],
        compiler_params=_cp("arbitrary", "arbitrary"),
    )(qn, w_uq_al, g_al, cos_t, sin_t, dq)


def _kv_up_fwd(kvn, w_ukv, proj, g_al, cos_t, sin_t, kpe_col, name):
    T, KVL = kvn.shape
    n_shard = w_ukv.shape[2]
    HP = N_DEV * n_shard
    tb = _tile(T, TOKEN_BLOCK, SUBLANES)
    hw = _tile(n_shard, COL_BLOCK, HEAD_PAD)
    per = n_shard // hw
    nh = hw // HEAD_PAD

    def body(x_ref, w_ref, kpe_ref, g_ref, c_ref, s_ref, k_ref, v_ref):
        raw = _dot(x_ref[...], w_ref[...], NN)
        gv = g_ref[...]
        cv, sv = c_ref[...], s_ref[...]
        kpe = kpe_ref[...]
        kpe_ss = jnp.sum(kpe * kpe, axis=1, keepdims=True)
        for h in range(nh):
            lo, mid, hi = h * HEAD_PAD, h * HEAD_PAD + LANES, (h + 1) * HEAD_PAD
            xn = raw[:, lo:mid]
            rs = lax.rsqrt((jnp.sum(xn * xn, axis=1, keepdims=True) + kpe_ss) * (1.0 / QK_HEAD) + EPS)
            k_ref[:, lo:mid] = (xn * rs * gv[:, :LANES]).astype(BF16)
            k_ref[:, mid:hi] = _rope_fwd(kpe * rs * gv[:, LANES:], cv, sv).astype(BF16)
            v_ref[:, h * V_HEAD:(h + 1) * V_HEAD] = raw[:, mid:hi].astype(BF16)

    return pl.pallas_call(
        body, name=name, grid=(T // tb, HP // hw),
        in_specs=[pl.BlockSpec((tb, KVL), lambda i, j: (i, 0)),
                  pl.BlockSpec((None, KVL, hw), lambda i, j: (j // per, 0, j % per)),
                  pl.BlockSpec((tb, LANES), lambda i, j: (i, kpe_col)),
                  pl.BlockSpec((1, HEAD_PAD), lambda i, j: (0, 0)),
                  pl.BlockSpec((tb, LANES), lambda i, j: (i, 0)), pl.BlockSpec((tb, LANES), lambda i, j: (i, 0))],
        out_specs=[pl.BlockSpec((tb, hw), lambda i, j: (i, j)), pl.BlockSpec((tb, nh * V_HEAD), lambda i, j: (i, j))],
        out_shape=[SDS((T, HP), BF16), SDS((T, HP // 2), BF16)],
        compiler_params=_cp("parallel", "parallel"),
    )(kvn, w_ukv, proj, g_al, cos_t, sin_t)


def _kv_up_bwd(kvn, w_ukv, proj, g_al, cos_t, sin_t, dk, dv, kpe_col, name):
    T, KVL = kvn.shape
    n_shard = w_ukv.shape[2]
    HP = N_DEV * n_shard
    tb = _tile(T, TOKEN_BLOCK, SUBLANES)
    hw = _tile(n_shard, COL_BLOCK, HEAD_PAD)
    per = n_shard // hw
    nh = hw // HEAD_PAD

    def body(x_ref, w_ref, kpe_ref, g_ref, c_ref, s_ref, dk_ref, dv_ref, o_ref, dkpe_ref, gg_ref):
        i, j = pl.program_id(0), pl.program_id(1)

        @pl.when((i == 0) & (j == 0))
        def _():
            gg_ref[...] = jnp.zeros_like(gg_ref)

        @pl.when(j == 0)
        def _():
            dkpe_ref[...] = jnp.zeros_like(dkpe_ref)

        raw = _dot(x_ref[...], w_ref[...], NN)
        gv = g_ref[...]
        cv, sv = c_ref[...], s_ref[...]
        kpe = kpe_ref[...]
        kpe_ss = jnp.sum(kpe * kpe, axis=1, keepdims=True)
        for h in range(nh):
            lo, mid, hi = h * HEAD_PAD, h * HEAD_PAD + LANES, (h + 1) * HEAD_PAD
            xn = raw[:, lo:mid]
            rs = lax.rsqrt((jnp.sum(xn * xn, axis=1, keepdims=True) + kpe_ss) * (1.0 / QK_HEAD) + EPS)
            dyn = dk_ref[:, lo:mid]
            dyr = _rope_bwd(dk_ref[:, mid:hi], cv, sv)
            dxn, dxr, ggn, ggr = _head_norm_bwd(xn, kpe, rs, gv[:, :LANES], gv[:, LANES:], dyn, dyr)
            o_ref[:, lo:mid] = dxn.astype(BF16)
            o_ref[:, mid:hi] = dv_ref[:, h * V_HEAD:(h + 1) * V_HEAD].astype(BF16)
            dkpe_ref[...] += dxr
            gg_ref[:, :LANES] += jnp.sum(ggn, axis=0, keepdims=True)
            gg_ref[:, LANES:] += jnp.sum(ggr, axis=0, keepdims=True)

    return pl.pallas_call(
        body, name=name, grid=(T // tb, HP // hw),
        in_specs=[pl.BlockSpec((tb, KVL), lambda i, j: (i, 0)),
                  pl.BlockSpec((None, KVL, hw), lambda i, j: (j // per, 0, j % per)),
                  pl.BlockSpec((tb, LANES), lambda i, j: (i, kpe_col)),
                  pl.BlockSpec((1, HEAD_PAD), lambda i, j: (0, 0)),
                  pl.BlockSpec((tb, LANES), lambda i, j: (i, 0)), pl.BlockSpec((tb, LANES), lambda i, j: (i, 0)),
                  pl.BlockSpec((tb, hw), lambda i, j: (i, j)), pl.BlockSpec((tb, nh * V_HEAD), lambda i, j: (i, j))],
        out_specs=[pl.BlockSpec((tb, hw), lambda i, j: (i, j)), pl.BlockSpec((tb, LANES), lambda i, j: (i, 0)),
                   pl.BlockSpec((1, HEAD_PAD), lambda i, j: (0, 0))],
        out_shape=[SDS((T, HP), BF16), SDS((T, LANES), F32), SDS((1, HEAD_PAD), F32)],
        compiler_params=_cp("arbitrary", "arbitrary"),
    )(kvn, w_ukv, proj, g_al, cos_t, sin_t, dk, dv)


def _attn_fwd(q, k, v, H, name, comm=None):
    B, S, _ = q.shape
    tq = _tile(S, ATTN_BLOCK, LANES)
    scale = QK_HEAD ** -0.5

    def body(q_ref, k_ref, v_ref, o_ref, l_ref):
        qi = pl.program_id(2)
        qv = q_ref[...]

        def step(j, carry, diagonal):
            m, l, acc = carry
            off = pl.multiple_of(j * tq, tq)
            kj = k_ref[pl.ds(off, tq), :]
            vj = v_ref[pl.ds(off, tq), :]
            s = _dot(qv, kj, NT) * scale
            if diagonal:
                row = lax.broadcasted_iota(jnp.int32, (tq, tq), 0)
                col = lax.broadcasted_iota(jnp.int32, (tq, tq), 1)
                s = jnp.where(col <= row, s, -jnp.inf)
            m_new = jnp.maximum(m, jnp.max(s, axis=1, keepdims=True))
            alpha = jnp.exp(m - m_new)
            p = jnp.exp(s - m_new)
            l = alpha * l + jnp.sum(p, axis=1, keepdims=True)
            acc = alpha * acc + _dot(p.astype(BF16), vj, NN)
            return m_new, l, acc

        init = (jnp.full((tq, 1), -1e30, F32), jnp.zeros((tq, 1), F32), jnp.zeros((tq, V_HEAD), F32))
        carry = lax.fori_loop(0, qi, functools.partial(step, diagonal=False), init)
        m, l, acc = step(qi, carry, diagonal=True)
        o_ref[...] = (acc / l).astype(BF16)
        l_ref[...] = m + jnp.log(l)

    grid = (B, H, S // tq)
    in_specs = [pl.BlockSpec((None, tq, HEAD_PAD), lambda b, h, i: (b, i, h)),
                pl.BlockSpec((None, S, HEAD_PAD), lambda b, h, i: (b, 0, h)),
                pl.BlockSpec((None, S, V_HEAD), lambda b, h, i: (b, 0, h))]
    out_specs = [pl.BlockSpec((None, tq, V_HEAD), lambda b, h, i: (b, i, h)),
                 pl.BlockSpec((None, None, tq, 1), lambda b, h, i: (b, h, i, 0))]
    out_shape = [SDS((B, S, H * V_HEAD), BF16), SDS((B, H, S, 1), F32)]
    if comm is None:
        return pl.pallas_call(
            body, name=name, grid=grid, in_specs=in_specs, out_specs=out_specs, out_shape=out_shape,
            compiler_params=_cp("parallel", "parallel", "parallel"),
        )(q, k, v)

    def first_last():
        b, h, i = pl.program_id(0), pl.program_id(1), pl.program_id(2)
        return (b == 0) & (h == 0) & (i == 0), (b == B - 1) & (h == H - 1) & (i == grid[2] - 1)

    outs = pl.pallas_call(
        _carry(body, comm, 3, 2, 0, first_last), name=name, grid=grid,
        in_specs=in_specs + [HBM_SPEC] * len(comm.ins), out_specs=out_specs + [HBM_SPEC] * len(comm.out_shape),
        out_shape=out_shape + comm.out_shape, scratch_shapes=comm.scratch,
        compiler_params=_cp("arbitrary", "arbitrary", "arbitrary"),
    )(q, k, v, *comm.ins)
    return outs[0], outs[1], comm.results(outs[2:])


def _attn_bwd(q, k, v, o, lse, do, H, name, comm=None):
    B, S, _ = q.shape
    tq = _tile(S, ATTN_BWD_BLOCK, LANES)
    nq = S // tq
    scale = QK_HEAD ** -0.5

    def body(q_ref, k_ref, v_ref, o_ref, l_ref, do_ref, dq_ref, dk_ref, dv_ref, delta_ref, dq_acc):
        dq_acc[...] = jnp.zeros_like(dq_acc)
        for i in range(nq):
            rows = slice(i * tq, (i + 1) * tq)
            delta_ref[rows, :] = jnp.sum(do_ref[rows, :].astype(F32) * o_ref[rows, :].astype(F32), axis=1, keepdims=True)
        def kv_step(j, carry):
            offk = pl.multiple_of(j * tq, tq)
            kj = k_ref[pl.ds(offk, tq), :]
            vj = v_ref[pl.ds(offk, tq), :]

            def q_step(i, acc, diagonal):
                dk_acc, dv_acc = acc
                offq = pl.multiple_of(i * tq, tq)
                qi = q_ref[pl.ds(offq, tq), :]
                doi = do_ref[pl.ds(offq, tq), :]
                s = _dot(qi, kj, NT) * scale
                p = jnp.exp(s - l_ref[pl.ds(offq, tq), :])
                if diagonal:
                    row = lax.broadcasted_iota(jnp.int32, (tq, tq), 0)
                    col = lax.broadcasted_iota(jnp.int32, (tq, tq), 1)
                    p = jnp.where(col <= row, p, 0.0)
                dv_acc = dv_acc + _dot(p.astype(BF16), doi, TN)
                dp = _dot(doi, vj, NT)
                ds = (p * (dp - delta_ref[pl.ds(offq, tq), :]) * scale).astype(BF16)
                dk_acc = dk_acc + _dot(ds, qi, TN)
                dq_acc[pl.ds(offq, tq), :] += _dot(ds, kj, NN)
                return dk_acc, dv_acc

            acc = q_step(j, (jnp.zeros((tq, HEAD_PAD), F32), jnp.zeros((tq, V_HEAD), F32)), diagonal=True)
            dk_acc, dv_acc = lax.fori_loop(j + 1, nq, functools.partial(q_step, diagonal=False), acc)
            dk_ref[pl.ds(offk, tq), :] = dk_acc.astype(BF16)
            dv_ref[pl.ds(offk, tq), :] = dv_acc.astype(BF16)
            return carry

        lax.fori_loop(0, nq, kv_step, 0)
        dq_ref[...] = dq_acc[...].astype(BF16)

    qk_spec = pl.BlockSpec((None, S, HEAD_PAD), lambda b, h: (b, 0, h))
    v_spec = pl.BlockSpec((None, S, V_HEAD), lambda b, h: (b, 0, h))
    in_specs = [qk_spec, qk_spec, v_spec, v_spec, pl.BlockSpec((None, None, S, 1), lambda b, h: (b, h, 0, 0)), v_spec]
    out_specs = [qk_spec, qk_spec, v_spec]
    out_shape = [SDS((B, S, H * HEAD_PAD), BF16), SDS((B, S, H * HEAD_PAD), BF16), SDS((B, S, H * V_HEAD), BF16)]
    scratch = [pltpu.VMEM((S, 1), F32), pltpu.VMEM((S, HEAD_PAD), F32)]
    if comm is None:
        return pl.pallas_call(
            body, name=name, grid=(B, H), in_specs=in_specs, out_specs=out_specs, out_shape=out_shape,
            scratch_shapes=scratch, compiler_params=_cp("parallel", "parallel"),
        )(q, k, v, o, lse, do)

    def first_last():
        b, h = pl.program_id(0), pl.program_id(1)
        return (b == 0) & (h == 0), (b == B - 1) & (h == H - 1)

    outs = pl.pallas_call(
        _carry(body, comm, 6, 3, 2, first_last), name=name, grid=(B, H),
        in_specs=in_specs + [HBM_SPEC] * len(comm.ins), out_specs=out_specs + [HBM_SPEC] * len(comm.out_shape),
        out_shape=out_shape + comm.out_shape, scratch_shapes=scratch + comm.scratch,
        compiler_params=_cp("arbitrary", "arbitrary"),
    )(q, k, v, o, lse, do, *comm.ins)
    return outs[0], outs[1], outs[2], comm.results(outs[3:])


def _natural(sm):
    nd, R, n = sm.shape
    return jnp.transpose(sm, (1, 0, 2)).reshape(R, nd * n)


def _col_shards(full):
    R, N = full.shape
    return jnp.transpose(full.reshape(R, N_DEV, N // N_DEV), (1, 0, 2))


def _shard_rows(sm, lo, hi, n):
    out = []
    for j in range(N_DEV):
        a, b = max(lo, j * n), min(hi, (j + 1) * n)
        if a < b:
            out.append(sm[j, a - j * n:b - j * n])
    return out


def _al_rows(g_t, lo, hi, off_gate, n_gate):
    parts = []
    if lo < off_gate:
        parts.append(g_t[n_gate + lo:n_gate + min(hi, off_gate)])
    if hi > off_gate:
        parts.append(g_t[max(lo, off_gate) - off_gate:hi - off_gate])
    return parts[0] if len(parts) == 1 else jnp.concatenate(parts, axis=0)


def _pad_heads(w, H):
    R = w.shape[0]
    return jnp.pad(w.reshape(R, H, QK_HEAD), ((0, 0), (0, 0), (0, HEAD_PAD - QK_HEAD))).reshape(R, H * HEAD_PAD)


def _unpad_heads(w, H):
    R = w.shape[0]
    return w.reshape(R, H, HEAD_PAD)[:, :, :QK_HEAD].reshape(R, H * QK_HEAD)


def _pad_gain(g):
    return jnp.pad(g, ((0, 0), (0, HEAD_PAD - QK_HEAD)))


def _rope_tables(positions):
    inv_freq = 1.0 / (ROPE_THETA ** (jnp.arange(0, ROPE, 2, dtype=F32) / ROPE))
    ang = positions.astype(F32).reshape(-1, 1) * inv_freq
    cos, sin = jnp.cos(ang), jnp.sin(ang)
    zeros = jnp.zeros((ang.shape[0], LANES - ROPE), F32)
    return jnp.concatenate([cos, cos, zeros], axis=1), jnp.concatenate([-sin, sin, zeros], axis=1)


def kernel(x, c, positions, w_ada, b_ada, g_norm1, w_in, g_v, w_s, b_s, g_q_lat, g_kv_lat, w_uq, w_ukv, g_qn, g_kn, w_branch_a, w_branch_b, w_out, g_norm2, w_ff1, w_ff2, loss_target, m_w_ada, m_b_ada, m_g_norm1, m_w_in, m_g_v, m_w_s, m_b_s, m_g_q_lat, m_g_kv_lat, m_w_uq, m_w_ukv, m_g_qn, m_g_kn, m_w_branch_a, m_w_branch_b, m_w_out, m_g_norm2, m_w_ff1, m_w_ff2, v_w_ada, v_b_ada, v_g_norm1, v_w_in, v_g_v, v_w_s, v_b_s, v_g_q_lat, v_g_kv_lat, v_w_uq, v_w_ukv, v_g_qn, v_g_kn, v_w_branch_a, v_w_branch_b, v_w_out, v_g_norm2, v_w_ff1, v_w_ff2):
    B, S, D = x.shape
    T = B * S
    GW = g_v.shape[-1]
    QL = g_q_lat.shape[-1]
    KVL = g_kv_lat.shape[-1]
    H = w_uq.shape[-1] * N_DEV // QK_HEAD
    IN = w_in.shape[-1] * N_DEV
    OFF_GATE = IN - 2 * D
    IN_AL = _round_up(2 * D + OFF_GATE + (LANES - ROPE), 512)
    assert OFF_GATE == 2 * GW + QL + KVL + ROPE
    assert (2 * D) % GW == 0 and (2 * D + 2 * GW) % QL == 0 and (2 * D + 2 * GW + QL) % KVL == 0
    kpe_col = (2 * D + 2 * GW + QL + KVL) // LANES

    xi, yi, ci = _here()
    dev = 4 * xi + 2 * yi + ci
    c_idx = jnp.reshape(ci, (1,)).astype(jnp.int32)

    big = [w_in, w_uq, w_ukv, w_branch_a, w_branch_b, w_out, w_ff1, w_ff2]
    s_uq, s_ukv, s_ba, s_bb, s_out, s_ff1, s_ff2 = [w[0].astype(BF16) for w in big[1:]]
    n_in = IN // N_DEV
    s_in_t = jnp.transpose(w_in[0]).astype(BF16)

    n_ada = w_ada.shape[-1]
    c_all = _all_gather_small(c, "ag_c").reshape(N_DEV * B, D)
    b_cols = lax.dynamic_slice(b_ada, (0, dev * n_ada), (1, n_ada))
    mod_cols = _ada_fwd(c_all, w_ada[0], b_cols, "ada_fwd")
    mod_all = _all_gather_small(mod_cols, "ag_mod")
    mod_mine = lax.dynamic_slice(mod_all, (0, dev * B, 0), (N_DEV, B, n_ada))
    mod_mine = jnp.transpose(mod_mine, (1, 0, 2)).reshape(B, 6, 1, D)
    sh1, sc1, ga1, sh2, sc2, ga2 = [mod_mine[:, k] for k in range(6)]

    cos_t, sin_t = _rope_tables(positions)
    g_qn_al, g_kn_al = _pad_gain(g_qn), _pad_gain(g_kn)
    b_col = b_s[0].reshape(GROUPS, CHUNK, 1)
    gw = GW // GROUPS
    sel = (jnp.arange(GW)[:, None] // gw == jnp.arange(LANES)[None, :]).astype(F32)

    h1, rstd1, ((g_in_t,),) = _norm_mod_fwd(x, g_norm1, sh1, sc1, "norm1_fwd", comm=_Comm([_GatherJob([s_in_t])]))
    w_al_t = jnp.concatenate(_shard_rows(g_in_t, OFF_GATE, IN, n_in) + _shard_rows(g_in_t, 0, OFF_GATE, n_in)
                             + [jnp.zeros((IN_AL - IN, D), BF16)], axis=0)
    h1f = h1.reshape(T, D)
    proj, ((g_uq, g_ukv, g_ba, g_bb, g_out),) = _matmul(
        h1f, w_al_t, mode="nt", out_dtypes=[F32], name="mm_proj", comm=_Comm([_GatherJob([s_uq, s_ukv, s_ba, s_bb, s_out])]))
    w_uq_al = _pad_heads(_natural(g_uq), H)
    w_bb_f = g_bb.reshape(-1, D)
    w_out_f = g_out.reshape(-1, D)
    ga_act, rstd_v = _gmlp_fwd(proj, g_v, w_s[0], b_col, D, GW, "gmlp_fwd")
    y_a = _matmul(ga_act, g_ba, mode="nn", out_dtypes=[BF16], name="mm_ya", b_shards=True)
    qn, kvn, rstd_q, rstd_kv = _lat_norm_fwd(proj, g_q_lat, g_kv_lat, D, GW, QL, KVL, "latnorm_fwd")
    q_al = _q_up_fwd(qn, w_uq_al, g_qn_al, cos_t, sin_t, "q_up_fwd")
    k_al, v_al = _kv_up_fwd(kvn, g_ukv, proj, g_kn_al, cos_t, sin_t, kpe_col, "kv_up_fwd")
    q3, k3, v3 = q_al.reshape(B, S, -1), k_al.reshape(B, S, -1), v_al.reshape(B, S, -1)
    attn, lse, ((g_ff1,),) = _attn_fwd(q3, k3, v3, H, "attn_fwd", comm=_Comm([_GatherJob([s_ff1])]))
    attn_f = attn.reshape(T, H * V_HEAD)
    y_b = _matmul(attn_f, w_bb_f, mode="nn", out_dtypes=[BF16], name="mm_yb")
    mixed = _mix_fwd(proj, y_a, y_b, D, "mix_fwd")
    o = _matmul(mixed, w_out_f, mode="nn", out_dtypes=[F32], name="mm_o")
    x1, h2, rstd2 = _norm_mod_fwd(x, g_norm2, sh2, sc2, "norm2_fwd", o=o.reshape(B, S, D), ga=ga1)

    def relu_sq(acc):
        r = jnp.maximum(acc, 0.0)
        return r * r, r

    (a_act, r_act), ((g_ff2,),) = _matmul(h2.reshape(T, D), g_ff1, mode="nn", out_dtypes=[BF16, BF16], name="mm_ff1",
                                          epi=relu_sq, comm=_Comm([_GatherJob([s_ff2])]), b_shards=True)
    w_ff2_f = g_ff2.reshape(-1, D)
    ff = _matmul(a_act, w_ff2_f, mode="nn", out_dtypes=[F32], name="mm_ff2")
    loss_part, dy, dff, d_ga2 = _loss_head(x1, ff.reshape(B, S, D), loss_target, ga2, "loss_head")
    loss = lax.psum(loss_part[0, 0], ("x", "y", "c"))

    dff_f = dff.reshape(T, D)
    df1 = _matmul(dff_f, w_ff2_f, mode="nt", out_dtypes=[BF16], name="mm_da", epi=lambda acc, r: (acc * (2.0 * r.astype(F32)),),
                  extras=(r_act,))
    gs_ff2 = _matmul(a_act, dff_f, mode="tn", out_dtypes=[BF16], name="mm_gw_ff2").reshape(N_DEV, -1, D)
    gs_ff1, ((sib_ff2,),) = _matmul(h2.reshape(T, D), df1, mode="tn", out_dtypes=[BF16], name="mm_gw_ff1",
                                    comm=_Comm([_SiblingJob([gs_ff2])]), out_shards=True)
    cs_ff2 = _chip_sum(gs_ff2, sib_ff2, c_idx, "chip_sum_w_ff2")
    dh2, ((parts_ff2,), (sib_ff1,)) = _matmul(df1, g_ff1, mode="nt", out_dtypes=[F32], name="mm_dh2", b_shards=True,
                                              comm=_Comm([_ChipsJob([cs_ff2]), _SiblingJob([gs_ff1])]))
    cs_ff1 = _chip_sum(gs_ff1, sib_ff1, c_idx, "chip_sum_w_ff1")
    dx1, d_sh2, d_sc2, gg_norm2, d_ga1, do = _norm_mod_bwd(
        dh2.reshape(B, S, D), x1, rstd2, sc2, g_norm2, dy, "norm2_bwd", o=o.reshape(B, S, D), ga=ga1)

    do_f = do.reshape(T, D)
    dmixed = _matmul(do_f, w_out_f, mode="nt", out_dtypes=[BF16], name="mm_dmixed")
    gw_out = _matmul(mixed, do_f, mode="tn", out_dtypes=[BF16], name="mm_gw_out")
    dy_a, dy_b, dproj = _mix_bwd(proj, y_a, y_b, dmixed, D, "mix_bwd")
    gs_ba = _matmul(ga_act, dy_a, mode="tn", out_dtypes=[BF16], name="mm_gw_ba", out_shards=True)
    dga_act = _matmul(dy_a, g_ba, mode="nt", out_dtypes=[BF16], name="mm_dga", b_shards=True)
    gw_bb = _matmul(attn_f, dy_b, mode="tn", out_dtypes=[BF16], name="mm_gw_bb")
    dattn = _matmul(dy_b, w_bb_f, mode="nt", out_dtypes=[BF16], name="mm_dattn")
    dproj, gg_ws, gg_bs_t, gg_gv = _gmlp_bwd(proj, dga_act, rstd_v, g_v, w_s[0], b_col, sel, dproj, D, GW, "gmlp_bwd")
    gs_early = [gs_ba, gw_bb.reshape(N_DEV, -1, D), gw_out.reshape(N_DEV, -1, D)]
    dq, dk, dv, ((parts_ff1,), sib_early) = _attn_bwd(q3, k3, v3, attn, lse, dattn.reshape(B, S, -1), H, "attn_bwd",
                                                      comm=_Comm([_ChipsJob([cs_ff1]), _SiblingJob(gs_early)]))
    cs_early = [_chip_sum(g, r, c_idx, "chip_sum_" + nm)
                for g, r, nm in zip(gs_early, sib_early, ["w_branch_a", "w_branch_b", "w_out"])]
    dq_raw, gg_qn = _q_up_bwd(qn, w_uq_al, g_qn_al, cos_t, sin_t, dq.reshape(T, -1), "q_up_bwd")
    dkv_raw, dkpe, gg_kn = _kv_up_bwd(kvn, g_ukv, proj, g_kn_al, cos_t, sin_t, dk.reshape(T, -1), dv.reshape(T, -1),
                                      kpe_col, "kv_up_bwd")
    gw_uq_al = _matmul(qn, dq_raw, mode="tn", out_dtypes=[BF16], name="mm_gw_uq")
    dqn = _matmul(dq_raw, w_uq_al, mode="nt", out_dtypes=[F32], name="mm_dqn")
    gs_ukv = _matmul(kvn, dkv_raw, mode="tn", out_dtypes=[BF16], name="mm_gw_ukv", out_shards=True)
    dkvn = _matmul(dkv_raw, g_ukv, mode="nt", out_dtypes=[F32], name="mm_dkvn", b_shards=True)
    dproj, gg_qlat, gg_kvlat = _lat_norm_bwd(
        proj, dqn, dkvn, dkpe, rstd_q, rstd_kv, g_q_lat, g_kv_lat, dproj, D, GW, QL, KVL, "latnorm_bwd")
    gs_late = [_col_shards(_unpad_heads(gw_uq_al, H)), gs_ukv]
    (sib_late,) = _run_comm([_SiblingJob(gs_late)], "rs_sibling_late")
    cs_mid = [_chip_sum(g, r, c_idx, "chip_sum_" + nm) for g, r, nm in zip(gs_late, sib_late, ["w_uq", "w_ukv"])] + cs_early
    gw_al_t, (parts_mid,) = _matmul(dproj, h1f, mode="tn", out_dtypes=[BF16], name="mm_gw_in", comm=_Comm([_ChipsJob(cs_mid)]))
    gs_in = jnp.stack([_al_rows(gw_al_t, j * n_in, (j + 1) * n_in, OFF_GATE, 2 * D) for j in range(N_DEV)])
    ((sib_in,),) = _run_comm([_SiblingJob([gs_in])], "rs_sibling_in")
    cs_in = _chip_sum(gs_in, sib_in, c_idx, "chip_sum_w_in")
    dh1, ((parts_in,),) = _matmul(dproj, w_al_t, mode="nn", out_dtypes=[F32], name="mm_dh1", comm=_Comm([_ChipsJob([cs_in])]))
    grad_x, d_sh1, d_sc1, gg_norm1 = _norm_mod_bwd(dh1.reshape(B, S, D), x, rstd1, sc1, g_norm1, dx1, "norm1_bwd")

    dmod_mine = jnp.concatenate([d_sh1, d_sc1, d_ga1, d_sh2, d_sc2, d_ga2], axis=2).reshape(B, 6 * D)
    dmod_all = _all_gather_small(dmod_mine, "ag_dmod").reshape(N_DEV * B, 6 * D)
    dmod_cols = lax.dynamic_slice(dmod_all, (0, dev * n_ada), (N_DEV * B, n_ada))
    ada_out = _ada_bwd_adam(c_all, dmod_cols, w_ada[0], m_w_ada[0], v_w_ada[0], "ada_bwd_adam")
    nb_rows = 8
    bada_out = _adam_from_parts(b_ada.reshape(nb_rows, -1), m_b_ada.reshape(nb_rows, -1), v_b_ada.reshape(nb_rows, -1),
                                dmod_all.reshape(N_DEV * B, nb_rows, -1), "adam_b_ada")

    names = ["w_in", "w_uq", "w_ukv", "w_branch_a", "w_branch_b", "w_out", "w_ff1", "w_ff2"]
    parts = [parts_in] + list(parts_mid) + [parts_ff1, parts_ff2]
    ms = [m_w_in, m_w_uq, m_w_ukv, m_w_branch_a, m_w_branch_b, m_w_out, m_w_ff1, m_w_ff2]
    vs = [v_w_in, v_w_uq, v_w_ukv, v_w_branch_a, v_w_branch_b, v_w_out, v_w_ff1, v_w_ff2]
    big_out = {}
    for nm, w, m, v, p in zip(names, big, ms, vs, parts):
        if nm == "w_in":
            res = _adam_from_parts(jnp.transpose(w[0]), jnp.transpose(m[0]), jnp.transpose(v[0]), p, "adam_" + nm)
            big_out[nm] = [jnp.transpose(r) for r in res]
        else:
            big_out[nm] = _adam_from_parts(w[0], m[0], v[0], p, "adam_" + nm)

    small = [("g_norm1", g_norm1, m_g_norm1, v_g_norm1, gg_norm1),
             ("g_v", g_v, m_g_v, v_g_v, gg_gv),
             ("w_s", w_s, m_w_s, v_w_s, gg_ws),
             ("b_s", b_s, m_b_s, v_b_s, jnp.transpose(gg_bs_t[:, :GROUPS])),
             ("g_q_lat", g_q_lat, m_g_q_lat, v_g_q_lat, gg_qlat),
             ("g_kv_lat", g_kv_lat, m_g_kv_lat, v_g_kv_lat, gg_kvlat),
             ("g_qn", g_qn, m_g_qn, v_g_qn, gg_qn[:, :QK_HEAD]),
             ("g_kn", g_kn, m_g_kn, v_g_kn, gg_kn[:, :QK_HEAD]),
             ("g_norm2", g_norm2, m_g_norm2, v_g_norm2, gg_norm2)]
    sizes = [w.size for _, w, _, _, _ in small]
    n_small = sum(sizes)
    n_small_pad = _round_up(n_small, 8 * LANES)

    def flat_cat(arrs):
        return jnp.pad(jnp.concatenate([a.reshape(-1) for a in arrs]), (0, n_small_pad - n_small))

    part_small = _all_gather_small(flat_cat([t[4] for t in small]), "ag_small_grads")
    small_out = _adam_from_parts(
        flat_cat([t[1] for t in small]).reshape(8, -1), flat_cat([t[2] for t in small]).reshape(8, -1),
        flat_cat([t[3] for t in small]).reshape(8, -1), part_small.reshape(N_DEV, 8, -1), "adam_small")
    offs = [sum(sizes[:i]) for i in range(len(sizes))]

    def small_piece(kind, i):
        return small_out[kind].reshape(-1)[offs[i]:offs[i] + sizes[i]].reshape(small[i][1].shape)

    small_idx = {t[0]: i for i, t in enumerate(small)}
    order = ["w_ada", "b_ada", "g_norm1", "w_in", "g_v", "w_s", "b_s", "g_q_lat", "g_kv_lat", "w_uq", "w_ukv", "g_qn", "g_kn",
             "w_branch_a", "w_branch_b", "w_out", "g_norm2", "w_ff1", "w_ff2"]

    def result(kind, nm):
        if nm == "w_ada":
            return ada_out[kind][None]
        if nm == "b_ada":
            return bada_out[kind].reshape(b_ada.shape)
        if nm in small_idx:
            return small_piece(kind, small_idx[nm])
        return big_out[nm][kind][None]

    outs = [loss, grad_x]
    for kind in range(4):
        outs += [result(kind, nm) for nm in order]
    return tuple(outs)
```
